```python
import jax, jax.numpy as jnp
from jax import lax
import numpy as np

D_MODEL = 2048
BATCH = 4
SEQ = 2048
DEPTH = 1
DEC_BATCH = 128
DEC_SEQ = 4
PAST_LEN = 16384
PAGE_SIZE = 128

CHUNK = 128
W_A = D_MODEL
N_GROUPS_A = 16
GROUP_A = W_A // N_GROUPS_A
W_B = D_MODEL
CONV_K = 3
D_FF = (11 * D_MODEL) // 4
IN_WIDTH = 2 * W_A + 3 * W_B + 2 * D_MODEL
N_MOD = 6
EPS = 1e-6

kernel_name = "hybrid_gmlp_shortconv_convffn_decoder_step"


def rmsnorm(x, g):
    xf = x.astype(jnp.float32)
    y = xf * lax.rsqrt(jnp.mean(xf * xf, axis=-1, keepdims=True) + EPS)
    return (y * g.astype(jnp.float32)).astype(x.dtype)


def causal_dwconv(x, buf, w):
    T = x.shape[1]
    xp = jnp.concatenate([buf.astype(x.dtype), x], axis=1)
    y = w[0] * xp[:, 0:T]
    for k in range(1, CONV_K):
        y = y + w[k] * xp[:, k:k + T]
    return y, xp[:, -(CONV_K - 1):]


def spatial_gate(v, w_s, b_s):
    Bn, T, _ = v.shape
    L = min(T, CHUNK)
    n = T // L
    mask = jnp.tril(jnp.ones((L, L), dtype=bool))
    w = jnp.where(mask, w_s[:, :L, :L], 0).astype(v.dtype)
    vc = v.reshape(Bn, n, L, N_GROUPS_A, GROUP_A)
    s = jnp.einsum('gts,bnsgd->bntgd', w, vc)
    s = s + jnp.transpose(b_s[:, :L])[None, None, :, :, None].astype(v.dtype)
    return s.reshape(Bn, T, W_A)


def decoder_layer(x, c, conv_b_buf, conv_f_buf, w_ada, b_ada, g_pre_mix, g_post_mix, w_in, g_v,
                  w_s, b_s, conv_b_w, w_out_a, w_out_b, w_o, g_pre_ffn, g_post_ffn, w_up,
                  conv_f_w, w_down):
    mod = jax.nn.silu(c) @ w_ada + b_ada
    sh1, sc1, gt1, sh2, sc2, gt2 = jnp.split(mod[:, None, :], N_MOD, axis=-1)

    h = rmsnorm(x, g_pre_mix) * (1 + sc1) + sh1
    z = h @ w_in
    cuts = [W_A, 2 * W_A, 2 * W_A + W_B, 2 * W_A + 2 * W_B, 2 * W_A + 3 * W_B,
            2 * W_A + 3 * W_B + D_MODEL]
    u, v, bg, cg, xb, ga, gb = jnp.split(z, cuts, axis=-1)
    v = rmsnorm(v, g_v)
    ya = u * spatial_gate(v, w_s, b_s)
    cb, new_b_buf = causal_dwconv(cg * xb, conv_b_buf, conv_b_w)
    yb = bg * cb
    merged = jax.nn.sigmoid(ga) * (ya @ w_out_a) + jax.nn.sigmoid(gb) * (yb @ w_out_b)
    x = x + gt1 * rmsnorm(merged @ w_o, g_post_mix)

    h = rmsnorm(x, g_pre_ffn) * (1 + sc2) + sh2
    a, b = jnp.split(h @ w_up, 2, axis=-1)
    a, new_f_buf = causal_dwconv(a, conv_f_buf, conv_f_w)
    f = (jax.nn.gelu(a) * b) @ w_down
    x = x + gt2 * rmsnorm(f, g_post_ffn)
    return x, new_b_buf, new_f_buf, v


def setup_inputs(seed: int = 0) -> dict:
    key = jax.random.key(seed)
    ks = jax.random.split(key, 24)
    nrm = lambda k, shape, s: jax.random.normal(k, shape, jnp.float32) * s
    gain = lambda k, n: 1.0 + 0.01 * jax.random.normal(k, (DEPTH, n), jnp.float32)
    return {
        "x_prompt": nrm(ks[0], (BATCH, SEQ, D_MODEL), 1.0),
        "x_sample": nrm(ks[1], (DEC_BATCH, DEC_SEQ, D_MODEL), 1.0),
        "c_prompt": nrm(ks[2], (BATCH, D_MODEL), 1.0),
        "c_sample": nrm(ks[3], (DEC_BATCH, D_MODEL), 1.0),
        "state_conv_b": nrm(ks[4], (DEPTH, DEC_BATCH, CONV_K - 1, W_B), 1.0),
        "state_conv_ffn": nrm(ks[5], (DEPTH, DEC_BATCH, CONV_K - 1, D_FF), 1.0),
        "w_ada": nrm(ks[6], (DEPTH, D_MODEL, N_MOD * D_MODEL), 0.5 * D_MODEL ** -0.5),
        "b_ada": nrm(ks[7], (DEPTH, N_MOD * D_MODEL), 0.01),
        "g_pre_mix": gain(ks[8], D_MODEL),
        "g_post_mix": gain(ks[9], D_MODEL),
        "w_in": nrm(ks[10], (DEPTH, D_MODEL, IN_WIDTH), D_MODEL ** -0.5),
        "g_v": gain(ks[11], W_A),
        "w_s": nrm(ks[12], (DEPTH, N_GROUPS_A, CHUNK, CHUNK), CHUNK ** -0.5),
        "b_s": 1.0 + nrm(ks[13], (DEPTH, N_GROUPS_A, CHUNK), 0.1),
        "conv_b_w": nrm(ks[14], (DEPTH, CONV_K, W_B), CONV_K ** -0.5),
        "w_out_a": nrm(ks[15], (DEPTH, W_A, D_MODEL), W_A ** -0.5),
        "w_out_b": nrm(ks[16], (DEPTH, W_B, D_MODEL), W_B ** -0.5),
        "w_o": nrm(ks[17], (DEPTH, D_MODEL, D_MODEL), D_MODEL ** -0.5),
        "g_pre_ffn": gain(ks[18], D_MODEL),
        "g_post_ffn": gain(ks[19], D_MODEL),
        "w_up": nrm(ks[20], (DEPTH, D_MODEL, 2 * D_FF), D_MODEL ** -0.5),
        "conv_f_w": nrm(ks[21], (DEPTH, CONV_K, D_FF), CONV_K ** -0.5),
        "w_down": nrm(ks[22], (DEPTH, D_FF, D_MODEL), D_FF ** -0.5),
    }


def reference(x_prompt, x_sample, c_prompt, c_sample, state_conv_b, state_conv_ffn, w_ada, b_ada,
              g_pre_mix, g_post_mix, w_in, g_v, w_s, b_s, conv_b_w, w_out_a, w_out_b, w_o,
              g_pre_ffn, g_post_ffn, w_up, conv_f_w, w_down):
    xp, xs = x_prompt, x_sample
    pb_list, sb_list, pf_list, sf_list, sv_list = [], [], [], [], []
    for l in range(DEPTH):
        params = (w_ada[l], b_ada[l], g_pre_mix[l], g_post_mix[l], w_in[l], g_v[l], w_s[l], b_s[l],
                  conv_b_w[l], w_out_a[l], w_out_b[l], w_o[l], g_pre_ffn[l], g_post_ffn[l],
                  w_up[l], conv_f_w[l], w_down[l])
        zero_b = jnp.zeros((xp.shape[0], CONV_K - 1, W_B), xp.dtype)
        zero_f = jnp.zeros((xp.shape[0], CONV_K - 1, D_FF), xp.dtype)
        xp, pb, pf, _ = decoder_layer(xp, c_prompt, zero_b, zero_f, *params)
        xs, sb, sf, sv = decoder_layer(xs, c_sample, state_conv_b[l], state_conv_ffn[l], *params)
        pb_list.append(pb)
        sb_list.append(sb)
        pf_list.append(pf)
        sf_list.append(sf)
        sv_list.append(sv)
    return (xp, xs, jnp.stack(pb_list), jnp.stack(sb_list), jnp.stack(pf_list),
            jnp.stack(sf_list), jnp.stack(sv_list))
```

```python
import functools

import jax
import jax.numpy as jnp
from jax import lax
from jax.experimental import pallas as pl
from jax.experimental.pallas import tpu as pltpu

EPS = 1e-6
CHUNK = 128
GROUP = 128
CONV_K = 3
N_MOD = 6
SUBLANES = 8
VMEM_LIMIT_BYTES = 56 * 1024 * 1024

_BF16 = jnp.bfloat16
_F32 = jnp.float32


def _dot(a, b):
    return jnp.dot(a, b, preferred_element_type=_F32)


def _params(n_axes):
    return pltpu.CompilerParams(
        dimension_semantics=("arbitrary",) * n_axes,
        vmem_limit_bytes=VMEM_LIMIT_BYTES,
    )


def _rms(xf, g):
    ms = jnp.mean(xf * xf, axis=-1, keepdims=True)
    return xf * lax.rsqrt(ms + EPS) * g


def _causal_conv_rows(p, prev, cw_ref):
    rows = p.shape[0]
    row = lax.broadcasted_iota(jnp.int32, (rows, 1), 0)
    m1 = jnp.where(row == 0, prev[7:8, :], pltpu.roll(p, 1, 0))
    m2 = jnp.where(row == 0, prev[6:7, :], jnp.where(row == 1, prev[7:8, :], pltpu.roll(p, 2, 0)))
    return cw_ref[0:1, :] * m2 + cw_ref[1:2, :] * m1 + cw_ref[2:3, :] * p


def _causal_conv_slabs(slabs, cw_ref):
    w0, w1, w2 = cw_ref[0:1, :], cw_ref[1:2, :], cw_ref[2:3, :]
    return [w0 * slabs[t] + w1 * slabs[t + 1] + w2 * slabs[t + 2] for t in range(len(slabs) - 2)]


def _mod_kernel(c_ref, w_ref, b_ref, o_ref):
    c = c_ref[...]
    a = (c * jax.nn.sigmoid(c)).astype(_BF16)
    o_ref[...] = _dot(a, w_ref[...].astype(_BF16)) + b_ref[...]


def _mod_call(c_all, w_ada, b_ada, tn=1024):
    rows, d = c_all.shape
    n = w_ada.shape[1]
    return pl.pallas_call(
        _mod_kernel,
        grid=(n // tn,),
        in_specs=[
            pl.BlockSpec((rows, d), lambda j: (0, 0)),
            pl.BlockSpec((d, tn), lambda j: (0, j)),
            pl.BlockSpec((1, tn), lambda j: (0, j)),
        ],
        out_specs=pl.BlockSpec((rows, tn), lambda j: (0, j)),
        out_shape=jax.ShapeDtypeStruct((rows, n), _F32),
        compiler_params=_params(1),
        name="adaln_mod",
    )(c_all, w_ada, b_ada)


def _gmlp_kernel(*refs, sample, tm, tn, n_blk, slab):
    if sample:
        (x_ref, sh_ref, sc_ref, gpre_ref, wv_ref, wu_ref, gv_ref, ws_ref, bias_ref,
         ya_ref, h_ref, vn_ref, h_scr, v_scr, s_scr) = refs
    else:
        (x_ref, sh_ref, sc_ref, gpre_ref, wv_ref, wu_ref, gv_ref, ws_ref, bias_ref,
         ya_ref, h_ref, h_scr, v_scr, s_scr, wt_scr) = refs
    i = pl.program_id(0)
    j = pl.program_id(1)
    d = n_blk * tn
    n_slab = tm // slab

    @pl.when(j == 0)
    def _():
        for s in range(n_slab):
            r = slice(s * slab, (s + 1) * slab)
            h = _rms(x_ref[r, :], gpre_ref[...]) * (1.0 + sc_ref[...]) + sh_ref[...]
            h = h.astype(_BF16)
            h_scr[r, :] = h
            h_ref[r, :] = h

    if not sample:
        @pl.when((i == 0) & (j == 0))
        def _():
            tril = (lax.broadcasted_iota(jnp.int32, (CHUNK, CHUNK), 0)
                    >= lax.broadcasted_iota(jnp.int32, (CHUNK, CHUNK), 1))
            for g in range(d // GROUP):
                wt_scr[g] = jnp.where(tril, ws_ref[g], 0.0).astype(_BF16)

    @pl.when(j < n_blk)
    def _():
        v_scr[j] = _dot(h_scr[...], wv_ref[...])

    def _normalise(r):
        ss = 0.0
        for k in range(n_blk):
            vk = v_scr[k, r, :]
            ss = ss + jnp.sum(vk * vk, axis=-1, keepdims=True)
        rs = lax.rsqrt(ss * (1.0 / d) + EPS)
        return [v_scr[k, r, :] * rs * gv_ref[:, k * tn:(k + 1) * tn] for k in range(n_blk)]

    @pl.when(j == n_blk - 1)
    def _():
        gpb = tn // GROUP
        if sample:
            for t in range(n_slab):
                r = slice(t * slab, (t + 1) * slab)
                for k, vn in enumerate(_normalise(r)):
                    v_scr[k, r, :] = vn
                    vn_ref[r, k * tn:(k + 1) * tn] = vn
            for t in range(n_slab):
                r = slice(t * slab, (t + 1) * slab)
                for k in range(n_blk):
                    c = slice(k * tn, (k + 1) * tn)
                    acc = ws_ref[t * n_slab:t * n_slab + 1, c] * v_scr[k, 0:slab, :]
                    for s in range(1, t + 1):
                        acc = acc + (ws_ref[t * n_slab + s:t * n_slab + s + 1, c]
                                     * v_scr[k, s * slab:(s + 1) * slab, :])
                    s_scr[k, r, :] = acc + bias_ref[t:t + 1, c]
        else:
            def chunk_body(c, carry):
                r = pl.ds(pl.multiple_of(c * CHUNK, CHUNK), CHUNK)
                for k, vn in enumerate(_normalise(r)):
                    vb = vn.astype(_BF16)
                    for gg in range(gpb):
                        g = k * gpb + gg
                        lanes = slice(gg * GROUP, (gg + 1) * GROUP)
                        sb = _dot(wt_scr[g], vb[:, lanes]) + bias_ref[:, g * GROUP:(g + 1) * GROUP]
                        s_scr[k, r, lanes] = sb
                return carry
            lax.fori_loop(0, tm // CHUNK, chunk_body, 0)

    @pl.when(j >= n_blk)
    def _():
        u = _dot(h_scr[...], wu_ref[...])
        ya_ref[...] = (u * s_scr[j - n_blk]).astype(_BF16)


def _gmlp_call(x, sh, sc, gpre, w_in, gv, ws, bias, *, sample, tm, tn, mod_specs):
    m, d = x.shape
    n_blk = d // tn
    slab = CHUNK if sample else tm
    mod_sh, mod_sc = mod_specs
    in_specs = [
        pl.BlockSpec((tm, d), lambda i, j: (i, 0)),
        mod_sh, mod_sc,
        pl.BlockSpec((1, d), lambda i, j: (0, 0)),
        pl.BlockSpec((d, tn), lambda i, j: (0, n_blk + jnp.minimum(j, n_blk - 1))),
        pl.BlockSpec((d, tn), lambda i, j: (0, jnp.maximum(j - n_blk, 0))),
        pl.BlockSpec((1, d), lambda i, j: (0, 0)),
        pl.BlockSpec(ws.shape, lambda i, j: (0,) * ws.ndim),
        pl.BlockSpec(bias.shape, lambda i, j: (0, 0)),
    ]
    out_specs = [
        pl.BlockSpec((tm, tn), lambda i, j: (i, jnp.maximum(j - n_blk, 0))),
        pl.BlockSpec((tm, d), lambda i, j: (i, 0)),
    ]
    out_shape = [jax.ShapeDtypeStruct((m, d), _BF16), jax.ShapeDtypeStruct((m, d), _BF16)]
    scratch = [pltpu.VMEM((tm, d), _BF16), pltpu.VMEM((n_blk, tm, tn), _F32),
               pltpu.VMEM((n_blk, tm, tn), _F32)]
    if sample:
        out_specs.append(pl.BlockSpec((tm, d), lambda i, j: (i, 0)))
        out_shape.append(jax.ShapeDtypeStruct((m, d), _F32))
    else:
        scratch.append(pltpu.VMEM((d // GROUP, CHUNK, CHUNK), _BF16))
    return pl.pallas_call(
        functools.partial(_gmlp_kernel, sample=sample, tm=tm, tn=tn, n_blk=n_blk, slab=slab),
        grid=(m // tm, 2 * n_blk),
        in_specs=in_specs, out_specs=out_specs, out_shape=out_shape, scratch_shapes=scratch,
        compiler_params=_params(2),
        name="gmlp_sample" if sample else "gmlp_prompt",
    )(x, sh, sc, gpre, w_in, w_in, gv, ws, bias)


def _shortconv_kernel(*refs, sample, tm, tiles_per_seq, slab):
    if sample:
        h_ref, wbg_ref, wcg_ref, wxb_ref, cw_ref, st_ref, yb_ref, tail_ref = refs
    else:
        h_ref, wbg_ref, wcg_ref, wxb_ref, cw_ref, yb_ref, tail_ref, carry_scr = refs
    i = pl.program_id(0)
    j = pl.program_id(1)
    h = h_ref[...]
    bg = _dot(h, wbg_ref[...])
    p = _dot(h, wcg_ref[...]) * _dot(h, wxb_ref[...])
    if sample:
        n_slab = tm // slab
        seq = [st_ref[0:slab, :], st_ref[slab:2 * slab, :]]
        seq += [p[t * slab:(t + 1) * slab, :] for t in range(n_slab)]
        cb = _causal_conv_slabs(seq, cw_ref)
        for t in range(n_slab):
            r = slice(t * slab, (t + 1) * slab)
            yb_ref[r, :] = (bg[r, :] * cb[t]).astype(_BF16)
        tail_ref[...] = p[(n_slab - 2) * slab:, :]
    else:
        @pl.when(i % tiles_per_seq == 0)
        def _():
            carry_scr[j] = jnp.zeros(carry_scr.shape[1:], _F32)

        yb_ref[...] = (bg * _causal_conv_rows(p, carry_scr[j], cw_ref)).astype(_BF16)
        tail = p[tm - SUBLANES:, :]
        carry_scr[j] = tail
        tail_ref[...] = tail


def _shortconv_call(h, w_in, col0, cw, state, *, sample, tm, tn, tiles_per_seq):
    m, d = h.shape
    w = cw.shape[1]
    n_blk = w // tn
    b0 = col0 // tn
    tail_rows = 2 * CHUNK if sample else SUBLANES
    in_specs = [
        pl.BlockSpec((tm, d), lambda i, j: (i, 0)),
        pl.BlockSpec((d, tn), lambda i, j: (0, b0 + j)),
        pl.BlockSpec((d, tn), lambda i, j: (0, b0 + n_blk + j)),
        pl.BlockSpec((d, tn), lambda i, j: (0, b0 + 2 * n_blk + j)),
        pl.BlockSpec((CONV_K, tn), lambda i, j: (0, j)),
    ]
    args = [h, w_in, w_in, w_in, cw]
    scratch = []
    if sample:
        in_specs.append(pl.BlockSpec((2 * CHUNK, tn), lambda i, j: (0, j)))
        args.append(state)
    else:
        scratch.append(pltpu.VMEM((n_blk, SUBLANES, tn), _F32))
    return pl.pallas_call(
        functools.partial(_shortconv_kernel, sample=sample, tm=tm, tiles_per_seq=tiles_per_seq,
                          slab=CHUNK),
        grid=(m // tm, n_blk),
        in_specs=in_specs,
        out_specs=[pl.BlockSpec((tm, tn), lambda i, j: (i, j)),
                   pl.BlockSpec((tail_rows, tn), lambda i, j: (i, j))],
        out_shape=[jax.ShapeDtypeStruct((m, w), _BF16),
                   jax.ShapeDtypeStruct((m // tm * tail_rows, w), _F32)],
        scratch_shapes=scratch,
        compiler_params=_params(2),
        name="shortconv_sample" if sample else "shortconv_prompt",
    )(*args)


def _merge_kernel(h_ref, ya_ref, yb_ref, wga_ref, wgb_ref, woa_ref, wob_ref, m_ref):
    h = h_ref[...]
    ga = jax.nn.sigmoid(_dot(h, wga_ref[...]))
    gb = jax.nn.sigmoid(_dot(h, wgb_ref[...]))
    m = ga * _dot(ya_ref[...], woa_ref[...]) + gb * _dot(yb_ref[...], wob_ref[...])
    m_ref[...] = m.astype(_BF16)


def _merge_call(h, ya, yb, w_in, gate_col0, w_out_a, w_out_b, *, tm, tn):
    m, d = h.shape
    n_blk = d // tn
    b0 = gate_col0 // tn
    row = pl.BlockSpec((tm, d), lambda i, j: (i, 0))
    return pl.pallas_call(
        _merge_kernel,
        grid=(m // tm, n_blk),
        in_specs=[row, row, row,
                  pl.BlockSpec((d, tn), lambda i, j: (0, b0 + j)),
                  pl.BlockSpec((d, tn), lambda i, j: (0, b0 + n_blk + j)),
                  pl.BlockSpec((d, tn), lambda i, j: (0, j)),
                  pl.BlockSpec((d, tn), lambda i, j: (0, j))],
        out_specs=pl.BlockSpec((tm, tn), lambda i, j: (i, j)),
        out_shape=jax.ShapeDtypeStruct((m, d), _BF16),
        compiler_params=_params(2),
        name="gated_merge",
    )(h, ya, yb, w_in, w_in, w_out_a, w_out_b)


def _proj_kernel(m_ref, wo_ref, x_ref, gt_ref, gpost_ref, sh_ref, sc_ref, gpre_ref,
                 x1_ref, h2_ref, *, tm, slab):
    for s in range(tm // slab):
        r = slice(s * slab, (s + 1) * slab)
        y = _dot(m_ref[r, :], wo_ref[...])
        x1 = x_ref[r, :] + gt_ref[...] * _rms(y, gpost_ref[...])
        x1_ref[r, :] = x1
        h2 = _rms(x1, gpre_ref[...]) * (1.0 + sc_ref[...]) + sh_ref[...]
        h2_ref[r, :] = h2.astype(_BF16)


def _proj_call(mg, w_o, x, gt, gpost, sh, sc, gpre, *, tm, slab, mod_specs):
    m, d = x.shape
    row = pl.BlockSpec((tm, d), lambda i: (i, 0))
    vec = pl.BlockSpec((1, d), lambda i: (0, 0))
    mod_gt, mod_sh, mod_sc = mod_specs
    return pl.pallas_call(
        functools.partial(_proj_kernel, tm=tm, slab=slab),
        grid=(m // tm,),
        in_specs=[row, pl.BlockSpec((d, d), lambda i: (0, 0)), row, mod_gt, vec, mod_sh, mod_sc, vec],
        out_specs=[row, row],
        out_shape=[jax.ShapeDtypeStruct((m, d), _F32), jax.ShapeDtypeStruct((m, d), _BF16)],
        compiler_params=_params(1),
        name="out_proj",
    )(mg, w_o, x, gt, gpost, sh, sc, gpre)


def _ffn_kernel(*refs, sample, tm, n_blk, tiles_per_seq, slab):
    if sample:
        (h_ref, wa_ref, wb_ref, cw_ref, wd_ref, x1_ref, gt_ref, gpost_ref, st_ref,
         out_ref, tail_ref, acc_scr) = refs
    else:
        (h_ref, wa_ref, wb_ref, cw_ref, wd_ref, x1_ref, gt_ref, gpost_ref,
         out_ref, tail_ref, acc_scr, carry_scr) = refs
    i = pl.program_id(0)
    j = pl.program_id(1)
    h = h_ref[...]
    a = _dot(h, wa_ref[...])
    b = _dot(h, wb_ref[...])
    if sample:
        n_slab = tm // slab
        seq = [st_ref[0:slab, :], st_ref[slab:2 * slab, :]]
        seq += [a[t * slab:(t + 1) * slab, :] for t in range(n_slab)]
        ac = jnp.concatenate(_causal_conv_slabs(seq, cw_ref), axis=0)
        tail_ref[...] = a[(n_slab - 2) * slab:, :]
    else:
        @pl.when(i % tiles_per_seq == 0)
        def _():
            carry_scr[j] = jnp.zeros(carry_scr.shape[1:], _F32)

        ac = _causal_conv_rows(a, carry_scr[j], cw_ref)
        tail = a[tm - SUBLANES:, :]
        carry_scr[j] = tail
        tail_ref[...] = tail
    g = (jax.nn.gelu(ac) * b).astype(_BF16)
    f = _dot(g, wd_ref[...])

    @pl.when(j == 0)
    def _():
        acc_scr[...] = f

    @pl.when(j > 0)
    def _():
        acc_scr[...] += f

    @pl.when(j == n_blk - 1)
    def _():
        for s in range(tm // slab):
            r = slice(s * slab, (s + 1) * slab)
            out_ref[r, :] = x1_ref[r, :] + gt_ref[...] * _rms(acc_scr[r, :], gpost_ref[...])


def _ffn_call(h2, w_up, cw, w_down, x1, gt, gpost, state, *, sample, tm, tn, tiles_per_seq,
              slab, mod_spec):
    m, d = x1.shape
    f = cw.shape[1]
    n_blk = f // tn
    tail_rows = 2 * CHUNK if sample else SUBLANES
    row = pl.BlockSpec((tm, d), lambda i, j: (i, 0))
    in_specs = [
        row,
        pl.BlockSpec((d, tn), lambda i, j: (0, j)),
        pl.BlockSpec((d, tn), lambda i, j: (0, n_blk + j)),
        pl.BlockSpec((CONV_K, tn), lambda i, j: (0, j)),
        pl.BlockSpec((tn, d), lambda i, j: (j, 0)),
        row, mod_spec,
        pl.BlockSpec((1, d), lambda i, j: (0, 0)),
    ]
    args = [h2, w_up, w_up, cw, w_down, x1, gt, gpost]
    scratch = [pltpu.VMEM((tm, d), _F32)]
    if sample:
        in_specs.append(pl.BlockSpec((2 * CHUNK, tn), lambda i, j: (0, j)))
        args.append(state)
    else:
        scratch.append(pltpu.VMEM((n_blk, SUBLANES, tn), _F32))
    return pl.pallas_call(
        functools.partial(_ffn_kernel, sample=sample, tm=tm, n_blk=n_blk,
                          tiles_per_seq=tiles_per_seq, slab=slab),
        grid=(m // tm, n_blk),
        in_specs=in_specs,
        out_specs=[row, pl.BlockSpec((tail_rows, tn), lambda i, j: (i, j))],
        out_shape=[jax.ShapeDtypeStruct((m, d), _F32),
                   jax.ShapeDtypeStruct((m // tm * tail_rows, f), _F32)],
        scratch_shapes=scratch,
        compiler_params=_params(2),
        name="convffn_sample" if sample else "convffn_prompt",
    )(*args)


def _layer(x, mod, conv_b_state, conv_f_state, p, *, sample, seq_len, tm, tn):
    m, d = x.shape
    tiles_per_seq = 1 if sample else seq_len // tm
    slab = CHUNK if sample else 256

    def mod_spec(k, n_axes):
        if sample:
            blk = (mod.shape[0], d)
            return pl.BlockSpec(blk, (lambda i, j: (0, k)) if n_axes == 2 else (lambda i: (0, k)))
        blk = (None, 1, d)
        if n_axes == 2:
            return pl.BlockSpec(blk, lambda i, j: (i // tiles_per_seq, 0, k))
        return pl.BlockSpec(blk, lambda i: (i // tiles_per_seq, 0, k))

    w_in = p["w_in"]
    wa = d
    outs = _gmlp_call(x, mod, mod, p["g_pre_mix"], w_in, p["g_v"], p["ws"], p["bias"],
                      sample=sample, tm=tm, tn=tn, mod_specs=(mod_spec(0, 2), mod_spec(1, 2)))
    ya, h1 = outs[0], outs[1]
    vn = outs[2] if sample else None
    yb, tail_b = _shortconv_call(h1, w_in, 2 * wa, p["conv_b_w"], conv_b_state, sample=sample,
                                 tm=tm, tn=tn, tiles_per_seq=tiles_per_seq)
    gate_col0 = 2 * wa + 3 * p["conv_b_w"].shape[1]
    mg = _merge_call(h1, ya, yb, w_in, gate_col0, p["w_out_a"], p["w_out_b"], tm=tm, tn=tn)
    x1, h2 = _proj_call(mg, p["w_o"], x, mod, p["g_post_mix"], mod, mod, p["g_pre_ffn"],
                        tm=tm, slab=slab,
                        mod_specs=(mod_spec(2, 1), mod_spec(3, 1), mod_spec(4, 1)))
    out, tail_f = _ffn_call(h2, p["w_up"], p["conv_f_w"], p["w_down"], x1, mod, p["g_post_ffn"],
                            conv_f_state, sample=sample, tm=tm, tn=tn,
                            tiles_per_seq=tiles_per_seq, slab=slab, mod_spec=mod_spec(5, 2))
    return out, tail_b, tail_f, vn


def kernel(x_prompt, x_sample, c_prompt, c_sample, state_conv_b, state_conv_ffn, w_ada, b_ada, g_pre_mix, g_post_mix, w_in, g_v, w_s, b_s, conv_b_w, w_out_a, w_out_b, w_o, g_pre_ffn, g_post_ffn, w_up, conv_f_w, w_down):
    depth = w_in.shape[0]
    bp, seq, d = x_prompt.shape
    bs, tdec, _ = x_sample.shape
    n_groups = w_s.shape[1]
    tm, tn = 512, 512

    xp = x_prompt.reshape(bp * seq, d)
    xs = jnp.transpose(x_sample, (1, 0, 2)).reshape(tdec * bs, d)
    pad = (-(bp + bs)) % SUBLANES
    c_all = jnp.concatenate([c_prompt, c_sample, jnp.zeros((pad, d), _F32)], axis=0)

    pb, sb, pf, sf, sv = [], [], [], [], []
    for l in range(depth):
        mod = _mod_call(c_all, w_ada[l], b_ada[l][None, :])
        mod_p = mod[:bp].reshape(bp, 1, N_MOD * d)
        mod_s = mod[bp:bp + bs]
        vec = lambda a: a[l][None, :]
        shared = {
            "w_in": w_in[l].astype(_BF16), "w_out_a": w_out_a[l].astype(_BF16),
            "w_out_b": w_out_b[l].astype(_BF16), "w_o": w_o[l].astype(_BF16),
            "w_up": w_up[l].astype(_BF16), "w_down": w_down[l].astype(_BF16),
            "g_pre_mix": vec(g_pre_mix), "g_post_mix": vec(g_post_mix), "g_v": vec(g_v),
            "g_pre_ffn": vec(g_pre_ffn), "g_post_ffn": vec(g_post_ffn),
            "conv_b_w": conv_b_w[l], "conv_f_w": conv_f_w[l],
        }
        bias_full = jnp.repeat(jnp.transpose(b_s[l]), GROUP, axis=1)
        p_prompt = dict(shared, ws=w_s[l], bias=bias_full)
        wvec = jnp.repeat(jnp.transpose(w_s[l][:, :tdec, :tdec], (1, 2, 0)).reshape(tdec * tdec, n_groups),
                          GROUP, axis=1)
        p_sample = dict(shared, ws=wvec, bias=bias_full[:tdec])

        xp, tb, tf, _ = _layer(xp, mod_p, None, None, p_prompt, sample=False, seq_len=seq, tm=tm, tn=tn)
        st_b = jnp.transpose(state_conv_b[l], (1, 0, 2)).reshape((CONV_K - 1) * bs, -1)
        st_f = jnp.transpose(state_conv_ffn[l], (1, 0, 2)).reshape((CONV_K - 1) * bs, -1)
        xs, sbt, sft, vn = _layer(xs, mod_s, st_b, st_f, p_sample, sample=True, seq_len=tdec,
                                  tm=tdec * bs, tn=tn)

        def prompt_tail(t):
            t = t.reshape(bp, seq // tm, SUBLANES, -1)
            return t[:, -1, SUBLANES - (CONV_K - 1):, :]

        def sample_rows(t, n):
            return jnp.transpose(t.reshape(n, bs, -1), (1, 0, 2))

        pb.append(prompt_tail(tb))
        pf.append(prompt_tail(tf))
        sb.append(sample_rows(sbt, CONV_K - 1))
        sf.append(sample_rows(sft, CONV_K - 1))
        sv.append(sample_rows(vn, tdec))

    y_prompt = xp.reshape(bp, seq, d)
    y_sample = jnp.transpose(xs.reshape(tdec, bs, d), (1, 0, 2))
    return (y_prompt, y_sample, jnp.stack(pb), jnp.stack(sb), jnp.stack(pf), jnp.stack(sf),
            jnp.stack(sv))
```

```python
import functools

import jax
import jax.numpy as jnp
from jax import lax
from jax.experimental import pallas as pl
from jax.experimental.pallas import tpu as pltpu

EPS = 1e-6
CHUNK = 128
GROUP = 128
CONV_K = 3
N_MOD = 6
SUBLANES = 8
VMEM_LIMIT_BYTES = 56 * 1024 * 1024
ROW_TILE = 512
PROMPT_SLAB = 256
MIX_COL_TILE = 1024
FFN_COL_TILE = 512

_BF16 = jnp.bfloat16
_F32 = jnp.float32


def _dot(a, b):
    return jnp.dot(a, b, preferred_element_type=_F32)


def _params(n_axes):
    return pltpu.CompilerParams(
        dimension_semantics=("arbitrary",) * n_axes,
        vmem_limit_bytes=VMEM_LIMIT_BYTES,
    )


def _rms(xf, g):
    ms = jnp.mean(xf * xf, axis=-1, keepdims=True)
    return xf * lax.rsqrt(ms + EPS) * g


def _causal_conv_rows(p, prev, cw_ref):
    rows = p.shape[0]
    row = lax.broadcasted_iota(jnp.int32, (rows, 1), 0)
    m1 = jnp.where(row == 0, prev[7:8, :], pltpu.roll(p, 1, 0))
    m2 = jnp.where(row == 0, prev[6:7, :], jnp.where(row == 1, prev[7:8, :], pltpu.roll(p, 2, 0)))
    return cw_ref[0:1, :] * m2 + cw_ref[1:2, :] * m1 + cw_ref[2:3, :] * p


def _causal_conv_slabs(slabs, cw_ref):
    w0, w1, w2 = cw_ref[0:1, :], cw_ref[1:2, :], cw_ref[2:3, :]
    return [w0 * slabs[t] + w1 * slabs[t + 1] + w2 * slabs[t + 2] for t in range(len(slabs) - 2)]


def _mod_kernel(c_ref, w_ref, b_ref, o_ref):
    c = c_ref[...]
    a = (c * jax.nn.sigmoid(c)).astype(_BF16)
    o_ref[...] = _dot(a, w_ref[...].astype(_BF16)) + b_ref[...]


def _mod_call(c_all, w_ada, b_ada, tn=1024):
    rows, d = c_all.shape
    n = w_ada.shape[1]
    return pl.pallas_call(
        _mod_kernel,
        grid=(n // tn,),
        in_specs=[
            pl.BlockSpec((rows, d), lambda j: (0, 0)),
            pl.BlockSpec((d, tn), lambda j: (0, j)),
            pl.BlockSpec((1, tn), lambda j: (0, j)),
        ],
        out_specs=pl.BlockSpec((rows, tn), lambda j: (0, j)),
        out_shape=jax.ShapeDtypeStruct((rows, n), _F32),
        compiler_params=_params(1),
        name="adaln_mod",
    )(c_all, w_ada, b_ada)


def _gmlp_kernel(*refs, sample, tm, tn, n_blk, slab):
    if sample:
        (x_ref, sh_ref, sc_ref, gpre_ref, wv_ref, wu_ref, gv_ref, ws_ref, bias_ref,
         ya_ref, h_ref, vn_ref, h_scr, v_scr) = refs
    else:
        (x_ref, sh_ref, sc_ref, gpre_ref, wv_ref, wu_ref, gv_ref, ws_ref, bias_ref,
         ya_ref, h_ref, h_scr, v_scr, wt_scr) = refs
    i = pl.program_id(0)
    j = pl.program_id(1)
    d = n_blk * tn
    n_slab = tm // slab
    slabs = [slice(s * slab, (s + 1) * slab) for s in range(n_slab)]

    if not sample:
        @pl.when((i == 0) & (j == 0))
        def _():
            tril = (lax.broadcasted_iota(jnp.int32, (CHUNK, CHUNK), 0)
                    >= lax.broadcasted_iota(jnp.int32, (CHUNK, CHUNK), 1))
            for g in range(d // GROUP):
                wt_scr[g] = jnp.where(tril, ws_ref[g], 0.0).astype(_BF16)

    @pl.when(j == 0)
    def _():
        for r in slabs:
            h = _rms(x_ref[r, :], gpre_ref[...]) * (1.0 + sc_ref[...]) + sh_ref[...]
            h = h.astype(_BF16)
            h_scr[r, :] = h
            h_ref[r, :] = h
            v_scr[0, r, :] = _dot(h, wv_ref[...])

    @pl.when((j > 0) & (j < n_blk))
    def _():
        for r in slabs:
            v_scr[j, r, :] = _dot(h_scr[r, :], wv_ref[...])

    def _normalise(r):
        ss = 0.0
        for k in range(n_blk):
            vk = v_scr[k, r, :]
            ss = ss + jnp.sum(vk * vk, axis=-1, keepdims=True)
        rs = lax.rsqrt(ss * (1.0 / d) + EPS)
        return [v_scr[k, r, :] * rs * gv_ref[:, k * tn:(k + 1) * tn] for k in range(n_blk)]

    def _gate_sample():
        for r in slabs:
            for k, vn in enumerate(_normalise(r)):
                v_scr[k, r, :] = vn
                vn_ref[r, k * tn:(k + 1) * tn] = vn
        for t in reversed(range(n_slab)):
            for k in range(n_blk):
                c = slice(k * tn, (k + 1) * tn)
                acc = ws_ref[t * n_slab:t * n_slab + 1, c] * v_scr[k, slabs[0], :]
                for s in range(1, t + 1):
                    acc = acc + ws_ref[t * n_slab + s:t * n_slab + s + 1, c] * v_scr[k, slabs[s], :]
                v_scr[k, slabs[t], :] = acc + bias_ref[t:t + 1, c]

    def _gate_prompt():
        gpb = tn // GROUP
        for c in range(tm // CHUNK):
            r = slice(c * CHUNK, (c + 1) * CHUNK)
            for k, vn in enumerate(_normalise(r)):
                vb = vn.astype(_BF16)
                for gg in range(gpb):
                    g = k * gpb + gg
                    lanes = slice(gg * GROUP, (gg + 1) * GROUP)
                    v_scr[k, r, lanes] = (_dot(wt_scr[g], vb[:, lanes])
                                          + bias_ref[:, g * GROUP:(g + 1) * GROUP])

    @pl.when(j == n_blk)
    def _():
        u = [_dot(h_scr[r, :], wu_ref[...]) for r in slabs]
        _gate_sample() if sample else _gate_prompt()
        for r, ur in zip(slabs, u):
            ya_ref[r, :] = (ur * v_scr[0, r, :]).astype(_BF16)

    @pl.when(j > n_blk)
    def _():
        for r in slabs:
            ya_ref[r, :] = (_dot(h_scr[r, :], wu_ref[...]) * v_scr[j - n_blk, r, :]).astype(_BF16)


def _gmlp_call(x, sh, sc, gpre, w_in, gv, ws, bias, *, sample, tm, tn, slab, mod_specs):
    m, d = x.shape
    n_blk = d // tn
    mod_sh, mod_sc = mod_specs
    in_specs = [
        pl.BlockSpec((tm, d), lambda i, j: (i, 0)),
        mod_sh, mod_sc,
        pl.BlockSpec((1, d), lambda i, j: (0, 0)),
        pl.BlockSpec((d, tn), lambda i, j: (0, n_blk + jnp.minimum(j, n_blk - 1))),
        pl.BlockSpec((d, tn), lambda i, j: (0, jnp.maximum(j - n_blk, 0))),
        pl.BlockSpec((1, d), lambda i, j: (0, 0)),
        pl.BlockSpec(ws.shape, lambda i, j: (0,) * ws.ndim),
        pl.BlockSpec(bias.shape, lambda i, j: (0, 0)),
    ]
    out_specs = [
        pl.BlockSpec((tm, tn), lambda i, j: (i, jnp.maximum(j - n_blk, 0))),
        pl.BlockSpec((tm, d), lambda i, j: (i, 0)),
    ]
    out_shape = [jax.ShapeDtypeStruct((m, d), _BF16), jax.ShapeDtypeStruct((m, d), _BF16)]
    scratch = [pltpu.VMEM((tm, d), _BF16), pltpu.VMEM((n_blk, tm, tn), _F32)]
    if sample:
        out_specs.append(pl.BlockSpec((tm, d), lambda i, j: (i, 0)))
        out_shape.append(jax.ShapeDtypeStruct((m, d), _F32))
    else:
        scratch.append(pltpu.VMEM((d // GROUP, CHUNK, CHUNK), _BF16))
    return pl.pallas_call(
        functools.partial(_gmlp_kernel, sample=sample, tm=tm, tn=tn, n_blk=n_blk, slab=slab),
        grid=(m // tm, 2 * n_blk),
        in_specs=in_specs, out_specs=out_specs, out_shape=out_shape, scratch_shapes=scratch,
        compiler_params=_params(2),
        name="gmlp_sample" if sample else "gmlp_prompt",
    )(x, sh, sc, gpre, w_in, w_in, gv, ws, bias)


def _shortconv_kernel(*refs, sample, tm, tiles_per_seq, slab):
    if sample:
        h_ref, wbg_ref, wcg_ref, wxb_ref, cw_ref, st_ref, yb_ref, tail_ref = refs
    else:
        h_ref, wbg_ref, wcg_ref, wxb_ref, cw_ref, yb_ref, tail_ref, carry_scr = refs
    i = pl.program_id(0)
    j = pl.program_id(1)
    n_slab = tm // slab
    if not sample:
        @pl.when(i % tiles_per_seq == 0)
        def _():
            carry_scr[j] = jnp.zeros(carry_scr.shape[1:], _F32)

    p_hist = [] if not sample else [st_ref[0:slab, :], st_ref[slab:2 * slab, :]]
    prev = None if sample else carry_scr[j]
    for s in range(n_slab):
        r = slice(s * slab, (s + 1) * slab)
        h = h_ref[r, :]
        bg = _dot(h, wbg_ref[...])
        p = _dot(h, wcg_ref[...]) * _dot(h, wxb_ref[...])
        if sample:
            p_hist.append(p)
            cb = _causal_conv_slabs(p_hist[s:s + CONV_K], cw_ref)[0]
        else:
            cb = _causal_conv_rows(p, prev, cw_ref)
            prev = p[slab - SUBLANES:, :]
        yb_ref[r, :] = (bg * cb).astype(_BF16)
    if sample:
        tail_ref[0:slab, :] = p_hist[-2]
        tail_ref[slab:, :] = p_hist[-1]
    else:
        carry_scr[j] = prev
        tail_ref[...] = prev


def _shortconv_call(h, w_in, col0, cw, state, *, sample, tm, tn, tiles_per_seq, slab):
    m, d = h.shape
    w = cw.shape[1]
    n_blk = w // tn
    b0 = col0 // tn
    tail_rows = 2 * CHUNK if sample else SUBLANES
    in_specs = [
        pl.BlockSpec((tm, d), lambda i, j: (i, 0)),
        pl.BlockSpec((d, tn), lambda i, j: (0, b0 + j)),
        pl.BlockSpec((d, tn), lambda i, j: (0, b0 + n_blk + j)),
        pl.BlockSpec((d, tn), lambda i, j: (0, b0 + 2 * n_blk + j)),
        pl.BlockSpec((CONV_K, tn), lambda i, j: (0, j)),
    ]
    args = [h, w_in, w_in, w_in, cw]
    scratch = []
    if sample:
        in_specs.append(pl.BlockSpec((2 * CHUNK, tn), lambda i, j: (0, j)))
        args.append(state)
    else:
        scratch.append(pltpu.VMEM((n_blk, SUBLANES, tn), _F32))
    return pl.pallas_call(
        functools.partial(_shortconv_kernel, sample=sample, tm=tm, tiles_per_seq=tiles_per_seq,
                          slab=slab),
        grid=(m // tm, n_blk),
        in_specs=in_specs,
        out_specs=[pl.BlockSpec((tm, tn), lambda i, j: (i, j)),
                   pl.BlockSpec((tail_rows, tn), lambda i, j: (i, j))],
        out_shape=[jax.ShapeDtypeStruct((m, w), _BF16),
                   jax.ShapeDtypeStruct((m // tm * tail_rows, w), _F32)],
        scratch_shapes=scratch,
        compiler_params=_params(2),
        name="shortconv_sample" if sample else "shortconv_prompt",
    )(*args)


def _merge_kernel(h_ref, ya_ref, yb_ref, wga_ref, wgb_ref, woa_ref, wob_ref, m_ref, *, tm, slab):
    for s in range(tm // slab):
        r = slice(s * slab, (s + 1) * slab)
        h = h_ref[r, :]
        ga = jax.nn.sigmoid(_dot(h, wga_ref[...]))
        gb = jax.nn.sigmoid(_dot(h, wgb_ref[...]))
        m = ga * _dot(ya_ref[r, :], woa_ref[...]) + gb * _dot(yb_ref[r, :], wob_ref[...])
        m_ref[r, :] = m.astype(_BF16)


def _merge_call(h, ya, yb, w_in, gate_col0, w_out_a, w_out_b, *, tm, tn, slab):
    m, d = h.shape
    n_blk = d // tn
    b0 = gate_col0 // tn
    row = pl.BlockSpec((tm, d), lambda i, j: (i, 0))
    return pl.pallas_call(
        functools.partial(_merge_kernel, tm=tm, slab=slab),
        grid=(m // tm, n_blk),
        in_specs=[row, row, row,
                  pl.BlockSpec((d, tn), lambda i, j: (0, b0 + j)),
                  pl.BlockSpec((d, tn), lambda i, j: (0, b0 + n_blk + j)),
                  pl.BlockSpec((d, tn), lambda i, j: (0, j)),
                  pl.BlockSpec((d, tn), lambda i, j: (0, j))],
        out_specs=pl.BlockSpec((tm, tn), lambda i, j: (i, j)),
        out_shape=jax.ShapeDtypeStruct((m, d), _BF16),
        compiler_params=_params(2),
        name="gated_merge",
    )(h, ya, yb, w_in, w_in, w_out_a, w_out_b)


def _proj_kernel(m_ref, wo_ref, x_ref, gt_ref, gpost_ref, sh_ref, sc_ref, gpre_ref,
                 x1_ref, h2_ref, *, tm, slab):
    for s in range(tm // slab):
        r = slice(s * slab, (s + 1) * slab)
        y = _dot(m_ref[r, :], wo_ref[...])
        x1 = x_ref[r, :] + gt_ref[...] * _rms(y, gpost_ref[...])
        x1_ref[r, :] = x1
        h2 = _rms(x1, gpre_ref[...]) * (1.0 + sc_ref[...]) + sh_ref[...]
        h2_ref[r, :] = h2.astype(_BF16)


def _proj_call(mg, w_o, x, gt, gpost, sh, sc, gpre, *, tm, slab, mod_specs):
    m, d = x.shape
    row = pl.BlockSpec((tm, d), lambda i: (i, 0))
    vec = pl.BlockSpec((1, d), lambda i: (0, 0))
    mod_gt, mod_sh, mod_sc = mod_specs
    return pl.pallas_call(
        functools.partial(_proj_kernel, tm=tm, slab=slab),
        grid=(m // tm,),
        in_specs=[row, pl.BlockSpec((d, d), lambda i: (0, 0)), row, mod_gt, vec, mod_sh, mod_sc, vec],
        out_specs=[row, row],
        out_shape=[jax.ShapeDtypeStruct((m, d), _F32), jax.ShapeDtypeStruct((m, d), _BF16)],
        compiler_params=_params(1),
        name="out_proj",
    )(mg, w_o, x, gt, gpost, sh, sc, gpre)


def _ffn_kernel(*refs, sample, tm, n_blk, tiles_per_seq, slab):
    if sample:
        (h_ref, wa_ref, wb_ref, cw_ref, wd_ref, x1_ref, gt_ref, gpost_ref, st_ref,
         out_ref, tail_ref, acc_scr) = refs
    else:
        (h_ref, wa_ref, wb_ref, cw_ref, wd_ref, x1_ref, gt_ref, gpost_ref,
         out_ref, tail_ref, acc_scr, carry_scr) = refs
    i = pl.program_id(0)
    j = pl.program_id(1)
    n_slab = tm // slab

    @pl.when(j == 0)
    def _():
        acc_scr[...] = jnp.zeros(acc_scr.shape, _F32)

    if not sample:
        @pl.when(i % tiles_per_seq == 0)
        def _():
            carry_scr[j] = jnp.zeros(carry_scr.shape[1:], _F32)

    a_hist = [] if not sample else [st_ref[0:slab, :], st_ref[slab:2 * slab, :]]
    prev = None if sample else carry_scr[j]
    for s in range(n_slab):
        r = slice(s * slab, (s + 1) * slab)
        h = h_ref[r, :]
        a = _dot(h, wa_ref[...])
        b = _dot(h, wb_ref[...])
        if sample:
            a_hist.append(a)
            ac = _causal_conv_slabs(a_hist[s:s + CONV_K], cw_ref)[0]
        else:
            ac = _causal_conv_rows(a, prev, cw_ref)
            prev = a[slab - SUBLANES:, :]
        g = (jax.nn.gelu(ac) * b).astype(_BF16)
        acc_scr[r, :] += _dot(g, wd_ref[...])
    if sample:
        tail_ref[0:slab, :] = a_hist[-2]
        tail_ref[slab:, :] = a_hist[-1]
    else:
        carry_scr[j] = prev
        tail_ref[...] = prev

    @pl.when(j == n_blk - 1)
    def _():
        for s in range(tm // slab):
            r = slice(s * slab, (s + 1) * slab)
            out_ref[r, :] = x1_ref[r, :] + gt_ref[...] * _rms(acc_scr[r, :], gpost_ref[...])


def _ffn_call(h2, w_up, cw, w_down, x1, gt, gpost, state, *, sample, tm, tn, tiles_per_seq,
              slab, mod_spec):
    m, d = x1.shape
    f = cw.shape[1]
    n_blk = f // tn
    tail_rows = 2 * CHUNK if sample else SUBLANES
    row = pl.BlockSpec((tm, d), lambda i, j: (i, 0))
    in_specs = [
        row,
        pl.BlockSpec((d, tn), lambda i, j: (0, j)),
        pl.BlockSpec((d, tn), lambda i, j: (0, n_blk + j)),
        pl.BlockSpec((CONV_K, tn), lambda i, j: (0, j)),
        pl.BlockSpec((tn, d), lambda i, j: (j, 0)),
        row, mod_spec,
        pl.BlockSpec((1, d), lambda i, j: (0, 0)),
    ]
    args = [h2, w_up, w_up, cw, w_down, x1, gt, gpost]
    scratch = [pltpu.VMEM((tm, d), _F32)]
    if sample:
        in_specs.append(pl.BlockSpec((2 * CHUNK, tn), lambda i, j: (0, j)))
        args.append(state)
    else:
        scratch.append(pltpu.VMEM((n_blk, SUBLANES, tn), _F32))
    return pl.pallas_call(
        functools.partial(_ffn_kernel, sample=sample, tm=tm, n_blk=n_blk,
                          tiles_per_seq=tiles_per_seq, slab=slab),
        grid=(m // tm, n_blk),
        in_specs=in_specs,
        out_specs=[row, pl.BlockSpec((tail_rows, tn), lambda i, j: (i, j))],
        out_shape=[jax.ShapeDtypeStruct((m, d), _F32),
                   jax.ShapeDtypeStruct((m // tm * tail_rows, f), _F32)],
        scratch_shapes=scratch,
        compiler_params=_params(2),
        name="convffn_sample" if sample else "convffn_prompt",
    )(*args)


def _layer(x, mod, conv_b_state, conv_f_state, p, *, sample, seq_len, tm):
    m, d = x.shape
    tiles_per_seq = 1 if sample else seq_len // tm
    slab = CHUNK if sample else PROMPT_SLAB
    tn, tn_ffn = MIX_COL_TILE, FFN_COL_TILE

    def mod_spec(k, n_axes):
        if sample:
            blk = (mod.shape[0], d)
            return pl.BlockSpec(blk, (lambda i, j: (0, k)) if n_axes == 2 else (lambda i: (0, k)))
        blk = (None, 1, d)
        if n_axes == 2:
            return pl.BlockSpec(blk, lambda i, j: (i // tiles_per_seq, 0, k))
        return pl.BlockSpec(blk, lambda i: (i // tiles_per_seq, 0, k))

    w_in = p["w_in"]
    wa = d
    outs = _gmlp_call(x, mod, mod, p["g_pre_mix"], w_in, p["g_v"], p["ws"], p["bias"],
                      sample=sample, tm=tm, tn=tn, slab=slab,
                      mod_specs=(mod_spec(0, 2), mod_spec(1, 2)))
    ya, h1 = outs[0], outs[1]
    vn = outs[2] if sample else None
    yb, tail_b = _shortconv_call(h1, w_in, 2 * wa, p["conv_b_w"], conv_b_state, sample=sample,
                                 tm=tm, tn=tn, tiles_per_seq=tiles_per_seq, slab=slab)
    gate_col0 = 2 * wa + 3 * p["conv_b_w"].shape[1]
    mg = _merge_call(h1, ya, yb, w_in, gate_col0, p["w_out_a"], p["w_out_b"], tm=tm, tn=tn,
                     slab=slab)
    x1, h2 = _proj_call(mg, p["w_o"], x, mod, p["g_post_mix"], mod, mod, p["g_pre_ffn"],
                        tm=tm, slab=slab,
                        mod_specs=(mod_spec(2, 1), mod_spec(3, 1), mod_spec(4, 1)))
    out, tail_f = _ffn_call(h2, p["w_up"], p["conv_f_w"], p["w_down"], x1, mod, p["g_post_ffn"],
                            conv_f_state, sample=sample, tm=tm, tn=tn_ffn,
                            tiles_per_seq=tiles_per_seq, slab=slab, mod_spec=mod_spec(5, 2))
    return out, tail_b, tail_f, vn


def kernel(x_prompt, x_sample, c_prompt, c_sample, state_conv_b, state_conv_ffn, w_ada, b_ada, g_pre_mix, g_post_mix, w_in, g_v, w_s, b_s, conv_b_w, w_out_a, w_out_b, w_o, g_pre_ffn, g_post_ffn, w_up, conv_f_w, w_down):
    depth = w_in.shape[0]
    bp, seq, d = x_prompt.shape
    bs, tdec, _ = x_sample.shape
    n_groups = w_s.shape[1]
    tm = ROW_TILE

    xp = x_prompt.reshape(bp * seq, d)
    xs = jnp.transpose(x_sample, (1, 0, 2)).reshape(tdec * bs, d)
    pad = (-(bp + bs)) % SUBLANES
    c_all = jnp.concatenate([c_prompt, c_sample, jnp.zeros((pad, d), _F32)], axis=0)

    pb, sb, pf, sf, sv = [], [], [], [], []
    for l in range(depth):
        mod = _mod_call(c_all, w_ada[l], b_ada[l][None, :])
        mod_p = mod[:bp].reshape(bp, 1, N_MOD * d)
        mod_s = mod[bp:bp + bs]
        vec = lambda a: a[l][None, :]
        shared = {
            "w_in": w_in[l].astype(_BF16), "w_out_a": w_out_a[l].astype(_BF16),
            "w_out_b": w_out_b[l].astype(_BF16), "w_o": w_o[l].astype(_BF16),
            "w_up": w_up[l].astype(_BF16), "w_down": w_down[l].astype(_BF16),
            "g_pre_mix": vec(g_pre_mix), "g_post_mix": vec(g_post_mix), "g_v": vec(g_v),
            "g_pre_ffn": vec(g_pre_ffn), "g_post_ffn": vec(g_post_ffn),
            "conv_b_w": conv_b_w[l], "conv_f_w": conv_f_w[l],
        }
        bias_full = jnp.repeat(jnp.transpose(b_s[l]), GROUP, axis=1)
        p_prompt = dict(shared, ws=w_s[l], bias=bias_full)
        wvec = jnp.repeat(jnp.transpose(w_s[l][:, :tdec, :tdec], (1, 2, 0)).reshape(tdec * tdec, n_groups),
                          GROUP, axis=1)
        p_sample = dict(shared, ws=wvec, bias=bias_full[:tdec])

        xp, tb, tf, _ = _layer(xp, mod_p, None, None, p_prompt, sample=False, seq_len=seq, tm=tm)
        st_b = jnp.transpose(state_conv_b[l], (1, 0, 2)).reshape((CONV_K - 1) * bs, -1)
        st_f = jnp.transpose(state_conv_ffn[l], (1, 0, 2)).reshape((CONV_K - 1) * bs, -1)
        xs, sbt, sft, vn = _layer(xs, mod_s, st_b, st_f, p_sample, sample=True, seq_len=tdec,
                                  tm=tdec * bs)

        def prompt_tail(t):
            t = t.reshape(bp, seq // tm, SUBLANES, -1)
            return t[:, -1, SUBLANES - (CONV_K - 1):, :]

        def sample_rows(t, n):
            return jnp.transpose(t.reshape(n, bs, -1), (1, 0, 2))

        pb.append(prompt_tail(tb))
        pf.append(prompt_tail(tf))
        sb.append(sample_rows(sbt, CONV_K - 1))
        sf.append(sample_rows(sft, CONV_K - 1))
        sv.append(sample_rows(vn, tdec))

    y_prompt = xp.reshape(bp, seq, d)
    y_sample = jnp.transpose(xs.reshape(tdec, bs, d), (1, 0, 2))
    return (y_prompt, y_sample, jnp.stack(pb), jnp.stack(sb), jnp.stack(pf), jnp.stack(sf),
            jnp.stack(sv))
```

```python
import functools

import jax
import jax.numpy as jnp
from jax import lax
from jax.experimental import pallas as pl
from jax.experimental.pallas import tpu as pltpu

EPS = 1e-6
CHUNK = 128
GROUP = 128
CONV_K = 3
N_MOD = 6
SUBLANES = 8
VMEM_LIMIT_BYTES = 56 * 1024 * 1024
ROW_TILE = 512
PROMPT_SLAB = 256
MIX_COL_TILE = 1024
FFN_COL_TILE = 512
SAMPLE_COL_TILE = 256
SAMPLE_COL_TILE_WIDE = 512

_BF16 = jnp.bfloat16
_F32 = jnp.float32


def _dot(a, b):
    return jnp.dot(a, b, preferred_element_type=_F32)


def _params(n_axes):
    return pltpu.CompilerParams(
        dimension_semantics=("arbitrary",) * n_axes,
        vmem_limit_bytes=VMEM_LIMIT_BYTES,
    )


def _rms(xf, g):
    ms = jnp.mean(xf * xf, axis=-1, keepdims=True)
    return xf * lax.rsqrt(ms + EPS) * g


def _cast_emit(w_ref, wb_ref):
    w = w_ref[...].astype(_BF16)
    wb_ref[...] = w
    return w


def _causal_conv_rows(p, prev, cw_ref):
    rows = p.shape[0]
    row = lax.broadcasted_iota(jnp.int32, (rows, 1), 0)
    m1 = jnp.where(row == 0, prev[7:8, :], pltpu.roll(p, 1, 0))
    m2 = jnp.where(row == 0, prev[6:7, :], jnp.where(row == 1, prev[7:8, :], pltpu.roll(p, 2, 0)))
    return cw_ref[0:1, :] * m2 + cw_ref[1:2, :] * m1 + cw_ref[2:3, :] * p


def _causal_conv_slabs(x, st_ref, cw_ref, slab):
    w0, w1, w2 = cw_ref[0:1, :], cw_ref[1:2, :], cw_ref[2:3, :]
    seq = [st_ref[0:slab, :], st_ref[slab:2 * slab, :]]
    seq += [x[t * slab:(t + 1) * slab, :] for t in range(x.shape[0] // slab)]
    return jnp.concatenate(
        [w0 * seq[t] + w1 * seq[t + 1] + w2 * seq[t + 2] for t in range(len(seq) - 2)], axis=0)


def _slabs(tm, slab):
    return [slice(s * slab, (s + 1) * slab) for s in range(tm // slab)]


def _mod_kernel(c_ref, w_ref, b_ref, o_ref):
    c = c_ref[...]
    a = (c * jax.nn.sigmoid(c)).astype(_BF16)
    o_ref[...] = _dot(a, w_ref[...].astype(_BF16)) + b_ref[...]


def _mod_call(c_all, w_ada, b_ada, tn=1024):
    rows, d = c_all.shape
    n = w_ada.shape[1]
    return pl.pallas_call(
        _mod_kernel,
        grid=(n // tn,),
        in_specs=[
            pl.BlockSpec((rows, d), lambda j: (0, 0)),
            pl.BlockSpec((d, tn), lambda j: (0, j)),
            pl.BlockSpec((1, tn), lambda j: (0, j)),
        ],
        out_specs=pl.BlockSpec((rows, tn), lambda j: (0, j)),
        out_shape=jax.ShapeDtypeStruct((rows, n), _F32),
        compiler_params=_params(1),
        name="adaln_mod",
    )(c_all, w_ada, b_ada)


def _s_gmlp_kernel(x_ref, sh_ref, sc_ref, gpre_ref, wv_ref, wu_ref, gv_ref, ws_ref, bias_ref,
                   ya_ref, h_ref, vn_ref, wvb_ref, wub_ref, h_scr, v_scr, *, tm, tn, n_blk, slab):
    j = pl.program_id(0)
    d = n_blk * tn
    slabs = _slabs(tm, slab)
    n_slab = len(slabs)

    @pl.when(j == 0)
    def _():
        for r in slabs:
            h = _rms(x_ref[r, :], gpre_ref[...]) * (1.0 + sc_ref[...]) + sh_ref[...]
            h = h.astype(_BF16)
            h_scr[r, :] = h
            h_ref[r, :] = h

    @pl.when(j < n_blk)
    def _():
        v_scr[j] = _dot(h_scr[...], _cast_emit(wv_ref, wvb_ref))

    @pl.when(j == n_blk)
    def _():
        for r in slabs:
            ss = 0.0
            for k in range(n_blk):
                vk = v_scr[k, r, :]
                ss = ss + jnp.sum(vk * vk, axis=-1, keepdims=True)
            rs = lax.rsqrt(ss * (1.0 / d) + EPS)
            for k in range(n_blk):
                c = slice(k * tn, (k + 1) * tn)
                vn = v_scr[k, r, :] * rs * gv_ref[:, c]
                v_scr[k, r, :] = vn
                vn_ref[r, c] = vn
        for t in reversed(range(n_slab)):
            for k in range(n_blk):
                c = slice(k * tn, (k + 1) * tn)
                acc = ws_ref[t * n_slab:t * n_slab + 1, c] * v_scr[k, slabs[0], :]
                for s in range(1, t + 1):
                    acc = acc + ws_ref[t * n_slab + s:t * n_slab + s + 1, c] * v_scr[k, slabs[s], :]
                v_scr[k, slabs[t], :] = acc + bias_ref[t:t + 1, c]

    @pl.when(j >= n_blk)
    def _():
        u = _dot(h_scr[...], _cast_emit(wu_ref, wub_ref))
        ya_ref[...] = (u * v_scr[j - n_blk]).astype(_BF16)


def _s_gmlp_call(x, mod, gpre, w_in, gv, wvec, bvec, *, tn, slab):
    tm, d = x.shape
    n_blk = d // tn
    full = lambda a: pl.BlockSpec(a.shape, lambda j: (0,) * a.ndim)
    v_map = lambda j: (0, n_blk + jnp.minimum(j, n_blk - 1))
    u_map = lambda j: (0, jnp.maximum(j - n_blk, 0))
    return pl.pallas_call(
        functools.partial(_s_gmlp_kernel, tm=tm, tn=tn, n_blk=n_blk, slab=slab),
        grid=(2 * n_blk,),
        in_specs=[full(x),
                  pl.BlockSpec((slab, d), lambda j: (0, 0)), pl.BlockSpec((slab, d), lambda j: (0, 1)),
                  full(gpre),
                  pl.BlockSpec((d, tn), v_map), pl.BlockSpec((d, tn), u_map),
                  full(gv), full(wvec), full(bvec)],
        out_specs=[pl.BlockSpec((tm, tn), u_map),
                   pl.BlockSpec((tm, d), lambda j: (0, 0)), pl.BlockSpec((tm, d), lambda j: (0, 0)),
                   pl.BlockSpec((d, tn), lambda j: (0, jnp.minimum(j, n_blk - 1))),
                   pl.BlockSpec((d, tn), u_map)],
        out_shape=[jax.ShapeDtypeStruct((tm, d), _BF16), jax.ShapeDtypeStruct((tm, d), _BF16),
                   jax.ShapeDtypeStruct((tm, d), _F32),
                   jax.ShapeDtypeStruct((d, d), _BF16), jax.ShapeDtypeStruct((d, d), _BF16)],
        scratch_shapes=[pltpu.VMEM((tm, d), _BF16), pltpu.VMEM((n_blk, tm, tn), _F32)],
        compiler_params=_params(1),
        name="gmlp_sample",
    )(x, mod, mod, gpre, w_in, w_in, gv, wvec, bvec)


def _s_shortconv_kernel(h_ref, wbg_ref, wcg_ref, wxb_ref, cw_ref, st_ref,
                        yb_ref, tail_ref, wbgb_ref, wcgb_ref, wxbb_ref, *, slab):
    h = h_ref[...]
    bg = _dot(h, _cast_emit(wbg_ref, wbgb_ref))
    p = _dot(h, _cast_emit(wcg_ref, wcgb_ref)) * _dot(h, _cast_emit(wxb_ref, wxbb_ref))
    yb_ref[...] = (bg * _causal_conv_slabs(p, st_ref, cw_ref, slab)).astype(_BF16)
    tail_ref[...] = p[p.shape[0] - 2 * slab:, :]


def _s_shortconv_call(h, w_in, col0, cw, state, *, tn, slab):
    tm, d = h.shape
    w = cw.shape[1]
    n_blk = w // tn
    b0 = col0 // tn
    wspec = lambda off: pl.BlockSpec((d, tn), lambda j: (0, off + j))
    return pl.pallas_call(
        functools.partial(_s_shortconv_kernel, slab=slab),
        grid=(n_blk,),
        in_specs=[pl.BlockSpec((tm, d), lambda j: (0, 0)),
                  wspec(b0), wspec(b0 + n_blk), wspec(b0 + 2 * n_blk),
                  pl.BlockSpec((CONV_K, tn), lambda j: (0, j)),
                  pl.BlockSpec((2 * slab, tn), lambda j: (0, j))],
        out_specs=[pl.BlockSpec((tm, tn), lambda j: (0, j)),
                   pl.BlockSpec((2 * slab, tn), lambda j: (0, j)),
                   wspec(0), wspec(0), wspec(0)],
        out_shape=[jax.ShapeDtypeStruct((tm, w), _BF16), jax.ShapeDtypeStruct((2 * slab, w), _F32)]
        + [jax.ShapeDtypeStruct((d, w), _BF16)] * 3,
        compiler_params=_params(1),
        name="shortconv_sample",
    )(h, w_in, w_in, w_in, cw, state)


def _s_merge_kernel(h_ref, ya_ref, yb_ref, wga_ref, wgb_ref, woa_ref, wob_ref,
                    m_ref, wgab_ref, wgbb_ref, woab_ref, wobb_ref):
    h = h_ref[...]
    ga = jax.nn.sigmoid(_dot(h, _cast_emit(wga_ref, wgab_ref)))
    gb = jax.nn.sigmoid(_dot(h, _cast_emit(wgb_ref, wgbb_ref)))
    m = (ga * _dot(ya_ref[...], _cast_emit(woa_ref, woab_ref))
         + gb * _dot(yb_ref[...], _cast_emit(wob_ref, wobb_ref)))
    m_ref[...] = m.astype(_BF16)


def _s_merge_call(h, ya, yb, w_in, gate_col0, w_out_a, w_out_b, *, tn):
    tm, d = h.shape
    n_blk = d // tn
    b0 = gate_col0 // tn
    row = pl.BlockSpec((tm, d), lambda j: (0, 0))
    wspec = lambda off: pl.BlockSpec((d, tn), lambda j: (0, off + j))
    return pl.pallas_call(
        _s_merge_kernel,
        grid=(n_blk,),
        in_specs=[row, row, row, wspec(b0), wspec(b0 + n_blk), wspec(0), wspec(0)],
        out_specs=[pl.BlockSpec((tm, tn), lambda j: (0, j)),
                   wspec(0), wspec(0), wspec(0), wspec(0)],
        out_shape=[jax.ShapeDtypeStruct((tm, d), _BF16)] + [jax.ShapeDtypeStruct((d, d), _BF16)] * 4,
        compiler_params=_params(1),
        name="gated_merge_sample",
    )(h, ya, yb, w_in, w_in, w_out_a, w_out_b)


def _post_mix(y, x, gt, gpost, sh, sc, gpre):
    x1 = x + gt * _rms(y, gpost)
    h2 = _rms(x1, gpre) * (1.0 + sc) + sh
    return x1, h2.astype(_BF16)


def _s_proj_kernel(m_ref, wo_ref, x_ref, gt_ref, gpost_ref, sh_ref, sc_ref, gpre_ref,
                   x1_ref, h2_ref, wob_ref, y_scr, *, tm, tn, n_blk, slab):
    j = pl.program_id(0)
    y_scr[j] = _dot(m_ref[...], _cast_emit(wo_ref, wob_ref))

    @pl.when(j == n_blk - 1)
    def _():
        for r in _slabs(tm, slab):
            y = jnp.concatenate([y_scr[k, r, :] for k in range(n_blk)], axis=-1)
            x1, h2 = _post_mix(y, x_ref[r, :], gt_ref[...], gpost_ref[...], sh_ref[...],
                               sc_ref[...], gpre_ref[...])
            x1_ref[r, :] = x1
            h2_ref[r, :] = h2


def _s_proj_call(mg, w_o, x, mod, gpost, gpre, *, tn, slab):
    tm, d = x.shape
    n_blk = d // tn
    row = pl.BlockSpec((tm, d), lambda j: (0, 0))
    vec = pl.BlockSpec((1, d), lambda j: (0, 0))
    mspec = lambda k: pl.BlockSpec((slab, d), lambda j: (0, k))
    return pl.pallas_call(
        functools.partial(_s_proj_kernel, tm=tm, tn=tn, n_blk=n_blk, slab=slab),
        grid=(n_blk,),
        in_specs=[row, pl.BlockSpec((d, tn), lambda j: (0, j)), row, mspec(2), vec, mspec(3),
                  mspec(4), vec],
        out_specs=[row, row, pl.BlockSpec((d, tn), lambda j: (0, j))],
        out_shape=[jax.ShapeDtypeStruct((tm, d), _F32), jax.ShapeDtypeStruct((tm, d), _BF16),
                   jax.ShapeDtypeStruct((d, d), _BF16)],
        scratch_shapes=[pltpu.VMEM((n_blk, tm, tn), _F32)],
        compiler_params=_params(1),
        name="out_proj_sample",
    )(mg, w_o, x, mod, gpost, mod, mod, gpre)


def _s_ffn_kernel(h_ref, wa_ref, wb_ref, cw_ref, wd_ref, x1_ref, gt_ref, gpost_ref, st_ref,
                  out_ref, tail_ref, wab_ref, wbb_ref, wdb_ref, acc_scr, *, tm, n_blk, slab):
    j = pl.program_id(0)

    @pl.when(j == 0)
    def _():
        acc_scr[...] = jnp.zeros(acc_scr.shape, _F32)

    h = h_ref[...]
    a = _dot(h, _cast_emit(wa_ref, wab_ref))
    b = _dot(h, _cast_emit(wb_ref, wbb_ref))
    g = (jax.nn.gelu(_causal_conv_slabs(a, st_ref, cw_ref, slab)) * b).astype(_BF16)
    acc_scr[...] += _dot(g, _cast_emit(wd_ref, wdb_ref))
    tail_ref[...] = a[tm - 2 * slab:, :]

    @pl.when(j == n_blk - 1)
    def _():
        for r in _slabs(tm, slab):
            out_ref[r, :] = x1_ref[r, :] + gt_ref[...] * _rms(acc_scr[r, :], gpost_ref[...])


def _s_ffn_call(h2, w_up, cw, w_down, x1, mod, gpost, state, *, tn, slab):
    tm, d = x1.shape
    f = cw.shape[1]
    n_blk = f // tn
    row = pl.BlockSpec((tm, d), lambda j: (0, 0))
    a_spec = pl.BlockSpec((d, tn), lambda j: (0, j))
    b_spec = pl.BlockSpec((d, tn), lambda j: (0, n_blk + j))
    d_spec = pl.BlockSpec((tn, d), lambda j: (j, 0))
    return pl.pallas_call(
        functools.partial(_s_ffn_kernel, tm=tm, n_blk=n_blk, slab=slab),
        grid=(n_blk,),
        in_specs=[row, a_spec, b_spec, pl.BlockSpec((CONV_K, tn), lambda j: (0, j)), d_spec, row,
                  pl.BlockSpec((slab, d), lambda j: (0, 5)), pl.BlockSpec((1, d), lambda j: (0, 0)),
                  pl.BlockSpec((2 * slab, tn), lambda j: (0, j))],
        out_specs=[row, pl.BlockSpec((2 * slab, tn), lambda j: (0, j)), a_spec, a_spec, d_spec],
        out_shape=[jax.ShapeDtypeStruct((tm, d), _F32), jax.ShapeDtypeStruct((2 * slab, f), _F32),
                   jax.ShapeDtypeStruct((d, f), _BF16), jax.ShapeDtypeStruct((d, f), _BF16),
                   jax.ShapeDtypeStruct((f, d), _BF16)],
        scratch_shapes=[pltpu.VMEM((tm, d), _F32)],
        compiler_params=_params(1),
        name="convffn_sample",
    )(h2, w_up, w_up, cw, w_down, x1, mod, gpost, state)


def _p_gmlp_kernel(x_ref, sh_ref, sc_ref, gpre_ref, wv_ref, wu_ref, gv_ref, ws_ref, bias_ref,
                   ya_ref, h_ref, h_scr, v_scr, wt_scr, *, tm, tn, n_blk, slab):
    i = pl.program_id(0)
    j = pl.program_id(1)
    d = n_blk * tn
    slabs = _slabs(tm, slab)

    @pl.when((i == 0) & (j == 0))
    def _():
        tril = (lax.broadcasted_iota(jnp.int32, (CHUNK, CHUNK), 0)
                >= lax.broadcasted_iota(jnp.int32, (CHUNK, CHUNK), 1))
        for g in range(d // GROUP):
            wt_scr[g] = jnp.where(tril, ws_ref[g], 0.0).astype(_BF16)

    @pl.when(j == 0)
    def _():
        for r in slabs:
            h = _rms(x_ref[r, :], gpre_ref[...]) * (1.0 + sc_ref[...]) + sh_ref[...]
            h = h.astype(_BF16)
            h_scr[r, :] = h
            h_ref[r, :] = h
            v_scr[0, r, :] = _dot(h, wv_ref[...])

    @pl.when((j > 0) & (j < n_blk))
    def _():
        for r in slabs:
            v_scr[j, r, :] = _dot(h_scr[r, :], wv_ref[...])

    def _gate():
        gpb = tn // GROUP
        for c in range(tm // CHUNK):
            r = slice(c * CHUNK, (c + 1) * CHUNK)
            ss = 0.0
            for k in range(n_blk):
                vk = v_scr[k, r, :]
                ss = ss + jnp.sum(vk * vk, axis=-1, keepdims=True)
            rs = lax.rsqrt(ss * (1.0 / d) + EPS)
            for k in range(n_blk):
                vb = (v_scr[k, r, :] * rs * gv_ref[:, k * tn:(k + 1) * tn]).astype(_BF16)
                for gg in range(gpb):
                    g = k * gpb + gg
                    lanes = slice(gg * GROUP, (gg + 1) * GROUP)
                    v_scr[k, r, lanes] = (_dot(wt_scr[g], vb[:, lanes])
                                          + bias_ref[:, g * GROUP:(g + 1) * GROUP])

    @pl.when(j == n_blk)
    def _():
        u = [_dot(h_scr[r, :], wu_ref[...]) for r in slabs]
        _gate()
        for r, ur in zip(slabs, u):
            ya_ref[r, :] = (ur * v_scr[0, r, :]).astype(_BF16)

    @pl.when(j > n_blk)
    def _():
        for r in slabs:
            ya_ref[r, :] = (_dot(h_scr[r, :], wu_ref[...]) * v_scr[j - n_blk, r, :]).astype(_BF16)


def _p_gmlp_call(x, mod, gpre, w_v, w_u, gv, ws, bias, *, tm, tn, slab, tiles_per_seq):
    m, d = x.shape
    n_blk = d // tn
    full = lambda a: pl.BlockSpec(a.shape, lambda i, j: (0,) * a.ndim)
    mspec = lambda k: pl.BlockSpec((None, 1, d), lambda i, j: (i // tiles_per_seq, 0, k))
    u_map = lambda i, j: (i, jnp.maximum(j - n_blk, 0))
    return pl.pallas_call(
        functools.partial(_p_gmlp_kernel, tm=tm, tn=tn, n_blk=n_blk, slab=slab),
        grid=(m // tm, 2 * n_blk),
        in_specs=[pl.BlockSpec((tm, d), lambda i, j: (i, 0)), mspec(0), mspec(1), full(gpre),
                  pl.BlockSpec((d, tn), lambda i, j: (0, jnp.minimum(j, n_blk - 1))),
                  pl.BlockSpec((d, tn), lambda i, j: (0, jnp.maximum(j - n_blk, 0))),
                  full(gv), full(ws), full(bias)],
        out_specs=[pl.BlockSpec((tm, tn), u_map), pl.BlockSpec((tm, d), lambda i, j: (i, 0))],
        out_shape=[jax.ShapeDtypeStruct((m, d), _BF16), jax.ShapeDtypeStruct((m, d), _BF16)],
        scratch_shapes=[pltpu.VMEM((tm, d), _BF16), pltpu.VMEM((n_blk, tm, tn), _F32),
                        pltpu.VMEM((d // GROUP, CHUNK, CHUNK), _BF16)],
        compiler_params=_params(2),
        name="gmlp_prompt",
    )(x, mod, mod, gpre, w_v, w_u, gv, ws, bias)


def _p_shortconv_kernel(h_ref, wbg_ref, wcg_ref, wxb_ref, cw_ref, yb_ref, tail_ref, carry_scr,
                        *, tm, tiles_per_seq, slab):
    i = pl.program_id(0)
    j = pl.program_id(1)

    @pl.when(i % tiles_per_seq == 0)
    def _():
        carry_scr[j] = jnp.zeros(carry_scr.shape[1:], _F32)

    prev = carry_scr[j]
    for r in _slabs(tm, slab):
        h = h_ref[r, :]
        bg = _dot(h, wbg_ref[...])
        p = _dot(h, wcg_ref[...]) * _dot(h, wxb_ref[...])
        yb_ref[r, :] = (bg * _causal_conv_rows(p, prev, cw_ref)).astype(_BF16)
        prev = p[slab - SUBLANES:, :]
    carry_scr[j] = prev
    tail_ref[...] = prev


def _p_shortconv_call(h, w_bg, w_cg, w_xb, cw, *, tm, tn, tiles_per_seq, slab):
    m, d = h.shape
    w = cw.shape[1]
    n_blk = w // tn
    wspec = lambda off: pl.BlockSpec((d, tn), lambda i, j: (0, off + j))
    return pl.pallas_call(
        functools.partial(_p_shortconv_kernel, tm=tm, tiles_per_seq=tiles_per_seq, slab=slab),
        grid=(m // tm, n_blk),
        in_specs=[pl.BlockSpec((tm, d), lambda i, j: (i, 0)),
                  wspec(0), wspec(0), wspec(0),
                  pl.BlockSpec((CONV_K, tn), lambda i, j: (0, j))],
        out_specs=[pl.BlockSpec((tm, tn), lambda i, j: (i, j)),
                   pl.BlockSpec((SUBLANES, tn), lambda i, j: (i, j))],
        out_shape=[jax.ShapeDtypeStruct((m, w), _BF16),
                   jax.ShapeDtypeStruct((m // tm * SUBLANES, w), _F32)],
        scratch_shapes=[pltpu.VMEM((n_blk, SUBLANES, tn), _F32)],
        compiler_params=_params(2),
        name="shortconv_prompt",
    )(h, w_bg, w_cg, w_xb, cw)


def _p_merge_kernel(h_ref, ya_ref, yb_ref, wga_ref, wgb_ref, woa_ref, wob_ref, m_ref, *, tm, slab):
    for r in _slabs(tm, slab):
        h = h_ref[r, :]
        ga = jax.nn.sigmoid(_dot(h, wga_ref[...]))
        gb = jax.nn.sigmoid(_dot(h, wgb_ref[...]))
        m = ga * _dot(ya_ref[r, :], woa_ref[...]) + gb * _dot(yb_ref[r, :], wob_ref[...])
        m_ref[r, :] = m.astype(_BF16)


def _p_merge_call(h, ya, yb, w_ga, w_gb, w_out_a, w_out_b, *, tm, tn, slab):
    m, d = h.shape
    n_blk = d // tn
    row = pl.BlockSpec((tm, d), lambda i, j: (i, 0))
    wspec = lambda off: pl.BlockSpec((d, tn), lambda i, j: (0, off + j))
    return pl.pallas_call(
        functools.partial(_p_merge_kernel, tm=tm, slab=slab),
        grid=(m // tm, n_blk),
        in_specs=[row, row, row, wspec(0), wspec(0), wspec(0), wspec(0)],
        out_specs=pl.BlockSpec((tm, tn), lambda i, j: (i, j)),
        out_shape=jax.ShapeDtypeStruct((m, d), _BF16),
        compiler_params=_params(2),
        name="gated_merge_prompt",
    )(h, ya, yb, w_ga, w_gb, w_out_a, w_out_b)


def _p_proj_kernel(m_ref, wo_ref, x_ref, gt_ref, gpost_ref, sh_ref, sc_ref, gpre_ref,
                   x1_ref, h2_ref, *, tm, slab):
    for r in _slabs(tm, slab):
        y = _dot(m_ref[r, :], wo_ref[...])
        x1, h2 = _post_mix(y, x_ref[r, :], gt_ref[...], gpost_ref[...], sh_ref[...], sc_ref[...],
                           gpre_ref[...])
        x1_ref[r, :] = x1
        h2_ref[r, :] = h2


def _p_proj_call(mg, w_o, x, mod, gpost, gpre, *, tm, slab, tiles_per_seq):
    m, d = x.shape
    row = pl.BlockSpec((tm, d), lambda i: (i, 0))
    vec = pl.BlockSpec((1, d), lambda i: (0, 0))
    mspec = lambda k: pl.BlockSpec((None, 1, d), lambda i: (i // tiles_per_seq, 0, k))
    return pl.pallas_call(
        functools.partial(_p_proj_kernel, tm=tm, slab=slab),
        grid=(m // tm,),
        in_specs=[row, pl.BlockSpec((d, d), lambda i: (0, 0)), row, mspec(2), vec, mspec(3),
                  mspec(4), vec],
        out_specs=[row, row],
        out_shape=[jax.ShapeDtypeStruct((m, d), _F32), jax.ShapeDtypeStruct((m, d), _BF16)],
        compiler_params=_params(1),
        name="out_proj_prompt",
    )(mg, w_o, x, mod, gpost, mod, mod, gpre)


def _p_ffn_kernel(h_ref, wa_ref, wb_ref, cw_ref, wd_ref, x1_ref, gt_ref, gpost_ref,
                  out_ref, tail_ref, acc_scr, carry_scr, *, tm, n_blk, tiles_per_seq, slab):
    i = pl.program_id(0)
    j = pl.program_id(1)
    slabs = _slabs(tm, slab)

    @pl.when(j == 0)
    def _():
        acc_scr[...] = jnp.zeros(acc_scr.shape, _F32)

    @pl.when(i % tiles_per_seq == 0)
    def _():
        carry_scr[j] = jnp.zeros(carry_scr.shape[1:], _F32)

    prev = carry_scr[j]
    for r in slabs:
        h = h_ref[r, :]
        a = _dot(h, wa_ref[...])
        b = _dot(h, wb_ref[...])
        g = (jax.nn.gelu(_causal_conv_rows(a, prev, cw_ref)) * b).astype(_BF16)
        prev = a[slab - SUBLANES:, :]
        acc_scr[r, :] += _dot(g, wd_ref[...])
    carry_scr[j] = prev
    tail_ref[...] = prev

    @pl.when(j == n_blk - 1)
    def _():
        for r in slabs:
            out_ref[r, :] = x1_ref[r, :] + gt_ref[...] * _rms(acc_scr[r, :], gpost_ref[...])


def _p_ffn_call(h2, w_a, w_b, cw, w_down, x1, mod, gpost, *, tm, tn, tiles_per_seq, slab):
    m, d = x1.shape
    f = cw.shape[1]
    n_blk = f // tn
    row = pl.BlockSpec((tm, d), lambda i, j: (i, 0))
    return pl.pallas_call(
        functools.partial(_p_ffn_kernel, tm=tm, n_blk=n_blk, tiles_per_seq=tiles_per_seq, slab=slab),
        grid=(m // tm, n_blk),
        in_specs=[row,
                  pl.BlockSpec((d, tn), lambda i, j: (0, j)),
                  pl.BlockSpec((d, tn), lambda i, j: (0, j)),
                  pl.BlockSpec((CONV_K, tn), lambda i, j: (0, j)),
                  pl.BlockSpec((tn, d), lambda i, j: (j, 0)),
                  row,
                  pl.BlockSpec((None, 1, d), lambda i, j: (i // tiles_per_seq, 0, 5)),
                  pl.BlockSpec((1, d), lambda i, j: (0, 0))],
        out_specs=[row, pl.BlockSpec((SUBLANES, tn), lambda i, j: (i, j))],
        out_shape=[jax.ShapeDtypeStruct((m, d), _F32),
                   jax.ShapeDtypeStruct((m // tm * SUBLANES, f), _F32)],
        scratch_shapes=[pltpu.VMEM((tm, d), _F32), pltpu.VMEM((n_blk, SUBLANES, tn), _F32)],
        compiler_params=_params(2),
        name="convffn_prompt",
    )(h2, w_a, w_b, cw, w_down, x1, mod, gpost)


def _sample_layer(x, mod, st_b, st_f, p):
    d = x.shape[1]
    slab = mod.shape[0]
    tn, tn_wide = SAMPLE_COL_TILE, SAMPLE_COL_TILE_WIDE
    w_in = p["w_in"]
    wb = p["conv_b_w"].shape[1]
    ya, h1, vn, w_v, w_u = _s_gmlp_call(x, mod, p["g_pre_mix"], w_in, p["g_v"], p["wvec"],
                                        p["bvec"], tn=tn, slab=slab)
    yb, tail_b, w_bg, w_cg, w_xb = _s_shortconv_call(h1, w_in, 2 * d, p["conv_b_w"], st_b,
                                                     tn=tn_wide, slab=slab)
    mg, w_ga, w_gb, w_oa, w_ob = _s_merge_call(h1, ya, yb, w_in, 2 * d + 3 * wb, p["w_out_a"],
                                               p["w_out_b"], tn=tn)
    x1, h2, w_o = _s_proj_call(mg, p["w_o"], x, mod, p["g_post_mix"], p["g_pre_ffn"],
                               tn=tn_wide, slab=slab)
    out, tail_f, w_a, w_b, w_d = _s_ffn_call(h2, p["w_up"], p["conv_f_w"], p["w_down"], x1, mod,
                                             p["g_post_ffn"], st_f, tn=tn, slab=slab)
    bf16_weights = dict(w_v=w_v, w_u=w_u, w_bg=w_bg, w_cg=w_cg, w_xb=w_xb, w_ga=w_ga, w_gb=w_gb,
                        w_out_a=w_oa, w_out_b=w_ob, w_o=w_o, w_a=w_a, w_b=w_b, w_down=w_d)
    return out, tail_b, tail_f, vn, bf16_weights


def _prompt_layer(x, mod, p, w, *, seq_len):
    tm, slab, tn = ROW_TILE, PROMPT_SLAB, MIX_COL_TILE
    tps = seq_len // tm
    ya, h1 = _p_gmlp_call(x, mod, p["g_pre_mix"], w["w_v"], w["w_u"], p["g_v"], p["w_s"], p["bias"],
                          tm=tm, tn=tn, slab=slab, tiles_per_seq=tps)
    yb, tail_b = _p_shortconv_call(h1, w["w_bg"], w["w_cg"], w["w_xb"], p["conv_b_w"],
                                   tm=tm, tn=tn, tiles_per_seq=tps, slab=slab)
    mg = _p_merge_call(h1, ya, yb, w["w_ga"], w["w_gb"], w["w_out_a"], w["w_out_b"],
                       tm=tm, tn=tn, slab=slab)
    x1, h2 = _p_proj_call(mg, w["w_o"], x, mod, p["g_post_mix"], p["g_pre_ffn"],
                          tm=tm, slab=slab, tiles_per_seq=tps)
    out, tail_f = _p_ffn_call(h2, w["w_a"], w["w_b"], p["conv_f_w"], w["w_down"], x1, mod,
                              p["g_post_ffn"], tm=tm, tn=FFN_COL_TILE, tiles_per_seq=tps, slab=slab)
    return out, tail_b, tail_f


def kernel(x_prompt, x_sample, c_prompt, c_sample, state_conv_b, state_conv_ffn, w_ada, b_ada, g_pre_mix, g_post_mix, w_in, g_v, w_s, b_s, conv_b_w, w_out_a, w_out_b, w_o, g_pre_ffn, g_post_ffn, w_up, conv_f_w, w_down):
    depth = w_in.shape[0]
    bp, seq, d = x_prompt.shape
    bs, tdec, _ = x_sample.shape
    n_groups = w_s.shape[1]
    assert bs == CHUNK and seq % ROW_TILE == 0 and tdec <= CHUNK

    xp = x_prompt.reshape(bp * seq, d)
    xs = jnp.transpose(x_sample, (1, 0, 2)).reshape(tdec * bs, d)
    pad = (-(bp + bs)) % SUBLANES
    c_all = jnp.concatenate([c_prompt, c_sample, jnp.zeros((pad, d), _F32)], axis=0)

    pb, sb, pf, sf, sv = [], [], [], [], []
    for l in range(depth):
        mod = _mod_call(c_all, w_ada[l], b_ada[l][None, :])
        mod_p = mod[:bp].reshape(bp, 1, N_MOD * d)
        mod_s = mod[bp:bp + bs]
        vec = lambda a: a[l][None, :]
        bias_full = jnp.repeat(jnp.transpose(b_s[l]), GROUP, axis=1)
        wvec = jnp.repeat(
            jnp.transpose(w_s[l][:, :tdec, :tdec], (1, 2, 0)).reshape(tdec * tdec, n_groups),
            GROUP, axis=1)
        p = {
            "w_in": w_in[l], "w_out_a": w_out_a[l], "w_out_b": w_out_b[l], "w_o": w_o[l],
            "w_up": w_up[l], "w_down": w_down[l],
            "g_pre_mix": vec(g_pre_mix), "g_post_mix": vec(g_post_mix), "g_v": vec(g_v),
            "g_pre_ffn": vec(g_pre_ffn), "g_post_ffn": vec(g_post_ffn),
            "conv_b_w": conv_b_w[l], "conv_f_w": conv_f_w[l],
            "w_s": w_s[l], "bias": bias_full, "wvec": wvec, "bvec": bias_full[:tdec],
        }
        st_b = jnp.transpose(state_conv_b[l], (1, 0, 2)).reshape((CONV_K - 1) * bs, -1)
        st_f = jnp.transpose(state_conv_ffn[l], (1, 0, 2)).reshape((CONV_K - 1) * bs, -1)
        xs, sbt, sft, vn, w_bf16 = _sample_layer(xs, mod_s, st_b, st_f, p)
        xp, tb, tf = _prompt_layer(xp, mod_p, p, w_bf16, seq_len=seq)

        def prompt_tail(t):
            t = t.reshape(bp, seq // ROW_TILE, SUBLANES, -1)
            return t[:, -1, SUBLANES - (CONV_K - 1):, :]

        def sample_rows(t, n):
            return jnp.transpose(t.reshape(n, bs, -1), (1, 0, 2))

        pb.append(prompt_tail(tb))
        pf.append(prompt_tail(tf))
        sb.append(sample_rows(sbt, CONV_K - 1))
        sf.append(sample_rows(sft, CONV_K - 1))
        sv.append(sample_rows(vn, tdec))

    y_prompt = xp.reshape(bp, seq, d)
    y_sample = jnp.transpose(xs.reshape(tdec, bs, d), (1, 0, 2))
    return (y_prompt, y_sample, jnp.stack(pb), jnp.stack(sb), jnp.stack(pf), jnp.stack(sf),
            jnp.stack(sv))
```

```python
import functools

import jax
import jax.numpy as jnp
from jax import lax
from jax.experimental import pallas as pl
from jax.experimental.pallas import tpu as pltpu

EPS = 1e-6
CHUNK = 128
GROUP = 128
CONV_K = 3
N_MOD = 6
SUBLANES = 8
VMEM_LIMIT_BYTES = 56 * 1024 * 1024
ROW_TILE = 512
FFN_ROW_TILE = 1024
PROMPT_SLAB = 256
MIX_COL_TILE = 1024
FFN_COL_TILE = 512
SAMPLE_COL_TILE = 256
SAMPLE_COL_TILE_WIDE = 512

_BF16 = jnp.bfloat16
_F32 = jnp.float32


def _dot(a, b):
    return jnp.dot(a, b, preferred_element_type=_F32)


def _params(n_axes):
    return pltpu.CompilerParams(
        dimension_semantics=("arbitrary",) * n_axes,
        vmem_limit_bytes=VMEM_LIMIT_BYTES,
    )


def _rms(xf, g):
    ms = jnp.mean(xf * xf, axis=-1, keepdims=True)
    return xf * lax.rsqrt(ms + EPS) * g


def _cast_emit(w_ref, wb_ref):
    w = w_ref[...].astype(_BF16)
    wb_ref[...] = w
    return w


def _causal_conv_rows(p, prev, cw_ref):
    rows = p.shape[0]
    row = lax.broadcasted_iota(jnp.int32, (rows, 1), 0)
    m1 = jnp.where(row == 0, prev[7:8, :], pltpu.roll(p, 1, 0))
    m2 = jnp.where(row == 0, prev[6:7, :], jnp.where(row == 1, prev[7:8, :], pltpu.roll(p, 2, 0)))
    return cw_ref[0:1, :] * m2 + cw_ref[1:2, :] * m1 + cw_ref[2:3, :] * p


def _causal_conv_slabs(x, prev, cw_ref, slab):
    w0, w1, w2 = cw_ref[0:1, :], cw_ref[1:2, :], cw_ref[2:3, :]
    seq = list(prev) + [x[t * slab:(t + 1) * slab, :] for t in range(x.shape[0] // slab)]
    y = [w0 * seq[t] + w1 * seq[t + 1] + w2 * seq[t + 2] for t in range(len(seq) - 2)]
    return jnp.concatenate(y, axis=0), seq[-2:]


def _slabs(tm, slab):
    return [slice(s * slab, (s + 1) * slab) for s in range(tm // slab)]


def _mod_kernel(c_ref, w_ref, b_ref, o_ref):
    c = c_ref[...]
    a = (c * jax.nn.sigmoid(c)).astype(_BF16)
    o_ref[...] = _dot(a, w_ref[...].astype(_BF16)) + b_ref[...]


def _mod_call(c_all, w_ada, b_ada, tn=1024):
    rows, d = c_all.shape
    n = w_ada.shape[1]
    return pl.pallas_call(
        _mod_kernel,
        grid=(n // tn,),
        in_specs=[
            pl.BlockSpec((rows, d), lambda j: (0, 0)),
            pl.BlockSpec((d, tn), lambda j: (0, j)),
            pl.BlockSpec((1, tn), lambda j: (0, j)),
        ],
        out_specs=pl.BlockSpec((rows, tn), lambda j: (0, j)),
        out_shape=jax.ShapeDtypeStruct((rows, n), _F32),
        compiler_params=_params(1),
        name="adaln_mod",
    )(c_all, w_ada, b_ada)


def _s_gmlp_kernel(x_ref, sh_ref, sc_ref, gpre_ref, wv_ref, wu_ref, gv_ref, ws_ref, bias_ref,
                   ya_ref, h_ref, vn_ref, wvb_ref, wub_ref, h_scr, v_scr, *, tm, tn, n_blk, slab):
    j = pl.program_id(0)
    d = n_blk * tn
    slabs = _slabs(tm, slab)
    n_slab = len(slabs)

    @pl.when(j == 0)
    def _():
        for t, r in enumerate(slabs):
            h = (_rms(x_ref[:, t * d:(t + 1) * d], gpre_ref[...]) * (1.0 + sc_ref[...])
                 + sh_ref[...]).astype(_BF16)
            h_scr[r, :] = h
            h_ref[r, :] = h

    @pl.when(j < n_blk)
    def _():
        v_scr[j] = _dot(h_scr[...], _cast_emit(wv_ref, wvb_ref))

    @pl.when(j == n_blk)
    def _():
        for t, r in enumerate(slabs):
            ss = 0.0
            for k in range(n_blk):
                vk = v_scr[k, r, :]
                ss = ss + jnp.sum(vk * vk, axis=-1, keepdims=True)
            rs = lax.rsqrt(ss * (1.0 / d) + EPS)
            for k in range(n_blk):
                vn = v_scr[k, r, :] * rs * gv_ref[:, k * tn:(k + 1) * tn]
                v_scr[k, r, :] = vn
                vn_ref[:, t * d + k * tn:t * d + (k + 1) * tn] = vn
        for t in reversed(range(n_slab)):
            for k in range(n_blk):
                c = slice(k * tn, (k + 1) * tn)
                acc = ws_ref[t * n_slab:t * n_slab + 1, c] * v_scr[k, slabs[0], :]
                for s in range(1, t + 1):
                    acc = acc + ws_ref[t * n_slab + s:t * n_slab + s + 1, c] * v_scr[k, slabs[s], :]
                v_scr[k, slabs[t], :] = acc + bias_ref[t:t + 1, c]

    @pl.when(j >= n_blk)
    def _():
        u = _dot(h_scr[...], _cast_emit(wu_ref, wub_ref))
        ya_ref[...] = (u * v_scr[j - n_blk]).astype(_BF16)


def _s_gmlp_call(x, mod, gpre, w_in, gv, wvec, bvec, *, tn):
    slab = x.shape[0]
    d = gpre.shape[1]
    tm = x.shape[1] // d * slab
    n_blk = d // tn
    full = lambda a: pl.BlockSpec(a.shape, lambda j: (0,) * a.ndim)
    v_map = lambda j: (0, n_blk + jnp.minimum(j, n_blk - 1))
    u_map = lambda j: (0, jnp.maximum(j - n_blk, 0))
    return pl.pallas_call(
        functools.partial(_s_gmlp_kernel, tm=tm, tn=tn, n_blk=n_blk, slab=slab),
        grid=(2 * n_blk,),
        in_specs=[full(x),
                  pl.BlockSpec((slab, d), lambda j: (0, 0)), pl.BlockSpec((slab, d), lambda j: (0, 1)),
                  full(gpre),
                  pl.BlockSpec((d, tn), v_map), pl.BlockSpec((d, tn), u_map),
                  full(gv), full(wvec), full(bvec)],
        out_specs=[pl.BlockSpec((tm, tn), u_map),
                   pl.BlockSpec((tm, d), lambda j: (0, 0)), full(x),
                   pl.BlockSpec((d, tn), lambda j: (0, jnp.minimum(j, n_blk - 1))),
                   pl.BlockSpec((d, tn), u_map)],
        out_shape=[jax.ShapeDtypeStruct((tm, d), _BF16), jax.ShapeDtypeStruct((tm, d), _BF16),
                   jax.ShapeDtypeStruct(x.shape, _F32),
                   jax.ShapeDtypeStruct((d, d), _BF16), jax.ShapeDtypeStruct((d, d), _BF16)],
        scratch_shapes=[pltpu.VMEM((tm, d), _BF16), pltpu.VMEM((n_blk, tm, tn), _F32)],
        compiler_params=_params(1),
        name="gmlp_sample",
    )(x, mod, mod, gpre, w_in, w_in, gv, wvec, bvec)


def _s_shortconv_kernel(h_ref, wbg_ref, wcg_ref, wxb_ref, cw_ref, st0_ref, st1_ref,
                        yb_ref, t0_ref, t1_ref, wbgb_ref, wcgb_ref, wxbb_ref, *, slab):
    h = h_ref[...]
    bg = _dot(h, _cast_emit(wbg_ref, wbgb_ref))
    p = _dot(h, _cast_emit(wcg_ref, wcgb_ref)) * _dot(h, _cast_emit(wxb_ref, wxbb_ref))
    cb, tail = _causal_conv_slabs(p, [st0_ref[...], st1_ref[...]], cw_ref, slab)
    yb_ref[...] = (bg * cb).astype(_BF16)
    t0_ref[...] = tail[0]
    t1_ref[...] = tail[1]


def _s_shortconv_call(h, w_in, col0, cw, state, *, tn):
    tm, d = h.shape
    slab = state.shape[0]
    w = cw.shape[1]
    n_blk = w // tn
    b0 = col0 // tn
    wspec = lambda off: pl.BlockSpec((d, tn), lambda j: (0, off + j))
    sspec = lambda off: pl.BlockSpec((slab, tn), lambda j: (0, off + j))
    return pl.pallas_call(
        functools.partial(_s_shortconv_kernel, slab=slab),
        grid=(n_blk,),
        in_specs=[pl.BlockSpec((tm, d), lambda j: (0, 0)),
                  wspec(b0), wspec(b0 + n_blk), wspec(b0 + 2 * n_blk),
                  pl.BlockSpec((CONV_K, tn), lambda j: (0, j)),
                  sspec(0), sspec(n_blk)],
        out_specs=[pl.BlockSpec((tm, tn), lambda j: (0, j)), sspec(0), sspec(0),
                   wspec(0), wspec(0), wspec(0)],
        out_shape=[jax.ShapeDtypeStruct((tm, w), _BF16)] + [jax.ShapeDtypeStruct((slab, w), _F32)] * 2
        + [jax.ShapeDtypeStruct((d, w), _BF16)] * 3,
        compiler_params=_params(1),
        name="shortconv_sample",
    )(h, w_in, w_in, w_in, cw, state, state)


def _s_merge_kernel(h_ref, ya_ref, yb_ref, wga_ref, wgb_ref, woa_ref, wob_ref,
                    m_ref, wgab_ref, wgbb_ref, woab_ref, wobb_ref):
    h = h_ref[...]
    ga = jax.nn.sigmoid(_dot(h, _cast_emit(wga_ref, wgab_ref)))
    gb = jax.nn.sigmoid(_dot(h, _cast_emit(wgb_ref, wgbb_ref)))
    m = (ga * _dot(ya_ref[...], _cast_emit(woa_ref, woab_ref))
         + gb * _dot(yb_ref[...], _cast_emit(wob_ref, wobb_ref)))
    m_ref[...] = m.astype(_BF16)


def _s_merge_call(h, ya, yb, w_in, gate_col0, w_out_a, w_out_b, *, tn):
    tm, d = h.shape
    n_blk = d // tn
    b0 = gate_col0 // tn
    row = pl.BlockSpec((tm, d), lambda j: (0, 0))
    wspec = lambda off: pl.BlockSpec((d, tn), lambda j: (0, off + j))
    return pl.pallas_call(
        _s_merge_kernel,
        grid=(n_blk,),
        in_specs=[row, row, row, wspec(b0), wspec(b0 + n_blk), wspec(0), wspec(0)],
        out_specs=[pl.BlockSpec((tm, tn), lambda j: (0, j)),
                   wspec(0), wspec(0), wspec(0), wspec(0)],
        out_shape=[jax.ShapeDtypeStruct((tm, d), _BF16)] + [jax.ShapeDtypeStruct((d, d), _BF16)] * 4,
        compiler_params=_params(1),
        name="gated_merge_sample",
    )(h, ya, yb, w_in, w_in, w_out_a, w_out_b)


def _post_mix(y, x, gt, gpost, sh, sc, gpre):
    x1 = x + gt * _rms(y, gpost)
    h2 = _rms(x1, gpre) * (1.0 + sc) + sh
    return x1, h2.astype(_BF16)


def _s_proj_kernel(m_ref, wo_ref, x_ref, gt_ref, gpost_ref, sh_ref, sc_ref, gpre_ref,
                   x1_ref, h2_ref, wob_ref, y_scr, *, tm, tn, n_blk, slab):
    j = pl.program_id(0)
    d = n_blk * tn
    y_scr[j] = _dot(m_ref[...], _cast_emit(wo_ref, wob_ref))

    @pl.when(j == n_blk - 1)
    def _():
        for t, r in enumerate(_slabs(tm, slab)):
            y = jnp.concatenate([y_scr[k, r, :] for k in range(n_blk)], axis=-1)
            x1, h2 = _post_mix(y, x_ref[:, t * d:(t + 1) * d], gt_ref[...], gpost_ref[...],
                               sh_ref[...], sc_ref[...], gpre_ref[...])
            x1_ref[r, :] = x1
            h2_ref[r, :] = h2


def _s_proj_call(mg, w_o, x, mod, gpost, gpre, *, tn):
    tm, d = mg.shape
    slab = x.shape[0]
    n_blk = d // tn
    row = pl.BlockSpec((tm, d), lambda j: (0, 0))
    vec = pl.BlockSpec((1, d), lambda j: (0, 0))
    mspec = lambda k: pl.BlockSpec((slab, d), lambda j: (0, k))
    return pl.pallas_call(
        functools.partial(_s_proj_kernel, tm=tm, tn=tn, n_blk=n_blk, slab=slab),
        grid=(n_blk,),
        in_specs=[row, pl.BlockSpec((d, tn), lambda j: (0, j)),
                  pl.BlockSpec(x.shape, lambda j: (0, 0)), mspec(2), vec, mspec(3), mspec(4), vec],
        out_specs=[row, row, pl.BlockSpec((d, tn), lambda j: (0, j))],
        out_shape=[jax.ShapeDtypeStruct((tm, d), _F32), jax.ShapeDtypeStruct((tm, d), _BF16),
                   jax.ShapeDtypeStruct((d, d), _BF16)],
        scratch_shapes=[pltpu.VMEM((n_blk, tm, tn), _F32)],
        compiler_params=_params(1),
        name="out_proj_sample",
    )(mg, w_o, x, mod, gpost, mod, mod, gpre)


def _s_ffn_kernel(h_ref, wa_ref, wb_ref, cw_ref, wd_ref, x1_ref, gt_ref, gpost_ref, st0_ref, st1_ref,
                  out_ref, t0_ref, t1_ref, wab_ref, wbb_ref, wdb_ref, acc_scr, *, tm, n_blk, slab):
    j = pl.program_id(0)
    d = acc_scr.shape[1]

    @pl.when(j == 0)
    def _():
        acc_scr[...] = jnp.zeros(acc_scr.shape, _F32)

    wa = _cast_emit(wa_ref, wab_ref)
    wb = _cast_emit(wb_ref, wbb_ref)
    wd = _cast_emit(wd_ref, wdb_ref)
    prev = [st0_ref[...], st1_ref[...]]
    for r in _slabs(tm, 2 * slab):
        h = h_ref[r, :]
        ac, prev = _causal_conv_slabs(_dot(h, wa), prev, cw_ref, slab)
        g = (jax.nn.gelu(ac) * _dot(h, wb)).astype(_BF16)
        acc_scr[r, :] += _dot(g, wd)
    t0_ref[...] = prev[0]
    t1_ref[...] = prev[1]

    @pl.when(j == n_blk - 1)
    def _():
        for t, r in enumerate(_slabs(tm, slab)):
            out_ref[:, t * d:(t + 1) * d] = (x1_ref[r, :]
                                             + gt_ref[...] * _rms(acc_scr[r, :], gpost_ref[...]))


def _s_ffn_call(h2, w_up, cw, w_down, x1, mod, gpost, state, *, tn):
    tm, d = x1.shape
    slab = state.shape[0]
    f = cw.shape[1]
    n_blk = f // tn
    row = pl.BlockSpec((tm, d), lambda j: (0, 0))
    a_spec = pl.BlockSpec((d, tn), lambda j: (0, j))
    b_spec = pl.BlockSpec((d, tn), lambda j: (0, n_blk + j))
    d_spec = pl.BlockSpec((tn, d), lambda j: (j, 0))
    sspec = lambda off: pl.BlockSpec((slab, tn), lambda j: (0, off + j))
    out = pl.BlockSpec((slab, tm // slab * d), lambda j: (0, 0))
    return pl.pallas_call(
        functools.partial(_s_ffn_kernel, tm=tm, n_blk=n_blk, slab=slab),
        grid=(n_blk,),
        in_specs=[row, a_spec, b_spec, pl.BlockSpec((CONV_K, tn), lambda j: (0, j)), d_spec, row,
                  pl.BlockSpec((slab, d), lambda j: (0, 5)), pl.BlockSpec((1, d), lambda j: (0, 0)),
                  sspec(0), sspec(n_blk)],
        out_specs=[out, sspec(0), sspec(0), a_spec, a_spec, d_spec],
        out_shape=[jax.ShapeDtypeStruct((slab, tm // slab * d), _F32)]
        + [jax.ShapeDtypeStruct((slab, f), _F32)] * 2
        + [jax.ShapeDtypeStruct((d, f), _BF16), jax.ShapeDtypeStruct((d, f), _BF16),
           jax.ShapeDtypeStruct((f, d), _BF16)],
        scratch_shapes=[pltpu.VMEM((tm, d), _F32)],
        compiler_params=_params(1),
        name="convffn_sample",
    )(h2, w_up, w_up, cw, w_down, x1, mod, gpost, state, state)


def _p_gmlp_kernel(x_ref, sh_ref, sc_ref, gpre_ref, wv_ref, wu_ref, gv_ref, ws_ref, bias_ref,
                   ya_ref, h_ref, h_scr, v_scr, wt_scr, *, tm, tn, n_blk, slab):
    i = pl.program_id(0)
    j = pl.program_id(1)
    d = n_blk * tn
    slabs = _slabs(tm, slab)

    @pl.when((i == 0) & (j == 0))
    def _():
        tril = (lax.broadcasted_iota(jnp.int32, (CHUNK, CHUNK), 0)
                >= lax.broadcasted_iota(jnp.int32, (CHUNK, CHUNK), 1))
        for g in range(d // GROUP):
            wt_scr[g] = jnp.where(tril, ws_ref[g], 0.0).astype(_BF16)

    @pl.when(j == 0)
    def _():
        for r in slabs:
            h = _rms(x_ref[r, :], gpre_ref[...]) * (1.0 + sc_ref[...]) + sh_ref[...]
            h = h.astype(_BF16)
            h_scr[r, :] = h
            h_ref[r, :] = h
            v_scr[0, r, :] = _dot(h, wv_ref[...])

    @pl.when((j > 0) & (j < n_blk))
    def _():
        for r in slabs:
            v_scr[j, r, :] = _dot(h_scr[r, :], wv_ref[...])

    def _gate():
        gpb = tn // GROUP
        for c in range(tm // CHUNK):
            r = slice(c * CHUNK, (c + 1) * CHUNK)
            ss = 0.0
            for k in range(n_blk):
                vk = v_scr[k, r, :]
                ss = ss + jnp.sum(vk * vk, axis=-1, keepdims=True)
            rs = lax.rsqrt(ss * (1.0 / d) + EPS)
            for k in range(n_blk):
                vb = (v_scr[k, r, :] * rs * gv_ref[:, k * tn:(k + 1) * tn]).astype(_BF16)
                for gg in range(gpb):
                    g = k * gpb + gg
                    lanes = slice(gg * GROUP, (gg + 1) * GROUP)
                    v_scr[k, r, lanes] = (_dot(wt_scr[g], vb[:, lanes])
                                          + bias_ref[:, g * GROUP:(g + 1) * GROUP])

    @pl.when(j == n_blk)
    def _():
        u = [_dot(h_scr[r, :], wu_ref[...]) for r in slabs]
        _gate()
        for r, ur in zip(slabs, u):
            ya_ref[r, :] = (ur * v_scr[0, r, :]).astype(_BF16)

    @pl.when(j > n_blk)
    def _():
        for r in slabs:
            ya_ref[r, :] = (_dot(h_scr[r, :], wu_ref[...]) * v_scr[j - n_blk, r, :]).astype(_BF16)


def _p_gmlp_call(x, mod, gpre, w_v, w_u, gv, ws, bias, *, tm, tn, slab, tiles_per_seq, mod_row0):
    m, d = x.shape
    n_blk = d // tn
    full = lambda a: pl.BlockSpec(a.shape, lambda i, j: (0,) * a.ndim)
    mspec = lambda k: pl.BlockSpec((None, 1, d), lambda i, j: (mod_row0 + i // tiles_per_seq, 0, k))
    u_map = lambda i, j: (i, jnp.maximum(j - n_blk, 0))
    return pl.pallas_call(
        functools.partial(_p_gmlp_kernel, tm=tm, tn=tn, n_blk=n_blk, slab=slab),
        grid=(m // tm, 2 * n_blk),
        in_specs=[pl.BlockSpec((tm, d), lambda i, j: (i, 0)), mspec(0), mspec(1), full(gpre),
                  pl.BlockSpec((d, tn), lambda i, j: (0, jnp.minimum(j, n_blk - 1))),
                  pl.BlockSpec((d, tn), lambda i, j: (0, jnp.maximum(j - n_blk, 0))),
                  full(gv), full(ws), full(bias)],
        out_specs=[pl.BlockSpec((tm, tn), u_map), pl.BlockSpec((tm, d), lambda i, j: (i, 0))],
        out_shape=[jax.ShapeDtypeStruct((m, d), _BF16), jax.ShapeDtypeStruct((m, d), _BF16)],
        scratch_shapes=[pltpu.VMEM((tm, d), _BF16), pltpu.VMEM((n_blk, tm, tn), _F32),
                        pltpu.VMEM((d // GROUP, CHUNK, CHUNK), _BF16)],
        compiler_params=_params(2),
        name="gmlp_prompt",
    )(x, mod, mod, gpre, w_v, w_u, gv, ws, bias)


def _p_shortconv_kernel(h_ref, wbg_ref, wcg_ref, wxb_ref, cw_ref, yb_ref, tail_ref, carry_scr,
                        *, tm, tiles_per_seq, slab):
    i = pl.program_id(0)
    j = pl.program_id(1)

    @pl.when(i % tiles_per_seq == 0)
    def _():
        carry_scr[j] = jnp.zeros(carry_scr.shape[1:], _F32)

    prev = carry_scr[j]
    for r in _slabs(tm, slab):
        h = h_ref[r, :]
        bg = _dot(h, wbg_ref[...])
        p = _dot(h, wcg_ref[...]) * _dot(h, wxb_ref[...])
        yb_ref[r, :] = (bg * _causal_conv_rows(p, prev, cw_ref)).astype(_BF16)
        prev = p[slab - SUBLANES:, :]
    carry_scr[j] = prev
    tail_ref[...] = prev


def _p_shortconv_call(h, w_bg, w_cg, w_xb, cw, *, tm, tn, tiles_per_seq, slab):
    m, d = h.shape
    w = cw.shape[1]
    n_blk = w // tn
    wspec = lambda off: pl.BlockSpec((d, tn), lambda i, j: (0, off + j))
    return pl.pallas_call(
        functools.partial(_p_shortconv_kernel, tm=tm, tiles_per_seq=tiles_per_seq, slab=slab),
        grid=(m // tm, n_blk),
        in_specs=[pl.BlockSpec((tm, d), lambda i, j: (i, 0)),
                  wspec(0), wspec(0), wspec(0),
                  pl.BlockSpec((CONV_K, tn), lambda i, j: (0, j))],
        out_specs=[pl.BlockSpec((tm, tn), lambda i, j: (i, j)),
                   pl.BlockSpec((SUBLANES, tn), lambda i, j: (i, j))],
        out_shape=[jax.ShapeDtypeStruct((m, w), _BF16),
                   jax.ShapeDtypeStruct((m // tm * SUBLANES, w), _F32)],
        scratch_shapes=[pltpu.VMEM((n_blk, SUBLANES, tn), _F32)],
        compiler_params=_params(2),
        name="shortconv_prompt",
    )(h, w_bg, w_cg, w_xb, cw)


def _p_merge_kernel(h_ref, ya_ref, yb_ref, wga_ref, wgb_ref, woa_ref, wob_ref, m_ref, *, tm, slab):
    for r in _slabs(tm, slab):
        h = h_ref[r, :]
        ga = jax.nn.sigmoid(_dot(h, wga_ref[...]))
        gb = jax.nn.sigmoid(_dot(h, wgb_ref[...]))
        m = ga * _dot(ya_ref[r, :], woa_ref[...]) + gb * _dot(yb_ref[r, :], wob_ref[...])
        m_ref[r, :] = m.astype(_BF16)


def _p_merge_call(h, ya, yb, w_ga, w_gb, w_out_a, w_out_b, *, tm, tn, slab):
    m, d = h.shape
    n_blk = d // tn
    row = pl.BlockSpec((tm, d), lambda i, j: (i, 0))
    wspec = lambda off: pl.BlockSpec((d, tn), lambda i, j: (0, off + j))
    return pl.pallas_call(
        functools.partial(_p_merge_kernel, tm=tm, slab=slab),
        grid=(m // tm, n_blk),
        in_specs=[row, row, row, wspec(0), wspec(0), wspec(0), wspec(0)],
        out_specs=pl.BlockSpec((tm, tn), lambda i, j: (i, j)),
        out_shape=jax.ShapeDtypeStruct((m, d), _BF16),
        compiler_params=_params(2),
        name="gated_merge_prompt",
    )(h, ya, yb, w_ga, w_gb, w_out_a, w_out_b)


def _p_proj_kernel(m_ref, wo_ref, x_ref, gt_ref, gpost_ref, sh_ref, sc_ref, gpre_ref,
                   x1_ref, h2_ref, *, tm, slab):
    for r in _slabs(tm, slab):
        y = _dot(m_ref[r, :], wo_ref[...])
        x1, h2 = _post_mix(y, x_ref[r, :], gt_ref[...], gpost_ref[...], sh_ref[...], sc_ref[...],
                           gpre_ref[...])
        x1_ref[r, :] = x1
        h2_ref[r, :] = h2


def _p_proj_call(mg, w_o, x, mod, gpost, gpre, *, tm, slab, tiles_per_seq, mod_row0):
    m, d = x.shape
    row = pl.BlockSpec((tm, d), lambda i: (i, 0))
    vec = pl.BlockSpec((1, d), lambda i: (0, 0))
    mspec = lambda k: pl.BlockSpec((None, 1, d), lambda i: (mod_row0 + i // tiles_per_seq, 0, k))
    return pl.pallas_call(
        functools.partial(_p_proj_kernel, tm=tm, slab=slab),
        grid=(m // tm,),
        in_specs=[row, pl.BlockSpec((d, d), lambda i: (0, 0)), row, mspec(2), vec, mspec(3),
                  mspec(4), vec],
        out_specs=[row, row],
        out_shape=[jax.ShapeDtypeStruct((m, d), _F32), jax.ShapeDtypeStruct((m, d), _BF16)],
        compiler_params=_params(1),
        name="out_proj_prompt",
    )(mg, w_o, x, mod, gpost, mod, mod, gpre)


def _p_ffn_kernel(h_ref, wa_ref, wb_ref, cw_ref, wd_ref, x1_hbm, gt_ref, gpost_ref,
                  out_ref, tail_ref, x1_buf, x1_sem, carry_scr, *, tm, n_blk, tiles_per_seq, slab):
    i = pl.program_id(0)
    j = pl.program_id(1)
    slabs = _slabs(tm, slab)
    x1_copy = pltpu.make_async_copy(x1_hbm.at[pl.ds(pl.multiple_of(i * tm, tm), tm), :],
                                    x1_buf, x1_sem)

    @pl.when(j == 0)
    def _():
        x1_copy.start()
        out_ref[...] = jnp.zeros(out_ref.shape, _F32)

    @pl.when(i % tiles_per_seq == 0)
    def _():
        carry_scr[j] = jnp.zeros(carry_scr.shape[1:], _F32)

    prev = carry_scr[j]
    for r in slabs:
        h = h_ref[r, :]
        a = _dot(h, wa_ref[...])
        b = _dot(h, wb_ref[...])
        g = (jax.nn.gelu(_causal_conv_rows(a, prev, cw_ref)) * b).astype(_BF16)
        prev = a[slab - SUBLANES:, :]
        out_ref[r, :] += _dot(g, wd_ref[...])
    carry_scr[j] = prev
    tail_ref[...] = prev

    @pl.when(j == n_blk - 1)
    def _():
        x1_copy.wait()
        for r in slabs:
            out_ref[r, :] = x1_buf[r, :] + gt_ref[...] * _rms(out_ref[r, :], gpost_ref[...])


def _p_ffn_call(h2, w_a, w_b, cw, w_down, x1, mod, gpost, *, tm, tn, tiles_per_seq, slab, mod_row0):
    m, d = x1.shape
    f = cw.shape[1]
    n_blk = f // tn
    row = pl.BlockSpec((tm, d), lambda i, j: (i, 0))
    return pl.pallas_call(
        functools.partial(_p_ffn_kernel, tm=tm, n_blk=n_blk, tiles_per_seq=tiles_per_seq, slab=slab),
        grid=(m // tm, n_blk),
        in_specs=[row,
                  pl.BlockSpec((d, tn), lambda i, j: (0, j)),
                  pl.BlockSpec((d, tn), lambda i, j: (0, j)),
                  pl.BlockSpec((CONV_K, tn), lambda i, j: (0, j)),
                  pl.BlockSpec((tn, d), lambda i, j: (j, 0)),
                  pl.BlockSpec(memory_space=pl.ANY),
                  pl.BlockSpec((None, 1, d), lambda i, j: (mod_row0 + i // tiles_per_seq, 0, 5)),
                  pl.BlockSpec((1, d), lambda i, j: (0, 0))],
        out_specs=[row, pl.BlockSpec((SUBLANES, tn), lambda i, j: (i, j))],
        out_shape=[jax.ShapeDtypeStruct((m, d), _F32),
                   jax.ShapeDtypeStruct((m // tm * SUBLANES, f), _F32)],
        scratch_shapes=[pltpu.VMEM((tm, d), _F32), pltpu.SemaphoreType.DMA(()),
                        pltpu.VMEM((n_blk, SUBLANES, tn), _F32)],
        compiler_params=_params(2),
        name="convffn_prompt",
    )(h2, w_a, w_b, cw, w_down, x1, mod, gpost)


def _sample_layer(x, mod, st_b, st_f, p):
    d = p["g_v"].shape[1]
    tn, tn_wide = SAMPLE_COL_TILE, SAMPLE_COL_TILE_WIDE
    w_in = p["w_in"]
    wb = p["conv_b_w"].shape[1]
    ya, h1, vn, w_v, w_u = _s_gmlp_call(x, mod, p["g_pre_mix"], w_in, p["g_v"], p["wvec"],
                                        p["bvec"], tn=tn)
    yb, tb0, tb1, w_bg, w_cg, w_xb = _s_shortconv_call(h1, w_in, 2 * d, p["conv_b_w"], st_b,
                                                       tn=tn_wide)
    mg, w_ga, w_gb, w_oa, w_ob = _s_merge_call(h1, ya, yb, w_in, 2 * d + 3 * wb, p["w_out_a"],
                                               p["w_out_b"], tn=tn)
    x1, h2, w_o = _s_proj_call(mg, p["w_o"], x, mod, p["g_post_mix"], p["g_pre_ffn"], tn=tn_wide)
    out, tf0, tf1, w_a, w_b, w_d = _s_ffn_call(h2, p["w_up"], p["conv_f_w"], p["w_down"], x1, mod,
                                               p["g_post_ffn"], st_f, tn=tn)
    bf16_weights = dict(w_v=w_v, w_u=w_u, w_bg=w_bg, w_cg=w_cg, w_xb=w_xb, w_ga=w_ga, w_gb=w_gb,
                        w_out_a=w_oa, w_out_b=w_ob, w_o=w_o, w_a=w_a, w_b=w_b, w_down=w_d)
    return out, (tb0, tb1), (tf0, tf1), vn, bf16_weights


def _prompt_layer(x, mod, p, w, *, seq_len, mod_row0):
    tm, slab, tn = ROW_TILE, PROMPT_SLAB, MIX_COL_TILE
    tps = seq_len // tm
    ya, h1 = _p_gmlp_call(x, mod, p["g_pre_mix"], w["w_v"], w["w_u"], p["g_v"], p["w_s"], p["bias"],
                          tm=tm, tn=tn, slab=slab, tiles_per_seq=tps, mod_row0=mod_row0)
    yb, tail_b = _p_shortconv_call(h1, w["w_bg"], w["w_cg"], w["w_xb"], p["conv_b_w"],
                                   tm=tm, tn=tn, tiles_per_seq=tps, slab=slab)
    mg = _p_merge_call(h1, ya, yb, w["w_ga"], w["w_gb"], w["w_out_a"], w["w_out_b"],
                       tm=tm, tn=tn, slab=slab)
    x1, h2 = _p_proj_call(mg, w["w_o"], x, mod, p["g_post_mix"], p["g_pre_ffn"],
                          tm=tm, slab=slab, tiles_per_seq=tps, mod_row0=mod_row0)
    out, tail_f = _p_ffn_call(h2, w["w_a"], w["w_b"], p["conv_f_w"], w["w_down"], x1, mod,
                              p["g_post_ffn"], tm=FFN_ROW_TILE, tn=FFN_COL_TILE,
                              tiles_per_seq=seq_len // FFN_ROW_TILE, slab=slab, mod_row0=mod_row0)
    return out, tail_b, tail_f


def kernel(x_prompt, x_sample, c_prompt, c_sample, state_conv_b, state_conv_ffn, w_ada, b_ada, g_pre_mix, g_post_mix, w_in, g_v, w_s, b_s, conv_b_w, w_out_a, w_out_b, w_o, g_pre_ffn, g_post_ffn, w_up, conv_f_w, w_down):
    depth = w_in.shape[0]
    bp, seq, d = x_prompt.shape
    bs, tdec, _ = x_sample.shape
    n_groups = w_s.shape[1]
    assert bs == CHUNK and seq % FFN_ROW_TILE == 0 and seq % ROW_TILE == 0 and tdec <= CHUNK

    xp = x_prompt.reshape(bp * seq, d)
    xs = x_sample.reshape(bs, tdec * d)
    pad = (-(bp + bs)) % SUBLANES
    c_all = jnp.concatenate([c_sample, c_prompt, jnp.zeros((pad, d), _F32)], axis=0)

    pb, sb, pf, sf, sv = [], [], [], [], []
    for l in range(depth):
        mod = _mod_call(c_all, w_ada[l], b_ada[l][None, :])
        mod_rows = mod.reshape(mod.shape[0], 1, N_MOD * d)
        vec = lambda a: a[l][None, :]
        bias_full = jnp.repeat(jnp.transpose(b_s[l]), GROUP, axis=1)
        wvec = jnp.repeat(
            jnp.transpose(w_s[l][:, :tdec, :tdec], (1, 2, 0)).reshape(tdec * tdec, n_groups),
            GROUP, axis=1)
        p = {
            "w_in": w_in[l], "w_out_a": w_out_a[l], "w_out_b": w_out_b[l], "w_o": w_o[l],
            "w_up": w_up[l], "w_down": w_down[l],
            "g_pre_mix": vec(g_pre_mix), "g_post_mix": vec(g_post_mix), "g_v": vec(g_v),
            "g_pre_ffn": vec(g_pre_ffn), "g_post_ffn": vec(g_post_ffn),
            "conv_b_w": conv_b_w[l], "conv_f_w": conv_f_w[l],
            "w_s": w_s[l], "bias": bias_full, "wvec": wvec, "bvec": bias_full[:tdec],
        }
        st_b = state_conv_b[l].reshape(bs, -1)
        st_f = state_conv_ffn[l].reshape(bs, -1)
        xs, sbt, sft, vn, w_bf16 = _sample_layer(xs, mod, st_b, st_f, p)
        xp, tb, tf = _prompt_layer(xp, mod_rows, p, w_bf16, seq_len=seq, mod_row0=bs)

        def prompt_tail(t, tm):
            t = t.reshape(bp, seq // tm, SUBLANES, -1)
            return t[:, -1, SUBLANES - (CONV_K - 1):, :]

        pb.append(prompt_tail(tb, ROW_TILE))
        pf.append(prompt_tail(tf, FFN_ROW_TILE))
        sb.append(jnp.stack(sbt, axis=1))
        sf.append(jnp.stack(sft, axis=1))
        sv.append(vn.reshape(bs, tdec, d))

    y_prompt = xp.reshape(bp, seq, d)
    y_sample = xs.reshape(bs, tdec, d)
    return (y_prompt, y_sample, jnp.stack(pb), jnp.stack(sb), jnp.stack(pf), jnp.stack(sf),
            jnp.stack(sv))
```

```python
import functools

import jax
import jax.numpy as jnp
from jax import lax
from jax.experimental import pallas as pl
from jax.experimental.pallas import tpu as pltpu

EPS = 1e-6
CHUNK = 128
GROUP = 128
CONV_K = 3
N_MOD = 6
SUBLANES = 8
VMEM_LIMIT_BYTES = 56 * 1024 * 1024
VMEM_LIMIT_BYTES_WIDE = 60 * 1024 * 1024
PROMPT_TILES = {
    "gmlp": (1024, 512, 256),
    "shortconv": (1024, 1024, 256),
    "merge": (512, 1024, 256),
    "proj": (512, None, 256),
    "ffn": (1024, 512, 256),
}
SAMPLE_COL_TILE = 256
SAMPLE_COL_TILE_WIDE = 512

_BF16 = jnp.bfloat16
_F32 = jnp.float32


def _dot(a, b):
    return jnp.dot(a, b, preferred_element_type=_F32)


def _params(n_axes, vmem_limit_bytes=VMEM_LIMIT_BYTES):
    return pltpu.CompilerParams(
        dimension_semantics=("arbitrary",) * n_axes,
        vmem_limit_bytes=vmem_limit_bytes,
    )


def _rms(xf, g):
    ms = jnp.mean(xf * xf, axis=-1, keepdims=True)
    return xf * lax.rsqrt(ms + EPS) * g


def _cast_emit(w_ref, wb_ref):
    w = w_ref[...].astype(_BF16)
    wb_ref[...] = w
    return w


def _causal_conv_rows(p, prev, cw_ref):
    rows = p.shape[0]
    row = lax.broadcasted_iota(jnp.int32, (rows, 1), 0)
    m1 = jnp.where(row == 0, prev[7:8, :], pltpu.roll(p, 1, 0))
    m2 = jnp.where(row == 0, prev[6:7, :], jnp.where(row == 1, prev[7:8, :], pltpu.roll(p, 2, 0)))
    return cw_ref[0:1, :] * m2 + cw_ref[1:2, :] * m1 + cw_ref[2:3, :] * p


def _causal_conv_slabs(x, prev, cw_ref, slab):
    w0, w1, w2 = cw_ref[0:1, :], cw_ref[1:2, :], cw_ref[2:3, :]
    seq = list(prev) + [x[t * slab:(t + 1) * slab, :] for t in range(x.shape[0] // slab)]
    y = [w0 * seq[t] + w1 * seq[t + 1] + w2 * seq[t + 2] for t in range(len(seq) - 2)]
    return jnp.concatenate(y, axis=0), seq[-2:]


def _slabs(tm, slab):
    return [slice(s * slab, (s + 1) * slab) for s in range(tm // slab)]


def _mod_kernel(c_ref, w_ref, b_ref, o_ref):
    c = c_ref[...]
    a = (c * jax.nn.sigmoid(c)).astype(_BF16)
    o_ref[...] = _dot(a, w_ref[...].astype(_BF16)) + b_ref[...]


def _mod_call(c_all, w_ada, b_ada, tn=1024):
    rows, d = c_all.shape
    n = w_ada.shape[1]
    return pl.pallas_call(
        _mod_kernel,
        grid=(n // tn,),
        in_specs=[
            pl.BlockSpec((rows, d), lambda j: (0, 0)),
            pl.BlockSpec((d, tn), lambda j: (0, j)),
            pl.BlockSpec((1, tn), lambda j: (0, j)),
        ],
        out_specs=pl.BlockSpec((rows, tn), lambda j: (0, j)),
        out_shape=jax.ShapeDtypeStruct((rows, n), _F32),
        compiler_params=_params(1),
        name="adaln_mod",
    )(c_all, w_ada, b_ada)


def _s_gmlp_kernel(x_ref, sh_ref, sc_ref, gpre_ref, wv_ref, wu_ref, gv_ref, ws_ref, bias_ref,
                   ya_ref, h_ref, vn_ref, wvb_ref, wub_ref, h_scr, v_scr, *, tm, tn, n_blk, slab):
    j = pl.program_id(0)
    d = n_blk * tn
    slabs = _slabs(tm, slab)
    n_slab = len(slabs)

    @pl.when(j == 0)
    def _():
        for t, r in enumerate(slabs):
            h = (_rms(x_ref[:, t * d:(t + 1) * d], gpre_ref[...]) * (1.0 + sc_ref[...])
                 + sh_ref[...]).astype(_BF16)
            h_scr[r, :] = h
            h_ref[r, :] = h

    @pl.when(j < n_blk)
    def _():
        v_scr[j] = _dot(h_scr[...], _cast_emit(wv_ref, wvb_ref))

    @pl.when(j == n_blk)
    def _():
        for t, r in enumerate(slabs):
            ss = 0.0
            for k in range(n_blk):
                vk = v_scr[k, r, :]
                ss = ss + jnp.sum(vk * vk, axis=-1, keepdims=True)
            rs = lax.rsqrt(ss * (1.0 / d) + EPS)
            for k in range(n_blk):
                vn = v_scr[k, r, :] * rs * gv_ref[:, k * tn:(k + 1) * tn]
                v_scr[k, r, :] = vn
                vn_ref[:, t * d + k * tn:t * d + (k + 1) * tn] = vn
        for t in reversed(range(n_slab)):
            for k in range(n_blk):
                c = slice(k * tn, (k + 1) * tn)
                acc = ws_ref[t * n_slab:t * n_slab + 1, c] * v_scr[k, slabs[0], :]
                for s in range(1, t + 1):
                    acc = acc + ws_ref[t * n_slab + s:t * n_slab + s + 1, c] * v_scr[k, slabs[s], :]
                v_scr[k, slabs[t], :] = acc + bias_ref[t:t + 1, c]

    @pl.when(j >= n_blk)
    def _():
        u = _dot(h_scr[...], _cast_emit(wu_ref, wub_ref))
        ya_ref[...] = (u * v_scr[j - n_blk]).astype(_BF16)


def _s_gmlp_call(x, mod, gpre, w_in, gv, wvec, bvec, *, tn):
    slab = x.shape[0]
    d = gpre.shape[1]
    tm = x.shape[1] // d * slab
    n_blk = d // tn
    full = lambda a: pl.BlockSpec(a.shape, lambda j: (0,) * a.ndim)
    v_map = lambda j: (0, n_blk + jnp.minimum(j, n_blk - 1))
    u_map = lambda j: (0, jnp.maximum(j - n_blk, 0))
    return pl.pallas_call(
        functools.partial(_s_gmlp_kernel, tm=tm, tn=tn, n_blk=n_blk, slab=slab),
        grid=(2 * n_blk,),
        in_specs=[full(x),
                  pl.BlockSpec((slab, d), lambda j: (0, 0)), pl.BlockSpec((slab, d), lambda j: (0, 1)),
                  full(gpre),
                  pl.BlockSpec((d, tn), v_map), pl.BlockSpec((d, tn), u_map),
                  full(gv), full(wvec), full(bvec)],
        out_specs=[pl.BlockSpec((tm, tn), u_map),
                   pl.BlockSpec((tm, d), lambda j: (0, 0)), full(x),
                   pl.BlockSpec((d, tn), lambda j: (0, jnp.minimum(j, n_blk - 1))),
                   pl.BlockSpec((d, tn), u_map)],
        out_shape=[jax.ShapeDtypeStruct((tm, d), _BF16), jax.ShapeDtypeStruct((tm, d), _BF16),
                   jax.ShapeDtypeStruct(x.shape, _F32),
                   jax.ShapeDtypeStruct((d, d), _BF16), jax.ShapeDtypeStruct((d, d), _BF16)],
        scratch_shapes=[pltpu.VMEM((tm, d), _BF16), pltpu.VMEM((n_blk, tm, tn), _F32)],
        compiler_params=_params(1),
        name="gmlp_sample",
    )(x, mod, mod, gpre, w_in, w_in, gv, wvec, bvec)


def _s_shortconv_kernel(h_ref, wbg_ref, wcg_ref, wxb_ref, cw_ref, st0_ref, st1_ref,
                        yb_ref, t0_ref, t1_ref, wbgb_ref, wcgb_ref, wxbb_ref, *, slab):
    h = h_ref[...]
    bg = _dot(h, _cast_emit(wbg_ref, wbgb_ref))
    p = _dot(h, _cast_emit(wcg_ref, wcgb_ref)) * _dot(h, _cast_emit(wxb_ref, wxbb_ref))
    cb, tail = _causal_conv_slabs(p, [st0_ref[...], st1_ref[...]], cw_ref, slab)
    yb_ref[...] = (bg * cb).astype(_BF16)
    t0_ref[...] = tail[0]
    t1_ref[...] = tail[1]


def _s_shortconv_call(h, w_in, col0, cw, state, *, tn):
    tm, d = h.shape
    slab = state.shape[0]
    w = cw.shape[1]
    n_blk = w // tn
    b0 = col0 // tn
    wspec = lambda off: pl.BlockSpec((d, tn), lambda j: (0, off + j))
    sspec = lambda off: pl.BlockSpec((slab, tn), lambda j: (0, off + j))
    return pl.pallas_call(
        functools.partial(_s_shortconv_kernel, slab=slab),
        grid=(n_blk,),
        in_specs=[pl.BlockSpec((tm, d), lambda j: (0, 0)),
                  wspec(b0), wspec(b0 + n_blk), wspec(b0 + 2 * n_blk),
                  pl.BlockSpec((CONV_K, tn), lambda j: (0, j)),
                  sspec(0), sspec(n_blk)],
        out_specs=[pl.BlockSpec((tm, tn), lambda j: (0, j)), sspec(0), sspec(0),
                   wspec(0), wspec(0), wspec(0)],
        out_shape=[jax.ShapeDtypeStruct((tm, w), _BF16)] + [jax.ShapeDtypeStruct((slab, w), _F32)] * 2
        + [jax.ShapeDtypeStruct((d, w), _BF16)] * 3,
        compiler_params=_params(1),
        name="shortconv_sample",
    )(h, w_in, w_in, w_in, cw, state, state)


def _s_merge_kernel(h_ref, ya_ref, yb_ref, wga_ref, wgb_ref, woa_ref, wob_ref,
                    m_ref, wgab_ref, wgbb_ref, woab_ref, wobb_ref):
    h = h_ref[...]
    ga = jax.nn.sigmoid(_dot(h, _cast_emit(wga_ref, wgab_ref)))
    gb = jax.nn.sigmoid(_dot(h, _cast_emit(wgb_ref, wgbb_ref)))
    m = (ga * _dot(ya_ref[...], _cast_emit(woa_ref, woab_ref))
         + gb * _dot(yb_ref[...], _cast_emit(wob_ref, wobb_ref)))
    m_ref[...] = m.astype(_BF16)


def _s_merge_call(h, ya, yb, w_in, gate_col0, w_out_a, w_out_b, *, tn):
    tm, d = h.shape
    n_blk = d // tn
    b0 = gate_col0 // tn
    row = pl.BlockSpec((tm, d), lambda j: (0, 0))
    wspec = lambda off: pl.BlockSpec((d, tn), lambda j: (0, off + j))
    return pl.pallas_call(
        _s_merge_kernel,
        grid=(n_blk,),
        in_specs=[row, row, row, wspec(b0), wspec(b0 + n_blk), wspec(0), wspec(0)],
        out_specs=[pl.BlockSpec((tm, tn), lambda j: (0, j)),
                   wspec(0), wspec(0), wspec(0), wspec(0)],
        out_shape=[jax.ShapeDtypeStruct((tm, d), _BF16)] + [jax.ShapeDtypeStruct((d, d), _BF16)] * 4,
        compiler_params=_params(1),
        name="gated_merge_sample",
    )(h, ya, yb, w_in, w_in, w_out_a, w_out_b)


def _post_mix(y, x, gt, gpost, sh, sc, gpre):
    x1 = x + gt * _rms(y, gpost)
    h2 = _rms(x1, gpre) * (1.0 + sc) + sh
    return x1, h2.astype(_BF16)


def _s_proj_kernel(m_ref, wo_ref, x_ref, gt_ref, gpost_ref, sh_ref, sc_ref, gpre_ref,
                   x1_ref, h2_ref, wob_ref, y_scr, *, tm, tn, n_blk, slab):
    j = pl.program_id(0)
    d = n_blk * tn
    y_scr[j] = _dot(m_ref[...], _cast_emit(wo_ref, wob_ref))

    @pl.when(j == n_blk - 1)
    def _():
        for t, r in enumerate(_slabs(tm, slab)):
            y = jnp.concatenate([y_scr[k, r, :] for k in range(n_blk)], axis=-1)
            x1, h2 = _post_mix(y, x_ref[:, t * d:(t + 1) * d], gt_ref[...], gpost_ref[...],
                               sh_ref[...], sc_ref[...], gpre_ref[...])
            x1_ref[r, :] = x1
            h2_ref[r, :] = h2


def _s_proj_call(mg, w_o, x, mod, gpost, gpre, *, tn):
    tm, d = mg.shape
    slab = x.shape[0]
    n_blk = d // tn
    row = pl.BlockSpec((tm, d), lambda j: (0, 0))
    vec = pl.BlockSpec((1, d), lambda j: (0, 0))
    mspec = lambda k: pl.BlockSpec((slab, d), lambda j: (0, k))
    return pl.pallas_call(
        functools.partial(_s_proj_kernel, tm=tm, tn=tn, n_blk=n_blk, slab=slab),
        grid=(n_blk,),
        in_specs=[row, pl.BlockSpec((d, tn), lambda j: (0, j)),
                  pl.BlockSpec(x.shape, lambda j: (0, 0)), mspec(2), vec, mspec(3), mspec(4), vec],
        out_specs=[row, row, pl.BlockSpec((d, tn), lambda j: (0, j))],
        out_shape=[jax.ShapeDtypeStruct((tm, d), _F32), jax.ShapeDtypeStruct((tm, d), _BF16),
                   jax.ShapeDtypeStruct((d, d), _BF16)],
        scratch_shapes=[pltpu.VMEM((n_blk, tm, tn), _F32)],
        compiler_params=_params(1),
        name="out_proj_sample",
    )(mg, w_o, x, mod, gpost, mod, mod, gpre)


def _s_ffn_kernel(h_ref, wa_ref, wb_ref, cw_ref, wd_ref, x1_ref, gt_ref, gpost_ref, st0_ref, st1_ref,
                  out_ref, t0_ref, t1_ref, wab_ref, wbb_ref, wdb_ref, *, tm, n_blk, slab):
    j = pl.program_id(0)
    d = x1_ref.shape[1]
    lanes = [slice(t * d, (t + 1) * d) for t in range(tm // slab)]

    @pl.when(j == 0)
    def _():
        out_ref[...] = jnp.zeros(out_ref.shape, _F32)

    wa = _cast_emit(wa_ref, wab_ref)
    wb = _cast_emit(wb_ref, wbb_ref)
    wd = _cast_emit(wd_ref, wdb_ref)
    prev = [st0_ref[...], st1_ref[...]]
    for c, r in enumerate(_slabs(tm, 2 * slab)):
        h = h_ref[r, :]
        ac, prev = _causal_conv_slabs(_dot(h, wa), prev, cw_ref, slab)
        g = (jax.nn.gelu(ac) * _dot(h, wb)).astype(_BF16)
        f = _dot(g, wd)
        out_ref[:, lanes[2 * c]] += f[:slab, :]
        out_ref[:, lanes[2 * c + 1]] += f[slab:, :]
    t0_ref[...] = prev[0]
    t1_ref[...] = prev[1]

    @pl.when(j == n_blk - 1)
    def _():
        for t, r in enumerate(_slabs(tm, slab)):
            out_ref[:, lanes[t]] = (x1_ref[r, :]
                                    + gt_ref[...] * _rms(out_ref[:, lanes[t]], gpost_ref[...]))


def _s_ffn_call(h2, w_up, cw, w_down, x1, mod, gpost, state, *, tn):
    tm, d = x1.shape
    slab = state.shape[0]
    f = cw.shape[1]
    n_blk = f // tn
    row = pl.BlockSpec((tm, d), lambda j: (0, 0), pipeline_mode=pl.Buffered(1))
    a_spec = pl.BlockSpec((d, tn), lambda j: (0, j))
    b_spec = pl.BlockSpec((d, tn), lambda j: (0, n_blk + j))
    d_spec = pl.BlockSpec((tn, d), lambda j: (j, 0))
    sspec = lambda off: pl.BlockSpec((slab, tn), lambda j: (0, off + j))
    out = pl.BlockSpec((slab, tm // slab * d), lambda j: (0, 0))
    return pl.pallas_call(
        functools.partial(_s_ffn_kernel, tm=tm, n_blk=n_blk, slab=slab),
        grid=(n_blk,),
        in_specs=[row, a_spec, b_spec, pl.BlockSpec((CONV_K, tn), lambda j: (0, j)), d_spec, row,
                  pl.BlockSpec((slab, d), lambda j: (0, 5)), pl.BlockSpec((1, d), lambda j: (0, 0)),
                  sspec(0), sspec(n_blk)],
        out_specs=[out, sspec(0), sspec(0), a_spec, a_spec, d_spec],
        out_shape=[jax.ShapeDtypeStruct((slab, tm // slab * d), _F32)]
        + [jax.ShapeDtypeStruct((slab, f), _F32)] * 2
        + [jax.ShapeDtypeStruct((d, f), _BF16), jax.ShapeDtypeStruct((d, f), _BF16),
           jax.ShapeDtypeStruct((f, d), _BF16)],
        compiler_params=_params(1, VMEM_LIMIT_BYTES_WIDE),
        name="convffn_sample",
    )(h2, w_up, w_up, cw, w_down, x1, mod, gpost, state, state)


def _p_gmlp_kernel(x_ref, sh_ref, sc_ref, gpre_ref, wv_ref, wu_ref, gv_ref, ws_ref, bias_ref,
                   ya_ref, h_ref, v_scr, wt_scr, *, tm, tn, n_blk, slab):
    i = pl.program_id(0)
    j = pl.program_id(1)
    d = n_blk * tn
    slabs = _slabs(tm, slab)

    @pl.when((i == 0) & (j == 0))
    def _():
        tril = (lax.broadcasted_iota(jnp.int32, (CHUNK, CHUNK), 0)
                >= lax.broadcasted_iota(jnp.int32, (CHUNK, CHUNK), 1))
        for g in range(d // GROUP):
            wt_scr[g] = jnp.where(tril, ws_ref[g], 0.0).astype(_BF16)

    @pl.when(j == 0)
    def _():
        for r in slabs:
            h = _rms(x_ref[r, :], gpre_ref[...]) * (1.0 + sc_ref[...]) + sh_ref[...]
            h = h.astype(_BF16)
            h_ref[r, :] = h
            v_scr[0, r, :] = _dot(h, wv_ref[...])

    @pl.when((j > 0) & (j < n_blk))
    def _():
        for r in slabs:
            v_scr[j, r, :] = _dot(h_ref[r, :], wv_ref[...])

    def _gate():
        gpb = tn // GROUP
        for c in range(tm // CHUNK):
            r = slice(c * CHUNK, (c + 1) * CHUNK)
            ss = 0.0
            for k in range(n_blk):
                vk = v_scr[k, r, :]
                ss = ss + jnp.sum(vk * vk, axis=-1, keepdims=True)
            rs = lax.rsqrt(ss * (1.0 / d) + EPS)
            for k in range(n_blk):
                vb = (v_scr[k, r, :] * rs * gv_ref[:, k * tn:(k + 1) * tn]).astype(_BF16)
                for gg in range(gpb):
                    g = k * gpb + gg
                    lanes = slice(gg * GROUP, (gg + 1) * GROUP)
                    v_scr[k, r, lanes] = (_dot(wt_scr[g], vb[:, lanes])
                                          + bias_ref[:, g * GROUP:(g + 1) * GROUP])

    @pl.when(j == n_blk)
    def _():
        u = [_dot(h_ref[r, :], wu_ref[...]) for r in slabs]
        _gate()
        for r, ur in zip(slabs, u):
            ya_ref[r, :] = (ur * v_scr[0, r, :]).astype(_BF16)

    @pl.when(j > n_blk)
    def _():
        for r in slabs:
            ya_ref[r, :] = (_dot(h_ref[r, :], wu_ref[...]) * v_scr[j - n_blk, r, :]).astype(_BF16)


def _p_gmlp_call(x, mod, gpre, w_v, w_u, gv, ws, bias, *, tm, tn, slab, tiles_per_seq, mod_row0):
    m, d = x.shape
    n_blk = d // tn
    full = lambda a: pl.BlockSpec(a.shape, lambda i, j: (0,) * a.ndim)
    mspec = lambda k: pl.BlockSpec((None, 1, d), lambda i, j: (mod_row0 + i // tiles_per_seq, 0, k))
    u_map = lambda i, j: (i, jnp.maximum(j - n_blk, 0))
    return pl.pallas_call(
        functools.partial(_p_gmlp_kernel, tm=tm, tn=tn, n_blk=n_blk, slab=slab),
        grid=(m // tm, 2 * n_blk),
        in_specs=[pl.BlockSpec((tm, d), lambda i, j: (i, 0)), mspec(0), mspec(1), full(gpre),
                  pl.BlockSpec((d, tn), lambda i, j: (0, jnp.minimum(j, n_blk - 1))),
                  pl.BlockSpec((d, tn), lambda i, j: (0, jnp.maximum(j - n_blk, 0))),
                  full(gv), full(ws), full(bias)],
        out_specs=[pl.BlockSpec((tm, tn), u_map), pl.BlockSpec((tm, d), lambda i, j: (i, 0))],
        out_shape=[jax.ShapeDtypeStruct((m, d), _BF16), jax.ShapeDtypeStruct((m, d), _BF16)],
        scratch_shapes=[pltpu.VMEM((n_blk, tm, tn), _F32),
                        pltpu.VMEM((d // GROUP, CHUNK, CHUNK), _BF16)],
        compiler_params=_params(2),
        name="gmlp_prompt",
    )(x, mod, mod, gpre, w_v, w_u, gv, ws, bias)


def _p_shortconv_kernel(h_ref, wbg_ref, wcg_ref, wxb_ref, cw_ref, yb_ref, tail_ref, carry_scr,
                        *, tm, tiles_per_seq, slab):
    i = pl.program_id(0)
    j = pl.program_id(1)

    @pl.when(i % tiles_per_seq == 0)
    def _():
        carry_scr[j] = jnp.zeros(carry_scr.shape[1:], _F32)

    prev = carry_scr[j]
    for r in _slabs(tm, slab):
        h = h_ref[r, :]
        bg = _dot(h, wbg_ref[...])
        p = _dot(h, wcg_ref[...]) * _dot(h, wxb_ref[...])
        yb_ref[r, :] = (bg * _causal_conv_rows(p, prev, cw_ref)).astype(_BF16)
        prev = p[slab - SUBLANES:, :]
    carry_scr[j] = prev
    tail_ref[...] = prev


def _p_shortconv_call(h, w_bg, w_cg, w_xb, cw, *, tm, tn, tiles_per_seq, slab):
    m, d = h.shape
    w = cw.shape[1]
    n_blk = w // tn
    wspec = lambda off: pl.BlockSpec((d, tn), lambda i, j: (0, off + j))
    return pl.pallas_call(
        functools.partial(_p_shortconv_kernel, tm=tm, tiles_per_seq=tiles_per_seq, slab=slab),
        grid=(m // tm, n_blk),
        in_specs=[pl.BlockSpec((tm, d), lambda i, j: (i, 0)),
                  wspec(0), wspec(0), wspec(0),
                  pl.BlockSpec((CONV_K, tn), lambda i, j: (0, j))],
        out_specs=[pl.BlockSpec((tm, tn), lambda i, j: (i, j)),
                   pl.BlockSpec((SUBLANES, tn), lambda i, j: (i, j))],
        out_shape=[jax.ShapeDtypeStruct((m, w), _BF16),
                   jax.ShapeDtypeStruct((m // tm * SUBLANES, w), _F32)],
        scratch_shapes=[pltpu.VMEM((n_blk, SUBLANES, tn), _F32)],
        compiler_params=_params(2),
        name="shortconv_prompt",
    )(h, w_bg, w_cg, w_xb, cw)


def _p_merge_kernel(h_ref, ya_ref, yb_ref, wga_ref, wgb_ref, woa_ref, wob_ref, m_ref, *, tm, slab):
    for r in _slabs(tm, slab):
        h = h_ref[r, :]
        ga = jax.nn.sigmoid(_dot(h, wga_ref[...]))
        gb = jax.nn.sigmoid(_dot(h, wgb_ref[...]))
        m = ga * _dot(ya_ref[r, :], woa_ref[...]) + gb * _dot(yb_ref[r, :], wob_ref[...])
        m_ref[r, :] = m.astype(_BF16)


def _p_merge_call(h, ya, yb, w_ga, w_gb, w_out_a, w_out_b, *, tm, tn, slab):
    m, d = h.shape
    n_blk = d // tn
    row = pl.BlockSpec((tm, d), lambda i, j: (i, 0))
    wspec = lambda off: pl.BlockSpec((d, tn), lambda i, j: (0, off + j))
    return pl.pallas_call(
        functools.partial(_p_merge_kernel, tm=tm, slab=slab),
        grid=(m // tm, n_blk),
        in_specs=[row, row, row, wspec(0), wspec(0), wspec(0), wspec(0)],
        out_specs=pl.BlockSpec((tm, tn), lambda i, j: (i, j)),
        out_shape=jax.ShapeDtypeStruct((m, d), _BF16),
        compiler_params=_params(2),
        name="gated_merge_prompt",
    )(h, ya, yb, w_ga, w_gb, w_out_a, w_out_b)


def _p_proj_kernel(m_ref, wo_ref, x_ref, gt_ref, gpost_ref, sh_ref, sc_ref, gpre_ref,
                   x1_ref, h2_ref, *, tm, slab):
    for r in _slabs(tm, slab):
        y = _dot(m_ref[r, :], wo_ref[...])
        x1, h2 = _post_mix(y, x_ref[r, :], gt_ref[...], gpost_ref[...], sh_ref[...], sc_ref[...],
                           gpre_ref[...])
        x1_ref[r, :] = x1
        h2_ref[r, :] = h2


def _p_proj_call(mg, w_o, x, mod, gpost, gpre, *, tm, slab, tiles_per_seq, mod_row0):
    m, d = x.shape
    row = pl.BlockSpec((tm, d), lambda i: (i, 0))
    vec = pl.BlockSpec((1, d), lambda i: (0, 0))
    mspec = lambda k: pl.BlockSpec((None, 1, d), lambda i: (mod_row0 + i // tiles_per_seq, 0, k))
    return pl.pallas_call(
        functools.partial(_p_proj_kernel, tm=tm, slab=slab),
        grid=(m // tm,),
        in_specs=[row, pl.BlockSpec((d, d), lambda i: (0, 0)), row, mspec(2), vec, mspec(3),
                  mspec(4), vec],
        out_specs=[row, row],
        out_shape=[jax.ShapeDtypeStruct((m, d), _F32), jax.ShapeDtypeStruct((m, d), _BF16)],
        compiler_params=_params(1),
        name="out_proj_prompt",
    )(mg, w_o, x, mod, gpost, mod, mod, gpre)


def _p_ffn_kernel(h_ref, wa_ref, wb_ref, cw_ref, wd_ref, x1_hbm, gt_ref, gpost_ref,
                  out_ref, tail_ref, x1_buf, x1_sem, carry_scr, *, tm, n_blk, tiles_per_seq, slab):
    i = pl.program_id(0)
    j = pl.program_id(1)
    slabs = _slabs(tm, slab)
    x1_copy = pltpu.make_async_copy(x1_hbm.at[pl.ds(pl.multiple_of(i * tm, tm), tm), :],
                                    x1_buf, x1_sem)

    @pl.when(j == 0)
    def _():
        x1_copy.start()
        out_ref[...] = jnp.zeros(out_ref.shape, _F32)

    @pl.when(i % tiles_per_seq == 0)
    def _():
        carry_scr[j] = jnp.zeros(carry_scr.shape[1:], _F32)

    prev = carry_scr[j]
    for r in slabs:
        h = h_ref[r, :]
        a = _dot(h, wa_ref[...])
        b = _dot(h, wb_ref[...])
        g = (jax.nn.gelu(_causal_conv_rows(a, prev, cw_ref)) * b).astype(_BF16)
        prev = a[slab - SUBLANES:, :]
        out_ref[r, :] += _dot(g, wd_ref[...])
    carry_scr[j] = prev
    tail_ref[...] = prev

    @pl.when(j == n_blk - 1)
    def _():
        x1_copy.wait()
        for r in slabs:
            out_ref[r, :] = x1_buf[r, :] + gt_ref[...] * _rms(out_ref[r, :], gpost_ref[...])


def _p_ffn_call(h2, w_a, w_b, cw, w_down, x1, mod, gpost, *, tm, tn, tiles_per_seq, slab, mod_row0):
    m, d = x1.shape
    f = cw.shape[1]
    n_blk = f // tn
    row = pl.BlockSpec((tm, d), lambda i, j: (i, 0))
    return pl.pallas_call(
        functools.partial(_p_ffn_kernel, tm=tm, n_blk=n_blk, tiles_per_seq=tiles_per_seq, slab=slab),
        grid=(m // tm, n_blk),
        in_specs=[row,
                  pl.BlockSpec((d, tn), lambda i, j: (0, j)),
                  pl.BlockSpec((d, tn), lambda i, j: (0, j)),
                  pl.BlockSpec((CONV_K, tn), lambda i, j: (0, j)),
                  pl.BlockSpec((tn, d), lambda i, j: (j, 0)),
                  pl.BlockSpec(memory_space=pl.ANY),
                  pl.BlockSpec((None, 1, d), lambda i, j: (mod_row0 + i // tiles_per_seq, 0, 5)),
                  pl.BlockSpec((1, d), lambda i, j: (0, 0))],
        out_specs=[row, pl.BlockSpec((SUBLANES, tn), lambda i, j: (i, j))],
        out_shape=[jax.ShapeDtypeStruct((m, d), _F32),
                   jax.ShapeDtypeStruct((m // tm * SUBLANES, f), _F32)],
        scratch_shapes=[pltpu.VMEM((tm, d), _F32), pltpu.SemaphoreType.DMA(()),
                        pltpu.VMEM((n_blk, SUBLANES, tn), _F32)],
        compiler_params=_params(2),
        name="convffn_prompt",
    )(h2, w_a, w_b, cw, w_down, x1, mod, gpost)


def _sample_layer(x, mod, st_b, st_f, p):
    d = p["g_v"].shape[1]
    tn, tn_wide = SAMPLE_COL_TILE, SAMPLE_COL_TILE_WIDE
    w_in = p["w_in"]
    wb = p["conv_b_w"].shape[1]
    ya, h1, vn, w_v, w_u = _s_gmlp_call(x, mod, p["g_pre_mix"], w_in, p["g_v"], p["wvec"],
                                        p["bvec"], tn=tn_wide)
    yb, tb0, tb1, w_bg, w_cg, w_xb = _s_shortconv_call(h1, w_in, 2 * d, p["conv_b_w"], st_b,
                                                       tn=tn_wide)
    mg, w_ga, w_gb, w_oa, w_ob = _s_merge_call(h1, ya, yb, w_in, 2 * d + 3 * wb, p["w_out_a"],
                                               p["w_out_b"], tn=tn)
    x1, h2, w_o = _s_proj_call(mg, p["w_o"], x, mod, p["g_post_mix"], p["g_pre_ffn"], tn=tn_wide)
    out, tf0, tf1, w_a, w_b, w_d = _s_ffn_call(h2, p["w_up"], p["conv_f_w"], p["w_down"], x1, mod,
                                               p["g_post_ffn"], st_f, tn=tn_wide)
    bf16_weights = dict(w_v=w_v, w_u=w_u, w_bg=w_bg, w_cg=w_cg, w_xb=w_xb, w_ga=w_ga, w_gb=w_gb,
                        w_out_a=w_oa, w_out_b=w_ob, w_o=w_o, w_a=w_a, w_b=w_b, w_down=w_d)
    return out, (tb0, tb1), (tf0, tf1), vn, bf16_weights


def _prompt_layer(x, mod, p, w, *, seq_len, mod_row0):
    def tiles(name):
        tm, tn, slab = PROMPT_TILES[name]
        return dict(tm=tm, tn=tn, slab=slab, tiles_per_seq=seq_len // tm)

    def without(kw, *names):
        return {k: v for k, v in kw.items() if k not in names}

    ya, h1 = _p_gmlp_call(x, mod, p["g_pre_mix"], w["w_v"], w["w_u"], p["g_v"], p["w_s"], p["bias"],
                          mod_row0=mod_row0, **tiles("gmlp"))
    yb, tail_b = _p_shortconv_call(h1, w["w_bg"], w["w_cg"], w["w_xb"], p["conv_b_w"],
                                   **tiles("shortconv"))
    mg = _p_merge_call(h1, ya, yb, w["w_ga"], w["w_gb"], w["w_out_a"], w["w_out_b"],
                       **without(tiles("merge"), "tiles_per_seq"))
    x1, h2 = _p_proj_call(mg, w["w_o"], x, mod, p["g_post_mix"], p["g_pre_ffn"],
                          mod_row0=mod_row0, **without(tiles("proj"), "tn"))
    out, tail_f = _p_ffn_call(h2, w["w_a"], w["w_b"], p["conv_f_w"], w["w_down"], x1, mod,
                              p["g_post_ffn"], mod_row0=mod_row0, **tiles("ffn"))
    return out, tail_b, tail_f


def kernel(x_prompt, x_sample, c_prompt, c_sample, state_conv_b, state_conv_ffn, w_ada, b_ada, g_pre_mix, g_post_mix, w_in, g_v, w_s, b_s, conv_b_w, w_out_a, w_out_b, w_o, g_pre_ffn, g_post_ffn, w_up, conv_f_w, w_down):
    depth = w_in.shape[0]
    bp, seq, d = x_prompt.shape
    bs, tdec, _ = x_sample.shape
    n_groups = w_s.shape[1]
    assert bs == CHUNK and tdec <= CHUNK
    assert all(seq % tm == 0 for tm, _, _ in PROMPT_TILES.values())

    xp = x_prompt.reshape(bp * seq, d)
    xs = x_sample.reshape(bs, tdec * d)
    pad = (-(bp + bs)) % SUBLANES
    c_all = jnp.concatenate([c_sample, c_prompt, jnp.zeros((pad, d), _F32)], axis=0)

    pb, sb, pf, sf, sv = [], [], [], [], []
    for l in range(depth):
        mod = _mod_call(c_all, w_ada[l], b_ada[l][None, :])
        mod_rows = mod.reshape(mod.shape[0], 1, N_MOD * d)
        vec = lambda a: a[l][None, :]
        bias_full = jnp.repeat(jnp.transpose(b_s[l]), GROUP, axis=1)
        wvec = jnp.repeat(
            jnp.transpose(w_s[l][:, :tdec, :tdec], (1, 2, 0)).reshape(tdec * tdec, n_groups),
            GROUP, axis=1)
        p = {
            "w_in": w_in[l], "w_out_a": w_out_a[l], "w_out_b": w_out_b[l], "w_o": w_o[l],
            "w_up": w_up[l], "w_down": w_down[l],
            "g_pre_mix": vec(g_pre_mix), "g_post_mix": vec(g_post_mix), "g_v": vec(g_v),
            "g_pre_ffn": vec(g_pre_ffn), "g_post_ffn": vec(g_post_ffn),
            "conv_b_w": conv_b_w[l], "conv_f_w": conv_f_w[l],
            "w_s": w_s[l], "bias": bias_full, "wvec": wvec, "bvec": bias_full[:tdec],
        }
        st_b = state_conv_b[l].reshape(bs, -1)
        st_f = state_conv_ffn[l].reshape(bs, -1)
        xs, sbt, sft, vn, w_bf16 = _sample_layer(xs, mod, st_b, st_f, p)
        xp, tb, tf = _prompt_layer(xp, mod_rows, p, w_bf16, seq_len=seq, mod_row0=bs)

        def prompt_tail(t, tm):
            t = t.reshape(bp, seq // tm, SUBLANES, -1)
            return t[:, -1, SUBLANES - (CONV_K - 1):, :]

        pb.append(prompt_tail(tb, PROMPT_TILES["shortconv"][0]))
        pf.append(prompt_tail(tf, PROMPT_TILES["ffn"][0]))
        sb.append(jnp.stack(sbt, axis=1))
        sf.append(jnp.stack(sft, axis=1))
        sv.append(vn.reshape(bs, tdec, d))

    y_prompt = xp.reshape(bp, seq, d)
    y_sample = xs.reshape(bs, tdec, d)
    return (y_prompt, y_sample, jnp.stack(pb), jnp.stack(sb), jnp.stack(pf), jnp.stack(sf),
            jnp.stack(sv))
```

```python
import functools
from typing import NamedTuple

import jax
import jax.numpy as jnp
from jax import lax
from jax.experimental import pallas as pl
from jax.experimental.pallas import tpu as pltpu

EPS = 1e-6
CHUNK = 128
GROUP = 128
CONV_K = 3
N_MOD = 6
SUBLANES = 8
VMEM_LIMIT_BYTES = 56 * 1024 * 1024
VMEM_LIMIT_BYTES_WIDE = 60 * 1024 * 1024
PROMPT_TILES = {
    "gmlp": (1024, 512, 256),
    "shortconv": (1024, 1024, 256),
    "merge": (512, 1024, 256),
    "proj": (512, None, 256),
    "ffn": (1024, 512, 256),
}
SAMPLE_COL_TILE = 512
ADALN_COL_TILE = 1024

_BF16 = jnp.bfloat16
_F32 = jnp.float32


def _dot(a, b):
    return jnp.dot(a, b, preferred_element_type=_F32)


def _rms(xf, g):
    ms = jnp.mean(xf * xf, axis=-1, keepdims=True)
    return xf * lax.rsqrt(ms + EPS) * g


def _causal_conv_rows(p, prev, cw_ref):
    rows = p.shape[0]
    row = lax.broadcasted_iota(jnp.int32, (rows, 1), 0)
    m1 = jnp.where(row == 0, prev[7:8, :], pltpu.roll(p, 1, 0))
    m2 = jnp.where(row == 0, prev[6:7, :], jnp.where(row == 1, prev[7:8, :], pltpu.roll(p, 2, 0)))
    return cw_ref[0:1, :] * m2 + cw_ref[1:2, :] * m1 + cw_ref[2:3, :] * p


def _causal_conv_slabs(x, prev, cw_ref, slab):
    w0, w1, w2 = cw_ref[0:1, :], cw_ref[1:2, :], cw_ref[2:3, :]
    seq = list(prev) + [x[t * slab:(t + 1) * slab, :] for t in range(x.shape[0] // slab)]
    y = [w0 * seq[t] + w1 * seq[t + 1] + w2 * seq[t + 2] for t in range(len(seq) - 2)]
    return jnp.concatenate(y, axis=0), seq[-2:]


def _slabs(tm, slab):
    return [slice(s * slab, (s + 1) * slab) for s in range(tm // slab)]


def _post_mix(y, x, gt, gpost, sh, sc, gpre):
    x1 = x + gt * _rms(y, gpost)
    h2 = _rms(x1, gpre) * (1.0 + sc) + sh
    return x1, h2.astype(_BF16)


class _Side(NamedTuple):
    src: jax.Array
    axis: int
    start: int
    block: int
    n_blocks: int


def _convert(side):
    for src_ref, dst_ref in side:
        dst_ref[...] = src_ref[...].astype(_BF16)


def _run(body, *, grid, in_specs, args, out_specs, out_shape, scratch=(), side=(), name,
         vmem_limit_bytes=VMEM_LIMIT_BYTES):
    n_in, n_out, n_side = len(args), len(out_shape), len(side)
    n_steps = functools.reduce(lambda a, b: a * b, grid)
    step_of = (lambda i: i) if len(grid) == 1 else (lambda i, j: i * grid[1] + j)
    side_in, side_out, side_shape = [], [], []
    for s in side:
        assert s.n_blocks <= n_steps, (name, s.n_blocks, n_steps)
        other = s.src.shape[1 - s.axis]
        pos = lambda *ids, s=s: jnp.minimum(step_of(*ids), s.n_blocks - 1)
        if s.axis == 1:
            blk, full = (other, s.block), (other, s.block * s.n_blocks)
            side_in.append(pl.BlockSpec(blk, lambda *ids, s=s, pos=pos: (0, s.start + pos(*ids))))
            side_out.append(pl.BlockSpec(blk, lambda *ids, pos=pos: (0, pos(*ids))))
        else:
            blk, full = (s.block, other), (s.block * s.n_blocks, other)
            side_in.append(pl.BlockSpec(blk, lambda *ids, s=s, pos=pos: (s.start + pos(*ids), 0)))
            side_out.append(pl.BlockSpec(blk, lambda *ids, pos=pos: (pos(*ids), 0)))
        side_shape.append(jax.ShapeDtypeStruct(full, _BF16))

    def kern(*refs):
        o0 = n_in + n_side
        s0 = o0 + n_out + n_side
        body(*refs[:n_in], *refs[o0:o0 + n_out], *refs[s0:],
             side=tuple(zip(refs[n_in:o0], refs[o0 + n_out:s0])))

    res = pl.pallas_call(
        kern, grid=grid,
        in_specs=list(in_specs) + side_in, out_specs=list(out_specs) + side_out,
        out_shape=list(out_shape) + side_shape, scratch_shapes=list(scratch),
        compiler_params=pltpu.CompilerParams(dimension_semantics=("arbitrary",) * len(grid),
                                             vmem_limit_bytes=vmem_limit_bytes),
        name=name,
    )(*args, *[s.src for s in side])
    return res[:n_out], res[n_out:]


def _mod_kernel(c_ref, w_ref, b_ref, o_ref, *, side):
    _convert(side)
    c = c_ref[...]
    a = (c * jax.nn.sigmoid(c)).astype(_BF16)
    o_ref[...] = _dot(a, w_ref[...].astype(_BF16)) + b_ref[...]


def _mod_call(c_all, w_ada, b_ada, *, side):
    rows, d = c_all.shape
    n = w_ada.shape[1]
    tn = ADALN_COL_TILE
    (mod,), copies = _run(
        _mod_kernel, grid=(n // tn,),
        in_specs=[pl.BlockSpec((rows, d), lambda j: (0, 0)),
                  pl.BlockSpec((d, tn), lambda j: (0, j)),
                  pl.BlockSpec((1, tn), lambda j: (0, j))],
        args=(c_all, w_ada, b_ada),
        out_specs=[pl.BlockSpec((rows, tn), lambda j: (0, j))],
        out_shape=[jax.ShapeDtypeStruct((rows, n), _F32)],
        side=side, name="adaln_mod")
    return mod, copies


def _s_gmlp_kernel(x_ref, sh_ref, sc_ref, gpre_ref, wv_ref, wu_ref, gv_ref, ws_ref, bias_ref,
                   ya_ref, h_ref, vn_ref, v_scr, *, tm, tn, n_blk, slab, side):
    j = pl.program_id(0)
    d = n_blk * tn
    slabs = _slabs(tm, slab)
    n_slab = len(slabs)

    @pl.when(j == 0)
    def _():
        for t, r in enumerate(slabs):
            h_ref[r, :] = (_rms(x_ref[:, t * d:(t + 1) * d], gpre_ref[...]) * (1.0 + sc_ref[...])
                           + sh_ref[...]).astype(_BF16)

    @pl.when(j < n_blk)
    def _():
        v_scr[j] = _dot(h_ref[...], wv_ref[...])

    @pl.when(j == n_blk)
    def _():
        for t, r in enumerate(slabs):
            ss = 0.0
            for k in range(n_blk):
                vk = v_scr[k, r, :]
                ss = ss + jnp.sum(vk * vk, axis=-1, keepdims=True)
            rs = lax.rsqrt(ss * (1.0 / d) + EPS)
            for k in range(n_blk):
                vn = v_scr[k, r, :] * rs * gv_ref[:, k * tn:(k + 1) * tn]
                v_scr[k, r, :] = vn
                vn_ref[:, t * d + k * tn:t * d + (k + 1) * tn] = vn
        for t in reversed(range(n_slab)):
            for k in range(n_blk):
                c = slice(k * tn, (k + 1) * tn)
                acc = ws_ref[t * n_slab:t * n_slab + 1, c] * v_scr[k, slabs[0], :]
                for s in range(1, t + 1):
                    acc = acc + ws_ref[t * n_slab + s:t * n_slab + s + 1, c] * v_scr[k, slabs[s], :]
                v_scr[k, slabs[t], :] = acc + bias_ref[t:t + 1, c]

    @pl.when(j >= n_blk)
    def _():
        ya_ref[...] = (_dot(h_ref[...], wu_ref[...]) * v_scr[j - n_blk]).astype(_BF16)


def _s_gmlp_call(x, mod, gpre, w_uv, gv, wvec, bvec, *, tn):
    slab = x.shape[0]
    d = gpre.shape[1]
    tm = x.shape[1] // d * slab
    n_blk = d // tn
    full = lambda a: pl.BlockSpec(a.shape, lambda j: (0,) * a.ndim)
    u_map = lambda j: (0, jnp.maximum(j - n_blk, 0))
    (ya, h, vn), _ = _run(
        functools.partial(_s_gmlp_kernel, tm=tm, tn=tn, n_blk=n_blk, slab=slab),
        grid=(2 * n_blk,),
        in_specs=[full(x),
                  pl.BlockSpec((slab, d), lambda j: (0, 0)), pl.BlockSpec((slab, d), lambda j: (0, 1)),
                  full(gpre),
                  pl.BlockSpec((d, tn), lambda j: (0, n_blk + jnp.minimum(j, n_blk - 1))),
                  pl.BlockSpec((d, tn), u_map),
                  full(gv), full(wvec), full(bvec)],
        args=(x, mod, mod, gpre, w_uv, w_uv, gv, wvec, bvec),
        out_specs=[pl.BlockSpec((tm, tn), u_map), pl.BlockSpec((tm, d), lambda j: (0, 0)), full(x)],
        out_shape=[jax.ShapeDtypeStruct((tm, d), _BF16), jax.ShapeDtypeStruct((tm, d), _BF16),
                   jax.ShapeDtypeStruct(x.shape, _F32)],
        scratch=[pltpu.VMEM((n_blk, tm, tn), _F32)],
        name="gmlp_sample")
    return ya, h, vn


def _s_shortconv_kernel(h_ref, wbg_ref, wcg_ref, wxb_ref, cw_ref, st0_ref, st1_ref,
                        yb_ref, t0_ref, t1_ref, *, slab, side):
    h = h_ref[...]
    bg = _dot(h, wbg_ref[...])
    p = _dot(h, wcg_ref[...]) * _dot(h, wxb_ref[...])
    cb, tail = _causal_conv_slabs(p, [st0_ref[...], st1_ref[...]], cw_ref, slab)
    yb_ref[...] = (bg * cb).astype(_BF16)
    t0_ref[...] = tail[0]
    t1_ref[...] = tail[1]


def _s_shortconv_call(h, w_bcx, cw, state, *, tn):
    tm, d = h.shape
    slab = state.shape[0]
    w = cw.shape[1]
    n_blk = w // tn
    wspec = lambda off: pl.BlockSpec((d, tn), lambda j: (0, off + j))
    sspec = lambda off: pl.BlockSpec((slab, tn), lambda j: (0, off + j))
    (yb, t0, t1), _ = _run(
        functools.partial(_s_shortconv_kernel, slab=slab),
        grid=(n_blk,),
        in_specs=[pl.BlockSpec((tm, d), lambda j: (0, 0)),
                  wspec(0), wspec(n_blk), wspec(2 * n_blk),
                  pl.BlockSpec((CONV_K, tn), lambda j: (0, j)),
                  sspec(0), sspec(n_blk)],
        args=(h, w_bcx, w_bcx, w_bcx, cw, state, state),
        out_specs=[pl.BlockSpec((tm, tn), lambda j: (0, j)), sspec(0), sspec(0)],
        out_shape=[jax.ShapeDtypeStruct((tm, w), _BF16)] + [jax.ShapeDtypeStruct((slab, w), _F32)] * 2,
        name="shortconv_sample")
    return yb, t0, t1


def _merge_kernel(h_ref, ya_ref, yb_ref, wga_ref, wgb_ref, woa_ref, wob_ref, m_ref, *, tm, slab, side):
    _convert(side)
    for r in _slabs(tm, slab):
        h = h_ref[r, :]
        ga = jax.nn.sigmoid(_dot(h, wga_ref[...]))
        gb = jax.nn.sigmoid(_dot(h, wgb_ref[...]))
        m = ga * _dot(ya_ref[r, :], woa_ref[...]) + gb * _dot(yb_ref[r, :], wob_ref[...])
        m_ref[r, :] = m.astype(_BF16)


def _merge_call(h, ya, yb, w_gate, w_out_a, w_out_b, *, tm, tn, slab, side=(), name):
    m, d = h.shape
    n_blk = d // tn
    row = pl.BlockSpec((tm, d), lambda i, j: (i, 0))
    wspec = lambda off: pl.BlockSpec((d, tn), lambda i, j: (0, off + j))
    (mg,), copies = _run(
        functools.partial(_merge_kernel, tm=tm, slab=slab),
        grid=(m // tm, n_blk),
        in_specs=[row, row, row, wspec(0), wspec(n_blk), wspec(0), wspec(0)],
        args=(h, ya, yb, w_gate, w_gate, w_out_a, w_out_b),
        out_specs=[pl.BlockSpec((tm, tn), lambda i, j: (i, j))],
        out_shape=[jax.ShapeDtypeStruct((m, d), _BF16)],
        side=side, name=name)
    return mg, copies


def _s_proj_kernel(m_ref, wo_ref, x_ref, gt_ref, gpost_ref, sh_ref, sc_ref, gpre_ref,
                   x1_ref, h2_ref, y_scr, *, tm, tn, n_blk, slab, side):
    j = pl.program_id(0)
    d = n_blk * tn
    y_scr[j] = _dot(m_ref[...], wo_ref[...])

    @pl.when(j == n_blk - 1)
    def _():
        for t, r in enumerate(_slabs(tm, slab)):
            y = jnp.concatenate([y_scr[k, r, :] for k in range(n_blk)], axis=-1)
            x1, h2 = _post_mix(y, x_ref[:, t * d:(t + 1) * d], gt_ref[...], gpost_ref[...],
                               sh_ref[...], sc_ref[...], gpre_ref[...])
            x1_ref[r, :] = x1
            h2_ref[r, :] = h2


def _s_proj_call(mg, w_o, x, mod, gpost, gpre, *, tn):
    tm, d = mg.shape
    slab = x.shape[0]
    n_blk = d // tn
    row = pl.BlockSpec((tm, d), lambda j: (0, 0))
    vec = pl.BlockSpec((1, d), lambda j: (0, 0))
    mspec = lambda k: pl.BlockSpec((slab, d), lambda j: (0, k))
    (x1, h2), _ = _run(
        functools.partial(_s_proj_kernel, tm=tm, tn=tn, n_blk=n_blk, slab=slab),
        grid=(n_blk,),
        in_specs=[row, pl.BlockSpec((d, tn), lambda j: (0, j)),
                  pl.BlockSpec(x.shape, lambda j: (0, 0)), mspec(2), vec, mspec(3), mspec(4), vec],
        args=(mg, w_o, x, mod, gpost, mod, mod, gpre),
        out_specs=[row, row],
        out_shape=[jax.ShapeDtypeStruct((tm, d), _F32), jax.ShapeDtypeStruct((tm, d), _BF16)],
        scratch=[pltpu.VMEM((n_blk, tm, tn), _F32)],
        name="out_proj_sample")
    return x1, h2


def _s_ffn_kernel(h_ref, wa_ref, wb_ref, cw_ref, wd_ref, x1_ref, gt_ref, gpost_ref, st0_ref, st1_ref,
                  out_ref, t0_ref, t1_ref, *, tm, n_blk, slab, side):
    j = pl.program_id(0)
    d = x1_ref.shape[1]
    lanes = [slice(t * d, (t + 1) * d) for t in range(tm // slab)]

    @pl.when(j == 0)
    def _():
        out_ref[...] = jnp.zeros(out_ref.shape, _F32)

    prev = [st0_ref[...], st1_ref[...]]
    for c, r in enumerate(_slabs(tm, 2 * slab)):
        h = h_ref[r, :]
        ac, prev = _causal_conv_slabs(_dot(h, wa_ref[...]), prev, cw_ref, slab)
        g = (jax.nn.gelu(ac) * _dot(h, wb_ref[...])).astype(_BF16)
        f = _dot(g, wd_ref[...])
        out_ref[:, lanes[2 * c]] += f[:slab, :]
        out_ref[:, lanes[2 * c + 1]] += f[slab:, :]
    t0_ref[...] = prev[0]
    t1_ref[...] = prev[1]

    @pl.when(j == n_blk - 1)
    def _():
        for t, r in enumerate(_slabs(tm, slab)):
            out_ref[:, lanes[t]] = (x1_ref[r, :]
                                    + gt_ref[...] * _rms(out_ref[:, lanes[t]], gpost_ref[...]))


def _s_ffn_call(h2, w_a, w_b, cw, w_down, x1, mod, gpost, state, *, tn):
    tm, d = x1.shape
    slab = state.shape[0]
    f = cw.shape[1]
    n_blk = f // tn
    row = pl.BlockSpec((tm, d), lambda j: (0, 0))
    wspec = pl.BlockSpec((d, tn), lambda j: (0, j))
    sspec = lambda off: pl.BlockSpec((slab, tn), lambda j: (0, off + j))
    (out, t0, t1), _ = _run(
        functools.partial(_s_ffn_kernel, tm=tm, n_blk=n_blk, slab=slab),
        grid=(n_blk,),
        in_specs=[row, wspec, wspec, pl.BlockSpec((CONV_K, tn), lambda j: (0, j)),
                  pl.BlockSpec((tn, d), lambda j: (j, 0)), row,
                  pl.BlockSpec((slab, d), lambda j: (0, 5)), pl.BlockSpec((1, d), lambda j: (0, 0)),
                  sspec(0), sspec(n_blk)],
        args=(h2, w_a, w_b, cw, w_down, x1, mod, gpost, state, state),
        out_specs=[pl.BlockSpec((slab, tm // slab * d), lambda j: (0, 0)), sspec(0), sspec(0)],
        out_shape=[jax.ShapeDtypeStruct((slab, tm // slab * d), _F32)]
        + [jax.ShapeDtypeStruct((slab, f), _F32)] * 2,
        name="convffn_sample")
    return out, t0, t1


def _p_gmlp_kernel(x_ref, sh_ref, sc_ref, gpre_ref, wv_ref, wu_ref, gv_ref, ws_ref, bias_ref,
                   ya_ref, h_ref, v_scr, wt_scr, *, tm, tn, n_blk, slab, side):
    i = pl.program_id(0)
    j = pl.program_id(1)
    d = n_blk * tn
    slabs = _slabs(tm, slab)

    @pl.when((i == 0) & (j == 0))
    def _():
        tril = (lax.broadcasted_iota(jnp.int32, (CHUNK, CHUNK), 0)
                >= lax.broadcasted_iota(jnp.int32, (CHUNK, CHUNK), 1))
        for g in range(d // GROUP):
            wt_scr[g] = jnp.where(tril, ws_ref[g], 0.0).astype(_BF16)

    @pl.when(j == 0)
    def _():
        _convert(side)
        for r in slabs:
            h = _rms(x_ref[r, :], gpre_ref[...]) * (1.0 + sc_ref[...]) + sh_ref[...]
            h = h.astype(_BF16)
            h_ref[r, :] = h
            v_scr[0, r, :] = _dot(h, wv_ref[...])

    @pl.when((j > 0) & (j < n_blk))
    def _():
        _convert(side)
        for r in slabs:
            v_scr[j, r, :] = _dot(h_ref[r, :], wv_ref[...])

    def _gate():
        gpb = tn // GROUP
        for c in range(tm // CHUNK):
            r = slice(c * CHUNK, (c + 1) * CHUNK)
            ss = 0.0
            for k in range(n_blk):
                vk = v_scr[k, r, :]
                ss = ss + jnp.sum(vk * vk, axis=-1, keepdims=True)
            rs = lax.rsqrt(ss * (1.0 / d) + EPS)
            for k in range(n_blk):
                vb = (v_scr[k, r, :] * rs * gv_ref[:, k * tn:(k + 1) * tn]).astype(_BF16)
                for gg in range(gpb):
                    g = k * gpb + gg
                    lanes = slice(gg * GROUP, (gg + 1) * GROUP)
                    v_scr[k, r, lanes] = (_dot(wt_scr[g], vb[:, lanes])
                                          + bias_ref[:, g * GROUP:(g + 1) * GROUP])

    @pl.when(j == n_blk)
    def _():
        _convert(side)
        u = [_dot(h_ref[r, :], wu_ref[...]) for r in slabs]
        _gate()
        for r, ur in zip(slabs, u):
            ya_ref[r, :] = (ur * v_scr[0, r, :]).astype(_BF16)

    @pl.when(j > n_blk)
    def _():
        _convert(side)
        for r in slabs:
            ya_ref[r, :] = (_dot(h_ref[r, :], wu_ref[...]) * v_scr[j - n_blk, r, :]).astype(_BF16)


def _p_gmlp_call(x, mod, gpre, w_uv, gv, ws, bias, *, tm, tn, slab, tiles_per_seq, mod_row0, side):
    m, d = x.shape
    n_blk = d // tn
    full = lambda a: pl.BlockSpec(a.shape, lambda i, j: (0,) * a.ndim)
    mspec = lambda k: pl.BlockSpec((None, 1, d), lambda i, j: (mod_row0 + i // tiles_per_seq, 0, k))
    u_map = lambda i, j: (i, jnp.maximum(j - n_blk, 0))
    (ya, h), copies = _run(
        functools.partial(_p_gmlp_kernel, tm=tm, tn=tn, n_blk=n_blk, slab=slab),
        grid=(m // tm, 2 * n_blk),
        in_specs=[pl.BlockSpec((tm, d), lambda i, j: (i, 0)), mspec(0), mspec(1), full(gpre),
                  pl.BlockSpec((d, tn), lambda i, j: (0, n_blk + jnp.minimum(j, n_blk - 1))),
                  pl.BlockSpec((d, tn), lambda i, j: (0, jnp.maximum(j - n_blk, 0))),
                  full(gv), full(ws), full(bias)],
        args=(x, mod, mod, gpre, w_uv, w_uv, gv, ws, bias),
        out_specs=[pl.BlockSpec((tm, tn), u_map), pl.BlockSpec((tm, d), lambda i, j: (i, 0))],
        out_shape=[jax.ShapeDtypeStruct((m, d), _BF16), jax.ShapeDtypeStruct((m, d), _BF16)],
        scratch=[pltpu.VMEM((n_blk, tm, tn), _F32), pltpu.VMEM((d // GROUP, CHUNK, CHUNK), _BF16)],
        side=side, name="gmlp_prompt", vmem_limit_bytes=VMEM_LIMIT_BYTES_WIDE)
    return ya, h, copies


def _p_shortconv_kernel(h_ref, wbg_ref, wcg_ref, wxb_ref, cw_ref, yb_ref, tail_ref, carry_scr,
                        *, tm, tiles_per_seq, slab, side):
    i = pl.program_id(0)
    j = pl.program_id(1)

    @pl.when(i % tiles_per_seq == 0)
    def _():
        carry_scr[j] = jnp.zeros(carry_scr.shape[1:], _F32)

    _convert(side)
    prev = carry_scr[j]
    for r in _slabs(tm, slab):
        h = h_ref[r, :]
        bg = _dot(h, wbg_ref[...])
        p = _dot(h, wcg_ref[...]) * _dot(h, wxb_ref[...])
        yb_ref[r, :] = (bg * _causal_conv_rows(p, prev, cw_ref)).astype(_BF16)
        prev = p[slab - SUBLANES:, :]
    carry_scr[j] = prev
    tail_ref[...] = prev


def _p_shortconv_call(h, w_bcx, cw, *, tm, tn, tiles_per_seq, slab, side):
    m, d = h.shape
    w = cw.shape[1]
    n_blk = w // tn
    wspec = lambda off: pl.BlockSpec((d, tn), lambda i, j: (0, off + j))
    (yb, tail), copies = _run(
        functools.partial(_p_shortconv_kernel, tm=tm, tiles_per_seq=tiles_per_seq, slab=slab),
        grid=(m // tm, n_blk),
        in_specs=[pl.BlockSpec((tm, d), lambda i, j: (i, 0)),
                  wspec(0), wspec(n_blk), wspec(2 * n_blk),
                  pl.BlockSpec((CONV_K, tn), lambda i, j: (0, j))],
        args=(h, w_bcx, w_bcx, w_bcx, cw),
        out_specs=[pl.BlockSpec((tm, tn), lambda i, j: (i, j)),
                   pl.BlockSpec((SUBLANES, tn), lambda i, j: (i, j))],
        out_shape=[jax.ShapeDtypeStruct((m, w), _BF16),
                   jax.ShapeDtypeStruct((m // tm * SUBLANES, w), _F32)],
        scratch=[pltpu.VMEM((n_blk, SUBLANES, tn), _F32)],
        side=side, name="shortconv_prompt")
    return yb, tail, copies


def _p_proj_kernel(m_ref, wo_ref, x_ref, gt_ref, gpost_ref, sh_ref, sc_ref, gpre_ref,
                   x1_ref, h2_ref, *, tm, slab, side):
    for r in _slabs(tm, slab):
        y = _dot(m_ref[r, :], wo_ref[...])
        x1, h2 = _post_mix(y, x_ref[r, :], gt_ref[...], gpost_ref[...], sh_ref[...], sc_ref[...],
                           gpre_ref[...])
        x1_ref[r, :] = x1
        h2_ref[r, :] = h2


def _p_proj_call(mg, w_o, x, mod, gpost, gpre, *, tm, slab, tiles_per_seq, mod_row0):
    m, d = x.shape
    row = pl.BlockSpec((tm, d), lambda i: (i, 0))
    vec = pl.BlockSpec((1, d), lambda i: (0, 0))
    mspec = lambda k: pl.BlockSpec((None, 1, d), lambda i: (mod_row0 + i // tiles_per_seq, 0, k))
    (x1, h2), _ = _run(
        functools.partial(_p_proj_kernel, tm=tm, slab=slab),
        grid=(m // tm,),
        in_specs=[row, pl.BlockSpec((d, d), lambda i: (0, 0)), row, mspec(2), vec, mspec(3),
                  mspec(4), vec],
        args=(mg, w_o, x, mod, gpost, mod, mod, gpre),
        out_specs=[row, row],
        out_shape=[jax.ShapeDtypeStruct((m, d), _F32), jax.ShapeDtypeStruct((m, d), _BF16)],
        name="out_proj_prompt")
    return x1, h2


def _p_ffn_kernel(h_ref, wa_ref, wb_ref, cw_ref, wd_ref, x1_hbm, gt_ref, gpost_ref,
                  out_ref, tail_ref, x1_buf, x1_sem, carry_scr, *, tm, n_blk, tiles_per_seq, slab,
                  side):
    i = pl.program_id(0)
    j = pl.program_id(1)
    slabs = _slabs(tm, slab)
    x1_copy = pltpu.make_async_copy(x1_hbm.at[pl.ds(pl.multiple_of(i * tm, tm), tm), :],
                                    x1_buf, x1_sem)

    @pl.when(j == 0)
    def _():
        x1_copy.start()
        out_ref[...] = jnp.zeros(out_ref.shape, _F32)

    @pl.when(i % tiles_per_seq == 0)
    def _():
        carry_scr[j] = jnp.zeros(carry_scr.shape[1:], _F32)

    prev = carry_scr[j]
    for r in slabs:
        h = h_ref[r, :]
        a = _dot(h, wa_ref[...])
        b = _dot(h, wb_ref[...])
        g = (jax.nn.gelu(_causal_conv_rows(a, prev, cw_ref)) * b).astype(_BF16)
        prev = a[slab - SUBLANES:, :]
        out_ref[r, :] += _dot(g, wd_ref[...])
    carry_scr[j] = prev
    tail_ref[...] = prev

    @pl.when(j == n_blk - 1)
    def _():
        x1_copy.wait()
        for r in slabs:
            out_ref[r, :] = x1_buf[r, :] + gt_ref[...] * _rms(out_ref[r, :], gpost_ref[...])


def _p_ffn_call(h2, w_a, w_b, cw, w_down, x1, mod, gpost, *, tm, tn, tiles_per_seq, slab, mod_row0):
    m, d = x1.shape
    f = cw.shape[1]
    n_blk = f // tn
    row = pl.BlockSpec((tm, d), lambda i, j: (i, 0))
    wspec = pl.BlockSpec((d, tn), lambda i, j: (0, j))
    (out, tail), _ = _run(
        functools.partial(_p_ffn_kernel, tm=tm, n_blk=n_blk, tiles_per_seq=tiles_per_seq, slab=slab),
        grid=(m // tm, n_blk),
        in_specs=[row, wspec, wspec,
                  pl.BlockSpec((CONV_K, tn), lambda i, j: (0, j)),
                  pl.BlockSpec((tn, d), lambda i, j: (j, 0)),
                  pl.BlockSpec(memory_space=pl.ANY),
                  pl.BlockSpec((None, 1, d), lambda i, j: (mod_row0 + i // tiles_per_seq, 0, 5)),
                  pl.BlockSpec((1, d), lambda i, j: (0, 0))],
        args=(h2, w_a, w_b, cw, w_down, x1, mod, gpost),
        out_specs=[row, pl.BlockSpec((SUBLANES, tn), lambda i, j: (i, j))],
        out_shape=[jax.ShapeDtypeStruct((m, d), _F32),
                   jax.ShapeDtypeStruct((m // tm * SUBLANES, f), _F32)],
        scratch=[pltpu.VMEM((tm, d), _F32), pltpu.SemaphoreType.DMA(()),
                 pltpu.VMEM((n_blk, SUBLANES, tn), _F32)],
        name="convffn_prompt")
    return out, tail


def _cols(src, first_col, n_cols, n_blocks):
    block = n_cols // n_blocks
    assert block * n_blocks == n_cols and block % 128 == 0 and first_col % block == 0
    return _Side(src, 1, first_col // block, block, n_blocks)


def _layer(xs, xp, mod, st_b, st_f, p, *, seq_len, mod_row0):
    d = p["g_v"].shape[1]
    wb = p["conv_b_w"].shape[1]
    f = p["conv_f_w"].shape[1]
    tn_s = SAMPLE_COL_TILE
    mod_rows = mod.reshape(mod.shape[0], 1, N_MOD * d)

    def tiles(name):
        tm, tn, slab = PROMPT_TILES[name]
        return dict(tm=tm, tn=tn, slab=slab, tiles_per_seq=seq_len // tm)

    def n_steps(name):
        tm, tn, _ = PROMPT_TILES[name]
        cols = {"gmlp": 2 * d, "shortconv": wb, "merge": d, "ffn": f}[name]
        return xp.shape[0] // tm * (cols // tn)

    def without(kw, *names):
        return {k: v for k, v in kw.items() if k not in names}

    def blocks(n_cols, budget):
        return max(n for n in range(1, budget + 1) if n_cols % (128 * n) == 0)

    w_in, w_up = p["w_in"], p["w_up"]
    w_uv = p["w_uv"]

    ya_s, h1_s, vn = _s_gmlp_call(xs, mod, p["g_pre_mix"], w_uv, p["g_v"], p["wvec"], p["bvec"],
                                  tn=tn_s)
    n = n_steps("gmlp")
    ya_p, h1_p, (w_bcx, w_a, w_b) = _p_gmlp_call(
        xp, mod_rows, p["g_pre_mix"], w_uv, p["g_v"], p["w_s"], p["bias"], mod_row0=mod_row0,
        side=(_cols(w_in, 2 * d, 3 * wb, blocks(3 * wb, n)), _cols(w_up, 0, f, blocks(f, n)),
              _cols(w_up, f, f, blocks(f, n))),
        **tiles("gmlp"))

    yb_s, tb0, tb1 = _s_shortconv_call(h1_s, w_bcx, p["conv_b_w"], st_b, tn=tn_s)
    n = n_steps("shortconv")
    yb_p, tail_b, (w_gate, w_oa, w_ob, w_o) = _p_shortconv_call(
        h1_p, w_bcx, p["conv_b_w"],
        side=(_cols(w_in, 2 * d + 3 * wb, 2 * d, blocks(2 * d, n)),
              _cols(p["w_out_a"], 0, d, blocks(d, n)), _cols(p["w_out_b"], 0, d, blocks(d, n)),
              _cols(p["w_o"], 0, d, blocks(d, n))),
        **tiles("shortconv"))

    tm_s = h1_s.shape[0]
    mg_s, _ = _merge_call(h1_s, ya_s, yb_s, w_gate, w_oa, w_ob, tm=tm_s, tn=tn_s, slab=tm_s // 2,
                          name="gated_merge_sample")
    n_row_blocks = max(k for k in range(1, n_steps("merge") + 1)
                       if f % k == 0 and (f // k) % (2 * SUBLANES) == 0)
    mg_p, (w_d,) = _merge_call(h1_p, ya_p, yb_p, w_gate, w_oa, w_ob, name="gated_merge_prompt",
                               side=(_Side(p["w_down"], 0, 0, f // n_row_blocks, n_row_blocks),),
                               **without(tiles("merge"), "tiles_per_seq"))

    x1_s, h2_s = _s_proj_call(mg_s, w_o, xs, mod, p["g_post_mix"], p["g_pre_ffn"], tn=tn_s)
    x1_p, h2_p = _p_proj_call(mg_p, w_o, xp, mod_rows, p["g_post_mix"], p["g_pre_ffn"],
                              mod_row0=mod_row0, **without(tiles("proj"), "tn"))

    out_s, tf0, tf1 = _s_ffn_call(h2_s, w_a, w_b, p["conv_f_w"], w_d, x1_s, mod, p["g_post_ffn"],
                                  st_f, tn=tn_s)
    out_p, tail_f = _p_ffn_call(h2_p, w_a, w_b, p["conv_f_w"], w_d, x1_p, mod_rows, p["g_post_ffn"],
                                mod_row0=mod_row0, **tiles("ffn"))
    return out_s, out_p, (tb0, tb1), (tf0, tf1), vn, tail_b, tail_f


def kernel(x_prompt, x_sample, c_prompt, c_sample, state_conv_b, state_conv_ffn, w_ada, b_ada, g_pre_mix, g_post_mix, w_in, g_v, w_s, b_s, conv_b_w, w_out_a, w_out_b, w_o, g_pre_ffn, g_post_ffn, w_up, conv_f_w, w_down):
    depth = w_in.shape[0]
    bp, seq, d = x_prompt.shape
    bs, tdec, _ = x_sample.shape
    n_groups = w_s.shape[1]
    assert bs == CHUNK and tdec <= CHUNK
    assert all(seq % tm == 0 for tm, _, _ in PROMPT_TILES.values())

    xp = x_prompt.reshape(bp * seq, d)
    xs = x_sample.reshape(bs, tdec * d)
    pad = (-(bp + bs)) % SUBLANES
    c_all = jnp.concatenate([c_sample, c_prompt, jnp.zeros((pad, d), _F32)], axis=0)

    pb, sb, pf, sf, sv = [], [], [], [], []
    for l in range(depth):
        n_ada = w_ada.shape[2] // ADALN_COL_TILE
        uv_blocks = max(n for n in range(1, n_ada + 1) if (2 * d) % (128 * n) == 0)
        mod, (w_uv,) = _mod_call(c_all, w_ada[l], b_ada[l][None, :],
                                 side=(_cols(w_in[l], 0, 2 * d, uv_blocks),))
        vec = lambda a: a[l][None, :]
        bias_full = jnp.repeat(jnp.transpose(b_s[l]), GROUP, axis=1)
        wvec = jnp.repeat(
            jnp.transpose(w_s[l][:, :tdec, :tdec], (1, 2, 0)).reshape(tdec * tdec, n_groups),
            GROUP, axis=1)
        p = {
            "w_in": w_in[l], "w_uv": w_uv, "w_out_a": w_out_a[l], "w_out_b": w_out_b[l],
            "w_o": w_o[l], "w_up": w_up[l], "w_down": w_down[l],
            "g_pre_mix": vec(g_pre_mix), "g_post_mix": vec(g_post_mix), "g_v": vec(g_v),
            "g_pre_ffn": vec(g_pre_ffn), "g_post_ffn": vec(g_post_ffn),
            "conv_b_w": conv_b_w[l], "conv_f_w": conv_f_w[l],
            "w_s": w_s[l], "bias": bias_full, "wvec": wvec, "bvec": bias_full[:tdec],
        }
        st_b = state_conv_b[l].reshape(bs, -1)
        st_f = state_conv_ffn[l].reshape(bs, -1)
        xs, xp, sbt, sft, vn, tb, tf = _layer(xs, xp, mod, st_b, st_f, p, seq_len=seq, mod_row0=bs)

        def prompt_tail(t, tm):
            t = t.reshape(bp, seq // tm, SUBLANES, -1)
            return t[:, -1, SUBLANES - (CONV_K - 1):, :]

        pb.append(prompt_tail(tb, PROMPT_TILES["shortconv"][0]))
        pf.append(prompt_tail(tf, PROMPT_TILES["ffn"][0]))
        sb.append(jnp.stack(sbt, axis=1))
        sf.append(jnp.stack(sft, axis=1))
        sv.append(vn.reshape(bs, tdec, d))

    y_prompt = xp.reshape(bp, seq, d)
    y_sample = xs.reshape(bs, tdec, d)
    return (y_prompt, y_sample, jnp.stack(pb), jnp.stack(sb), jnp.stack(pf), jnp.stack(sf),
            jnp.stack(sv))
```

```python
import functools
from typing import NamedTuple

import jax
import jax.numpy as jnp
from jax import lax
from jax.experimental import pallas as pl
from jax.experimental.pallas import tpu as pltpu

EPS = 1e-6
CHUNK = 128
GROUP = 128
CONV_K = 3
N_MOD = 6
SUBLANES = 8
VMEM_LIMIT_BYTES = 56 * 1024 * 1024
VMEM_LIMIT_BYTES_WIDE = 60 * 1024 * 1024
PROMPT_TILES = {
    "gmlp": (1024, 512, 256),
    "shortconv": (1024, 1024, 256),
    "merge": (512, 1024, 256),
    "proj": (512, None, 256),
    "ffn": (1024, 512, 256),
}
SAMPLE_COL_TILE = 512
ADALN_COL_TILE = 1024

_BF16 = jnp.bfloat16
_F32 = jnp.float32


def _dot(a, b):
    return jnp.dot(a, b, preferred_element_type=_F32)


def _rms(xf, g):
    ms = jnp.mean(xf * xf, axis=-1, keepdims=True)
    return xf * lax.rsqrt(ms + EPS) * g


def _causal_conv_rows(p, prev, cw_ref):
    rows = p.shape[0]
    row = lax.broadcasted_iota(jnp.int32, (rows, 1), 0)
    m1 = jnp.where(row == 0, prev[7:8, :], pltpu.roll(p, 1, 0))
    m2 = jnp.where(row == 0, prev[6:7, :], jnp.where(row == 1, prev[7:8, :], pltpu.roll(p, 2, 0)))
    return cw_ref[0:1, :] * m2 + cw_ref[1:2, :] * m1 + cw_ref[2:3, :] * p


def _causal_conv_slabs(x, prev, cw_ref, slab):
    w0, w1, w2 = cw_ref[0:1, :], cw_ref[1:2, :], cw_ref[2:3, :]
    seq = list(prev) + [x[t * slab:(t + 1) * slab, :] for t in range(x.shape[0] // slab)]
    y = [w0 * seq[t] + w1 * seq[t + 1] + w2 * seq[t + 2] for t in range(len(seq) - 2)]
    return jnp.concatenate(y, axis=0), seq[-2:]


def _slabs(tm, slab):
    return [slice(s * slab, (s + 1) * slab) for s in range(tm // slab)]


def _col_blocks(a, tn):
    r, c = a.shape
    return jnp.transpose(a.reshape(r, c // tn, tn), (1, 0, 2))


def _post_mix(y, x, gt, gpost, sh, sc, gpre):
    x1 = x + gt * _rms(y, gpost)
    h2 = _rms(x1, gpre) * (1.0 + sc) + sh
    return x1, h2.astype(_BF16)


class _Side(NamedTuple):
    src: jax.Array
    axis: int
    start: int
    block: int
    n_blocks: int


def _convert(side):
    for src_ref, dst_ref in side:
        dst_ref[...] = src_ref[...].astype(_BF16)


def _run(body, *, grid, in_specs, args, out_specs, out_shape, scratch=(), side=(), name,
         vmem_limit_bytes=VMEM_LIMIT_BYTES):
    n_in, n_out, n_side = len(args), len(out_shape), len(side)
    n_steps = functools.reduce(lambda a, b: a * b, grid)
    step_of = (lambda i: i) if len(grid) == 1 else (lambda i, j: i * grid[1] + j)
    side_in, side_out, side_shape = [], [], []
    for s in side:
        assert s.n_blocks <= n_steps, (name, s.n_blocks, n_steps)
        other = s.src.shape[1 - s.axis]
        pos = lambda *ids, s=s: jnp.minimum(step_of(*ids), s.n_blocks - 1)
        if s.axis == 1:
            blk, full = (other, s.block), (other, s.block * s.n_blocks)
            side_in.append(pl.BlockSpec(blk, lambda *ids, s=s, pos=pos: (0, s.start + pos(*ids))))
            side_out.append(pl.BlockSpec(blk, lambda *ids, pos=pos: (0, pos(*ids))))
        else:
            blk, full = (s.block, other), (s.block * s.n_blocks, other)
            side_in.append(pl.BlockSpec(blk, lambda *ids, s=s, pos=pos: (s.start + pos(*ids), 0)))
            side_out.append(pl.BlockSpec(blk, lambda *ids, pos=pos: (pos(*ids), 0)))
        side_shape.append(jax.ShapeDtypeStruct(full, _BF16))

    def kern(*refs):
        o0 = n_in + n_side
        s0 = o0 + n_out + n_side
        body(*refs[:n_in], *refs[o0:o0 + n_out], *refs[s0:],
             side=tuple(zip(refs[n_in:o0], refs[o0 + n_out:s0])))

    res = pl.pallas_call(
        kern, grid=grid,
        in_specs=list(in_specs) + side_in, out_specs=list(out_specs) + side_out,
        out_shape=list(out_shape) + side_shape, scratch_shapes=list(scratch),
        compiler_params=pltpu.CompilerParams(dimension_semantics=("arbitrary",) * len(grid),
                                             vmem_limit_bytes=vmem_limit_bytes),
        name=name,
    )(*args, *[s.src for s in side])
    return res[:n_out], res[n_out:]


def _mod_kernel(c_ref, w_ref, b_ref, o_ref, *, side):
    _convert(side)
    c = c_ref[...]
    a = (c * jax.nn.sigmoid(c)).astype(_BF16)
    o_ref[...] = _dot(a, w_ref[...].astype(_BF16)) + b_ref[...]


def _mod_call(c_all, w_ada, b_ada, *, side):
    rows, d = c_all.shape
    n = w_ada.shape[1]
    tn = ADALN_COL_TILE
    (mod,), copies = _run(
        _mod_kernel, grid=(n // tn,),
        in_specs=[pl.BlockSpec((rows, d), lambda j: (0, 0)),
                  pl.BlockSpec((d, tn), lambda j: (0, j)),
                  pl.BlockSpec((1, tn), lambda j: (0, j))],
        args=(c_all, w_ada, b_ada),
        out_specs=[pl.BlockSpec((rows, tn), lambda j: (0, j))],
        out_shape=[jax.ShapeDtypeStruct((rows, n), _F32)],
        side=side, name="adaln_mod")
    return mod, copies


def _s_gmlp_kernel(x_ref, sh_ref, sc_ref, gpre_ref, wv_ref, wu_ref, gv_ref, ws_ref, bias_ref,
                   ya_ref, h_ref, vn_ref, v_scr, *, tm, tn, n_blk, slab, side):
    j = pl.program_id(0)
    d = n_blk * tn
    slabs = _slabs(tm, slab)
    n_slab = len(slabs)

    @pl.when(j == 0)
    def _():
        for t, r in enumerate(slabs):
            h_ref[r, :] = (_rms(x_ref[:, t * d:(t + 1) * d], gpre_ref[...]) * (1.0 + sc_ref[...])
                           + sh_ref[...]).astype(_BF16)

    @pl.when(j < n_blk)
    def _():
        v_scr[j] = _dot(h_ref[...], wv_ref[...])

    @pl.when(j == n_blk)
    def _():
        for t, r in enumerate(slabs):
            ss = 0.0
            for k in range(n_blk):
                vk = v_scr[k, r, :]
                ss = ss + jnp.sum(vk * vk, axis=-1, keepdims=True)
            rs = lax.rsqrt(ss * (1.0 / d) + EPS)
            for k in range(n_blk):
                vn = v_scr[k, r, :] * rs * gv_ref[:, k * tn:(k + 1) * tn]
                v_scr[k, r, :] = vn
                vn_ref[:, t * d + k * tn:t * d + (k + 1) * tn] = vn
        for t in reversed(range(n_slab)):
            for k in range(n_blk):
                c = slice(k * tn, (k + 1) * tn)
                acc = ws_ref[t * n_slab:t * n_slab + 1, c] * v_scr[k, slabs[0], :]
                for s in range(1, t + 1):
                    acc = acc + ws_ref[t * n_slab + s:t * n_slab + s + 1, c] * v_scr[k, slabs[s], :]
                v_scr[k, slabs[t], :] = acc + bias_ref[t:t + 1, c]

    @pl.when(j >= n_blk)
    def _():
        ya_ref[...] = (_dot(h_ref[...], wu_ref[...]) * v_scr[j - n_blk]).astype(_BF16)


def _s_gmlp_call(x, mod, gpre, w_uv, gv, wvec, bvec, *, tn):
    slab = x.shape[0]
    d = gpre.shape[1]
    tm = x.shape[1] // d * slab
    n_blk = d // tn
    full = lambda a: pl.BlockSpec(a.shape, lambda j: (0,) * a.ndim)
    u_map = lambda j: (0, jnp.maximum(j - n_blk, 0))
    (ya, h, vn), _ = _run(
        functools.partial(_s_gmlp_kernel, tm=tm, tn=tn, n_blk=n_blk, slab=slab),
        grid=(2 * n_blk,),
        in_specs=[full(x),
                  pl.BlockSpec((slab, d), lambda j: (0, 0)), pl.BlockSpec((slab, d), lambda j: (0, 1)),
                  full(gpre),
                  pl.BlockSpec((d, tn), lambda j: (0, n_blk + jnp.minimum(j, n_blk - 1))),
                  pl.BlockSpec((d, tn), u_map),
                  full(gv), full(wvec), full(bvec)],
        args=(x, mod, mod, gpre, w_uv, w_uv, gv, wvec, bvec),
        out_specs=[pl.BlockSpec((tm, tn), u_map), pl.BlockSpec((tm, d), lambda j: (0, 0)), full(x)],
        out_shape=[jax.ShapeDtypeStruct((tm, d), _BF16), jax.ShapeDtypeStruct((tm, d), _BF16),
                   jax.ShapeDtypeStruct(x.shape, _F32)],
        scratch=[pltpu.VMEM((n_blk, tm, tn), _F32)],
        name="gmlp_sample")
    return ya, h, vn


def _s_shortconv_kernel(h_ref, wbg_ref, wcg_ref, wxb_ref, cw_ref, st0_ref, st1_ref,
                        yb_ref, t0_ref, t1_ref, *, slab, side):
    h = h_ref[...]
    bg = _dot(h, wbg_ref[...])
    p = _dot(h, wcg_ref[...]) * _dot(h, wxb_ref[...])
    cb, tail = _causal_conv_slabs(p, [st0_ref[...], st1_ref[...]], cw_ref, slab)
    yb_ref[...] = (bg * cb).astype(_BF16)
    t0_ref[...] = tail[0]
    t1_ref[...] = tail[1]


def _s_shortconv_call(h, w_bcx, cw, state, *, tn):
    tm, d = h.shape
    slab = state.shape[0]
    w = cw.shape[1]
    n_blk = w // tn
    wspec = lambda off: pl.BlockSpec((d, tn), lambda j: (0, off + j))
    sspec = lambda off: pl.BlockSpec((slab, tn), lambda j: (0, off + j))
    (yb, t0, t1), _ = _run(
        functools.partial(_s_shortconv_kernel, slab=slab),
        grid=(n_blk,),
        in_specs=[pl.BlockSpec((tm, d), lambda j: (0, 0)),
                  wspec(0), wspec(n_blk), wspec(2 * n_blk),
                  pl.BlockSpec((CONV_K, tn), lambda j: (0, j)),
                  sspec(0), sspec(n_blk)],
        args=(h, w_bcx, w_bcx, w_bcx, cw, state, state),
        out_specs=[pl.BlockSpec((tm, tn), lambda j: (0, j)), sspec(0), sspec(0)],
        out_shape=[jax.ShapeDtypeStruct((tm, w), _BF16)] + [jax.ShapeDtypeStruct((slab, w), _F32)] * 2,
        name="shortconv_sample")
    return yb, t0, t1


def _merge_kernel(h_ref, ya_ref, yb_ref, wga_ref, wgb_ref, woa_ref, wob_ref, m_ref, *, tm, slab, side):
    _convert(side)
    for r in _slabs(tm, slab):
        h = h_ref[r, :]
        ga = jax.nn.sigmoid(_dot(h, wga_ref[...]))
        gb = jax.nn.sigmoid(_dot(h, wgb_ref[...]))
        m = ga * _dot(ya_ref[r, :], woa_ref[...]) + gb * _dot(yb_ref[r, :], wob_ref[...])
        m_ref[r, :] = m.astype(_BF16)


def _merge_call(h, ya, yb, w_gate, w_out_a, w_out_b, *, tm, tn, slab, side=(), name):
    m, d = h.shape
    n_blk = d // tn
    row = pl.BlockSpec((tm, d), lambda i, j: (i, 0))
    wspec = lambda off: pl.BlockSpec((d, tn), lambda i, j: (0, off + j))
    (mg,), copies = _run(
        functools.partial(_merge_kernel, tm=tm, slab=slab),
        grid=(m // tm, n_blk),
        in_specs=[row, row, row, wspec(0), wspec(n_blk), wspec(0), wspec(0)],
        args=(h, ya, yb, w_gate, w_gate, w_out_a, w_out_b),
        out_specs=[pl.BlockSpec((tm, tn), lambda i, j: (i, j))],
        out_shape=[jax.ShapeDtypeStruct((m, d), _BF16)],
        side=side, name=name)
    return mg, copies


def _s_proj_kernel(m_ref, wo_ref, x_ref, gt_ref, gpost_ref, sh_ref, sc_ref, gpre_ref,
                   x1_ref, h2_ref, y_scr, *, tm, tn, n_blk, slab, side):
    j = pl.program_id(0)
    d = n_blk * tn
    y_scr[j] = _dot(m_ref[...], wo_ref[...])

    @pl.when(j == n_blk - 1)
    def _():
        for t, r in enumerate(_slabs(tm, slab)):
            y = jnp.concatenate([y_scr[k, r, :] for k in range(n_blk)], axis=-1)
            x1, h2 = _post_mix(y, x_ref[:, t * d:(t + 1) * d], gt_ref[...], gpost_ref[...],
                               sh_ref[...], sc_ref[...], gpre_ref[...])
            x1_ref[r, :] = x1
            h2_ref[r, :] = h2


def _s_proj_call(mg, w_o, x, mod, gpost, gpre, *, tn):
    tm, d = mg.shape
    slab = x.shape[0]
    n_blk = d // tn
    row = pl.BlockSpec((tm, d), lambda j: (0, 0))
    vec = pl.BlockSpec((1, d), lambda j: (0, 0))
    mspec = lambda k: pl.BlockSpec((slab, d), lambda j: (0, k))
    (x1, h2), _ = _run(
        functools.partial(_s_proj_kernel, tm=tm, tn=tn, n_blk=n_blk, slab=slab),
        grid=(n_blk,),
        in_specs=[row, pl.BlockSpec((d, tn), lambda j: (0, j)),
                  pl.BlockSpec(x.shape, lambda j: (0, 0)), mspec(2), vec, mspec(3), mspec(4), vec],
        args=(mg, w_o, x, mod, gpost, mod, mod, gpre),
        out_specs=[row, row],
        out_shape=[jax.ShapeDtypeStruct((tm, d), _F32), jax.ShapeDtypeStruct((tm, d), _BF16)],
        scratch=[pltpu.VMEM((n_blk, tm, tn), _F32)],
        name="out_proj_sample")
    return x1, h2


def _s_ffn_kernel(h_ref, wa_ref, wb_ref, cw_ref, wd_ref, x1_ref, gt_ref, gpost_ref, st0_ref, st1_ref,
                  out_ref, t0_ref, t1_ref, *, tm, n_blk, slab, side):
    j = pl.program_id(0)
    d = x1_ref.shape[1]
    lanes = [slice(t * d, (t + 1) * d) for t in range(tm // slab)]

    @pl.when(j == 0)
    def _():
        out_ref[...] = jnp.zeros(out_ref.shape, _F32)

    prev = [st0_ref[...], st1_ref[...]]
    for c, r in enumerate(_slabs(tm, 2 * slab)):
        h = h_ref[r, :]
        ac, prev = _causal_conv_slabs(_dot(h, wa_ref[...]), prev, cw_ref, slab)
        g = (jax.nn.gelu(ac) * _dot(h, wb_ref[...])).astype(_BF16)
        f = _dot(g, wd_ref[...])
        out_ref[:, lanes[2 * c]] += f[:slab, :]
        out_ref[:, lanes[2 * c + 1]] += f[slab:, :]
    t0_ref[...] = prev[0]
    t1_ref[...] = prev[1]

    @pl.when(j == n_blk - 1)
    def _():
        for t, r in enumerate(_slabs(tm, slab)):
            out_ref[:, lanes[t]] = (x1_ref[r, :]
                                    + gt_ref[...] * _rms(out_ref[:, lanes[t]], gpost_ref[...]))


def _s_ffn_call(h2, w_a, w_b, cw, w_down, x1, mod, gpost, state, *, tn):
    tm, d = x1.shape
    slab = state.shape[0]
    f = cw.shape[1]
    n_blk = f // tn
    row = pl.BlockSpec((tm, d), lambda j: (0, 0))
    wspec = pl.BlockSpec((d, tn), lambda j: (0, j))
    sspec = lambda off: pl.BlockSpec((slab, tn), lambda j: (0, off + j))
    (out, t0, t1), _ = _run(
        functools.partial(_s_ffn_kernel, tm=tm, n_blk=n_blk, slab=slab),
        grid=(n_blk,),
        in_specs=[row, wspec, wspec, pl.BlockSpec((CONV_K, tn), lambda j: (0, j)),
                  pl.BlockSpec((tn, d), lambda j: (j, 0)), row,
                  pl.BlockSpec((slab, d), lambda j: (0, 5)), pl.BlockSpec((1, d), lambda j: (0, 0)),
                  sspec(0), sspec(n_blk)],
        args=(h2, w_a, w_b, cw, w_down, x1, mod, gpost, state, state),
        out_specs=[pl.BlockSpec((slab, tm // slab * d), lambda j: (0, 0)), sspec(0), sspec(0)],
        out_shape=[jax.ShapeDtypeStruct((slab, tm // slab * d), _F32)]
        + [jax.ShapeDtypeStruct((slab, f), _F32)] * 2,
        name="convffn_sample")
    return out, t0, t1


def _p_gmlp_kernel(x_ref, sh_ref, sc_ref, gpre_ref, wv_ref, wu_ref, gv_ref, ws_ref, bias_ref,
                   ya_ref, h_ref, v_scr, wt_scr, *, tm, tn, n_blk, slab, side):
    i = pl.program_id(0)
    j = pl.program_id(1)
    d = n_blk * tn
    slabs = _slabs(tm, slab)

    @pl.when((i == 0) & (j == 0))
    def _():
        tril = (lax.broadcasted_iota(jnp.int32, (CHUNK, CHUNK), 0)
                >= lax.broadcasted_iota(jnp.int32, (CHUNK, CHUNK), 1))
        for g in range(d // GROUP):
            wt_scr[g] = jnp.where(tril, ws_ref[g], 0.0).astype(_BF16)

    @pl.when(j == 0)
    def _():
        _convert(side)
        for r in slabs:
            h = _rms(x_ref[r, :], gpre_ref[...]) * (1.0 + sc_ref[...]) + sh_ref[...]
            h = h.astype(_BF16)
            h_ref[r, :] = h
            v_scr[0, r, :] = _dot(h, wv_ref[...])

    @pl.when((j > 0) & (j < n_blk))
    def _():
        _convert(side)
        for r in slabs:
            v_scr[j, r, :] = _dot(h_ref[r, :], wv_ref[...])

    def _gate():
        gpb = tn // GROUP
        for c in range(tm // CHUNK):
            r = slice(c * CHUNK, (c + 1) * CHUNK)
            ss = 0.0
            for k in range(n_blk):
                vk = v_scr[k, r, :]
                ss = ss + jnp.sum(vk * vk, axis=-1, keepdims=True)
            rs = lax.rsqrt(ss * (1.0 / d) + EPS)
            for k in range(n_blk):
                vb = (v_scr[k, r, :] * rs * gv_ref[:, k * tn:(k + 1) * tn]).astype(_BF16)
                for gg in range(gpb):
                    g = k * gpb + gg
                    lanes = slice(gg * GROUP, (gg + 1) * GROUP)
                    v_scr[k, r, lanes] = (_dot(wt_scr[g], vb[:, lanes])
                                          + bias_ref[:, g * GROUP:(g + 1) * GROUP])

    @pl.when(j == n_blk)
    def _():
        _convert(side)
        u = [_dot(h_ref[r, :], wu_ref[...]) for r in slabs]
        _gate()
        for r, ur in zip(slabs, u):
            ya_ref[r, :] = (ur * v_scr[0, r, :]).astype(_BF16)

    @pl.when(j > n_blk)
    def _():
        _convert(side)
        for r in slabs:
            ya_ref[r, :] = (_dot(h_ref[r, :], wu_ref[...]) * v_scr[j - n_blk, r, :]).astype(_BF16)


def _p_gmlp_call(x, mod, gpre, w_uv, gv, ws, bias, *, tm, tn, slab, tiles_per_seq, mod_row0, side):
    m, d = x.shape
    n_blk = d // tn
    full = lambda a: pl.BlockSpec(a.shape, lambda i, j: (0,) * a.ndim)
    mspec = lambda k: pl.BlockSpec((None, 1, d), lambda i, j: (mod_row0 + i // tiles_per_seq, 0, k))
    u_map = lambda i, j: (i, jnp.maximum(j - n_blk, 0))
    (ya, h), copies = _run(
        functools.partial(_p_gmlp_kernel, tm=tm, tn=tn, n_blk=n_blk, slab=slab),
        grid=(m // tm, 2 * n_blk),
        in_specs=[pl.BlockSpec((tm, d), lambda i, j: (i, 0)), mspec(0), mspec(1), full(gpre),
                  pl.BlockSpec((d, tn), lambda i, j: (0, n_blk + jnp.minimum(j, n_blk - 1))),
                  pl.BlockSpec((d, tn), lambda i, j: (0, jnp.maximum(j - n_blk, 0))),
                  full(gv), full(ws), full(bias)],
        args=(x, mod, mod, gpre, w_uv, w_uv, gv, ws, bias),
        out_specs=[pl.BlockSpec((tm, tn), u_map), pl.BlockSpec((tm, d), lambda i, j: (i, 0))],
        out_shape=[jax.ShapeDtypeStruct((m, d), _BF16), jax.ShapeDtypeStruct((m, d), _BF16)],
        scratch=[pltpu.VMEM((n_blk, tm, tn), _F32), pltpu.VMEM((d // GROUP, CHUNK, CHUNK), _BF16)],
        side=side, name="gmlp_prompt", vmem_limit_bytes=VMEM_LIMIT_BYTES_WIDE)
    return ya, h, copies


def _p_shortconv_kernel(h_ref, wbg_ref, wcg_ref, wxb_ref, cw_ref, yb_ref, tail_ref, carry_scr,
                        *, tm, tiles_per_seq, slab, side):
    i = pl.program_id(0)
    j = pl.program_id(1)

    @pl.when(i % tiles_per_seq == 0)
    def _():
        carry_scr[j] = jnp.zeros(carry_scr.shape[1:], _F32)

    _convert(side)
    prev = carry_scr[j]
    cw = cw_ref.at[j]
    for r in _slabs(tm, slab):
        h = h_ref[r, :]
        bg = _dot(h, wbg_ref[...])
        p = _dot(h, wcg_ref[...]) * _dot(h, wxb_ref[...])
        yb_ref[r, :] = (bg * _causal_conv_rows(p, prev, cw)).astype(_BF16)
        prev = p[slab - SUBLANES:, :]
    carry_scr[j] = prev
    tail_ref[j] = prev


def _p_shortconv_call(h, w_bcx, cw, *, tm, tn, tiles_per_seq, slab, side):
    m, d = h.shape
    w = cw.shape[1]
    n_blk = w // tn
    wspec = lambda off: pl.BlockSpec((d, tn), lambda i, j: (0, off + j))
    (yb, tail), copies = _run(
        functools.partial(_p_shortconv_kernel, tm=tm, tiles_per_seq=tiles_per_seq, slab=slab),
        grid=(m // tm, n_blk),
        in_specs=[pl.BlockSpec((tm, d), lambda i, j: (i, 0)),
                  wspec(0), wspec(n_blk), wspec(2 * n_blk),
                  pl.BlockSpec((n_blk, CONV_K, tn), lambda i, j: (0, 0, 0))],
        args=(h, w_bcx, w_bcx, w_bcx, _col_blocks(cw, tn)),
        out_specs=[pl.BlockSpec((tm, tn), lambda i, j: (i, j)),
                   pl.BlockSpec((None, n_blk, SUBLANES, tn), lambda i, j: (i, 0, 0, 0))],
        out_shape=[jax.ShapeDtypeStruct((m, w), _BF16),
                   jax.ShapeDtypeStruct((m // tm, n_blk, SUBLANES, tn), _F32)],
        scratch=[pltpu.VMEM((n_blk, SUBLANES, tn), _F32)],
        side=side, name="shortconv_prompt")
    return yb, tail, copies


def _p_proj_kernel(m_ref, wo_ref, x_ref, gt_ref, gpost_ref, sh_ref, sc_ref, gpre_ref,
                   x1_ref, h2_ref, *, tm, slab, side):
    for r in _slabs(tm, slab):
        y = _dot(m_ref[r, :], wo_ref[...])
        x1, h2 = _post_mix(y, x_ref[r, :], gt_ref[...], gpost_ref[...], sh_ref[...], sc_ref[...],
                           gpre_ref[...])
        x1_ref[r, :] = x1
        h2_ref[r, :] = h2


def _p_proj_call(mg, w_o, x, mod, gpost, gpre, *, tm, slab, tiles_per_seq, mod_row0):
    m, d = x.shape
    row = pl.BlockSpec((tm, d), lambda i: (i, 0))
    vec = pl.BlockSpec((1, d), lambda i: (0, 0))
    mspec = lambda k: pl.BlockSpec((None, 1, d), lambda i: (mod_row0 + i // tiles_per_seq, 0, k))
    (x1, h2), _ = _run(
        functools.partial(_p_proj_kernel, tm=tm, slab=slab),
        grid=(m // tm,),
        in_specs=[row, pl.BlockSpec((d, d), lambda i: (0, 0)), row, mspec(2), vec, mspec(3),
                  mspec(4), vec],
        args=(mg, w_o, x, mod, gpost, mod, mod, gpre),
        out_specs=[row, row],
        out_shape=[jax.ShapeDtypeStruct((m, d), _F32), jax.ShapeDtypeStruct((m, d), _BF16)],
        name="out_proj_prompt")
    return x1, h2


def _p_ffn_kernel(h_ref, wa_ref, wb_ref, cw_ref, wd_ref, x1_hbm, gt_ref, gpost_ref,
                  out_ref, tail_ref, x1_buf, x1_sem, carry_scr, *, tm, n_blk, tiles_per_seq, slab,
                  side):
    i = pl.program_id(0)
    j = pl.program_id(1)
    x1_copy = pltpu.make_async_copy(x1_hbm.at[pl.ds(pl.multiple_of(i * tm, tm), tm), :],
                                    x1_buf, x1_sem)

    @pl.when(i % tiles_per_seq == 0)
    def _():
        carry_scr[j] = jnp.zeros(carry_scr.shape[1:], _F32)

    def step(first, last):
        prev = carry_scr[j]
        cw = cw_ref.at[j]
        for r in _slabs(tm, slab):
            h = h_ref[r, :]
            a = _dot(h, wa_ref[...])
            b = _dot(h, wb_ref[...])
            g = (jax.nn.gelu(_causal_conv_rows(a, prev, cw)) * b).astype(_BF16)
            prev = a[slab - SUBLANES:, :]
            f = _dot(g, wd_ref[...])
            acc = f if first else out_ref[r, :] + f
            if last:
                acc = x1_buf[r, :] + gt_ref[...] * _rms(acc, gpost_ref[...])
            out_ref[r, :] = acc
        carry_scr[j] = prev
        tail_ref[j] = prev

    @pl.when(j == 0)
    def _():
        x1_copy.start()
        step(True, False)

    @pl.when((j > 0) & (j < n_blk - 1))
    def _():
        step(False, False)

    @pl.when(j == n_blk - 1)
    def _():
        x1_copy.wait()
        step(False, True)


def _p_ffn_call(h2, w_a, w_b, cw, w_down, x1, mod, gpost, *, tm, tn, tiles_per_seq, slab, mod_row0):
    m, d = x1.shape
    f = cw.shape[1]
    n_blk = f // tn
    row = pl.BlockSpec((tm, d), lambda i, j: (i, 0))
    wspec = pl.BlockSpec((d, tn), lambda i, j: (0, j))
    (out, tail), _ = _run(
        functools.partial(_p_ffn_kernel, tm=tm, n_blk=n_blk, tiles_per_seq=tiles_per_seq, slab=slab),
        grid=(m // tm, n_blk),
        in_specs=[row, wspec, wspec,
                  pl.BlockSpec((n_blk, CONV_K, tn), lambda i, j: (0, 0, 0)),
                  pl.BlockSpec((tn, d), lambda i, j: (j, 0)),
                  pl.BlockSpec(memory_space=pl.ANY),
                  pl.BlockSpec((None, 1, d), lambda i, j: (mod_row0 + i // tiles_per_seq, 0, 5)),
                  pl.BlockSpec((1, d), lambda i, j: (0, 0))],
        args=(h2, w_a, w_b, _col_blocks(cw, tn), w_down, x1, mod, gpost),
        out_specs=[row, pl.BlockSpec((None, n_blk, SUBLANES, tn), lambda i, j: (i, 0, 0, 0))],
        out_shape=[jax.ShapeDtypeStruct((m, d), _F32),
                   jax.ShapeDtypeStruct((m // tm, n_blk, SUBLANES, tn), _F32)],
        scratch=[pltpu.VMEM((tm, d), _F32), pltpu.SemaphoreType.DMA(()),
                 pltpu.VMEM((n_blk, SUBLANES, tn), _F32)],
        name="convffn_prompt")
    return out, tail


def _cols(src, first_col, n_cols, n_blocks):
    block = n_cols // n_blocks
    assert block * n_blocks == n_cols and block % 128 == 0 and first_col % block == 0
    return _Side(src, 1, first_col // block, block, n_blocks)


def _layer(xs, xp, mod, st_b, st_f, p, *, seq_len, mod_row0):
    d = p["g_v"].shape[1]
    wb = p["conv_b_w"].shape[1]
    f = p["conv_f_w"].shape[1]
    tn_s = SAMPLE_COL_TILE
    mod_rows = mod.reshape(mod.shape[0], 1, N_MOD * d)

    def tiles(name):
        tm, tn, slab = PROMPT_TILES[name]
        return dict(tm=tm, tn=tn, slab=slab, tiles_per_seq=seq_len // tm)

    def n_steps(name):
        tm, tn, _ = PROMPT_TILES[name]
        cols = {"gmlp": 2 * d, "shortconv": wb, "merge": d, "ffn": f}[name]
        return xp.shape[0] // tm * (cols // tn)

    def without(kw, *names):
        return {k: v for k, v in kw.items() if k not in names}

    def blocks(n_cols, budget):
        return max(n for n in range(1, budget + 1) if n_cols % (128 * n) == 0)

    w_in, w_up = p["w_in"], p["w_up"]
    w_uv = p["w_uv"]

    ya_s, h1_s, vn = _s_gmlp_call(xs, mod, p["g_pre_mix"], w_uv, p["g_v"], p["wvec"], p["bvec"],
                                  tn=tn_s)
    n = n_steps("gmlp")
    ya_p, h1_p, (w_bcx, w_a, w_b) = _p_gmlp_call(
        xp, mod_rows, p["g_pre_mix"], w_uv, p["g_v"], p["w_s"], p["bias"], mod_row0=mod_row0,
        side=(_cols(w_in, 2 * d, 3 * wb, blocks(3 * wb, n)), _cols(w_up, 0, f, blocks(f, n)),
              _cols(w_up, f, f, blocks(f, n))),
        **tiles("gmlp"))

    yb_s, tb0, tb1 = _s_shortconv_call(h1_s, w_bcx, p["conv_b_w"], st_b, tn=tn_s)
    n = n_steps("shortconv")
    yb_p, tail_b, (w_gate, w_oa, w_ob, w_o) = _p_shortconv_call(
        h1_p, w_bcx, p["conv_b_w"],
        side=(_cols(w_in, 2 * d + 3 * wb, 2 * d, blocks(2 * d, n)),
              _cols(p["w_out_a"], 0, d, blocks(d, n)), _cols(p["w_out_b"], 0, d, blocks(d, n)),
              _cols(p["w_o"], 0, d, blocks(d, n))),
        **tiles("shortconv"))

    tm_s = h1_s.shape[0]
    mg_s, _ = _merge_call(h1_s, ya_s, yb_s, w_gate, w_oa, w_ob, tm=tm_s, tn=tn_s, slab=tm_s // 2,
                          name="gated_merge_sample")
    n_row_blocks = max(k for k in range(1, n_steps("merge") + 1)
                       if f % k == 0 and (f // k) % (2 * SUBLANES) == 0)
    mg_p, (w_d,) = _merge_call(h1_p, ya_p, yb_p, w_gate, w_oa, w_ob, name="gated_merge_prompt",
                               side=(_Side(p["w_down"], 0, 0, f // n_row_blocks, n_row_blocks),),
                               **without(tiles("merge"), "tiles_per_seq"))

    x1_s, h2_s = _s_proj_call(mg_s, w_o, xs, mod, p["g_post_mix"], p["g_pre_ffn"], tn=tn_s)
    x1_p, h2_p = _p_proj_call(mg_p, w_o, xp, mod_rows, p["g_post_mix"], p["g_pre_ffn"],
                              mod_row0=mod_row0, **without(tiles("proj"), "tn"))

    out_s, tf0, tf1 = _s_ffn_call(h2_s, w_a, w_b, p["conv_f_w"], w_d, x1_s, mod, p["g_post_ffn"],
                                  st_f, tn=tn_s)
    out_p, tail_f = _p_ffn_call(h2_p, w_a, w_b, p["conv_f_w"], w_d, x1_p, mod_rows, p["g_post_ffn"],
                                mod_row0=mod_row0, **tiles("ffn"))
    return out_s, out_p, (tb0, tb1), (tf0, tf1), vn, tail_b, tail_f


def kernel(x_prompt, x_sample, c_prompt, c_sample, state_conv_b, state_conv_ffn, w_ada, b_ada, g_pre_mix, g_post_mix, w_in, g_v, w_s, b_s, conv_b_w, w_out_a, w_out_b, w_o, g_pre_ffn, g_post_ffn, w_up, conv_f_w, w_down):
    depth = w_in.shape[0]
    bp, seq, d = x_prompt.shape
    bs, tdec, _ = x_sample.shape
    n_groups = w_s.shape[1]
    assert bs == CHUNK and tdec <= CHUNK
    assert all(seq % tm == 0 for tm, _, _ in PROMPT_TILES.values())

    xp = x_prompt.reshape(bp * seq, d)
    xs = x_sample.reshape(bs, tdec * d)
    pad = (-(bp + bs)) % SUBLANES
    c_all = jnp.concatenate([c_sample, c_prompt, jnp.zeros((pad, d), _F32)], axis=0)

    pb, sb, pf, sf, sv = [], [], [], [], []
    for l in range(depth):
        n_ada = w_ada.shape[2] // ADALN_COL_TILE
        uv_blocks = max(n for n in range(1, n_ada + 1) if (2 * d) % (128 * n) == 0)
        mod, (w_uv,) = _mod_call(c_all, w_ada[l], b_ada[l][None, :],
                                 side=(_cols(w_in[l], 0, 2 * d, uv_blocks),))
        vec = lambda a: a[l][None, :]
        bias_full = jnp.repeat(jnp.transpose(b_s[l]), GROUP, axis=1)
        wvec = jnp.repeat(
            jnp.transpose(w_s[l][:, :tdec, :tdec], (1, 2, 0)).reshape(tdec * tdec, n_groups),
            GROUP, axis=1)
        p = {
            "w_in": w_in[l], "w_uv": w_uv, "w_out_a": w_out_a[l], "w_out_b": w_out_b[l],
            "w_o": w_o[l], "w_up": w_up[l], "w_down": w_down[l],
            "g_pre_mix": vec(g_pre_mix), "g_post_mix": vec(g_post_mix), "g_v": vec(g_v),
            "g_pre_ffn": vec(g_pre_ffn), "g_post_ffn": vec(g_post_ffn),
            "conv_b_w": conv_b_w[l], "conv_f_w": conv_f_w[l],
            "w_s": w_s[l], "bias": bias_full, "wvec": wvec, "bvec": bias_full[:tdec],
        }
        st_b = state_conv_b[l].reshape(bs, -1)
        st_f = state_conv_ffn[l].reshape(bs, -1)
        xs, xp, sbt, sft, vn, tb, tf = _layer(xs, xp, mod, st_b, st_f, p, seq_len=seq, mod_row0=bs)

        def prompt_tail(t):
            n_tiles, n_blk, _, tn = t.shape
            t = t.reshape(bp, n_tiles // bp, n_blk, SUBLANES, tn)[:, -1, :, SUBLANES - (CONV_K - 1):, :]
            return jnp.transpose(t, (0, 2, 1, 3)).reshape(bp, CONV_K - 1, n_blk * tn)

        pb.append(prompt_tail(tb))
        pf.append(prompt_tail(tf))
        sb.append(jnp.stack(sbt, axis=1))
        sf.append(jnp.stack(sft, axis=1))
        sv.append(vn.reshape(bs, tdec, d))

    y_prompt = xp.reshape(bp, seq, d)
    y_sample = xs.reshape(bs, tdec, d)
    return (y_prompt, y_sample, jnp.stack(pb), jnp.stack(sb), jnp.stack(pf), jnp.stack(sf),
            jnp.stack(sv))
```

```python
import functools
from typing import NamedTuple

import jax
import jax.numpy as jnp
from jax import lax
from jax.experimental import pallas as pl
from jax.experimental.pallas import tpu as pltpu

EPS = 1e-6
CHUNK = 128
GROUP = 128
CONV_K = 3
N_MOD = 6
SUBLANES = 8
VMEM_LIMIT_BYTES = 56 * 1024 * 1024
PROMPT_TILES = {
    "gmlp": (1024, 512, 256),
    "shortconv": (1024, 1024, 256),
    "merge": (512, 1024, 256),
    "proj": (512, None, 256),
    "ffn": (1024, 512, 256),
}
SAMPLE_COL_TILE = 512
ADALN_COL_TILE = 1024

_BF16 = jnp.bfloat16
_F32 = jnp.float32


def _dot(a, b):
    return jnp.dot(a, b, preferred_element_type=_F32)


def _rms(xf, g):
    ms = jnp.mean(xf * xf, axis=-1, keepdims=True)
    return xf * lax.rsqrt(ms + EPS) * g


def _causal_conv_rows(p, prev, cw_ref):
    rows = p.shape[0]
    row = lax.broadcasted_iota(jnp.int32, (rows, 1), 0)
    m1 = jnp.where(row == 0, prev[7:8, :], pltpu.roll(p, 1, 0))
    m2 = jnp.where(row == 0, prev[6:7, :], jnp.where(row == 1, prev[7:8, :], pltpu.roll(p, 2, 0)))
    return cw_ref[0:1, :] * m2 + cw_ref[1:2, :] * m1 + cw_ref[2:3, :] * p


def _causal_conv_slabs(x, prev, cw_ref, slab):
    w0, w1, w2 = cw_ref[0:1, :], cw_ref[1:2, :], cw_ref[2:3, :]
    seq = list(prev) + [x[t * slab:(t + 1) * slab, :] for t in range(x.shape[0] // slab)]
    y = [w0 * seq[t] + w1 * seq[t + 1] + w2 * seq[t + 2] for t in range(len(seq) - 2)]
    return jnp.concatenate(y, axis=0), seq[-2:]


def _slabs(tm, slab):
    return [slice(s * slab, (s + 1) * slab) for s in range(tm // slab)]


def _col_blocks(a, tn):
    r, c = a.shape
    return jnp.transpose(a.reshape(r, c // tn, tn), (1, 0, 2))


def _post_mix(y, x, gt, gpost, sh, sc, gpre):
    x1 = x + gt * _rms(y, gpost)
    h2 = _rms(x1, gpre) * (1.0 + sc) + sh
    return x1, h2.astype(_BF16)


class _Side(NamedTuple):
    src: jax.Array
    axis: int
    start: int
    block: int
    n_blocks: int


def _convert(side):
    for src_ref, dst_ref in side:
        dst_ref[...] = src_ref[...].astype(_BF16)


def _run(body, *, grid, in_specs, args, out_specs, out_shape, scratch=(), side=(), name,
         vmem_limit_bytes=VMEM_LIMIT_BYTES):
    n_in, n_out, n_side = len(args), len(out_shape), len(side)
    n_steps = functools.reduce(lambda a, b: a * b, grid)
    step_of = (lambda i: i) if len(grid) == 1 else (lambda i, j: i * grid[1] + j)
    side_in, side_out, side_shape = [], [], []
    for s in side:
        assert s.n_blocks <= n_steps, (name, s.n_blocks, n_steps)
        other = s.src.shape[1 - s.axis]
        pos = lambda *ids, s=s: jnp.minimum(step_of(*ids), s.n_blocks - 1)
        if s.axis == 1:
            blk, full = (other, s.block), (other, s.block * s.n_blocks)
            side_in.append(pl.BlockSpec(blk, lambda *ids, s=s, pos=pos: (0, s.start + pos(*ids))))
            side_out.append(pl.BlockSpec(blk, lambda *ids, pos=pos: (0, pos(*ids))))
        else:
            blk, full = (s.block, other), (s.block * s.n_blocks, other)
            side_in.append(pl.BlockSpec(blk, lambda *ids, s=s, pos=pos: (s.start + pos(*ids), 0)))
            side_out.append(pl.BlockSpec(blk, lambda *ids, pos=pos: (pos(*ids), 0)))
        side_shape.append(jax.ShapeDtypeStruct(full, _BF16))

    def kern(*refs):
        o0 = n_in + n_side
        s0 = o0 + n_out + n_side
        body(*refs[:n_in], *refs[o0:o0 + n_out], *refs[s0:],
             side=tuple(zip(refs[n_in:o0], refs[o0 + n_out:s0])))

    res = pl.pallas_call(
        kern, grid=grid,
        in_specs=list(in_specs) + side_in, out_specs=list(out_specs) + side_out,
        out_shape=list(out_shape) + side_shape, scratch_shapes=list(scratch),
        compiler_params=pltpu.CompilerParams(dimension_semantics=("arbitrary",) * len(grid),
                                             vmem_limit_bytes=vmem_limit_bytes),
        name=name,
    )(*args, *[s.src for s in side])
    return res[:n_out], res[n_out:]


def _mod_kernel(c_ref, w_ref, b_ref, o_ref, *, side):
    _convert(side)
    c = c_ref[...]
    a = (c * jax.nn.sigmoid(c)).astype(_BF16)
    o_ref[...] = _dot(a, w_ref[...].astype(_BF16)) + b_ref[...]


def _mod_call(c_all, w_ada, b_ada, *, side):
    rows, d = c_all.shape
    n = w_ada.shape[1]
    tn = ADALN_COL_TILE
    (mod,), copies = _run(
        _mod_kernel, grid=(n // tn,),
        in_specs=[pl.BlockSpec((rows, d), lambda j: (0, 0)),
                  pl.BlockSpec((d, tn), lambda j: (0, j)),
                  pl.BlockSpec((1, tn), lambda j: (0, j))],
        args=(c_all, w_ada, b_ada),
        out_specs=[pl.BlockSpec((rows, tn), lambda j: (0, j))],
        out_shape=[jax.ShapeDtypeStruct((rows, n), _F32)],
        side=side, name="adaln_mod")
    return mod, copies


def _s_gmlp_kernel(x_ref, sh_ref, sc_ref, gpre_ref, wv_ref, wu_ref, gv_ref, ws_ref, bias_ref,
                   ya_ref, h_ref, vn_ref, v_scr, *, tm, tn, n_blk, slab, side):
    j = pl.program_id(0)
    d = n_blk * tn
    slabs = _slabs(tm, slab)
    n_slab = len(slabs)

    @pl.when(j == 0)
    def _():
        for t, r in enumerate(slabs):
            h_ref[r, :] = (_rms(x_ref[:, t * d:(t + 1) * d], gpre_ref[...]) * (1.0 + sc_ref[...])
                           + sh_ref[...]).astype(_BF16)

    @pl.when(j < n_blk)
    def _():
        v_scr[j] = _dot(h_ref[...], wv_ref[...])

    @pl.when(j == n_blk)
    def _():
        for t, r in enumerate(slabs):
            ss = 0.0
            for k in range(n_blk):
                vk = v_scr[k, r, :]
                ss = ss + jnp.sum(vk * vk, axis=-1, keepdims=True)
            rs = lax.rsqrt(ss * (1.0 / d) + EPS)
            for k in range(n_blk):
                vn = v_scr[k, r, :] * rs * gv_ref[:, k * tn:(k + 1) * tn]
                v_scr[k, r, :] = vn
                vn_ref[:, t * d + k * tn:t * d + (k + 1) * tn] = vn
        for t in reversed(range(n_slab)):
            for k in range(n_blk):
                c = slice(k * tn, (k + 1) * tn)
                acc = ws_ref[t * n_slab:t * n_slab + 1, c] * v_scr[k, slabs[0], :]
                for s in range(1, t + 1):
                    acc = acc + ws_ref[t * n_slab + s:t * n_slab + s + 1, c] * v_scr[k, slabs[s], :]
                v_scr[k, slabs[t], :] = acc + bias_ref[t:t + 1, c]

    @pl.when(j >= n_blk)
    def _():
        ya_ref[...] = (_dot(h_ref[...], wu_ref[...]) * v_scr[j - n_blk]).astype(_BF16)


def _s_gmlp_call(x, mod, gpre, w_uv, gv, wvec, bvec, *, tn):
    slab = x.shape[0]
    d = gpre.shape[1]
    tm = x.shape[1] // d * slab
    n_blk = d // tn
    full = lambda a: pl.BlockSpec(a.shape, lambda j: (0,) * a.ndim)
    u_map = lambda j: (0, jnp.maximum(j - n_blk, 0))
    (ya, h, vn), _ = _run(
        functools.partial(_s_gmlp_kernel, tm=tm, tn=tn, n_blk=n_blk, slab=slab),
        grid=(2 * n_blk,),
        in_specs=[full(x),
                  pl.BlockSpec((slab, d), lambda j: (0, 0)), pl.BlockSpec((slab, d), lambda j: (0, 1)),
                  full(gpre),
                  pl.BlockSpec((d, tn), lambda j: (0, n_blk + jnp.minimum(j, n_blk - 1))),
                  pl.BlockSpec((d, tn), u_map),
                  full(gv), full(wvec), full(bvec)],
        args=(x, mod, mod, gpre, w_uv, w_uv, gv, wvec, bvec),
        out_specs=[pl.BlockSpec((tm, tn), u_map), pl.BlockSpec((tm, d), lambda j: (0, 0)), full(x)],
        out_shape=[jax.ShapeDtypeStruct((tm, d), _BF16), jax.ShapeDtypeStruct((tm, d), _BF16),
                   jax.ShapeDtypeStruct(x.shape, _F32)],
        scratch=[pltpu.VMEM((n_blk, tm, tn), _F32)],
        name="gmlp_sample")
    return ya, h, vn


def _s_shortconv_kernel(h_ref, wbg_ref, wcg_ref, wxb_ref, cw_ref, st0_ref, st1_ref,
                        yb_ref, t0_ref, t1_ref, *, slab, side):
    h = h_ref[...]
    bg = _dot(h, wbg_ref[...])
    p = _dot(h, wcg_ref[...]) * _dot(h, wxb_ref[...])
    cb, tail = _causal_conv_slabs(p, [st0_ref[...], st1_ref[...]], cw_ref, slab)
    yb_ref[...] = (bg * cb).astype(_BF16)
    t0_ref[...] = tail[0]
    t1_ref[...] = tail[1]


def _s_shortconv_call(h, w_bcx, cw, state, *, tn):
    tm, d = h.shape
    slab = state.shape[0]
    w = cw.shape[1]
    n_blk = w // tn
    wspec = lambda off: pl.BlockSpec((d, tn), lambda j: (0, off + j))
    sspec = lambda k: pl.BlockSpec((slab, tn), lambda j: (0, k * n_blk + j))
    tspec = pl.BlockSpec((slab, tn), lambda j: (0, j))
    (yb, t0, t1), _ = _run(
        functools.partial(_s_shortconv_kernel, slab=slab),
        grid=(n_blk,),
        in_specs=[pl.BlockSpec((tm, d), lambda j: (0, 0)),
                  wspec(0), wspec(n_blk), wspec(2 * n_blk),
                  pl.BlockSpec((CONV_K, tn), lambda j: (0, j)),
                  sspec(0), sspec(1)],
        args=(h, w_bcx, w_bcx, w_bcx, cw, state, state),
        out_specs=[pl.BlockSpec((tm, tn), lambda j: (0, j)), tspec, tspec],
        out_shape=[jax.ShapeDtypeStruct((tm, w), _BF16)] + [jax.ShapeDtypeStruct((slab, w), _F32)] * 2,
        name="shortconv_sample")
    return yb, t0, t1


def _merge_kernel(h_ref, ya_ref, yb_ref, wga_ref, wgb_ref, woa_ref, wob_ref, m_ref, *, tm, slab, side):
    _convert(side)
    for r in _slabs(tm, slab):
        h = h_ref[r, :]
        ga = jax.nn.sigmoid(_dot(h, wga_ref[...]))
        gb = jax.nn.sigmoid(_dot(h, wgb_ref[...]))
        m = ga * _dot(ya_ref[r, :], woa_ref[...]) + gb * _dot(yb_ref[r, :], wob_ref[...])
        m_ref[r, :] = m.astype(_BF16)


def _merge_call(h, ya, yb, w_gate, w_out_a, w_out_b, *, tm, tn, slab, side=(), name):
    m, d = h.shape
    n_blk = d // tn
    row = pl.BlockSpec((tm, d), lambda i, j: (i, 0))
    wspec = lambda off: pl.BlockSpec((d, tn), lambda i, j: (0, off + j))
    (mg,), copies = _run(
        functools.partial(_merge_kernel, tm=tm, slab=slab),
        grid=(m // tm, n_blk),
        in_specs=[row, row, row, wspec(0), wspec(n_blk), wspec(0), wspec(0)],
        args=(h, ya, yb, w_gate, w_gate, w_out_a, w_out_b),
        out_specs=[pl.BlockSpec((tm, tn), lambda i, j: (i, j))],
        out_shape=[jax.ShapeDtypeStruct((m, d), _BF16)],
        side=side, name=name)
    return mg, copies


def _s_proj_kernel(m_ref, wo_ref, x_ref, gt_ref, gpost_ref, sh_ref, sc_ref, gpre_ref,
                   x1_ref, h2_ref, y_scr, *, tm, tn, n_blk, slab, side):
    j = pl.program_id(0)
    d = n_blk * tn
    y_scr[j] = _dot(m_ref[...], wo_ref[...])

    @pl.when(j == n_blk - 1)
    def _():
        for t, r in enumerate(_slabs(tm, slab)):
            y = jnp.concatenate([y_scr[k, r, :] for k in range(n_blk)], axis=-1)
            x1, h2 = _post_mix(y, x_ref[:, t * d:(t + 1) * d], gt_ref[...], gpost_ref[...],
                               sh_ref[...], sc_ref[...], gpre_ref[...])
            x1_ref[r, :] = x1
            h2_ref[r, :] = h2


def _s_proj_call(mg, w_o, x, mod, gpost, gpre, *, tn):
    tm, d = mg.shape
    slab = x.shape[0]
    n_blk = d // tn
    row = pl.BlockSpec((tm, d), lambda j: (0, 0))
    vec = pl.BlockSpec((1, d), lambda j: (0, 0))
    mspec = lambda k: pl.BlockSpec((slab, d), lambda j: (0, k))
    (x1, h2), _ = _run(
        functools.partial(_s_proj_kernel, tm=tm, tn=tn, n_blk=n_blk, slab=slab),
        grid=(n_blk,),
        in_specs=[row, pl.BlockSpec((d, tn), lambda j: (0, j)),
                  pl.BlockSpec(x.shape, lambda j: (0, 0)), mspec(2), vec, mspec(3), mspec(4), vec],
        args=(mg, w_o, x, mod, gpost, mod, mod, gpre),
        out_specs=[row, row],
        out_shape=[jax.ShapeDtypeStruct((tm, d), _F32), jax.ShapeDtypeStruct((tm, d), _BF16)],
        scratch=[pltpu.VMEM((n_blk, tm, tn), _F32)],
        name="out_proj_sample")
    return x1, h2


def _s_ffn_kernel(h_ref, wa_ref, wb_ref, cw_ref, wd_ref, x1_ref, gt_ref, gpost_ref, st0_ref, st1_ref,
                  out_ref, t0_ref, t1_ref, *, tm, n_blk, slab, side):
    j = pl.program_id(0)
    d = x1_ref.shape[1]
    lanes = [slice(t * d, (t + 1) * d) for t in range(tm // slab)]

    @pl.when(j == 0)
    def _():
        out_ref[...] = jnp.zeros(out_ref.shape, _F32)

    prev = [st0_ref[...], st1_ref[...]]
    for c, r in enumerate(_slabs(tm, 2 * slab)):
        h = h_ref[r, :]
        ac, prev = _causal_conv_slabs(_dot(h, wa_ref[...]), prev, cw_ref, slab)
        g = (jax.nn.gelu(ac) * _dot(h, wb_ref[...])).astype(_BF16)
        f = _dot(g, wd_ref[...])
        out_ref[:, lanes[2 * c]] += f[:slab, :]
        out_ref[:, lanes[2 * c + 1]] += f[slab:, :]
    t0_ref[...] = prev[0]
    t1_ref[...] = prev[1]

    @pl.when(j == n_blk - 1)
    def _():
        for t, r in enumerate(_slabs(tm, slab)):
            out_ref[:, lanes[t]] = (x1_ref[r, :]
                                    + gt_ref[...] * _rms(out_ref[:, lanes[t]], gpost_ref[...]))


def _s_ffn_call(h2, w_a, w_b, cw, w_down, x1, mod, gpost, state, *, tn):
    tm, d = x1.shape
    slab = state.shape[0]
    f = cw.shape[1]
    n_blk = f // tn
    out_shape = (slab, tm // slab * d)
    row = pl.BlockSpec((tm, d), lambda j: (0, 0))
    wspec = pl.BlockSpec((d, tn), lambda j: (0, j))
    sspec = lambda k: pl.BlockSpec((slab, tn), lambda j: (0, k * n_blk + j))
    tspec = pl.BlockSpec((slab, tn), lambda j: (0, j))
    (out, t0, t1), _ = _run(
        functools.partial(_s_ffn_kernel, tm=tm, n_blk=n_blk, slab=slab),
        grid=(n_blk,),
        in_specs=[row, wspec, wspec, pl.BlockSpec((CONV_K, tn), lambda j: (0, j)),
                  pl.BlockSpec((tn, d), lambda j: (j, 0)), row,
                  pl.BlockSpec((slab, d), lambda j: (0, 5)), pl.BlockSpec((1, d), lambda j: (0, 0)),
                  sspec(0), sspec(1)],
        args=(h2, w_a, w_b, cw, w_down, x1, mod, gpost, state, state),
        out_specs=[pl.BlockSpec(out_shape, lambda j: (0, 0)), tspec, tspec],
        out_shape=[jax.ShapeDtypeStruct(out_shape, _F32)]
        + [jax.ShapeDtypeStruct((slab, f), _F32)] * 2,
        name="convffn_sample")
    return out, t0, t1


def _p_gmlp_kernel(x_hbm, sh_ref, sc_ref, gpre_ref, wv_ref, wu_ref, gv_ref, ws_ref, bias_ref,
                   ya_ref, h_ref, x_buf, x_sem, v_scr, wt_scr, *, tm, tn, n_blk, slab, side):
    i = pl.program_id(0)
    j = pl.program_id(1)
    d = n_blk * tn
    slabs = _slabs(tm, slab)

    def x_copy(tile):
        return pltpu.make_async_copy(x_hbm.at[pl.ds(pl.multiple_of(tile * tm, tm), tm), :],
                                     x_buf, x_sem)

    @pl.when((i == 0) & (j == 0))
    def _():
        x_copy(0).start()
        tril = (lax.broadcasted_iota(jnp.int32, (CHUNK, CHUNK), 0)
                >= lax.broadcasted_iota(jnp.int32, (CHUNK, CHUNK), 1))
        for g in range(d // GROUP):
            wt_scr[g] = jnp.where(tril, ws_ref[g], 0.0).astype(_BF16)

    @pl.when(j == 0)
    def _():
        x_copy(i).wait()
        _convert(side)
        for r in slabs:
            h = _rms(x_buf[r, :], gpre_ref[...]) * (1.0 + sc_ref[...]) + sh_ref[...]
            h = h.astype(_BF16)
            h_ref[r, :] = h
            v_scr[0, r, :] = _dot(h, wv_ref[...])

    @pl.when((j == 0) & (i + 1 < pl.num_programs(0)))
    def _():
        x_copy(i + 1).start()

    @pl.when((j > 0) & (j < n_blk))
    def _():
        _convert(side)
        for r in slabs:
            v_scr[j, r, :] = _dot(h_ref[r, :], wv_ref[...])

    def _gate():
        gpb = tn // GROUP
        for c in range(tm // CHUNK):
            r = slice(c * CHUNK, (c + 1) * CHUNK)
            ss = 0.0
            for k in range(n_blk):
                vk = v_scr[k, r, :]
                ss = ss + jnp.sum(vk * vk, axis=-1, keepdims=True)
            rs = lax.rsqrt(ss * (1.0 / d) + EPS)
            for k in range(n_blk):
                vb = (v_scr[k, r, :] * rs * gv_ref[:, k * tn:(k + 1) * tn]).astype(_BF16)
                for gg in range(gpb):
                    g = k * gpb + gg
                    lanes = slice(gg * GROUP, (gg + 1) * GROUP)
                    v_scr[k, r, lanes] = (_dot(wt_scr[g], vb[:, lanes])
                                          + bias_ref[:, g * GROUP:(g + 1) * GROUP])

    @pl.when(j == n_blk)
    def _():
        _convert(side)
        u = [_dot(h_ref[r, :], wu_ref[...]) for r in slabs]
        _gate()
        for r, ur in zip(slabs, u):
            ya_ref[r, :] = (ur * v_scr[0, r, :]).astype(_BF16)

    @pl.when(j > n_blk)
    def _():
        _convert(side)
        for r in slabs:
            ya_ref[r, :] = (_dot(h_ref[r, :], wu_ref[...]) * v_scr[j - n_blk, r, :]).astype(_BF16)


def _p_gmlp_call(x, mod, gpre, w_uv, gv, ws, bias, *, tm, tn, slab, tiles_per_seq, mod_row0, side):
    m, d = x.shape
    n_blk = d // tn
    full = lambda a: pl.BlockSpec(a.shape, lambda i, j: (0,) * a.ndim)
    mspec = lambda k: pl.BlockSpec((None, 1, d), lambda i, j: (mod_row0 + i // tiles_per_seq, 0, k))
    u_map = lambda i, j: (i, jnp.maximum(j - n_blk, 0))
    (ya, h), copies = _run(
        functools.partial(_p_gmlp_kernel, tm=tm, tn=tn, n_blk=n_blk, slab=slab),
        grid=(m // tm, 2 * n_blk),
        in_specs=[pl.BlockSpec(memory_space=pl.ANY), mspec(0), mspec(1), full(gpre),
                  pl.BlockSpec((d, tn), lambda i, j: (0, n_blk + jnp.minimum(j, n_blk - 1))),
                  pl.BlockSpec((d, tn), lambda i, j: (0, jnp.maximum(j - n_blk, 0))),
                  full(gv), full(ws), full(bias)],
        args=(x, mod, mod, gpre, w_uv, w_uv, gv, ws, bias),
        out_specs=[pl.BlockSpec((tm, tn), u_map), pl.BlockSpec((tm, d), lambda i, j: (i, 0))],
        out_shape=[jax.ShapeDtypeStruct((m, d), _BF16), jax.ShapeDtypeStruct((m, d), _BF16)],
        scratch=[pltpu.VMEM((tm, d), _F32), pltpu.SemaphoreType.DMA(()),
                 pltpu.VMEM((n_blk, tm, tn), _F32), pltpu.VMEM((d // GROUP, CHUNK, CHUNK), _BF16)],
        side=side, name="gmlp_prompt")
    return ya, h, copies


def _p_shortconv_kernel(h_ref, wbg_ref, wcg_ref, wxb_ref, cw_ref, yb_ref, tail_ref, carry_scr,
                        *, tm, tiles_per_seq, slab, side):
    i = pl.program_id(0)
    j = pl.program_id(1)

    @pl.when(i % tiles_per_seq == 0)
    def _():
        carry_scr[j] = jnp.zeros(carry_scr.shape[1:], _F32)

    _convert(side)
    prev = carry_scr[j]
    cw = cw_ref.at[j]
    for r in _slabs(tm, slab):
        h = h_ref[r, :]
        bg = _dot(h, wbg_ref[...])
        p = _dot(h, wcg_ref[...]) * _dot(h, wxb_ref[...])
        yb_ref[r, :] = (bg * _causal_conv_rows(p, prev, cw)).astype(_BF16)
        prev = p[slab - SUBLANES:, :]
    carry_scr[j] = prev
    tail_ref[j] = prev


def _p_shortconv_call(h, w_bcx, cw, *, tm, tn, tiles_per_seq, slab, side):
    m, d = h.shape
    w = cw.shape[1]
    n_blk = w // tn
    wspec = lambda off: pl.BlockSpec((d, tn), lambda i, j: (0, off + j))
    (yb, tail), copies = _run(
        functools.partial(_p_shortconv_kernel, tm=tm, tiles_per_seq=tiles_per_seq, slab=slab),
        grid=(m // tm, n_blk),
        in_specs=[pl.BlockSpec((tm, d), lambda i, j: (i, 0)),
                  wspec(0), wspec(n_blk), wspec(2 * n_blk),
                  pl.BlockSpec((n_blk, CONV_K, tn), lambda i, j: (0, 0, 0))],
        args=(h, w_bcx, w_bcx, w_bcx, _col_blocks(cw, tn)),
        out_specs=[pl.BlockSpec((tm, tn), lambda i, j: (i, j)),
                   pl.BlockSpec((None, n_blk, SUBLANES, tn), lambda i, j: (i, 0, 0, 0))],
        out_shape=[jax.ShapeDtypeStruct((m, w), _BF16),
                   jax.ShapeDtypeStruct((m // tm, n_blk, SUBLANES, tn), _F32)],
        scratch=[pltpu.VMEM((n_blk, SUBLANES, tn), _F32)],
        side=side, name="shortconv_prompt")
    return yb, tail, copies


def _p_proj_kernel(m_ref, wo_ref, x_ref, gt_ref, gpost_ref, sh_ref, sc_ref, gpre_ref,
                   x1_ref, h2_ref, *, tm, slab, side):
    for r in _slabs(tm, slab):
        y = _dot(m_ref[r, :], wo_ref[...])
        x1, h2 = _post_mix(y, x_ref[r, :], gt_ref[...], gpost_ref[...], sh_ref[...], sc_ref[...],
                           gpre_ref[...])
        x1_ref[r, :] = x1
        h2_ref[r, :] = h2


def _p_proj_call(mg, w_o, x, mod, gpost, gpre, *, tm, slab, tiles_per_seq, mod_row0):
    m, d = x.shape
    row = pl.BlockSpec((tm, d), lambda i: (i, 0))
    vec = pl.BlockSpec((1, d), lambda i: (0, 0))
    mspec = lambda k: pl.BlockSpec((None, 1, d), lambda i: (mod_row0 + i // tiles_per_seq, 0, k))
    (x1, h2), _ = _run(
        functools.partial(_p_proj_kernel, tm=tm, slab=slab),
        grid=(m // tm,),
        in_specs=[row, pl.BlockSpec((d, d), lambda i: (0, 0)), row, mspec(2), vec, mspec(3),
                  mspec(4), vec],
        args=(mg, w_o, x, mod, gpost, mod, mod, gpre),
        out_specs=[row, row],
        out_shape=[jax.ShapeDtypeStruct((m, d), _F32), jax.ShapeDtypeStruct((m, d), _BF16)],
        name="out_proj_prompt")
    return x1, h2


def _p_ffn_kernel(h_ref, wa_ref, wb_ref, cw_ref, wd_ref, x1_hbm, gt_ref, gpost_ref,
                  out_ref, tail_ref, x1_buf, x1_sem, carry_scr, *, tm, n_blk, tiles_per_seq, slab,
                  side):
    i = pl.program_id(0)
    j = pl.program_id(1)
    x1_copy = pltpu.make_async_copy(x1_hbm.at[pl.ds(pl.multiple_of(i * tm, tm), tm), :],
                                    x1_buf, x1_sem)

    @pl.when(i % tiles_per_seq == 0)
    def _():
        carry_scr[j] = jnp.zeros(carry_scr.shape[1:], _F32)

    def step(first, last):
        prev = carry_scr[j]
        cw = cw_ref.at[j]
        for r in _slabs(tm, slab):
            h = h_ref[r, :]
            a = _dot(h, wa_ref[...])
            b = _dot(h, wb_ref[...])
            g = (jax.nn.gelu(_causal_conv_rows(a, prev, cw)) * b).astype(_BF16)
            prev = a[slab - SUBLANES:, :]
            f = _dot(g, wd_ref[...])
            acc = f if first else out_ref[r, :] + f
            if last:
                acc = x1_buf[r, :] + gt_ref[...] * _rms(acc, gpost_ref[...])
            out_ref[r, :] = acc
        carry_scr[j] = prev
        tail_ref[j] = prev

    @pl.when(j == 0)
    def _():
        x1_copy.start()
        step(True, False)

    @pl.when((j > 0) & (j < n_blk - 1))
    def _():
        step(False, False)

    @pl.when(j == n_blk - 1)
    def _():
        x1_copy.wait()
        step(False, True)


def _p_ffn_call(h2, w_a, w_b, cw, w_down, x1, mod, gpost, *, tm, tn, tiles_per_seq, slab, mod_row0):
    m, d = x1.shape
    f = cw.shape[1]
    n_blk = f // tn
    row = pl.BlockSpec((tm, d), lambda i, j: (i, 0))
    wspec = pl.BlockSpec((d, tn), lambda i, j: (0, j))
    (out, tail), _ = _run(
        functools.partial(_p_ffn_kernel, tm=tm, n_blk=n_blk, tiles_per_seq=tiles_per_seq, slab=slab),
        grid=(m // tm, n_blk),
        in_specs=[row, wspec, wspec,
                  pl.BlockSpec((n_blk, CONV_K, tn), lambda i, j: (0, 0, 0)),
                  pl.BlockSpec((tn, d), lambda i, j: (j, 0)),
                  pl.BlockSpec(memory_space=pl.ANY),
                  pl.BlockSpec((None, 1, d), lambda i, j: (mod_row0 + i // tiles_per_seq, 0, 5)),
                  pl.BlockSpec((1, d), lambda i, j: (0, 0))],
        args=(h2, w_a, w_b, _col_blocks(cw, tn), w_down, x1, mod, gpost),
        out_specs=[row, pl.BlockSpec((None, n_blk, SUBLANES, tn), lambda i, j: (i, 0, 0, 0))],
        out_shape=[jax.ShapeDtypeStruct((m, d), _F32),
                   jax.ShapeDtypeStruct((m // tm, n_blk, SUBLANES, tn), _F32)],
        scratch=[pltpu.VMEM((tm, d), _F32), pltpu.SemaphoreType.DMA(()),
                 pltpu.VMEM((n_blk, SUBLANES, tn), _F32)],
        name="convffn_prompt")
    return out, tail


def _cols(src, first_col, n_cols, n_blocks):
    block = n_cols // n_blocks
    assert block * n_blocks == n_cols and block % 128 == 0 and first_col % block == 0
    return _Side(src, 1, first_col // block, block, n_blocks)


def _layer(xs, xp, mod, st_b, st_f, p, *, seq_len, mod_row0):
    d = p["g_v"].shape[1]
    wb = p["conv_b_w"].shape[1]
    f = p["conv_f_w"].shape[1]
    tn_s = SAMPLE_COL_TILE
    mod_rows = mod.reshape(mod.shape[0], 1, N_MOD * d)

    def tiles(name):
        tm, tn, slab = PROMPT_TILES[name]
        return dict(tm=tm, tn=tn, slab=slab, tiles_per_seq=seq_len // tm)

    def n_steps(name):
        tm, tn, _ = PROMPT_TILES[name]
        cols = {"gmlp": 2 * d, "shortconv": wb, "merge": d, "ffn": f}[name]
        return xp.shape[0] // tm * (cols // tn)

    def without(kw, *names):
        return {k: v for k, v in kw.items() if k not in names}

    def blocks(n_cols, budget):
        return max(n for n in range(1, budget + 1) if n_cols % (128 * n) == 0)

    w_in, w_up = p["w_in"], p["w_up"]
    w_uv = p["w_uv"]

    ya_s, h1_s, vn = _s_gmlp_call(xs, mod, p["g_pre_mix"], w_uv, p["g_v"], p["wvec"], p["bvec"],
                                  tn=tn_s)
    n = n_steps("gmlp")
    ya_p, h1_p, (w_bcx, w_a, w_b) = _p_gmlp_call(
        xp, mod_rows, p["g_pre_mix"], w_uv, p["g_v"], p["w_s"], p["bias"], mod_row0=mod_row0,
        side=(_cols(w_in, 2 * d, 3 * wb, blocks(3 * wb, n)), _cols(w_up, 0, f, blocks(f, n)),
              _cols(w_up, f, f, blocks(f, n))),
        **tiles("gmlp"))

    yb_s, tb0, tb1 = _s_shortconv_call(h1_s, w_bcx, p["conv_b_w"], st_b, tn=tn_s)
    n = n_steps("shortconv")
    yb_p, tail_b, (w_gate, w_oa, w_ob, w_o) = _p_shortconv_call(
        h1_p, w_bcx, p["conv_b_w"],
        side=(_cols(w_in, 2 * d + 3 * wb, 2 * d, blocks(2 * d, n)),
              _cols(p["w_out_a"], 0, d, blocks(d, n)), _cols(p["w_out_b"], 0, d, blocks(d, n)),
              _cols(p["w_o"], 0, d, blocks(d, n))),
        **tiles("shortconv"))

    tm_s = h1_s.shape[0]
    mg_s, _ = _merge_call(h1_s, ya_s, yb_s, w_gate, w_oa, w_ob, tm=tm_s, tn=tn_s, slab=tm_s // 2,
                          name="gated_merge_sample")
    n_row_blocks = max(k for k in range(1, n_steps("merge") + 1)
                       if f % k == 0 and (f // k) % (2 * SUBLANES) == 0)
    mg_p, (w_d,) = _merge_call(h1_p, ya_p, yb_p, w_gate, w_oa, w_ob, name="gated_merge_prompt",
                               side=(_Side(p["w_down"], 0, 0, f // n_row_blocks, n_row_blocks),),
                               **without(tiles("merge"), "tiles_per_seq"))

    x1_s, h2_s = _s_proj_call(mg_s, w_o, xs, mod, p["g_post_mix"], p["g_pre_ffn"], tn=tn_s)
    x1_p, h2_p = _p_proj_call(mg_p, w_o, xp, mod_rows, p["g_post_mix"], p["g_pre_ffn"],
                              mod_row0=mod_row0, **without(tiles("proj"), "tn"))

    out_s, tf0, tf1 = _s_ffn_call(h2_s, w_a, w_b, p["conv_f_w"], w_d, x1_s, mod, p["g_post_ffn"],
                                  st_f, tn=tn_s)
    out_p, tail_f = _p_ffn_call(h2_p, w_a, w_b, p["conv_f_w"], w_d, x1_p, mod_rows, p["g_post_ffn"],
                                mod_row0=mod_row0, **tiles("ffn"))
    return out_s, out_p, (tb0, tb1), (tf0, tf1), vn, tail_b, tail_f


def kernel(x_prompt, x_sample, c_prompt, c_sample, state_conv_b, state_conv_ffn, w_ada, b_ada, g_pre_mix, g_post_mix, w_in, g_v, w_s, b_s, conv_b_w, w_out_a, w_out_b, w_o, g_pre_ffn, g_post_ffn, w_up, conv_f_w, w_down):
    depth = w_in.shape[0]
    bp, seq, d = x_prompt.shape
    bs, tdec, _ = x_sample.shape
    n_groups = w_s.shape[1]
    assert bs == CHUNK and tdec <= CHUNK
    assert all(seq % tm == 0 for tm, _, _ in PROMPT_TILES.values())

    xp = x_prompt.reshape(bp * seq, d)
    xs = x_sample.reshape(bs, tdec * d)
    pad = (-(bp + bs)) % SUBLANES
    c_all = jnp.concatenate([c_sample, c_prompt, jnp.zeros((pad, d), _F32)], axis=0)

    pb, sb, pf, sf, sv = [], [], [], [], []
    for l in range(depth):
        n_ada = w_ada.shape[2] // ADALN_COL_TILE
        uv_blocks = max(n for n in range(1, n_ada + 1) if (2 * d) % (128 * n) == 0)
        mod, (w_uv,) = _mod_call(c_all, w_ada[l], b_ada[l][None, :],
                                 side=(_cols(w_in[l], 0, 2 * d, uv_blocks),))
        vec = lambda a: a[l][None, :]
        bias_full = jnp.repeat(jnp.transpose(b_s[l]), GROUP, axis=1)
        wvec = jnp.repeat(
            jnp.transpose(w_s[l][:, :tdec, :tdec], (1, 2, 0)).reshape(tdec * tdec, n_groups),
            GROUP, axis=1)
        p = {
            "w_in": w_in[l], "w_uv": w_uv, "w_out_a": w_out_a[l], "w_out_b": w_out_b[l],
            "w_o": w_o[l], "w_up": w_up[l], "w_down": w_down[l],
            "g_pre_mix": vec(g_pre_mix), "g_post_mix": vec(g_post_mix), "g_v": vec(g_v),
            "g_pre_ffn": vec(g_pre_ffn), "g_post_ffn": vec(g_post_ffn),
            "conv_b_w": conv_b_w[l], "conv_f_w": conv_f_w[l],
            "w_s": w_s[l], "bias": bias_full, "wvec": wvec, "bvec": bias_full[:tdec],
        }
        st_b = state_conv_b[l].reshape(bs, -1)
        st_f = state_conv_ffn[l].reshape(bs, -1)
        xs, xp, sbt, sft, vn, tb, tf = _layer(xs, xp, mod, st_b, st_f, p, seq_len=seq, mod_row0=bs)

        def prompt_tail(t):
            n_tiles, n_blk, _, tn = t.shape
            t = t.reshape(bp, n_tiles // bp, n_blk, SUBLANES, tn)[:, -1, :, SUBLANES - (CONV_K - 1):, :]
            return jnp.transpose(t, (0, 2, 1, 3)).reshape(bp, CONV_K - 1, n_blk * tn)

        pb.append(prompt_tail(tb))
        pf.append(prompt_tail(tf))
        sb.append(jnp.stack(sbt, axis=1))
        sf.append(jnp.stack(sft, axis=1))
        sv.append(vn.reshape(bs, tdec, d))

    y_prompt = xp.reshape(bp, seq, d)
    return (y_prompt, xs.reshape(bs, tdec, d), jnp.stack(pb), jnp.stack(sb), jnp.stack(pf), jnp.stack(sf),
            jnp.stack(sv))
```

```python
import functools
from typing import NamedTuple

import jax
import jax.numpy as jnp
from jax import lax
from jax.experimental import pallas as pl
from jax.experimental.pallas import tpu as pltpu

EPS = 1e-6
CHUNK = 128
GROUP = 128
CONV_K = 3
N_MOD = 6
SUBLANES = 8
VMEM_LIMIT_BYTES = 56 * 1024 * 1024
PROMPT_TILES = {
    "gmlp": (1024, 512, 256),
    "shortconv": (1024, 512, 256),
    "merge": (512, 1024, 256),
    "proj": (512, None, 256),
    "ffn": (1024, 512, 256),
}
SAMPLE_COL_TILE = 512
ADALN_COL_TILE = 1024

_BF16 = jnp.bfloat16
_F32 = jnp.float32


def _dot(a, b):
    return jnp.dot(a, b, preferred_element_type=_F32)


def _rms(xf, g):
    ms = jnp.mean(xf * xf, axis=-1, keepdims=True)
    return xf * lax.rsqrt(ms + EPS) * g


def _causal_conv_rows(p, prev, cw_ref):
    rows = p.shape[0]
    row = lax.broadcasted_iota(jnp.int32, (rows, 1), 0)
    m1 = jnp.where(row == 0, prev[7:8, :], pltpu.roll(p, 1, 0))
    m2 = jnp.where(row == 0, prev[6:7, :], jnp.where(row == 1, prev[7:8, :], pltpu.roll(p, 2, 0)))
    return cw_ref[0:1, :] * m2 + cw_ref[1:2, :] * m1 + cw_ref[2:3, :] * p


def _causal_conv_slabs(x, prev, cw_ref, slab):
    w0, w1, w2 = cw_ref[0:1, :], cw_ref[1:2, :], cw_ref[2:3, :]
    seq = list(prev) + [x[t * slab:(t + 1) * slab, :] for t in range(x.shape[0] // slab)]
    y = [w0 * seq[t] + w1 * seq[t + 1] + w2 * seq[t + 2] for t in range(len(seq) - 2)]
    return jnp.concatenate(y, axis=0), seq[-2:]


def _slabs(tm, slab):
    return [slice(s * slab, (s + 1) * slab) for s in range(tm // slab)]


def _col_blocks(a, tn):
    r, c = a.shape
    return jnp.transpose(a.reshape(r, c // tn, tn), (1, 0, 2))


def _post_mix(y, x, gt, gpost, sh, sc, gpre):
    x1 = x + gt * _rms(y, gpost)
    h2 = _rms(x1, gpre) * (1.0 + sc) + sh
    return x1, h2.astype(_BF16)


class _Side(NamedTuple):
    src: jax.Array
    axis: int
    start: int
    block: int
    n_blocks: int


def _convert(side):
    for src_ref, dst_ref in side:
        dst_ref[...] = src_ref[...].astype(_BF16)


def _run(body, *, grid, in_specs, args, out_specs, out_shape, scratch=(), side=(), name,
         vmem_limit_bytes=VMEM_LIMIT_BYTES):
    n_in, n_out, n_side = len(args), len(out_shape), len(side)
    n_steps = functools.reduce(lambda a, b: a * b, grid)
    step_of = (lambda i: i) if len(grid) == 1 else (lambda i, j: i * grid[1] + j)
    side_in, side_out, side_shape = [], [], []
    for s in side:
        assert s.n_blocks <= n_steps, (name, s.n_blocks, n_steps)
        other = s.src.shape[1 - s.axis]
        pos = lambda *ids, s=s: jnp.minimum(step_of(*ids), s.n_blocks - 1)
        if s.axis == 1:
            blk, full = (other, s.block), (other, s.block * s.n_blocks)
            side_in.append(pl.BlockSpec(blk, lambda *ids, s=s, pos=pos: (0, s.start + pos(*ids))))
            side_out.append(pl.BlockSpec(blk, lambda *ids, pos=pos: (0, pos(*ids))))
        else:
            blk, full = (s.block, other), (s.block * s.n_blocks, other)
            side_in.append(pl.BlockSpec(blk, lambda *ids, s=s, pos=pos: (s.start + pos(*ids), 0)))
            side_out.append(pl.BlockSpec(blk, lambda *ids, pos=pos: (pos(*ids), 0)))
        side_shape.append(jax.ShapeDtypeStruct(full, _BF16))

    def kern(*refs):
        o0 = n_in + n_side
        s0 = o0 + n_out + n_side
        body(*refs[:n_in], *refs[o0:o0 + n_out], *refs[s0:],
             side=tuple(zip(refs[n_in:o0], refs[o0 + n_out:s0])))

    res = pl.pallas_call(
        kern, grid=grid,
        in_specs=list(in_specs) + side_in, out_specs=list(out_specs) + side_out,
        out_shape=list(out_shape) + side_shape, scratch_shapes=list(scratch),
        compiler_params=pltpu.CompilerParams(dimension_semantics=("arbitrary",) * len(grid),
                                             vmem_limit_bytes=vmem_limit_bytes),
        name=name,
    )(*args, *[s.src for s in side])
    return res[:n_out], res[n_out:]


def _mod_kernel(c_ref, w_ref, b_ref, o_ref, *, side):
    _convert(side)
    c = c_ref[...]
    a = (c * jax.nn.sigmoid(c)).astype(_BF16)
    o_ref[...] = _dot(a, w_ref[...].astype(_BF16)) + b_ref[...]


def _mod_call(c_all, w_ada, b_ada, *, side):
    rows, d = c_all.shape
    n = w_ada.shape[1]
    tn = ADALN_COL_TILE
    (mod,), copies = _run(
        _mod_kernel, grid=(n // tn,),
        in_specs=[pl.BlockSpec((rows, d), lambda j: (0, 0)),
                  pl.BlockSpec((d, tn), lambda j: (0, j)),
                  pl.BlockSpec((1, tn), lambda j: (0, j))],
        args=(c_all, w_ada, b_ada),
        out_specs=[pl.BlockSpec((rows, tn), lambda j: (0, j))],
        out_shape=[jax.ShapeDtypeStruct((rows, n), _F32)],
        side=side, name="adaln_mod")
    return mod, copies


def _s_gmlp_kernel(x_ref, sh_ref, sc_ref, gpre_ref, wv_ref, wu_ref, gv_ref, ws_ref, bias_ref,
                   ya_ref, h_ref, vn_ref, v_scr, *, tm, tn, n_blk, slab, side):
    j = pl.program_id(0)
    d = n_blk * tn
    slabs = _slabs(tm, slab)
    n_slab = len(slabs)

    @pl.when(j == 0)
    def _():
        for t, r in enumerate(slabs):
            h_ref[r, :] = (_rms(x_ref[:, t * d:(t + 1) * d], gpre_ref[...]) * (1.0 + sc_ref[...])
                           + sh_ref[...]).astype(_BF16)

    @pl.when(j < n_blk)
    def _():
        v_scr[j] = _dot(h_ref[...], wv_ref[...])

    @pl.when(j == n_blk)
    def _():
        for t, r in enumerate(slabs):
            ss = 0.0
            for k in range(n_blk):
                vk = v_scr[k, r, :]
                ss = ss + jnp.sum(vk * vk, axis=-1, keepdims=True)
            rs = lax.rsqrt(ss * (1.0 / d) + EPS)
            for k in range(n_blk):
                vn = v_scr[k, r, :] * rs * gv_ref[:, k * tn:(k + 1) * tn]
                v_scr[k, r, :] = vn
                vn_ref[:, t * d + k * tn:t * d + (k + 1) * tn] = vn
        for t in reversed(range(n_slab)):
            for k in range(n_blk):
                c = slice(k * tn, (k + 1) * tn)
                acc = ws_ref[t * n_slab:t * n_slab + 1, c] * v_scr[k, slabs[0], :]
                for s in range(1, t + 1):
                    acc = acc + ws_ref[t * n_slab + s:t * n_slab + s + 1, c] * v_scr[k, slabs[s], :]
                v_scr[k, slabs[t], :] = acc + bias_ref[t:t + 1, c]

    @pl.when(j >= n_blk)
    def _():
        ya_ref[...] = (_dot(h_ref[...], wu_ref[...]) * v_scr[j - n_blk]).astype(_BF16)


def _s_gmlp_call(x, mod, gpre, w_uv, gv, wvec, bvec, *, tn):
    slab = x.shape[0]
    d = gpre.shape[1]
    tm = x.shape[1] // d * slab
    n_blk = d // tn
    full = lambda a: pl.BlockSpec(a.shape, lambda j: (0,) * a.ndim)
    u_map = lambda j: (0, jnp.maximum(j - n_blk, 0))
    (ya, h, vn), _ = _run(
        functools.partial(_s_gmlp_kernel, tm=tm, tn=tn, n_blk=n_blk, slab=slab),
        grid=(2 * n_blk,),
        in_specs=[full(x),
                  pl.BlockSpec((slab, d), lambda j: (0, 0)), pl.BlockSpec((slab, d), lambda j: (0, 1)),
                  full(gpre),
                  pl.BlockSpec((d, tn), lambda j: (0, n_blk + jnp.minimum(j, n_blk - 1))),
                  pl.BlockSpec((d, tn), u_map),
                  full(gv), full(wvec), full(bvec)],
        args=(x, mod, mod, gpre, w_uv, w_uv, gv, wvec, bvec),
        out_specs=[pl.BlockSpec((tm, tn), u_map), pl.BlockSpec((tm, d), lambda j: (0, 0)), full(x)],
        out_shape=[jax.ShapeDtypeStruct((tm, d), _BF16), jax.ShapeDtypeStruct((tm, d), _BF16),
                   jax.ShapeDtypeStruct(x.shape, _F32)],
        scratch=[pltpu.VMEM((n_blk, tm, tn), _F32)],
        name="gmlp_sample")
    return ya, h, vn


def _s_shortconv_kernel(h_ref, wbg_ref, wcg_ref, wxb_ref, cw_ref, st0_ref, st1_ref,
                        yb_ref, t0_ref, t1_ref, *, slab, side):
    h = h_ref[...]
    bg = _dot(h, wbg_ref[...])
    p = _dot(h, wcg_ref[...]) * _dot(h, wxb_ref[...])
    cb, tail = _causal_conv_slabs(p, [st0_ref[...], st1_ref[...]], cw_ref, slab)
    yb_ref[...] = (bg * cb).astype(_BF16)
    t0_ref[...] = tail[0]
    t1_ref[...] = tail[1]


def _s_shortconv_call(h, w_bcx, cw, state, *, tn):
    tm, d = h.shape
    slab = state.shape[0]
    w = cw.shape[1]
    n_blk = w // tn
    wspec = lambda off: pl.BlockSpec((d, tn), lambda j: (0, off + j))
    sspec = lambda k: pl.BlockSpec((slab, tn), lambda j: (0, k * n_blk + j))
    tspec = pl.BlockSpec((slab, tn), lambda j: (0, j))
    (yb, t0, t1), _ = _run(
        functools.partial(_s_shortconv_kernel, slab=slab),
        grid=(n_blk,),
        in_specs=[pl.BlockSpec((tm, d), lambda j: (0, 0)),
                  wspec(0), wspec(n_blk), wspec(2 * n_blk),
                  pl.BlockSpec((CONV_K, tn), lambda j: (0, j)),
                  sspec(0), sspec(1)],
        args=(h, w_bcx, w_bcx, w_bcx, cw, state, state),
        out_specs=[pl.BlockSpec((tm, tn), lambda j: (0, j)), tspec, tspec],
        out_shape=[jax.ShapeDtypeStruct((tm, w), _BF16)] + [jax.ShapeDtypeStruct((slab, w), _F32)] * 2,
        name="shortconv_sample")
    return yb, t0, t1


def _merge_kernel(h_ref, ya_ref, yb_ref, wga_ref, wgb_ref, woa_ref, wob_ref, m_ref, *, tm, slab, side):
    _convert(side)
    for r in _slabs(tm, slab):
        h = h_ref[r, :]
        ga = jax.nn.sigmoid(_dot(h, wga_ref[...]))
        gb = jax.nn.sigmoid(_dot(h, wgb_ref[...]))
        m = ga * _dot(ya_ref[r, :], woa_ref[...]) + gb * _dot(yb_ref[r, :], wob_ref[...])
        m_ref[r, :] = m.astype(_BF16)


def _merge_call(h, ya, yb, w_gate, w_out_a, w_out_b, *, tm, tn, slab, side=(), name):
    m, d = h.shape
    n_blk = d // tn
    row = pl.BlockSpec((tm, d), lambda i, j: (i, 0))
    wspec = lambda off: pl.BlockSpec((d, tn), lambda i, j: (0, off + j))
    (mg,), copies = _run(
        functools.partial(_merge_kernel, tm=tm, slab=slab),
        grid=(m // tm, n_blk),
        in_specs=[row, row, row, wspec(0), wspec(n_blk), wspec(0), wspec(0)],
        args=(h, ya, yb, w_gate, w_gate, w_out_a, w_out_b),
        out_specs=[pl.BlockSpec((tm, tn), lambda i, j: (i, j))],
        out_shape=[jax.ShapeDtypeStruct((m, d), _BF16)],
        side=side, name=name)
    return mg, copies


def _s_proj_kernel(m_ref, wo_ref, x_ref, gt_ref, gpost_ref, sh_ref, sc_ref, gpre_ref,
                   x1_ref, h2_ref, y_scr, *, tm, tn, n_blk, slab, side):
    j = pl.program_id(0)
    d = n_blk * tn
    y_scr[j] = _dot(m_ref[...], wo_ref[...])

    @pl.when(j == n_blk - 1)
    def _():
        for t, r in enumerate(_slabs(tm, slab)):
            y = jnp.concatenate([y_scr[k, r, :] for k in range(n_blk)], axis=-1)
            x1, h2 = _post_mix(y, x_ref[:, t * d:(t + 1) * d], gt_ref[...], gpost_ref[...],
                               sh_ref[...], sc_ref[...], gpre_ref[...])
            x1_ref[r, :] = x1
            h2_ref[r, :] = h2


def _s_proj_call(mg, w_o, x, mod, gpost, gpre, *, tn):
    tm, d = mg.shape
    slab = x.shape[0]
    n_blk = d // tn
    row = pl.BlockSpec((tm, d), lambda j: (0, 0))
    vec = pl.BlockSpec((1, d), lambda j: (0, 0))
    mspec = lambda k: pl.BlockSpec((slab, d), lambda j: (0, k))
    (x1, h2), _ = _run(
        functools.partial(_s_proj_kernel, tm=tm, tn=tn, n_blk=n_blk, slab=slab),
        grid=(n_blk,),
        in_specs=[row, pl.BlockSpec((d, tn), lambda j: (0, j)),
                  pl.BlockSpec(x.shape, lambda j: (0, 0)), mspec(2), vec, mspec(3), mspec(4), vec],
        args=(mg, w_o, x, mod, gpost, mod, mod, gpre),
        out_specs=[row, row],
        out_shape=[jax.ShapeDtypeStruct((tm, d), _F32), jax.ShapeDtypeStruct((tm, d), _BF16)],
        scratch=[pltpu.VMEM((n_blk, tm, tn), _F32)],
        name="out_proj_sample")
    return x1, h2


def _s_ffn_kernel(h_ref, wa_ref, wb_ref, cw_ref, wd_ref, x1_ref, gt_ref, gpost_ref, st0_ref, st1_ref,
                  out_ref, t0_ref, t1_ref, *, tm, n_blk, slab, side):
    j = pl.program_id(0)
    d = x1_ref.shape[1]
    lanes = [slice(t * d, (t + 1) * d) for t in range(tm // slab)]

    @pl.when(j == 0)
    def _():
        out_ref[...] = jnp.zeros(out_ref.shape, _F32)

    prev = [st0_ref[...], st1_ref[...]]
    for c, r in enumerate(_slabs(tm, 2 * slab)):
        h = h_ref[r, :]
        ac, prev = _causal_conv_slabs(_dot(h, wa_ref[...]), prev, cw_ref, slab)
        g = (jax.nn.gelu(ac) * _dot(h, wb_ref[...])).astype(_BF16)
        f = _dot(g, wd_ref[...])
        out_ref[:, lanes[2 * c]] += f[:slab, :]
        out_ref[:, lanes[2 * c + 1]] += f[slab:, :]
    t0_ref[...] = prev[0]
    t1_ref[...] = prev[1]

    @pl.when(j == n_blk - 1)
    def _():
        for t, r in enumerate(_slabs(tm, slab)):
            out_ref[:, lanes[t]] = (x1_ref[r, :]
                                    + gt_ref[...] * _rms(out_ref[:, lanes[t]], gpost_ref[...]))


def _s_ffn_call(h2, w_a, w_b, cw, w_down, x1, mod, gpost, state, *, tn):
    tm, d = x1.shape
    slab = state.shape[0]
    f = cw.shape[1]
    n_blk = f // tn
    out_shape = (slab, tm // slab * d)
    row = pl.BlockSpec((tm, d), lambda j: (0, 0))
    wspec = pl.BlockSpec((d, tn), lambda j: (0, j))
    sspec = lambda k: pl.BlockSpec((slab, tn), lambda j: (0, k * n_blk + j))
    tspec = pl.BlockSpec((slab, tn), lambda j: (0, j))
    (out, t0, t1), _ = _run(
        functools.partial(_s_ffn_kernel, tm=tm, n_blk=n_blk, slab=slab),
        grid=(n_blk,),
        in_specs=[row, wspec, wspec, pl.BlockSpec((CONV_K, tn), lambda j: (0, j)),
                  pl.BlockSpec((tn, d), lambda j: (j, 0)), row,
                  pl.BlockSpec((slab, d), lambda j: (0, 5)), pl.BlockSpec((1, d), lambda j: (0, 0)),
                  sspec(0), sspec(1)],
        args=(h2, w_a, w_b, cw, w_down, x1, mod, gpost, state, state),
        out_specs=[pl.BlockSpec(out_shape, lambda j: (0, 0)), tspec, tspec],
        out_shape=[jax.ShapeDtypeStruct(out_shape, _F32)]
        + [jax.ShapeDtypeStruct((slab, f), _F32)] * 2,
        name="convffn_sample")
    return out, t0, t1


def _p_gmlp_kernel(x_hbm, sh_ref, sc_ref, gpre_ref, wv_ref, wu_ref, gv_ref, ws_ref, bias_ref,
                   ya_ref, h_ref, x_buf, x_sem, v_scr, wt_scr, *, tm, tn, n_blk, slab, side):
    i = pl.program_id(0)
    j = pl.program_id(1)
    d = n_blk * tn
    slabs = _slabs(tm, slab)

    def x_copy(tile):
        return pltpu.make_async_copy(x_hbm.at[pl.ds(pl.multiple_of(tile * tm, tm), tm), :],
                                     x_buf, x_sem)

    @pl.when((i == 0) & (j == 0))
    def _():
        x_copy(0).start()
        tril = (lax.broadcasted_iota(jnp.int32, (CHUNK, CHUNK), 0)
                >= lax.broadcasted_iota(jnp.int32, (CHUNK, CHUNK), 1))
        for g in range(d // GROUP):
            wt_scr[g] = jnp.where(tril, ws_ref[g], 0.0).astype(_BF16)

    @pl.when(j == 0)
    def _():
        x_copy(i).wait()
        _convert(side)
        for r in slabs:
            h = _rms(x_buf[r, :], gpre_ref[...]) * (1.0 + sc_ref[...]) + sh_ref[...]
            h = h.astype(_BF16)
            h_ref[r, :] = h
            v_scr[0, r, :] = _dot(h, wv_ref[...])

    @pl.when((j == 0) & (i + 1 < pl.num_programs(0)))
    def _():
        x_copy(i + 1).start()

    @pl.when((j > 0) & (j < n_blk))
    def _():
        _convert(side)
        for r in slabs:
            v_scr[j, r, :] = _dot(h_ref[r, :], wv_ref[...])

    def _gate():
        gpb = tn // GROUP
        for c in range(tm // CHUNK):
            r = slice(c * CHUNK, (c + 1) * CHUNK)
            ss = 0.0
            for k in range(n_blk):
                vk = v_scr[k, r, :]
                ss = ss + jnp.sum(vk * vk, axis=-1, keepdims=True)
            rs = lax.rsqrt(ss * (1.0 / d) + EPS)
            for k in range(n_blk):
                vb = (v_scr[k, r, :] * rs * gv_ref[:, k * tn:(k + 1) * tn]).astype(_BF16)
                for gg in range(gpb):
                    g = k * gpb + gg
                    lanes = slice(gg * GROUP, (gg + 1) * GROUP)
                    v_scr[k, r, lanes] = (_dot(wt_scr[g], vb[:, lanes])
                                          + bias_ref[:, g * GROUP:(g + 1) * GROUP])

    @pl.when(j == n_blk)
    def _():
        _convert(side)
        u = [_dot(h_ref[r, :], wu_ref[...]) for r in slabs]
        _gate()
        for r, ur in zip(slabs, u):
            ya_ref[r, :] = (ur * v_scr[0, r, :]).astype(_BF16)

    @pl.when(j > n_blk)
    def _():
        _convert(side)
        for r in slabs:
            ya_ref[r, :] = (_dot(h_ref[r, :], wu_ref[...]) * v_scr[j - n_blk, r, :]).astype(_BF16)


def _p_gmlp_call(x, mod, gpre, w_uv, gv, ws, bias, *, tm, tn, slab, tiles_per_seq, mod_row0, side):
    m, d = x.shape
    n_blk = d // tn
    full = lambda a: pl.BlockSpec(a.shape, lambda i, j: (0,) * a.ndim)
    mspec = lambda k: pl.BlockSpec((None, 1, d), lambda i, j: (mod_row0 + i // tiles_per_seq, 0, k))
    u_map = lambda i, j: (i, jnp.maximum(j - n_blk, 0))
    (ya, h), copies = _run(
        functools.partial(_p_gmlp_kernel, tm=tm, tn=tn, n_blk=n_blk, slab=slab),
        grid=(m // tm, 2 * n_blk),
        in_specs=[pl.BlockSpec(memory_space=pl.ANY), mspec(0), mspec(1), full(gpre),
                  pl.BlockSpec((d, tn), lambda i, j: (0, n_blk + jnp.minimum(j, n_blk - 1))),
                  pl.BlockSpec((d, tn), lambda i, j: (0, jnp.maximum(j - n_blk, 0))),
                  full(gv), full(ws), full(bias)],
        args=(x, mod, mod, gpre, w_uv, w_uv, gv, ws, bias),
        out_specs=[pl.BlockSpec((tm, tn), u_map), pl.BlockSpec((tm, d), lambda i, j: (i, 0))],
        out_shape=[jax.ShapeDtypeStruct((m, d), _BF16), jax.ShapeDtypeStruct((m, d), _BF16)],
        scratch=[pltpu.VMEM((tm, d), _F32), pltpu.SemaphoreType.DMA(()),
                 pltpu.VMEM((n_blk, tm, tn), _F32), pltpu.VMEM((d // GROUP, CHUNK, CHUNK), _BF16)],
        side=side, name="gmlp_prompt")
    return ya, h, copies


def _p_shortconv_kernel(h_ref, wbg_ref, wcg_ref, wxb_ref, cw_ref, yb_ref, tail_ref, carry_scr,
                        *, tm, tiles_per_seq, slab, side):
    i = pl.program_id(0)
    j = pl.program_id(1)

    @pl.when(i % tiles_per_seq == 0)
    def _():
        carry_scr[j] = jnp.zeros(carry_scr.shape[1:], _F32)

    _convert(side)
    prev = carry_scr[j]
    cw = cw_ref.at[j]
    for r in _slabs(tm, slab):
        h = h_ref[r, :]
        bg = _dot(h, wbg_ref[...])
        p = _dot(h, wcg_ref[...]) * _dot(h, wxb_ref[...])
        yb_ref[r, :] = (bg * _causal_conv_rows(p, prev, cw)).astype(_BF16)
        prev = p[slab - SUBLANES:, :]
    carry_scr[j] = prev
    tail_ref[j] = prev


def _p_shortconv_call(h, w_bcx, cw, *, tm, tn, tiles_per_seq, slab, side):
    m, d = h.shape
    w = cw.shape[1]
    n_blk = w // tn
    wspec = lambda off: pl.BlockSpec((d, tn), lambda i, j: (0, off + j))
    (yb, tail), copies = _run(
        functools.partial(_p_shortconv_kernel, tm=tm, tiles_per_seq=tiles_per_seq, slab=slab),
        grid=(m // tm, n_blk),
        in_specs=[pl.BlockSpec((tm, d), lambda i, j: (i, 0)),
                  wspec(0), wspec(n_blk), wspec(2 * n_blk),
                  pl.BlockSpec((n_blk, CONV_K, tn), lambda i, j: (0, 0, 0))],
        args=(h, w_bcx, w_bcx, w_bcx, _col_blocks(cw, tn)),
        out_specs=[pl.BlockSpec((tm, tn), lambda i, j: (i, j)),
                   pl.BlockSpec((None, n_blk, SUBLANES, tn), lambda i, j: (i, 0, 0, 0))],
        out_shape=[jax.ShapeDtypeStruct((m, w), _BF16),
                   jax.ShapeDtypeStruct((m // tm, n_blk, SUBLANES, tn), _F32)],
        scratch=[pltpu.VMEM((n_blk, SUBLANES, tn), _F32)],
        side=side, name="shortconv_prompt")
    return yb, tail, copies


def _p_proj_kernel(m_ref, wo_ref, x_ref, gt_ref, gpost_ref, sh_ref, sc_ref, gpre_ref,
                   x1_ref, h2_ref, *, tm, slab, side):
    for r in _slabs(tm, slab):
        y = _dot(m_ref[r, :], wo_ref[...])
        x1, h2 = _post_mix(y, x_ref[r, :], gt_ref[...], gpost_ref[...], sh_ref[...], sc_ref[...],
                           gpre_ref[...])
        x1_ref[r, :] = x1
        h2_ref[r, :] = h2


def _p_proj_call(mg, w_o, x, mod, gpost, gpre, *, tm, slab, tiles_per_seq, mod_row0):
    m, d = x.shape
    row = pl.BlockSpec((tm, d), lambda i: (i, 0))
    vec = pl.BlockSpec((1, d), lambda i: (0, 0))
    mspec = lambda k: pl.BlockSpec((None, 1, d), lambda i: (mod_row0 + i // tiles_per_seq, 0, k))
    (x1, h2), _ = _run(
        functools.partial(_p_proj_kernel, tm=tm, slab=slab),
        grid=(m // tm,),
        in_specs=[row, pl.BlockSpec((d, d), lambda i: (0, 0)), row, mspec(2), vec, mspec(3),
                  mspec(4), vec],
        args=(mg, w_o, x, mod, gpost, mod, mod, gpre),
        out_specs=[row, row],
        out_shape=[jax.ShapeDtypeStruct((m, d), _F32), jax.ShapeDtypeStruct((m, d), _BF16)],
        name="out_proj_prompt")
    return x1, h2


def _p_ffn_kernel(h_ref, wa_ref, wb_ref, cw_ref, wd_ref, x1_hbm, gt_ref, gpost_ref,
                  out_ref, tail_ref, x1_buf, x1_sem, carry_scr, *, tm, n_blk, tiles_per_seq, slab,
                  side):
    i = pl.program_id(0)
    j = pl.program_id(1)
    x1_copy = pltpu.make_async_copy(x1_hbm.at[pl.ds(pl.multiple_of(i * tm, tm), tm), :],
                                    x1_buf, x1_sem)

    @pl.when(i % tiles_per_seq == 0)
    def _():
        carry_scr[j] = jnp.zeros(carry_scr.shape[1:], _F32)

    def step(first, last):
        prev = carry_scr[j]
        cw = cw_ref.at[j]
        for r in _slabs(tm, slab):
            h = h_ref[r, :]
            a = _dot(h, wa_ref[...])
            b = _dot(h, wb_ref[...])
            g = (jax.nn.gelu(_causal_conv_rows(a, prev, cw)) * b).astype(_BF16)
            prev = a[slab - SUBLANES:, :]
            f = _dot(g, wd_ref[...])
            acc = f if first else out_ref[r, :] + f
            if last:
                acc = x1_buf[r, :] + gt_ref[...] * _rms(acc, gpost_ref[...])
            out_ref[r, :] = acc
        carry_scr[j] = prev
        tail_ref[j] = prev

    @pl.when(j == 0)
    def _():
        x1_copy.start()
        step(True, False)

    @pl.when((j > 0) & (j < n_blk - 1))
    def _():
        step(False, False)

    @pl.when(j == n_blk - 1)
    def _():
        x1_copy.wait()
        step(False, True)


def _p_ffn_call(h2, w_a, w_b, cw, w_down, x1, mod, gpost, *, tm, tn, tiles_per_seq, slab, mod_row0):
    m, d = x1.shape
    f = cw.shape[1]
    n_blk = f // tn
    row = pl.BlockSpec((tm, d), lambda i, j: (i, 0))
    wspec = pl.BlockSpec((d, tn), lambda i, j: (0, j))
    (out, tail), _ = _run(
        functools.partial(_p_ffn_kernel, tm=tm, n_blk=n_blk, tiles_per_seq=tiles_per_seq, slab=slab),
        grid=(m // tm, n_blk),
        in_specs=[row, wspec, wspec,
                  pl.BlockSpec((n_blk, CONV_K, tn), lambda i, j: (0, 0, 0)),
                  pl.BlockSpec((tn, d), lambda i, j: (j, 0)),
                  pl.BlockSpec(memory_space=pl.ANY),
                  pl.BlockSpec((None, 1, d), lambda i, j: (mod_row0 + i // tiles_per_seq, 0, 5)),
                  pl.BlockSpec((1, d), lambda i, j: (0, 0))],
        args=(h2, w_a, w_b, _col_blocks(cw, tn), w_down, x1, mod, gpost),
        out_specs=[row, pl.BlockSpec((None, n_blk, SUBLANES, tn), lambda i, j: (i, 0, 0, 0))],
        out_shape=[jax.ShapeDtypeStruct((m, d), _F32),
                   jax.ShapeDtypeStruct((m // tm, n_blk, SUBLANES, tn), _F32)],
        scratch=[pltpu.VMEM((tm, d), _F32), pltpu.SemaphoreType.DMA(()),
                 pltpu.VMEM((n_blk, SUBLANES, tn), _F32)],
        name="convffn_prompt")
    return out, tail


def _cols(src, first_col, n_cols, n_blocks):
    block = n_cols // n_blocks
    assert block * n_blocks == n_cols and block % 128 == 0 and first_col % block == 0
    return _Side(src, 1, first_col // block, block, n_blocks)


def _layer(xs, xp, mod, st_b, st_f, p, *, seq_len, mod_row0):
    d = p["g_v"].shape[1]
    wb = p["conv_b_w"].shape[1]
    f = p["conv_f_w"].shape[1]
    tn_s = SAMPLE_COL_TILE
    mod_rows = mod.reshape(mod.shape[0], 1, N_MOD * d)

    def tiles(name):
        tm, tn, slab = PROMPT_TILES[name]
        return dict(tm=tm, tn=tn, slab=slab, tiles_per_seq=seq_len // tm)

    def n_steps(name):
        tm, tn, _ = PROMPT_TILES[name]
        cols = {"gmlp": 2 * d, "shortconv": wb, "merge": d, "ffn": f}[name]
        return xp.shape[0] // tm * (cols // tn)

    def without(kw, *names):
        return {k: v for k, v in kw.items() if k not in names}

    def blocks(n_cols, budget):
        return max(n for n in range(1, budget + 1) if n_cols % (128 * n) == 0)

    w_in, w_up = p["w_in"], p["w_up"]
    w_uv = p["w_uv"]

    ya_s, h1_s, vn = _s_gmlp_call(xs, mod, p["g_pre_mix"], w_uv, p["g_v"], p["wvec"], p["bvec"],
                                  tn=tn_s)
    n = n_steps("gmlp")
    ya_p, h1_p, (w_bcx,) = _p_gmlp_call(
        xp, mod_rows, p["g_pre_mix"], w_uv, p["g_v"], p["w_s"], p["bias"], mod_row0=mod_row0,
        side=(_cols(w_in, 2 * d, 3 * wb, blocks(3 * wb, n)),),
        **tiles("gmlp"))

    yb_s, tb0, tb1 = _s_shortconv_call(h1_s, w_bcx, p["conv_b_w"], st_b, tn=tn_s)
    n = n_steps("shortconv")
    yb_p, tail_b, (w_gate, w_oa, w_ob, w_o, w_a, w_b) = _p_shortconv_call(
        h1_p, w_bcx, p["conv_b_w"],
        side=(_cols(w_in, 2 * d + 3 * wb, 2 * d, blocks(2 * d, n)),
              _cols(p["w_out_a"], 0, d, blocks(d, n)), _cols(p["w_out_b"], 0, d, blocks(d, n)),
              _cols(p["w_o"], 0, d, blocks(d, n)),
              _cols(w_up, 0, f, blocks(f, n)), _cols(w_up, f, f, blocks(f, n))),
        **tiles("shortconv"))

    tm_s = h1_s.shape[0]
    mg_s, _ = _merge_call(h1_s, ya_s, yb_s, w_gate, w_oa, w_ob, tm=tm_s, tn=tn_s, slab=tm_s // 2,
                          name="gated_merge_sample")
    n_row_blocks = max(k for k in range(1, n_steps("merge") + 1)
                       if f % k == 0 and (f // k) % (2 * SUBLANES) == 0)
    mg_p, (w_d,) = _merge_call(h1_p, ya_p, yb_p, w_gate, w_oa, w_ob, name="gated_merge_prompt",
                               side=(_Side(p["w_down"], 0, 0, f // n_row_blocks, n_row_blocks),),
                               **without(tiles("merge"), "tiles_per_seq"))

    x1_s, h2_s = _s_proj_call(mg_s, w_o, xs, mod, p["g_post_mix"], p["g_pre_ffn"], tn=tn_s)
    x1_p, h2_p = _p_proj_call(mg_p, w_o, xp, mod_rows, p["g_post_mix"], p["g_pre_ffn"],
                              mod_row0=mod_row0, **without(tiles("proj"), "tn"))

    out_s, tf0, tf1 = _s_ffn_call(h2_s, w_a, w_b, p["conv_f_w"], w_d, x1_s, mod, p["g_post_ffn"],
                                  st_f, tn=tn_s)
    out_p, tail_f = _p_ffn_call(h2_p, w_a, w_b, p["conv_f_w"], w_d, x1_p, mod_rows, p["g_post_ffn"],
                                mod_row0=mod_row0, **tiles("ffn"))
    return out_s, out_p, (tb0, tb1), (tf0, tf1), vn, tail_b, tail_f


def kernel(x_prompt, x_sample, c_prompt, c_sample, state_conv_b, state_conv_ffn, w_ada, b_ada, g_pre_mix, g_post_mix, w_in, g_v, w_s, b_s, conv_b_w, w_out_a, w_out_b, w_o, g_pre_ffn, g_post_ffn, w_up, conv_f_w, w_down):
    depth = w_in.shape[0]
    bp, seq, d = x_prompt.shape
    bs, tdec, _ = x_sample.shape
    n_groups = w_s.shape[1]
    assert bs == CHUNK and tdec <= CHUNK
    assert all(seq % tm == 0 for tm, _, _ in PROMPT_TILES.values())

    xp = x_prompt.reshape(bp * seq, d)
    xs = x_sample.reshape(bs, tdec * d)
    pad = (-(bp + bs)) % SUBLANES
    c_all = jnp.concatenate([c_sample, c_prompt, jnp.zeros((pad, d), _F32)], axis=0)

    pb, sb, pf, sf, sv = [], [], [], [], []
    for l in range(depth):
        n_ada = w_ada.shape[2] // ADALN_COL_TILE
        uv_blocks = max(n for n in range(1, n_ada + 1) if (2 * d) % (128 * n) == 0)
        mod, (w_uv,) = _mod_call(c_all, w_ada[l], b_ada[l][None, :],
                                 side=(_cols(w_in[l], 0, 2 * d, uv_blocks),))
        vec = lambda a: a[l][None, :]
        bias_full = jnp.repeat(jnp.transpose(b_s[l]), GROUP, axis=1)
        wvec = jnp.repeat(
            jnp.transpose(w_s[l][:, :tdec, :tdec], (1, 2, 0)).reshape(tdec * tdec, n_groups),
            GROUP, axis=1)
        p = {
            "w_in": w_in[l], "w_uv": w_uv, "w_out_a": w_out_a[l], "w_out_b": w_out_b[l],
            "w_o": w_o[l], "w_up": w_up[l], "w_down": w_down[l],
            "g_pre_mix": vec(g_pre_mix), "g_post_mix": vec(g_post_mix), "g_v": vec(g_v),
            "g_pre_ffn": vec(g_pre_ffn), "g_post_ffn": vec(g_post_ffn),
            "conv_b_w": conv_b_w[l], "conv_f_w": conv_f_w[l],
            "w_s": w_s[l], "bias": bias_full, "wvec": wvec, "bvec": bias_full[:tdec],
        }
        st_b = state_conv_b[l].reshape(bs, -1)
        st_f = state_conv_ffn[l].reshape(bs, -1)
        xs, xp, sbt, sft, vn, tb, tf = _layer(xs, xp, mod, st_b, st_f, p, seq_len=seq, mod_row0=bs)

        def prompt_tail(t):
            n_tiles, n_blk, _, tn = t.shape
            t = t.reshape(bp, n_tiles // bp, n_blk, SUBLANES, tn)[:, -1, :, SUBLANES - (CONV_K - 1):, :]
            return jnp.transpose(t, (0, 2, 1, 3)).reshape(bp, CONV_K - 1, n_blk * tn)

        pb.append(prompt_tail(tb))
        pf.append(prompt_tail(tf))
        sb.append(jnp.stack(sbt, axis=1))
        sf.append(jnp.stack(sft, axis=1))
        sv.append(vn.reshape(bs, tdec, d))

    y_prompt = xp.reshape(bp, seq, d)
    return (y_prompt, xs.reshape(bs, tdec, d), jnp.stack(pb), jnp.stack(sb), jnp.stack(pf), jnp.stack(sf),
            jnp.stack(sv))
```

```python
import functools
from typing import NamedTuple

import jax
import jax.numpy as jnp
from jax import lax
from jax.experimental import pallas as pl
from jax.experimental.pallas import tpu as pltpu

EPS = 1e-6
CHUNK = 128
GROUP = 128
CONV_K = 3
N_MOD = 6
SUBLANES = 8
VMEM_LIMIT_BYTES = 56 * 1024 * 1024
VMEM_LIMIT_BYTES_WIDE = 60 * 1024 * 1024
PROMPT_TILES = {
    "gmlp": (1024, 1024, 256),
    "shortconv": (1024, 512, 256),
    "merge": (512, 1024, 256),
    "proj": (512, None, 256),
    "ffn": (1024, 512, 256),
}
SAMPLE_COL_TILE = 512
ADALN_COL_TILE = 1024

_BF16 = jnp.bfloat16
_F32 = jnp.float32


def _dot(a, b):
    return jnp.dot(a, b, preferred_element_type=_F32)


def _rms(xf, g):
    ms = jnp.mean(xf * xf, axis=-1, keepdims=True)
    return xf * lax.rsqrt(ms + EPS) * g


def _causal_conv_rows(p, prev, cw_ref):
    rows = p.shape[0]
    row = lax.broadcasted_iota(jnp.int32, (rows, 1), 0)
    m1 = jnp.where(row == 0, prev[7:8, :], pltpu.roll(p, 1, 0))
    m2 = jnp.where(row == 0, prev[6:7, :], jnp.where(row == 1, prev[7:8, :], pltpu.roll(p, 2, 0)))
    return cw_ref[0:1, :] * m2 + cw_ref[1:2, :] * m1 + cw_ref[2:3, :] * p


def _causal_conv_slabs(x, prev, cw_ref, slab):
    w0, w1, w2 = cw_ref[0:1, :], cw_ref[1:2, :], cw_ref[2:3, :]
    seq = list(prev) + [x[t * slab:(t + 1) * slab, :] for t in range(x.shape[0] // slab)]
    y = [w0 * seq[t] + w1 * seq[t + 1] + w2 * seq[t + 2] for t in range(len(seq) - 2)]
    return jnp.concatenate(y, axis=0), seq[-2:]


def _slabs(tm, slab):
    return [slice(s * slab, (s + 1) * slab) for s in range(tm // slab)]


def _col_blocks(a, tn):
    r, c = a.shape
    return jnp.transpose(a.reshape(r, c // tn, tn), (1, 0, 2))


def _post_mix(y, x, gt, gpost, sh, sc, gpre):
    x1 = x + gt * _rms(y, gpost)
    h2 = _rms(x1, gpre) * (1.0 + sc) + sh
    return x1, h2.astype(_BF16)


class _Side(NamedTuple):
    src: jax.Array
    axis: int
    start: int
    block: int
    n_blocks: int


def _convert(side):
    for src_ref, dst_ref in side:
        dst_ref[...] = src_ref[...].astype(_BF16)


def _run(body, *, grid, in_specs, args, out_specs, out_shape, scratch=(), side=(), name,
         vmem_limit_bytes=VMEM_LIMIT_BYTES):
    n_in, n_out, n_side = len(args), len(out_shape), len(side)
    n_steps = functools.reduce(lambda a, b: a * b, grid)
    step_of = (lambda i: i) if len(grid) == 1 else (lambda i, j: i * grid[1] + j)
    side_in, side_out, side_shape = [], [], []
    for s in side:
        assert s.n_blocks <= n_steps, (name, s.n_blocks, n_steps)
        other = s.src.shape[1 - s.axis]
        pos = lambda *ids, s=s: jnp.minimum(step_of(*ids), s.n_blocks - 1)
        if s.axis == 1:
            blk, full = (other, s.block), (other, s.block * s.n_blocks)
            side_in.append(pl.BlockSpec(blk, lambda *ids, s=s, pos=pos: (0, s.start + pos(*ids))))
            side_out.append(pl.BlockSpec(blk, lambda *ids, pos=pos: (0, pos(*ids))))
        else:
            blk, full = (s.block, other), (s.block * s.n_blocks, other)
            side_in.append(pl.BlockSpec(blk, lambda *ids, s=s, pos=pos: (s.start + pos(*ids), 0)))
            side_out.append(pl.BlockSpec(blk, lambda *ids, pos=pos: (pos(*ids), 0)))
        side_shape.append(jax.ShapeDtypeStruct(full, _BF16))

    def kern(*refs):
        o0 = n_in + n_side
        s0 = o0 + n_out + n_side
        body(*refs[:n_in], *refs[o0:o0 + n_out], *refs[s0:],
             side=tuple(zip(refs[n_in:o0], refs[o0 + n_out:s0])))

    res = pl.pallas_call(
        kern, grid=grid,
        in_specs=list(in_specs) + side_in, out_specs=list(out_specs) + side_out,
        out_shape=list(out_shape) + side_shape, scratch_shapes=list(scratch),
        compiler_params=pltpu.CompilerParams(dimension_semantics=("arbitrary",) * len(grid),
                                             vmem_limit_bytes=vmem_limit_bytes),
        name=name,
    )(*args, *[s.src for s in side])
    return res[:n_out], res[n_out:]


def _mod_kernel(c_ref, w_ref, b_ref, o_ref, *, side):
    _convert(side)
    c = c_ref[...]
    a = (c * jax.nn.sigmoid(c)).astype(_BF16)
    o_ref[...] = _dot(a, w_ref[...].astype(_BF16)) + b_ref[...]


def _mod_call(c_all, w_ada, b_ada, *, side):
    rows, d = c_all.shape
    n = w_ada.shape[1]
    tn = ADALN_COL_TILE
    (mod,), copies = _run(
        _mod_kernel, grid=(n // tn,),
        in_specs=[pl.BlockSpec((rows, d), lambda j: (0, 0)),
                  pl.BlockSpec((d, tn), lambda j: (0, j)),
                  pl.BlockSpec((1, tn), lambda j: (0, j))],
        args=(c_all, w_ada, b_ada),
        out_specs=[pl.BlockSpec((rows, tn), lambda j: (0, j))],
        out_shape=[jax.ShapeDtypeStruct((rows, n), _F32)],
        side=side, name="adaln_mod")
    return mod, copies


def _s_gmlp_kernel(x_ref, sh_ref, sc_ref, gpre_ref, wv_ref, wu_ref, gv_ref, ws_ref, bias_ref,
                   ya_ref, h_ref, vn_ref, wvb_ref, wub_ref, v_scr, *, tm, tn, n_blk, slab, side):
    j = pl.program_id(0)
    d = n_blk * tn
    slabs = _slabs(tm, slab)
    n_slab = len(slabs)

    @pl.when(j == 0)
    def _():
        for t, r in enumerate(slabs):
            h_ref[r, :] = (_rms(x_ref[:, t * d:(t + 1) * d], gpre_ref[...]) * (1.0 + sc_ref[...])
                           + sh_ref[...]).astype(_BF16)

    @pl.when(j < n_blk)
    def _():
        wv = wv_ref[...].astype(_BF16)
        wvb_ref[...] = wv
        v_scr[j] = _dot(h_ref[...], wv)

    @pl.when(j == n_blk)
    def _():
        for t, r in enumerate(slabs):
            ss = 0.0
            for k in range(n_blk):
                vk = v_scr[k, r, :]
                ss = ss + jnp.sum(vk * vk, axis=-1, keepdims=True)
            rs = lax.rsqrt(ss * (1.0 / d) + EPS)
            for k in range(n_blk):
                vn = v_scr[k, r, :] * rs * gv_ref[:, k * tn:(k + 1) * tn]
                v_scr[k, r, :] = vn
                vn_ref[:, t * d + k * tn:t * d + (k + 1) * tn] = vn
        for t in reversed(range(n_slab)):
            for k in range(n_blk):
                c = slice(k * tn, (k + 1) * tn)
                acc = ws_ref[t * n_slab:t * n_slab + 1, c] * v_scr[k, slabs[0], :]
                for s in range(1, t + 1):
                    acc = acc + ws_ref[t * n_slab + s:t * n_slab + s + 1, c] * v_scr[k, slabs[s], :]
                v_scr[k, slabs[t], :] = acc + bias_ref[t:t + 1, c]

    @pl.when(j >= n_blk)
    def _():
        wu = wu_ref[...].astype(_BF16)
        wub_ref[...] = wu
        ya_ref[...] = (_dot(h_ref[...], wu) * v_scr[j - n_blk]).astype(_BF16)


def _s_gmlp_call(x, mod, gpre, w_in, gv, wvec, bvec, *, tn):
    slab = x.shape[0]
    d = gpre.shape[1]
    tm = x.shape[1] // d * slab
    n_blk = d // tn
    full = lambda a: pl.BlockSpec(a.shape, lambda j: (0,) * a.ndim)
    u_map = lambda j: (0, jnp.maximum(j - n_blk, 0))
    v_map = lambda j: (0, jnp.minimum(j, n_blk - 1))
    (ya, h, vn, w_v, w_u), _ = _run(
        functools.partial(_s_gmlp_kernel, tm=tm, tn=tn, n_blk=n_blk, slab=slab),
        grid=(2 * n_blk,),
        in_specs=[full(x),
                  pl.BlockSpec((slab, d), lambda j: (0, 0)), pl.BlockSpec((slab, d), lambda j: (0, 1)),
                  full(gpre),
                  pl.BlockSpec((d, tn), lambda j: (0, n_blk + jnp.minimum(j, n_blk - 1))),
                  pl.BlockSpec((d, tn), u_map),
                  full(gv), full(wvec), full(bvec)],
        args=(x, mod, mod, gpre, w_in, w_in, gv, wvec, bvec),
        out_specs=[pl.BlockSpec((tm, tn), u_map), pl.BlockSpec((tm, d), lambda j: (0, 0)), full(x),
                   pl.BlockSpec((d, tn), v_map), pl.BlockSpec((d, tn), u_map)],
        out_shape=[jax.ShapeDtypeStruct((tm, d), _BF16), jax.ShapeDtypeStruct((tm, d), _BF16),
                   jax.ShapeDtypeStruct(x.shape, _F32),
                   jax.ShapeDtypeStruct((d, d), _BF16), jax.ShapeDtypeStruct((d, d), _BF16)],
        scratch=[pltpu.VMEM((n_blk, tm, tn), _F32)],
        name="gmlp_sample")
    return ya, h, vn, w_v, w_u


def _s_shortconv_kernel(h_ref, wbg_ref, wcg_ref, wxb_ref, cw_ref, st0_ref, st1_ref,
                        yb_ref, t0_ref, t1_ref, *, slab, side):
    h = h_ref[...]
    bg = _dot(h, wbg_ref[...])
    p = _dot(h, wcg_ref[...]) * _dot(h, wxb_ref[...])
    cb, tail = _causal_conv_slabs(p, [st0_ref[...], st1_ref[...]], cw_ref, slab)
    yb_ref[...] = (bg * cb).astype(_BF16)
    t0_ref[...] = tail[0]
    t1_ref[...] = tail[1]


def _s_shortconv_call(h, w_bcx, cw, state, *, tn):
    tm, d = h.shape
    slab = state.shape[0]
    w = cw.shape[1]
    n_blk = w // tn
    wspec = lambda off: pl.BlockSpec((d, tn), lambda j: (0, off + j))
    sspec = lambda k: pl.BlockSpec((slab, tn), lambda j: (0, k * n_blk + j))
    tspec = pl.BlockSpec((slab, tn), lambda j: (0, j))
    (yb, t0, t1), _ = _run(
        functools.partial(_s_shortconv_kernel, slab=slab),
        grid=(n_blk,),
        in_specs=[pl.BlockSpec((tm, d), lambda j: (0, 0)),
                  wspec(0), wspec(n_blk), wspec(2 * n_blk),
                  pl.BlockSpec((CONV_K, tn), lambda j: (0, j)),
                  sspec(0), sspec(1)],
        args=(h, w_bcx, w_bcx, w_bcx, cw, state, state),
        out_specs=[pl.BlockSpec((tm, tn), lambda j: (0, j)), tspec, tspec],
        out_shape=[jax.ShapeDtypeStruct((tm, w), _BF16)] + [jax.ShapeDtypeStruct((slab, w), _F32)] * 2,
        name="shortconv_sample")
    return yb, t0, t1


def _merge_kernel(h_ref, ya_ref, yb_ref, wga_ref, wgb_ref, woa_ref, wob_ref, m_ref, *, tm, slab, side):
    _convert(side)
    for r in _slabs(tm, slab):
        h = h_ref[r, :]
        ga = jax.nn.sigmoid(_dot(h, wga_ref[...]))
        gb = jax.nn.sigmoid(_dot(h, wgb_ref[...]))
        m = ga * _dot(ya_ref[r, :], woa_ref[...]) + gb * _dot(yb_ref[r, :], wob_ref[...])
        m_ref[r, :] = m.astype(_BF16)


def _merge_call(h, ya, yb, w_gate, w_out_a, w_out_b, *, tm, tn, slab, side=(), name):
    m, d = h.shape
    n_blk = d // tn
    row = pl.BlockSpec((tm, d), lambda i, j: (i, 0))
    wspec = lambda off: pl.BlockSpec((d, tn), lambda i, j: (0, off + j))
    (mg,), copies = _run(
        functools.partial(_merge_kernel, tm=tm, slab=slab),
        grid=(m // tm, n_blk),
        in_specs=[row, row, row, wspec(0), wspec(n_blk), wspec(0), wspec(0)],
        args=(h, ya, yb, w_gate, w_gate, w_out_a, w_out_b),
        out_specs=[pl.BlockSpec((tm, tn), lambda i, j: (i, j))],
        out_shape=[jax.ShapeDtypeStruct((m, d), _BF16)],
        side=side, name=name)
    return mg, copies


def _s_proj_kernel(m_ref, wo_ref, x_ref, gt_ref, gpost_ref, sh_ref, sc_ref, gpre_ref,
                   x1_ref, h2_ref, y_scr, *, tm, tn, n_blk, slab, side):
    j = pl.program_id(0)
    d = n_blk * tn
    y_scr[j] = _dot(m_ref[...], wo_ref[...])

    @pl.when(j == n_blk - 1)
    def _():
        for t, r in enumerate(_slabs(tm, slab)):
            y = jnp.concatenate([y_scr[k, r, :] for k in range(n_blk)], axis=-1)
            x1, h2 = _post_mix(y, x_ref[:, t * d:(t + 1) * d], gt_ref[...], gpost_ref[...],
                               sh_ref[...], sc_ref[...], gpre_ref[...])
            x1_ref[r, :] = x1
            h2_ref[r, :] = h2


def _s_proj_call(mg, w_o, x, mod, gpost, gpre, *, tn):
    tm, d = mg.shape
    slab = x.shape[0]
    n_blk = d // tn
    row = pl.BlockSpec((tm, d), lambda j: (0, 0))
    vec = pl.BlockSpec((1, d), lambda j: (0, 0))
    mspec = lambda k: pl.BlockSpec((slab, d), lambda j: (0, k))
    (x1, h2), _ = _run(
        functools.partial(_s_proj_kernel, tm=tm, tn=tn, n_blk=n_blk, slab=slab),
        grid=(n_blk,),
        in_specs=[row, pl.BlockSpec((d, tn), lambda j: (0, j)),
                  pl.BlockSpec(x.shape, lambda j: (0, 0)), mspec(2), vec, mspec(3), mspec(4), vec],
        args=(mg, w_o, x, mod, gpost, mod, mod, gpre),
        out_specs=[row, row],
        out_shape=[jax.ShapeDtypeStruct((tm, d), _F32), jax.ShapeDtypeStruct((tm, d), _BF16)],
        scratch=[pltpu.VMEM((n_blk, tm, tn), _F32)],
        name="out_proj_sample")
    return x1, h2


def _s_ffn_kernel(h_ref, wa_ref, wb_ref, cw_ref, wd_ref, x1_ref, gt_ref, gpost_ref, st0_ref, st1_ref,
                  out_ref, t0_ref, t1_ref, *, tm, n_blk, slab, side):
    j = pl.program_id(0)
    d = x1_ref.shape[1]
    lanes = [slice(t * d, (t + 1) * d) for t in range(tm // slab)]

    @pl.when(j == 0)
    def _():
        out_ref[...] = jnp.zeros(out_ref.shape, _F32)

    prev = [st0_ref[...], st1_ref[...]]
    for c, r in enumerate(_slabs(tm, 2 * slab)):
        h = h_ref[r, :]
        ac, prev = _causal_conv_slabs(_dot(h, wa_ref[...]), prev, cw_ref, slab)
        g = (jax.nn.gelu(ac) * _dot(h, wb_ref[...])).astype(_BF16)
        f = _dot(g, wd_ref[...])
        out_ref[:, lanes[2 * c]] += f[:slab, :]
        out_ref[:, lanes[2 * c + 1]] += f[slab:, :]
    t0_ref[...] = prev[0]
    t1_ref[...] = prev[1]

    @pl.when(j == n_blk - 1)
    def _():
        for t, r in enumerate(_slabs(tm, slab)):
            out_ref[:, lanes[t]] = (x1_ref[r, :]
                                    + gt_ref[...] * _rms(out_ref[:, lanes[t]], gpost_ref[...]))


def _s_ffn_call(h2, w_a, w_b, cw, w_down, x1, mod, gpost, state, *, tn):
    tm, d = x1.shape
    slab = state.shape[0]
    f = cw.shape[1]
    n_blk = f // tn
    out_shape = (slab, tm // slab * d)
    row = pl.BlockSpec((tm, d), lambda j: (0, 0))
    wspec = pl.BlockSpec((d, tn), lambda j: (0, j))
    sspec = lambda k: pl.BlockSpec((slab, tn), lambda j: (0, k * n_blk + j))
    tspec = pl.BlockSpec((slab, tn), lambda j: (0, j))
    (out, t0, t1), _ = _run(
        functools.partial(_s_ffn_kernel, tm=tm, n_blk=n_blk, slab=slab),
        grid=(n_blk,),
        in_specs=[row, wspec, wspec, pl.BlockSpec((CONV_K, tn), lambda j: (0, j)),
                  pl.BlockSpec((tn, d), lambda j: (j, 0)), row,
                  pl.BlockSpec((slab, d), lambda j: (0, 5)), pl.BlockSpec((1, d), lambda j: (0, 0)),
                  sspec(0), sspec(1)],
        args=(h2, w_a, w_b, cw, w_down, x1, mod, gpost, state, state),
        out_specs=[pl.BlockSpec(out_shape, lambda j: (0, 0)), tspec, tspec],
        out_shape=[jax.ShapeDtypeStruct(out_shape, _F32)]
        + [jax.ShapeDtypeStruct((slab, f), _F32)] * 2,
        name="convffn_sample")
    return out, t0, t1


def _p_gmlp_kernel(x_hbm, sh_ref, sc_ref, gpre_ref, wv_ref, wu_ref, gv_ref, ws_ref, bias_ref,
                   ya_ref, h_ref, x_buf, x_sem, v_scr, wt_scr, *, tm, tn, n_blk, slab, side):
    i = pl.program_id(0)
    j = pl.program_id(1)
    d = n_blk * tn
    slabs = _slabs(tm, slab)

    def x_copy(tile):
        return pltpu.make_async_copy(x_hbm.at[pl.ds(pl.multiple_of(tile * tm, tm), tm), :],
                                     x_buf, x_sem)

    @pl.when((i == 0) & (j == 0))
    def _():
        x_copy(0).start()
        tril = (lax.broadcasted_iota(jnp.int32, (CHUNK, CHUNK), 0)
                >= lax.broadcasted_iota(jnp.int32, (CHUNK, CHUNK), 1))
        for g in range(d // GROUP):
            wt_scr[g] = jnp.where(tril, ws_ref[g], 0.0).astype(_BF16)

    @pl.when(j == 0)
    def _():
        x_copy(i).wait()
        _convert(side)
        for r in slabs:
            h = _rms(x_buf[r, :], gpre_ref[...]) * (1.0 + sc_ref[...]) + sh_ref[...]
            h = h.astype(_BF16)
            h_ref[r, :] = h
            v_scr[0, r, :] = _dot(h, wv_ref[...])

    @pl.when((j == 0) & (i + 1 < pl.num_programs(0)))
    def _():
        x_copy(i + 1).start()

    @pl.when((j > 0) & (j < n_blk))
    def _():
        _convert(side)
        for r in slabs:
            v_scr[j, r, :] = _dot(h_ref[r, :], wv_ref[...])

    def _gate():
        gpb = tn // GROUP
        for c in range(tm // CHUNK):
            r = slice(c * CHUNK, (c + 1) * CHUNK)
            ss = 0.0
            for k in range(n_blk):
                vk = v_scr[k, r, :]
                ss = ss + jnp.sum(vk * vk, axis=-1, keepdims=True)
            rs = lax.rsqrt(ss * (1.0 / d) + EPS)
            for k in range(n_blk):
                vb = (v_scr[k, r, :] * rs * gv_ref[:, k * tn:(k + 1) * tn]).astype(_BF16)
                for gg in range(gpb):
                    g = k * gpb + gg
                    lanes = slice(gg * GROUP, (gg + 1) * GROUP)
                    v_scr[k, r, lanes] = (_dot(wt_scr[g], vb[:, lanes])
                                          + bias_ref[:, g * GROUP:(g + 1) * GROUP])

    @pl.when(j == n_blk)
    def _():
        _convert(side)
        u = [_dot(h_ref[r, :], wu_ref[...]) for r in slabs]
        _gate()
        for r, ur in zip(slabs, u):
            ya_ref[r, :] = (ur * v_scr[0, r, :]).astype(_BF16)

    @pl.when(j > n_blk)
    def _():
        _convert(side)
        for r in slabs:
            ya_ref[r, :] = (_dot(h_ref[r, :], wu_ref[...]) * v_scr[j - n_blk, r, :]).astype(_BF16)


def _p_gmlp_call(x, mod, gpre, w_v, w_u, gv, ws, bias, *, tm, tn, slab, tiles_per_seq, mod_row0,
                 side):
    m, d = x.shape
    n_blk = d // tn
    full = lambda a: pl.BlockSpec(a.shape, lambda i, j: (0,) * a.ndim)
    once = lambda a: pl.BlockSpec(a.shape, lambda i, j: (0,) * a.ndim, pipeline_mode=pl.Buffered(1))
    mspec = lambda k: pl.BlockSpec((None, 1, d), lambda i, j: (mod_row0 + i // tiles_per_seq, 0, k))
    u_map = lambda i, j: (i, jnp.maximum(j - n_blk, 0))
    (ya, h), copies = _run(
        functools.partial(_p_gmlp_kernel, tm=tm, tn=tn, n_blk=n_blk, slab=slab),
        grid=(m // tm, 2 * n_blk),
        in_specs=[pl.BlockSpec(memory_space=pl.ANY), mspec(0), mspec(1), full(gpre),
                  pl.BlockSpec((d, tn), lambda i, j: (0, jnp.minimum(j, n_blk - 1))),
                  pl.BlockSpec((d, tn), lambda i, j: (0, jnp.maximum(j - n_blk, 0))),
                  full(gv), once(ws), once(bias)],
        args=(x, mod, mod, gpre, w_v, w_u, gv, ws, bias),
        out_specs=[pl.BlockSpec((tm, tn), u_map), pl.BlockSpec((tm, d), lambda i, j: (i, 0))],
        out_shape=[jax.ShapeDtypeStruct((m, d), _BF16), jax.ShapeDtypeStruct((m, d), _BF16)],
        scratch=[pltpu.VMEM((tm, d), _F32), pltpu.SemaphoreType.DMA(()),
                 pltpu.VMEM((n_blk, tm, tn), _F32), pltpu.VMEM((d // GROUP, CHUNK, CHUNK), _BF16)],
        side=side, name="gmlp_prompt", vmem_limit_bytes=VMEM_LIMIT_BYTES_WIDE)
    return ya, h, copies


def _p_shortconv_kernel(h_ref, wbg_ref, wcg_ref, wxb_ref, cw_ref, yb_ref, tail_ref, carry_scr,
                        *, tm, tiles_per_seq, slab, side):
    i = pl.program_id(0)
    j = pl.program_id(1)

    @pl.when(i % tiles_per_seq == 0)
    def _():
        carry_scr[j] = jnp.zeros(carry_scr.shape[1:], _F32)

    _convert(side)
    prev = carry_scr[j]
    cw = cw_ref.at[j]
    for r in _slabs(tm, slab):
        h = h_ref[r, :]
        bg = _dot(h, wbg_ref[...])
        p = _dot(h, wcg_ref[...]) * _dot(h, wxb_ref[...])
        yb_ref[r, :] = (bg * _causal_conv_rows(p, prev, cw)).astype(_BF16)
        prev = p[slab - SUBLANES:, :]
    carry_scr[j] = prev
    tail_ref[j] = prev


def _p_shortconv_call(h, w_bcx, cw, *, tm, tn, tiles_per_seq, slab, side):
    m, d = h.shape
    w = cw.shape[1]
    n_blk = w // tn
    wspec = lambda off: pl.BlockSpec((d, tn), lambda i, j: (0, off + j))
    (yb, tail), copies = _run(
        functools.partial(_p_shortconv_kernel, tm=tm, tiles_per_seq=tiles_per_seq, slab=slab),
        grid=(m // tm, n_blk),
        in_specs=[pl.BlockSpec((tm, d), lambda i, j: (i, 0)),
                  wspec(0), wspec(n_blk), wspec(2 * n_blk),
                  pl.BlockSpec((n_blk, CONV_K, tn), lambda i, j: (0, 0, 0))],
        args=(h, w_bcx, w_bcx, w_bcx, _col_blocks(cw, tn)),
        out_specs=[pl.BlockSpec((tm, tn), lambda i, j: (i, j)),
                   pl.BlockSpec((None, n_blk, SUBLANES, tn), lambda i, j: (i, 0, 0, 0))],
        out_shape=[jax.ShapeDtypeStruct((m, w), _BF16),
                   jax.ShapeDtypeStruct((m // tm, n_blk, SUBLANES, tn), _F32)],
        scratch=[pltpu.VMEM((n_blk, SUBLANES, tn), _F32)],
        side=side, name="shortconv_prompt")
    return yb, tail, copies


def _p_proj_kernel(m_ref, wo_ref, x_ref, gt_ref, gpost_ref, sh_ref, sc_ref, gpre_ref,
                   x1_ref, h2_ref, *, tm, slab, side):
    for r in _slabs(tm, slab):
        y = _dot(m_ref[r, :], wo_ref[...])
        x1, h2 = _post_mix(y, x_ref[r, :], gt_ref[...], gpost_ref[...], sh_ref[...], sc_ref[...],
                           gpre_ref[...])
        x1_ref[r, :] = x1
        h2_ref[r, :] = h2


def _p_proj_call(mg, w_o, x, mod, gpost, gpre, *, tm, slab, tiles_per_seq, mod_row0):
    m, d = x.shape
    row = pl.BlockSpec((tm, d), lambda i: (i, 0))
    vec = pl.BlockSpec((1, d), lambda i: (0, 0))
    mspec = lambda k: pl.BlockSpec((None, 1, d), lambda i: (mod_row0 + i // tiles_per_seq, 0, k))
    (x1, h2), _ = _run(
        functools.partial(_p_proj_kernel, tm=tm, slab=slab),
        grid=(m // tm,),
        in_specs=[row, pl.BlockSpec((d, d), lambda i: (0, 0)), row, mspec(2), vec, mspec(3),
                  mspec(4), vec],
        args=(mg, w_o, x, mod, gpost, mod, mod, gpre),
        out_specs=[row, row],
        out_shape=[jax.ShapeDtypeStruct((m, d), _F32), jax.ShapeDtypeStruct((m, d), _BF16)],
        name="out_proj_prompt")
    return x1, h2


def _p_ffn_kernel(h_ref, wa_ref, wb_ref, cw_ref, wd_ref, x1_hbm, gt_ref, gpost_ref,
                  out_ref, tail_ref, x1_buf, x1_sem, carry_scr, *, tm, n_blk, tiles_per_seq, slab,
                  side):
    i = pl.program_id(0)
    j = pl.program_id(1)
    x1_copy = pltpu.make_async_copy(x1_hbm.at[pl.ds(pl.multiple_of(i * tm, tm), tm), :],
                                    x1_buf, x1_sem)

    @pl.when(i % tiles_per_seq == 0)
    def _():
        carry_scr[j] = jnp.zeros(carry_scr.shape[1:], _F32)

    def step(first, last):
        prev = carry_scr[j]
        cw = cw_ref.at[j]
        for r in _slabs(tm, slab):
            h = h_ref[r, :]
            a = _dot(h, wa_ref[...])
            b = _dot(h, wb_ref[...])
            g = (jax.nn.gelu(_causal_conv_rows(a, prev, cw)) * b).astype(_BF16)
            prev = a[slab - SUBLANES:, :]
            f = _dot(g, wd_ref[...])
            acc = f if first else out_ref[r, :] + f
            if last:
                acc = x1_buf[r, :] + gt_ref[...] * _rms(acc, gpost_ref[...])
            out_ref[r, :] = acc
        carry_scr[j] = prev
        tail_ref[j] = prev

    @pl.when(j == 0)
    def _():
        x1_copy.start()
        step(True, False)

    @pl.when((j > 0) & (j < n_blk - 1))
    def _():
        step(False, False)

    @pl.when(j == n_blk - 1)
    def _():
        x1_copy.wait()
        step(False, True)


def _p_ffn_call(h2, w_a, w_b, cw, w_down, x1, mod, gpost, *, tm, tn, tiles_per_seq, slab, mod_row0):
    m, d = x1.shape
    f = cw.shape[1]
    n_blk = f // tn
    row = pl.BlockSpec((tm, d), lambda i, j: (i, 0))
    wspec = pl.BlockSpec((d, tn), lambda i, j: (0, j))
    (out, tail), _ = _run(
        functools.partial(_p_ffn_kernel, tm=tm, n_blk=n_blk, tiles_per_seq=tiles_per_seq, slab=slab),
        grid=(m // tm, n_blk),
        in_specs=[row, wspec, wspec,
                  pl.BlockSpec((n_blk, CONV_K, tn), lambda i, j: (0, 0, 0)),
                  pl.BlockSpec((tn, d), lambda i, j: (j, 0)),
                  pl.BlockSpec(memory_space=pl.ANY),
                  pl.BlockSpec((None, 1, d), lambda i, j: (mod_row0 + i // tiles_per_seq, 0, 5)),
                  pl.BlockSpec((1, d), lambda i, j: (0, 0))],
        args=(h2, w_a, w_b, _col_blocks(cw, tn), w_down, x1, mod, gpost),
        out_specs=[row, pl.BlockSpec((None, n_blk, SUBLANES, tn), lambda i, j: (i, 0, 0, 0))],
        out_shape=[jax.ShapeDtypeStruct((m, d), _F32),
                   jax.ShapeDtypeStruct((m // tm, n_blk, SUBLANES, tn), _F32)],
        scratch=[pltpu.VMEM((tm, d), _F32), pltpu.SemaphoreType.DMA(()),
                 pltpu.VMEM((n_blk, SUBLANES, tn), _F32)],
        name="convffn_prompt")
    return out, tail


def _cols(src, first_col, n_cols, n_blocks):
    block = n_cols // n_blocks
    assert block * n_blocks == n_cols and block % 128 == 0 and first_col % block == 0
    return _Side(src, 1, first_col // block, block, n_blocks)


def _layer(xs, xp, mod, st_b, st_f, p, *, seq_len, mod_row0):
    d = p["g_v"].shape[1]
    wb = p["conv_b_w"].shape[1]
    f = p["conv_f_w"].shape[1]
    tn_s = SAMPLE_COL_TILE
    mod_rows = mod.reshape(mod.shape[0], 1, N_MOD * d)

    def tiles(name):
        tm, tn, slab = PROMPT_TILES[name]
        return dict(tm=tm, tn=tn, slab=slab, tiles_per_seq=seq_len // tm)

    def n_steps(name):
        tm, tn, _ = PROMPT_TILES[name]
        cols = {"gmlp": 2 * d, "shortconv": wb, "merge": d, "ffn": f}[name]
        return xp.shape[0] // tm * (cols // tn)

    def without(kw, *names):
        return {k: v for k, v in kw.items() if k not in names}

    def blocks(n_cols, budget):
        return max(n for n in range(1, budget + 1) if n_cols % (128 * n) == 0)

    w_in, w_up = p["w_in"], p["w_up"]

    ya_s, h1_s, vn, w_v, w_u = _s_gmlp_call(xs, mod, p["g_pre_mix"], w_in, p["g_v"], p["wvec"],
                                            p["bvec"], tn=tn_s)
    n = n_steps("gmlp")
    ya_p, h1_p, (w_bcx,) = _p_gmlp_call(
        xp, mod_rows, p["g_pre_mix"], w_v, w_u, p["g_v"], p["w_s"], p["bias"], mod_row0=mod_row0,
        side=(_cols(w_in, 2 * d, 3 * wb, blocks(3 * wb, n)),),
        **tiles("gmlp"))

    yb_s, tb0, tb1 = _s_shortconv_call(h1_s, w_bcx, p["conv_b_w"], st_b, tn=tn_s)
    n = n_steps("shortconv")
    yb_p, tail_b, (w_gate, w_oa, w_ob, w_o, w_a, w_b) = _p_shortconv_call(
        h1_p, w_bcx, p["conv_b_w"],
        side=(_cols(w_in, 2 * d + 3 * wb, 2 * d, blocks(2 * d, n)),
              _cols(p["w_out_a"], 0, d, blocks(d, n)), _cols(p["w_out_b"], 0, d, blocks(d, n)),
              _cols(p["w_o"], 0, d, blocks(d, n)),
              _cols(w_up, 0, f, blocks(f, n)), _cols(w_up, f, f, blocks(f, n))),
        **tiles("shortconv"))

    tm_s = h1_s.shape[0]
    mg_s, _ = _merge_call(h1_s, ya_s, yb_s, w_gate, w_oa, w_ob, tm=tm_s, tn=tn_s, slab=tm_s // 2,
                          name="gated_merge_sample")
    n_row_blocks = max(k for k in range(1, n_steps("merge") + 1)
                       if f % k == 0 and (f // k) % (2 * SUBLANES) == 0)
    mg_p, (w_d,) = _merge_call(h1_p, ya_p, yb_p, w_gate, w_oa, w_ob, name="gated_merge_prompt",
                               side=(_Side(p["w_down"], 0, 0, f // n_row_blocks, n_row_blocks),),
                               **without(tiles("merge"), "tiles_per_seq"))

    x1_s, h2_s = _s_proj_call(mg_s, w_o, xs, mod, p["g_post_mix"], p["g_pre_ffn"], tn=tn_s)
    x1_p, h2_p = _p_proj_call(mg_p, w_o, xp, mod_rows, p["g_post_mix"], p["g_pre_ffn"],
                              mod_row0=mod_row0, **without(tiles("proj"), "tn"))

    out_s, tf0, tf1 = _s_ffn_call(h2_s, w_a, w_b, p["conv_f_w"], w_d, x1_s, mod, p["g_post_ffn"],
                                  st_f, tn=tn_s)
    out_p, tail_f = _p_ffn_call(h2_p, w_a, w_b, p["conv_f_w"], w_d, x1_p, mod_rows, p["g_post_ffn"],
                                mod_row0=mod_row0, **tiles("ffn"))
    return out_s, out_p, (tb0, tb1), (tf0, tf1), vn, tail_b, tail_f


def kernel(x_prompt, x_sample, c_prompt, c_sample, state_conv_b, state_conv_ffn, w_ada, b_ada, g_pre_mix, g_post_mix, w_in, g_v, w_s, b_s, conv_b_w, w_out_a, w_out_b, w_o, g_pre_ffn, g_post_ffn, w_up, conv_f_w, w_down):
    depth = w_in.shape[0]
    bp, seq, d = x_prompt.shape
    bs, tdec, _ = x_sample.shape
    n_groups = w_s.shape[1]
    assert bs == CHUNK and tdec <= CHUNK
    assert all(seq % tm == 0 for tm, _, _ in PROMPT_TILES.values())

    xp = x_prompt.reshape(bp * seq, d)
    xs = x_sample.reshape(bs, tdec * d)
    pad = (-(bp + bs)) % SUBLANES
    c_all = jnp.concatenate([c_sample, c_prompt, jnp.zeros((pad, d), _F32)], axis=0)

    pb, sb, pf, sf, sv = [], [], [], [], []
    for l in range(depth):
        mod, _ = _mod_call(c_all, w_ada[l], b_ada[l][None, :], side=())
        vec = lambda a: a[l][None, :]
        bias_full = jnp.repeat(jnp.transpose(b_s[l]), GROUP, axis=1)
        wvec = jnp.repeat(
            jnp.transpose(w_s[l][:, :tdec, :tdec], (1, 2, 0)).reshape(tdec * tdec, n_groups),
            GROUP, axis=1)
        p = {
            "w_in": w_in[l], "w_out_a": w_out_a[l], "w_out_b": w_out_b[l],
            "w_o": w_o[l], "w_up": w_up[l], "w_down": w_down[l],
            "g_pre_mix": vec(g_pre_mix), "g_post_mix": vec(g_post_mix), "g_v": vec(g_v),
            "g_pre_ffn": vec(g_pre_ffn), "g_post_ffn": vec(g_post_ffn),
            "conv_b_w": conv_b_w[l], "conv_f_w": conv_f_w[l],
            "w_s": w_s[l], "bias": bias_full, "wvec": wvec, "bvec": bias_full[:tdec],
        }
        st_b = state_conv_b[l].reshape(bs, -1)
        st_f = state_conv_ffn[l].reshape(bs, -1)
        xs, xp, sbt, sft, vn, tb, tf = _layer(xs, xp, mod, st_b, st_f, p, seq_len=seq, mod_row0=bs)

        def prompt_tail(t):
            n_tiles, n_blk, _, tn = t.shape
            t = t.reshape(bp, n_tiles // bp, n_blk, SUBLANES, tn)[:, -1, :, SUBLANES - (CONV_K - 1):, :]
            return jnp.transpose(t, (0, 2, 1, 3)).reshape(bp, CONV_K - 1, n_blk * tn)

        pb.append(prompt_tail(tb))
        pf.append(prompt_tail(tf))
        sb.append(jnp.stack(sbt, axis=1))
        sf.append(jnp.stack(sft, axis=1))
        sv.append(vn.reshape(bs, tdec, d))

    y_prompt = xp.reshape(bp, seq, d)
    return (y_prompt, xs.reshape(bs, tdec, d), jnp.stack(pb), jnp.stack(sb), jnp.stack(pf), jnp.stack(sf),
            jnp.stack(sv))
```

```python
import functools
from typing import NamedTuple

import jax
import jax.numpy as jnp
from jax import lax
from jax.experimental import pallas as pl
from jax.experimental.pallas import tpu as pltpu

EPS = 1e-6
CHUNK = 128
GROUP = 128
CONV_K = 3
N_MOD = 6
SUBLANES = 8
VMEM_LIMIT_BYTES = 56 * 1024 * 1024
VMEM_LIMIT_BYTES_WIDE = 60 * 1024 * 1024
PROMPT_TILES = {
    "gmlp": (1024, 1024, 256),
    "shortconv": (1024, 512, 256),
    "merge": (512, 1024, 256),
    "proj": (512, None, 256),
    "ffn": (1024, 512, 256),
}
SAMPLE_COL_TILE = 512
ADALN_COL_TILE = 1024

_BF16 = jnp.bfloat16
_F32 = jnp.float32


def _dot(a, b):
    return jnp.dot(a, b, preferred_element_type=_F32)


def _rms(xf, g):
    ms = jnp.mean(xf * xf, axis=-1, keepdims=True)
    return xf * lax.rsqrt(ms + EPS) * g


def _causal_conv_rows(p, prev, cw_ref):
    rows = p.shape[0]
    row = lax.broadcasted_iota(jnp.int32, (rows, 1), 0)
    m1 = jnp.where(row == 0, prev[7:8, :], pltpu.roll(p, 1, 0))
    m2 = jnp.where(row == 0, prev[6:7, :], jnp.where(row == 1, prev[7:8, :], pltpu.roll(p, 2, 0)))
    return cw_ref[0:1, :] * m2 + cw_ref[1:2, :] * m1 + cw_ref[2:3, :] * p


def _causal_conv_slabs(x, prev, cw_ref, slab):
    w0, w1, w2 = cw_ref[0:1, :], cw_ref[1:2, :], cw_ref[2:3, :]
    seq = list(prev) + [x[t * slab:(t + 1) * slab, :] for t in range(x.shape[0] // slab)]
    y = [w0 * seq[t] + w1 * seq[t + 1] + w2 * seq[t + 2] for t in range(len(seq) - 2)]
    return jnp.concatenate(y, axis=0), seq[-2:]


def _slabs(tm, slab):
    return [slice(s * slab, (s + 1) * slab) for s in range(tm // slab)]


def _col_blocks(a, tn):
    r, c = a.shape
    return jnp.transpose(a.reshape(r, c // tn, tn), (1, 0, 2))


def _post_mix(y, x, gt, gpost, sh, sc, gpre):
    x1 = x + gt * _rms(y, gpost)
    h2 = _rms(x1, gpre) * (1.0 + sc) + sh
    return x1, h2.astype(_BF16)


class _Side(NamedTuple):
    src: jax.Array
    axis: int
    start: int
    block: int
    n_blocks: int


def _convert(side):
    for src_ref, dst_ref in side:
        dst_ref[...] = src_ref[...].astype(_BF16)


def _run(body, *, grid, in_specs, args, out_specs, out_shape, scratch=(), side=(), name,
         vmem_limit_bytes=VMEM_LIMIT_BYTES):
    n_in, n_out, n_side = len(args), len(out_shape), len(side)
    n_steps = functools.reduce(lambda a, b: a * b, grid)
    step_of = (lambda i: i) if len(grid) == 1 else (lambda i, j: i * grid[1] + j)
    side_in, side_out, side_shape = [], [], []
    for s in side:
        assert s.n_blocks <= n_steps, (name, s.n_blocks, n_steps)
        other = s.src.shape[1 - s.axis]
        pos = lambda *ids, s=s: jnp.minimum(step_of(*ids), s.n_blocks - 1)
        if s.axis == 1:
            blk, full = (other, s.block), (other, s.block * s.n_blocks)
            side_in.append(pl.BlockSpec(blk, lambda *ids, s=s, pos=pos: (0, s.start + pos(*ids))))
            side_out.append(pl.BlockSpec(blk, lambda *ids, pos=pos: (0, pos(*ids))))
        else:
            blk, full = (s.block, other), (s.block * s.n_blocks, other)
            side_in.append(pl.BlockSpec(blk, lambda *ids, s=s, pos=pos: (s.start + pos(*ids), 0)))
            side_out.append(pl.BlockSpec(blk, lambda *ids, pos=pos: (pos(*ids), 0)))
        side_shape.append(jax.ShapeDtypeStruct(full, _BF16))

    def kern(*refs):
        o0 = n_in + n_side
        s0 = o0 + n_out + n_side
        body(*refs[:n_in], *refs[o0:o0 + n_out], *refs[s0:],
             side=tuple(zip(refs[n_in:o0], refs[o0 + n_out:s0])))

    res = pl.pallas_call(
        kern, grid=grid,
        in_specs=list(in_specs) + side_in, out_specs=list(out_specs) + side_out,
        out_shape=list(out_shape) + side_shape, scratch_shapes=list(scratch),
        compiler_params=pltpu.CompilerParams(dimension_semantics=("arbitrary",) * len(grid),
                                             vmem_limit_bytes=vmem_limit_bytes),
        name=name,
    )(*args, *[s.src for s in side])
    return res[:n_out], res[n_out:]


def _mod_kernel(c_ref, w_ref, b_ref, o_ref, *, side):
    _convert(side)
    c = c_ref[...]
    a = (c * jax.nn.sigmoid(c)).astype(_BF16)
    o_ref[...] = _dot(a, w_ref[...].astype(_BF16)) + b_ref[...]


def _mod_call(c_all, w_ada, b_ada, *, side):
    rows, d = c_all.shape
    n = w_ada.shape[1]
    tn = ADALN_COL_TILE
    (mod,), copies = _run(
        _mod_kernel, grid=(n // tn,),
        in_specs=[pl.BlockSpec((rows, d), lambda j: (0, 0)),
                  pl.BlockSpec((d, tn), lambda j: (0, j)),
                  pl.BlockSpec((1, tn), lambda j: (0, j))],
        args=(c_all, w_ada, b_ada),
        out_specs=[pl.BlockSpec((rows, tn), lambda j: (0, j))],
        out_shape=[jax.ShapeDtypeStruct((rows, n), _F32)],
        side=side, name="adaln_mod")
    return mod, copies


def _s_gmlp_kernel(x_ref, sh_ref, sc_ref, gpre_ref, wv_ref, wu_ref, gv_ref, ws_ref, bias_ref,
                   ya_ref, h_ref, vn_ref, wvb_ref, wub_ref, v_scr, *, tm, tn, n_blk, slab, side):
    j = pl.program_id(0)
    d = n_blk * tn
    slabs = _slabs(tm, slab)
    n_slab = len(slabs)

    @pl.when(j == 0)
    def _():
        for t, r in enumerate(slabs):
            h_ref[r, :] = (_rms(x_ref[:, t * d:(t + 1) * d], gpre_ref[...]) * (1.0 + sc_ref[...])
                           + sh_ref[...]).astype(_BF16)

    @pl.when(j < n_blk)
    def _():
        wv = wv_ref[...].astype(_BF16)
        wvb_ref[...] = wv
        v_scr[j] = _dot(h_ref[...], wv)

    @pl.when(j == n_blk)
    def _():
        for t, r in enumerate(slabs):
            ss = 0.0
            for k in range(n_blk):
                vk = v_scr[k, r, :]
                ss = ss + jnp.sum(vk * vk, axis=-1, keepdims=True)
            rs = lax.rsqrt(ss * (1.0 / d) + EPS)
            for k in range(n_blk):
                vn = v_scr[k, r, :] * rs * gv_ref[:, k * tn:(k + 1) * tn]
                v_scr[k, r, :] = vn
                vn_ref[:, t * d + k * tn:t * d + (k + 1) * tn] = vn
        for t in reversed(range(n_slab)):
            for k in range(n_blk):
                c = slice(k * tn, (k + 1) * tn)
                acc = ws_ref[t * n_slab:t * n_slab + 1, c] * v_scr[k, slabs[0], :]
                for s in range(1, t + 1):
                    acc = acc + ws_ref[t * n_slab + s:t * n_slab + s + 1, c] * v_scr[k, slabs[s], :]
                v_scr[k, slabs[t], :] = acc + bias_ref[t:t + 1, c]

    @pl.when(j >= n_blk)
    def _():
        wu = wu_ref[...].astype(_BF16)
        wub_ref[...] = wu
        ya_ref[...] = (_dot(h_ref[...], wu) * v_scr[j - n_blk]).astype(_BF16)


def _s_gmlp_call(x, mod, gpre, w_in, gv, wvec, bvec, *, tn):
    slab = x.shape[0]
    d = gpre.shape[1]
    tm = x.shape[1] // d * slab
    n_blk = d // tn
    full = lambda a: pl.BlockSpec(a.shape, lambda j: (0,) * a.ndim)
    u_map = lambda j: (0, jnp.maximum(j - n_blk, 0))
    v_map = lambda j: (0, jnp.minimum(j, n_blk - 1))
    (ya, h, vn, w_v, w_u), _ = _run(
        functools.partial(_s_gmlp_kernel, tm=tm, tn=tn, n_blk=n_blk, slab=slab),
        grid=(2 * n_blk,),
        in_specs=[full(x),
                  pl.BlockSpec((slab, d), lambda j: (0, 0)), pl.BlockSpec((slab, d), lambda j: (0, 1)),
                  full(gpre),
                  pl.BlockSpec((d, tn), lambda j: (0, n_blk + jnp.minimum(j, n_blk - 1))),
                  pl.BlockSpec((d, tn), u_map),
                  full(gv), full(wvec), full(bvec)],
        args=(x, mod, mod, gpre, w_in, w_in, gv, wvec, bvec),
        out_specs=[pl.BlockSpec((tm, tn), u_map), pl.BlockSpec((tm, d), lambda j: (0, 0)), full(x),
                   pl.BlockSpec((d, tn), v_map), pl.BlockSpec((d, tn), u_map)],
        out_shape=[jax.ShapeDtypeStruct((tm, d), _BF16), jax.ShapeDtypeStruct((tm, d), _BF16),
                   jax.ShapeDtypeStruct(x.shape, _F32),
                   jax.ShapeDtypeStruct((d, d), _BF16), jax.ShapeDtypeStruct((d, d), _BF16)],
        scratch=[pltpu.VMEM((n_blk, tm, tn), _F32)],
        name="gmlp_sample")
    return ya, h, vn, w_v, w_u


def _s_shortconv_kernel(h_ref, wbg_ref, wcg_ref, wxb_ref, cw_ref, st0_ref, st1_ref,
                        yb_ref, t0_ref, t1_ref, *, slab, side):
    h = h_ref[...]
    bg = _dot(h, wbg_ref[...])
    p = _dot(h, wcg_ref[...]) * _dot(h, wxb_ref[...])
    cb, tail = _causal_conv_slabs(p, [st0_ref[...], st1_ref[...]], cw_ref, slab)
    yb_ref[...] = (bg * cb).astype(_BF16)
    t0_ref[...] = tail[0]
    t1_ref[...] = tail[1]


def _s_shortconv_call(h, w_bcx, cw, state, *, tn):
    tm, d = h.shape
    slab = state.shape[0]
    w = cw.shape[1]
    n_blk = w // tn
    wspec = lambda off: pl.BlockSpec((d, tn), lambda j: (0, off + j))
    sspec = lambda k: pl.BlockSpec((slab, tn), lambda j: (0, k * n_blk + j))
    tspec = pl.BlockSpec((slab, tn), lambda j: (0, j))
    (yb, t0, t1), _ = _run(
        functools.partial(_s_shortconv_kernel, slab=slab),
        grid=(n_blk,),
        in_specs=[pl.BlockSpec((tm, d), lambda j: (0, 0)),
                  wspec(0), wspec(n_blk), wspec(2 * n_blk),
                  pl.BlockSpec((CONV_K, tn), lambda j: (0, j)),
                  sspec(0), sspec(1)],
        args=(h, w_bcx, w_bcx, w_bcx, cw, state, state),
        out_specs=[pl.BlockSpec((tm, tn), lambda j: (0, j)), tspec, tspec],
        out_shape=[jax.ShapeDtypeStruct((tm, w), _BF16)] + [jax.ShapeDtypeStruct((slab, w), _F32)] * 2,
        name="shortconv_sample")
    return yb, t0, t1


def _merge_kernel(h_ref, ya_ref, yb_ref, wga_ref, wgb_ref, woa_ref, wob_ref, m_ref, *, tm, slab, side):
    _convert(side)
    for r in _slabs(tm, slab):
        h = h_ref[r, :]
        ga = jax.nn.sigmoid(_dot(h, wga_ref[...]))
        gb = jax.nn.sigmoid(_dot(h, wgb_ref[...]))
        m = ga * _dot(ya_ref[r, :], woa_ref[...]) + gb * _dot(yb_ref[r, :], wob_ref[...])
        m_ref[r, :] = m.astype(_BF16)


def _merge_call(h, ya, yb, w_gate, w_out_a, w_out_b, *, tm, tn, slab, side=(), name):
    m, d = h.shape
    n_blk = d // tn
    row = pl.BlockSpec((tm, d), lambda i, j: (i, 0))
    wspec = lambda off: pl.BlockSpec((d, tn), lambda i, j: (0, off + j))
    (mg,), copies = _run(
        functools.partial(_merge_kernel, tm=tm, slab=slab),
        grid=(m // tm, n_blk),
        in_specs=[row, row, row, wspec(0), wspec(n_blk), wspec(0), wspec(0)],
        args=(h, ya, yb, w_gate, w_gate, w_out_a, w_out_b),
        out_specs=[pl.BlockSpec((tm, tn), lambda i, j: (i, j))],
        out_shape=[jax.ShapeDtypeStruct((m, d), _BF16)],
        side=side, name=name)
    return mg, copies


def _s_proj_kernel(m_ref, wo_ref, x_ref, gt_ref, gpost_ref, sh_ref, sc_ref, gpre_ref,
                   x1_ref, h2_ref, y_scr, *, tm, tn, n_blk, slab, side):
    j = pl.program_id(0)
    d = n_blk * tn
    y_scr[j] = _dot(m_ref[...], wo_ref[...])

    @pl.when(j == n_blk - 1)
    def _():
        for t, r in enumerate(_slabs(tm, slab)):
            y = jnp.concatenate([y_scr[k, r, :] for k in range(n_blk)], axis=-1)
            x1, h2 = _post_mix(y, x_ref[:, t * d:(t + 1) * d], gt_ref[...], gpost_ref[...],
                               sh_ref[...], sc_ref[...], gpre_ref[...])
            x1_ref[r, :] = x1
            h2_ref[r, :] = h2


def _s_proj_call(mg, w_o, x, mod, gpost, gpre, *, tn):
    tm, d = mg.shape
    slab = x.shape[0]
    n_blk = d // tn
    row = pl.BlockSpec((tm, d), lambda j: (0, 0))
    vec = pl.BlockSpec((1, d), lambda j: (0, 0))
    mspec = lambda k: pl.BlockSpec((slab, d), lambda j: (0, k))
    (x1, h2), _ = _run(
        functools.partial(_s_proj_kernel, tm=tm, tn=tn, n_blk=n_blk, slab=slab),
        grid=(n_blk,),
        in_specs=[row, pl.BlockSpec((d, tn), lambda j: (0, j)),
                  pl.BlockSpec(x.shape, lambda j: (0, 0)), mspec(2), vec, mspec(3), mspec(4), vec],
        args=(mg, w_o, x, mod, gpost, mod, mod, gpre),
        out_specs=[row, row],
        out_shape=[jax.ShapeDtypeStruct((tm, d), _F32), jax.ShapeDtypeStruct((tm, d), _BF16)],
        scratch=[pltpu.VMEM((n_blk, tm, tn), _F32)],
        name="out_proj_sample")
    return x1, h2


def _s_ffn_kernel(h_ref, wa_ref, wb_ref, cw_ref, wd_ref, x1_ref, gt_ref, gpost_ref, st0_ref, st1_ref,
                  out_ref, t0_ref, t1_ref, *, tm, n_blk, slab, side):
    j = pl.program_id(0)
    d = x1_ref.shape[1]
    lanes = [slice(t * d, (t + 1) * d) for t in range(tm // slab)]

    @pl.when(j == 0)
    def _():
        out_ref[...] = jnp.zeros(out_ref.shape, _F32)

    prev = [st0_ref[...], st1_ref[...]]
    for c, r in enumerate(_slabs(tm, 2 * slab)):
        h = h_ref[r, :]
        ac, prev = _causal_conv_slabs(_dot(h, wa_ref[...]), prev, cw_ref, slab)
        g = (jax.nn.gelu(ac) * _dot(h, wb_ref[...])).astype(_BF16)
        f = _dot(g, wd_ref[...])
        out_ref[:, lanes[2 * c]] += f[:slab, :]
        out_ref[:, lanes[2 * c + 1]] += f[slab:, :]
    t0_ref[...] = prev[0]
    t1_ref[...] = prev[1]

    @pl.when(j == n_blk - 1)
    def _():
        for t, r in enumerate(_slabs(tm, slab)):
            out_ref[:, lanes[t]] = (x1_ref[r, :]
                                    + gt_ref[...] * _rms(out_ref[:, lanes[t]], gpost_ref[...]))


def _s_ffn_call(h2, w_a, w_b, cw, w_down, x1, mod, gpost, state, *, tn):
    tm, d = x1.shape
    slab = state.shape[0]
    f = cw.shape[1]
    n_blk = f // tn
    out_shape = (slab, tm // slab * d)
    row = pl.BlockSpec((tm, d), lambda j: (0, 0))
    wspec = pl.BlockSpec((d, tn), lambda j: (0, j))
    sspec = lambda k: pl.BlockSpec((slab, tn), lambda j: (0, k * n_blk + j))
    tspec = pl.BlockSpec((slab, tn), lambda j: (0, j))
    (out, t0, t1), _ = _run(
        functools.partial(_s_ffn_kernel, tm=tm, n_blk=n_blk, slab=slab),
        grid=(n_blk,),
        in_specs=[row, wspec, wspec, pl.BlockSpec((CONV_K, tn), lambda j: (0, j)),
                  pl.BlockSpec((tn, d), lambda j: (j, 0)), row,
                  pl.BlockSpec((slab, d), lambda j: (0, 5)), pl.BlockSpec((1, d), lambda j: (0, 0)),
                  sspec(0), sspec(1)],
        args=(h2, w_a, w_b, cw, w_down, x1, mod, gpost, state, state),
        out_specs=[pl.BlockSpec(out_shape, lambda j: (0, 0)), tspec, tspec],
        out_shape=[jax.ShapeDtypeStruct(out_shape, _F32)]
        + [jax.ShapeDtypeStruct((slab, f), _F32)] * 2,
        name="convffn_sample")
    return out, t0, t1


def _p_gmlp_kernel(x_hbm, sh_ref, sc_ref, gpre_ref, wv_ref, wu_ref, gv_ref, ws_ref, bias_ref,
                   ya_ref, h_ref, x_buf, x_sem, v_scr, wt_scr, *, tm, tn, n_blk, slab, side):
    i = pl.program_id(0)
    j = pl.program_id(1)
    d = n_blk * tn
    slabs = _slabs(tm, slab)

    def x_copy(tile):
        return pltpu.make_async_copy(x_hbm.at[pl.ds(pl.multiple_of(tile * tm, tm), tm), :],
                                     x_buf, x_sem)

    @pl.when((i == 0) & (j == 0))
    def _():
        x_copy(0).start()
        tril = (lax.broadcasted_iota(jnp.int32, (CHUNK, CHUNK), 0)
                >= lax.broadcasted_iota(jnp.int32, (CHUNK, CHUNK), 1))
        for g in range(d // GROUP):
            wt_scr[g] = jnp.where(tril, ws_ref[g], 0.0).astype(_BF16)

    @pl.when(j == 0)
    def _():
        x_copy(i).wait()
        _convert(side)
        for r in slabs:
            h = _rms(x_buf[r, :], gpre_ref[...]) * (1.0 + sc_ref[...]) + sh_ref[...]
            h = h.astype(_BF16)
            h_ref[r, :] = h
            v_scr[0, r, :] = _dot(h, wv_ref[...])

    @pl.when((j == 0) & (i + 1 < pl.num_programs(0)))
    def _():
        x_copy(i + 1).start()

    @pl.when((j > 0) & (j < n_blk))
    def _():
        _convert(side)
        for r in slabs:
            v_scr[j, r, :] = _dot(h_ref[r, :], wv_ref[...])

    def _gate():
        gpb = tn // GROUP
        for c in range(tm // CHUNK):
            r = slice(c * CHUNK, (c + 1) * CHUNK)
            ss = 0.0
            for k in range(n_blk):
                vk = v_scr[k, r, :]
                ss = ss + jnp.sum(vk * vk, axis=-1, keepdims=True)
            rs = lax.rsqrt(ss * (1.0 / d) + EPS)
            for k in range(n_blk):
                vb = (v_scr[k, r, :] * rs * gv_ref[:, k * tn:(k + 1) * tn]).astype(_BF16)
                for gg in range(gpb):
                    g = k * gpb + gg
                    lanes = slice(gg * GROUP, (gg + 1) * GROUP)
                    v_scr[k, r, lanes] = (_dot(wt_scr[g], vb[:, lanes])
                                          + bias_ref[:, g * GROUP:(g + 1) * GROUP])

    @pl.when(j == n_blk)
    def _():
        _convert(side)
        u = [_dot(h_ref[r, :], wu_ref[...]) for r in slabs]
        _gate()
        for r, ur in zip(slabs, u):
            ya_ref[r, :] = (ur * v_scr[0, r, :]).astype(_BF16)

    @pl.when(j > n_blk)
    def _():
        _convert(side)
        for r in slabs:
            ya_ref[r, :] = (_dot(h_ref[r, :], wu_ref[...]) * v_scr[j - n_blk, r, :]).astype(_BF16)


def _p_gmlp_call(x, mod, gpre, w_v, w_u, gv, ws, bias, *, tm, tn, slab, tiles_per_seq, mod_row0,
                 side):
    m, d = x.shape
    n_blk = d // tn
    full = lambda a: pl.BlockSpec(a.shape, lambda i, j: (0,) * a.ndim)
    once = lambda a: pl.BlockSpec(a.shape, lambda i, j: (0,) * a.ndim, pipeline_mode=pl.Buffered(1))
    mspec = lambda k: pl.BlockSpec((None, 1, d), lambda i, j: (mod_row0 + i // tiles_per_seq, 0, k))
    u_map = lambda i, j: (i, jnp.maximum(j - n_blk, 0))
    (ya, h), copies = _run(
        functools.partial(_p_gmlp_kernel, tm=tm, tn=tn, n_blk=n_blk, slab=slab),
        grid=(m // tm, 2 * n_blk),
        in_specs=[pl.BlockSpec(memory_space=pl.ANY), mspec(0), mspec(1), full(gpre),
                  pl.BlockSpec((d, tn), lambda i, j: (0, jnp.minimum(j, n_blk - 1))),
                  pl.BlockSpec((d, tn), lambda i, j: (0, jnp.maximum(j - n_blk, 0))),
                  full(gv), once(ws), once(bias)],
        args=(x, mod, mod, gpre, w_v, w_u, gv, ws, bias),
        out_specs=[pl.BlockSpec((tm, tn), u_map), pl.BlockSpec((tm, d), lambda i, j: (i, 0))],
        out_shape=[jax.ShapeDtypeStruct((m, d), _BF16), jax.ShapeDtypeStruct((m, d), _BF16)],
        scratch=[pltpu.VMEM((tm, d), _F32), pltpu.SemaphoreType.DMA(()),
                 pltpu.VMEM((n_blk, tm, tn), _F32), pltpu.VMEM((d // GROUP, CHUNK, CHUNK), _BF16)],
        side=side, name="gmlp_prompt", vmem_limit_bytes=VMEM_LIMIT_BYTES_WIDE)
    return ya, h, copies


def _p_shortconv_kernel(h_ref, wbg_ref, wcg_ref, wxb_ref, cw_ref, yb_ref, tail_ref, carry_scr,
                        *, tm, tiles_per_seq, slab, side):
    i = pl.program_id(0)
    j = pl.program_id(1)

    @pl.when(i % tiles_per_seq == 0)
    def _():
        carry_scr[j] = jnp.zeros(carry_scr.shape[1:], _F32)

    _convert(side)
    prev = carry_scr[j]
    cw = cw_ref.at[j]
    for r in _slabs(tm, slab):
        h = h_ref[r, :]
        bg = _dot(h, wbg_ref[...])
        p = _dot(h, wcg_ref[...]) * _dot(h, wxb_ref[...])
        yb_ref[r, :] = (bg * _causal_conv_rows(p, prev, cw)).astype(_BF16)
        prev = p[p.shape[0] - SUBLANES:, :]
    carry_scr[j] = prev
    tail_ref[j] = prev


def _p_shortconv_call(h, w_bcx, cw, *, tm, tn, tiles_per_seq, slab, side):
    m, d = h.shape
    w = cw.shape[1]
    n_blk = w // tn
    wspec = lambda off: pl.BlockSpec((d, tn), lambda i, j: (0, off + j))
    (yb, tail), copies = _run(
        functools.partial(_p_shortconv_kernel, tm=tm, tiles_per_seq=tiles_per_seq, slab=slab),
        grid=(m // tm, n_blk),
        in_specs=[pl.BlockSpec((tm, d), lambda i, j: (i, 0)),
                  wspec(0), wspec(n_blk), wspec(2 * n_blk),
                  pl.BlockSpec((n_blk, CONV_K, tn), lambda i, j: (0, 0, 0))],
        args=(h, w_bcx, w_bcx, w_bcx, _col_blocks(cw, tn)),
        out_specs=[pl.BlockSpec((tm, tn), lambda i, j: (i, j)),
                   pl.BlockSpec((None, n_blk, SUBLANES, tn), lambda i, j: (i, 0, 0, 0))],
        out_shape=[jax.ShapeDtypeStruct((m, w), _BF16),
                   jax.ShapeDtypeStruct((m // tm, n_blk, SUBLANES, tn), _F32)],
        scratch=[pltpu.VMEM((n_blk, SUBLANES, tn), _F32)],
        side=side, name="shortconv_prompt")
    return yb, tail, copies


def _p_proj_kernel(m_ref, wo_ref, x_ref, gt_ref, gpost_ref, sh_ref, sc_ref, gpre_ref,
                   x1_ref, h2_ref, *, tm, slab, side):
    for r in _slabs(tm, slab):
        y = _dot(m_ref[r, :], wo_ref[...])
        x1, h2 = _post_mix(y, x_ref[r, :], gt_ref[...], gpost_ref[...], sh_ref[...], sc_ref[...],
                           gpre_ref[...])
        x1_ref[r, :] = x1
        h2_ref[r, :] = h2


def _p_proj_call(mg, w_o, x, mod, gpost, gpre, *, tm, slab, tiles_per_seq, mod_row0):
    m, d = x.shape
    row = pl.BlockSpec((tm, d), lambda i: (i, 0))
    vec = pl.BlockSpec((1, d), lambda i: (0, 0))
    mspec = lambda k: pl.BlockSpec((None, 1, d), lambda i: (mod_row0 + i // tiles_per_seq, 0, k))
    (x1, h2), _ = _run(
        functools.partial(_p_proj_kernel, tm=tm, slab=slab),
        grid=(m // tm,),
        in_specs=[row, pl.BlockSpec((d, d), lambda i: (0, 0)), row, mspec(2), vec, mspec(3),
                  mspec(4), vec],
        args=(mg, w_o, x, mod, gpost, mod, mod, gpre),
        out_specs=[row, row],
        out_shape=[jax.ShapeDtypeStruct((m, d), _F32), jax.ShapeDtypeStruct((m, d), _BF16)],
        name="out_proj_prompt")
    return x1, h2


def _p_ffn_kernel(h_ref, wa_ref, wb_ref, cw_ref, wd_ref, x1_hbm, gt_ref, gpost_ref,
                  out_ref, tail_ref, x1_buf, x1_sem, carry_scr, *, tm, n_blk, tiles_per_seq, slab,
                  side):
    i = pl.program_id(0)
    j = pl.program_id(1)
    x1_copy = pltpu.make_async_copy(x1_hbm.at[pl.ds(pl.multiple_of(i * tm, tm), tm), :],
                                    x1_buf, x1_sem)

    @pl.when(i % tiles_per_seq == 0)
    def _():
        carry_scr[j] = jnp.zeros(carry_scr.shape[1:], _F32)

    def step(first, last):
        prev = carry_scr[j]
        cw = cw_ref.at[j]
        slabs = _slabs(tm, slab)
        gs = []
        for r in slabs:
            h = h_ref[r, :]
            a = _dot(h, wa_ref[...])
            b = _dot(h, wb_ref[...])
            gs.append((jax.nn.gelu(_causal_conv_rows(a, prev, cw)) * b).astype(_BF16))
            prev = a[slab - SUBLANES:, :]
        for r, g in zip(slabs, gs):
            f = _dot(g, wd_ref[...])
            acc = f if first else out_ref[r, :] + f
            if last:
                acc = x1_buf[r, :] + gt_ref[...] * _rms(acc, gpost_ref[...])
            out_ref[r, :] = acc
        carry_scr[j] = prev
        tail_ref[j] = prev

    @pl.when(j == 0)
    def _():
        x1_copy.start()
        step(True, False)

    @pl.when((j > 0) & (j < n_blk - 1))
    def _():
        step(False, False)

    @pl.when(j == n_blk - 1)
    def _():
        x1_copy.wait()
        step(False, True)


def _p_ffn_call(h2, w_a, w_b, cw, w_down, x1, mod, gpost, *, tm, tn, tiles_per_seq, slab, mod_row0):
    m, d = x1.shape
    f = cw.shape[1]
    n_blk = f // tn
    row = pl.BlockSpec((tm, d), lambda i, j: (i, 0))
    wspec = pl.BlockSpec((d, tn), lambda i, j: (0, j))
    (out, tail), _ = _run(
        functools.partial(_p_ffn_kernel, tm=tm, n_blk=n_blk, tiles_per_seq=tiles_per_seq, slab=slab),
        grid=(m // tm, n_blk),
        in_specs=[row, wspec, wspec,
                  pl.BlockSpec((n_blk, CONV_K, tn), lambda i, j: (0, 0, 0)),
                  pl.BlockSpec((tn, d), lambda i, j: (j, 0)),
                  pl.BlockSpec(memory_space=pl.ANY),
                  pl.BlockSpec((None, 1, d), lambda i, j: (mod_row0 + i // tiles_per_seq, 0, 5)),
                  pl.BlockSpec((1, d), lambda i, j: (0, 0))],
        args=(h2, w_a, w_b, _col_blocks(cw, tn), w_down, x1, mod, gpost),
        out_specs=[row, pl.BlockSpec((None, n_blk, SUBLANES, tn), lambda i, j: (i, 0, 0, 0))],
        out_shape=[jax.ShapeDtypeStruct((m, d), _F32),
                   jax.ShapeDtypeStruct((m // tm, n_blk, SUBLANES, tn), _F32)],
        scratch=[pltpu.VMEM((tm, d), _F32), pltpu.SemaphoreType.DMA(()),
                 pltpu.VMEM((n_blk, SUBLANES, tn), _F32)],
        name="convffn_prompt")
    return out, tail


def _cols(src, first_col, n_cols, n_blocks):
    block = n_cols // n_blocks
    assert block * n_blocks == n_cols and block % 128 == 0 and first_col % block == 0
    return _Side(src, 1, first_col // block, block, n_blocks)


def _layer(xs, xp, mod, st_b, st_f, p, *, seq_len, mod_row0):
    d = p["g_v"].shape[1]
    wb = p["conv_b_w"].shape[1]
    f = p["conv_f_w"].shape[1]
    tn_s = SAMPLE_COL_TILE
    mod_rows = mod.reshape(mod.shape[0], 1, N_MOD * d)

    def tiles(name):
        tm, tn, slab = PROMPT_TILES[name]
        return dict(tm=tm, tn=tn, slab=slab, tiles_per_seq=seq_len // tm)

    def n_steps(name):
        tm, tn, _ = PROMPT_TILES[name]
        cols = {"gmlp": 2 * d, "shortconv": wb, "merge": d, "ffn": f}[name]
        return xp.shape[0] // tm * (cols // tn)

    def without(kw, *names):
        return {k: v for k, v in kw.items() if k not in names}

    def blocks(n_cols, budget):
        return max(n for n in range(1, budget + 1) if n_cols % (128 * n) == 0)

    w_in, w_up = p["w_in"], p["w_up"]

    ya_s, h1_s, vn, w_v, w_u = _s_gmlp_call(xs, mod, p["g_pre_mix"], w_in, p["g_v"], p["wvec"],
                                            p["bvec"], tn=tn_s)
    n = n_steps("gmlp")
    ya_p, h1_p, (w_bcx,) = _p_gmlp_call(
        xp, mod_rows, p["g_pre_mix"], w_v, w_u, p["g_v"], p["w_s"], p["bias"], mod_row0=mod_row0,
        side=(_cols(w_in, 2 * d, 3 * wb, blocks(3 * wb, n)),),
        **tiles("gmlp"))

    yb_s, tb0, tb1 = _s_shortconv_call(h1_s, w_bcx, p["conv_b_w"], st_b, tn=tn_s)
    n = n_steps("shortconv")
    yb_p, tail_b, (w_gate, w_oa, w_ob, w_o, w_a, w_b) = _p_shortconv_call(
        h1_p, w_bcx, p["conv_b_w"],
        side=(_cols(w_in, 2 * d + 3 * wb, 2 * d, blocks(2 * d, n)),
              _cols(p["w_out_a"], 0, d, blocks(d, n)), _cols(p["w_out_b"], 0, d, blocks(d, n)),
              _cols(p["w_o"], 0, d, blocks(d, n)),
              _cols(w_up, 0, f, blocks(f, n)), _cols(w_up, f, f, blocks(f, n))),
        **tiles("shortconv"))

    tm_s = h1_s.shape[0]
    mg_s, _ = _merge_call(h1_s, ya_s, yb_s, w_gate, w_oa, w_ob, tm=tm_s, tn=tn_s, slab=tm_s // 2,
                          name="gated_merge_sample")
    n_row_blocks = max(k for k in range(1, n_steps("merge") + 1)
                       if f % k == 0 and (f // k) % (2 * SUBLANES) == 0)
    mg_p, (w_d,) = _merge_call(h1_p, ya_p, yb_p, w_gate, w_oa, w_ob, name="gated_merge_prompt",
                               side=(_Side(p["w_down"], 0, 0, f // n_row_blocks, n_row_blocks),),
                               **without(tiles("merge"), "tiles_per_seq"))

    x1_s, h2_s = _s_proj_call(mg_s, w_o, xs, mod, p["g_post_mix"], p["g_pre_ffn"], tn=tn_s)
    x1_p, h2_p = _p_proj_call(mg_p, w_o, xp, mod_rows, p["g_post_mix"], p["g_pre_ffn"],
                              mod_row0=mod_row0, **without(tiles("proj"), "tn"))

    out_s, tf0, tf1 = _s_ffn_call(h2_s, w_a, w_b, p["conv_f_w"], w_d, x1_s, mod, p["g_post_ffn"],
                                  st_f, tn=tn_s)
    out_p, tail_f = _p_ffn_call(h2_p, w_a, w_b, p["conv_f_w"], w_d, x1_p, mod_rows, p["g_post_ffn"],
                                mod_row0=mod_row0, **tiles("ffn"))
    return out_s, out_p, (tb0, tb1), (tf0, tf1), vn, tail_b, tail_f


def kernel(x_prompt, x_sample, c_prompt, c_sample, state_conv_b, state_conv_ffn, w_ada, b_ada, g_pre_mix, g_post_mix, w_in, g_v, w_s, b_s, conv_b_w, w_out_a, w_out_b, w_o, g_pre_ffn, g_post_ffn, w_up, conv_f_w, w_down):
    depth = w_in.shape[0]
    bp, seq, d = x_prompt.shape
    bs, tdec, _ = x_sample.shape
    n_groups = w_s.shape[1]
    assert bs == CHUNK and tdec <= CHUNK
    assert all(seq % tm == 0 for tm, _, _ in PROMPT_TILES.values())

    xp = x_prompt.reshape(bp * seq, d)
    xs = x_sample.reshape(bs, tdec * d)
    pad = (-(bp + bs)) % SUBLANES
    c_all = jnp.concatenate([c_sample, c_prompt, jnp.zeros((pad, d), _F32)], axis=0)

    pb, sb, pf, sf, sv = [], [], [], [], []
    for l in range(depth):
        mod, _ = _mod_call(c_all, w_ada[l], b_ada[l][None, :], side=())
        vec = lambda a: a[l][None, :]
        bias_full = jnp.repeat(jnp.transpose(b_s[l]), GROUP, axis=1)
        wvec = jnp.repeat(
            jnp.transpose(w_s[l][:, :tdec, :tdec], (1, 2, 0)).reshape(tdec * tdec, n_groups),
            GROUP, axis=1)
        p = {
            "w_in": w_in[l], "w_out_a": w_out_a[l], "w_out_b": w_out_b[l],
            "w_o": w_o[l], "w_up": w_up[l], "w_down": w_down[l],
            "g_pre_mix": vec(g_pre_mix), "g_post_mix": vec(g_post_mix), "g_v": vec(g_v),
            "g_pre_ffn": vec(g_pre_ffn), "g_post_ffn": vec(g_post_ffn),
            "conv_b_w": conv_b_w[l], "conv_f_w": conv_f_w[l],
            "w_s": w_s[l], "bias": bias_full, "wvec": wvec, "bvec": bias_full[:tdec],
        }
        st_b = state_conv_b[l].reshape(bs, -1)
        st_f = state_conv_ffn[l].reshape(bs, -1)
        xs, xp, sbt, sft, vn, tb, tf = _layer(xs, xp, mod, st_b, st_f, p, seq_len=seq, mod_row0=bs)

        def prompt_tail(t):
            n_tiles, n_blk, _, tn = t.shape
            t = t.reshape(bp, n_tiles // bp, n_blk, SUBLANES, tn)[:, -1, :, SUBLANES - (CONV_K - 1):, :]
            return jnp.transpose(t, (0, 2, 1, 3)).reshape(bp, CONV_K - 1, n_blk * tn)

        pb.append(prompt_tail(tb))
        pf.append(prompt_tail(tf))
        sb.append(jnp.stack(sbt, axis=1))
        sf.append(jnp.stack(sft, axis=1))
        sv.append(vn.reshape(bs, tdec, d))

    y_prompt = xp.reshape(bp, seq, d)
    return (y_prompt, xs.reshape(bs, tdec, d), jnp.stack(pb), jnp.stack(sb), jnp.stack(pf), jnp.stack(sf),
            jnp.stack(sv))
```

```python
import functools
from typing import NamedTuple

import jax
import jax.numpy as jnp
from jax import lax
from jax.experimental import pallas as pl
from jax.experimental.pallas import tpu as pltpu

EPS = 1e-6
CHUNK = 128
GROUP = 128
CONV_K = 3
N_MOD = 6
SUBLANES = 8
VMEM_LIMIT_BYTES = 56 * 1024 * 1024
VMEM_LIMIT_BYTES_WIDE = 60 * 1024 * 1024
PROMPT_TILES = {
    "gmlp": (1024, 1024, 256),
    "shortconv": (1024, 512, 256),
    "merge": (512, 1024, 256),
    "proj": (512, None, 256),
    "ffn": (1024, 512, 256),
}
SAMPLE_COL_TILE = 512
ADALN_COL_TILE = 1024

_BF16 = jnp.bfloat16
_F32 = jnp.float32


def _dot(a, b):
    return jnp.dot(a, b, preferred_element_type=_F32)


def _rms(xf, g):
    ms = jnp.mean(xf * xf, axis=-1, keepdims=True)
    return xf * lax.rsqrt(ms + EPS) * g


def _causal_conv_rows(p, prev, cw_ref):
    r1 = pltpu.roll(p, 1, 0)
    r2 = pltpu.roll(p, 2, 0)
    row = lax.broadcasted_iota(jnp.int32, (SUBLANES, 1), 0)
    head1 = jnp.where(row == 0, prev[7:8, :], r1[:SUBLANES, :])
    head2 = jnp.where(row == 0, prev[6:7, :], jnp.where(row == 1, prev[7:8, :], r2[:SUBLANES, :]))
    m1 = jnp.concatenate([head1, r1[SUBLANES:, :]], axis=0)
    m2 = jnp.concatenate([head2, r2[SUBLANES:, :]], axis=0)
    return cw_ref[0:1, :] * m2 + cw_ref[1:2, :] * m1 + cw_ref[2:3, :] * p


def _causal_conv_slabs(x, prev, cw_ref, slab):
    w0, w1, w2 = cw_ref[0:1, :], cw_ref[1:2, :], cw_ref[2:3, :]
    seq = list(prev) + [x[t * slab:(t + 1) * slab, :] for t in range(x.shape[0] // slab)]
    y = [w0 * seq[t] + w1 * seq[t + 1] + w2 * seq[t + 2] for t in range(len(seq) - 2)]
    return jnp.concatenate(y, axis=0), seq[-2:]


def _slabs(tm, slab):
    return [slice(s * slab, (s + 1) * slab) for s in range(tm // slab)]


def _col_blocks(a, tn):
    r, c = a.shape
    return jnp.transpose(a.reshape(r, c // tn, tn), (1, 0, 2))


def _post_mix(y, x, gt, gpost, sh, sc, gpre):
    x1 = x + gt * _rms(y, gpost)
    h2 = _rms(x1, gpre) * (1.0 + sc) + sh
    return x1, h2.astype(_BF16)


class _Side(NamedTuple):
    src: jax.Array
    axis: int
    start: int
    block: int
    n_blocks: int


def _convert(side):
    for src_ref, dst_ref in side:
        dst_ref[...] = src_ref[...].astype(_BF16)


def _run(body, *, grid, in_specs, args, out_specs, out_shape, scratch=(), side=(), name,
         vmem_limit_bytes=VMEM_LIMIT_BYTES):
    n_in, n_out, n_side = len(args), len(out_shape), len(side)
    n_steps = functools.reduce(lambda a, b: a * b, grid)
    step_of = (lambda i: i) if len(grid) == 1 else (lambda i, j: i * grid[1] + j)
    side_in, side_out, side_shape = [], [], []
    for s in side:
        assert s.n_blocks <= n_steps, (name, s.n_blocks, n_steps)
        other = s.src.shape[1 - s.axis]
        pos = lambda *ids, s=s: jnp.minimum(step_of(*ids), s.n_blocks - 1)
        if s.axis == 1:
            blk, full = (other, s.block), (other, s.block * s.n_blocks)
            side_in.append(pl.BlockSpec(blk, lambda *ids, s=s, pos=pos: (0, s.start + pos(*ids))))
            side_out.append(pl.BlockSpec(blk, lambda *ids, pos=pos: (0, pos(*ids))))
        else:
            blk, full = (s.block, other), (s.block * s.n_blocks, other)
            side_in.append(pl.BlockSpec(blk, lambda *ids, s=s, pos=pos: (s.start + pos(*ids), 0)))
            side_out.append(pl.BlockSpec(blk, lambda *ids, pos=pos: (pos(*ids), 0)))
        side_shape.append(jax.ShapeDtypeStruct(full, _BF16))

    def kern(*refs):
        o0 = n_in + n_side
        s0 = o0 + n_out + n_side
        body(*refs[:n_in], *refs[o0:o0 + n_out], *refs[s0:],
             side=tuple(zip(refs[n_in:o0], refs[o0 + n_out:s0])))

    res = pl.pallas_call(
        kern, grid=grid,
        in_specs=list(in_specs) + side_in, out_specs=list(out_specs) + side_out,
        out_shape=list(out_shape) + side_shape, scratch_shapes=list(scratch),
        compiler_params=pltpu.CompilerParams(dimension_semantics=("arbitrary",) * len(grid),
                                             vmem_limit_bytes=vmem_limit_bytes),
        name=name,
    )(*args, *[s.src for s in side])
    return res[:n_out], res[n_out:]


def _mod_kernel(c_ref, w_ref, b_ref, o_ref, *, side):
    _convert(side)
    c = c_ref[...]
    a = (c * jax.nn.sigmoid(c)).astype(_BF16)
    o_ref[...] = _dot(a, w_ref[...].astype(_BF16)) + b_ref[...]


def _mod_call(c_all, w_ada, b_ada, *, side):
    rows, d = c_all.shape
    n = w_ada.shape[1]
    tn = ADALN_COL_TILE
    (mod,), copies = _run(
        _mod_kernel, grid=(n // tn,),
        in_specs=[pl.BlockSpec((rows, d), lambda j: (0, 0)),
                  pl.BlockSpec((d, tn), lambda j: (0, j)),
                  pl.BlockSpec((1, tn), lambda j: (0, j))],
        args=(c_all, w_ada, b_ada),
        out_specs=[pl.BlockSpec((rows, tn), lambda j: (0, j))],
        out_shape=[jax.ShapeDtypeStruct((rows, n), _F32)],
        side=side, name="adaln_mod")
    return mod, copies


def _s_gmlp_kernel(x_ref, sh_ref, sc_ref, gpre_ref, wv_ref, wu_ref, gv_ref, ws_ref, bias_ref,
                   ya_ref, h_ref, vn_ref, wvb_ref, wub_ref, v_scr, *, tm, tn, n_blk, slab, side):
    j = pl.program_id(0)
    d = n_blk * tn
    slabs = _slabs(tm, slab)
    n_slab = len(slabs)

    @pl.when(j == 0)
    def _():
        for t, r in enumerate(slabs):
            h_ref[r, :] = (_rms(x_ref[:, t * d:(t + 1) * d], gpre_ref[...]) * (1.0 + sc_ref[...])
                           + sh_ref[...]).astype(_BF16)

    @pl.when(j < n_blk)
    def _():
        wv = wv_ref[...].astype(_BF16)
        wvb_ref[...] = wv
        v_scr[j] = _dot(h_ref[...], wv)

    @pl.when(j == n_blk)
    def _():
        for t, r in enumerate(slabs):
            ss = 0.0
            for k in range(n_blk):
                vk = v_scr[k, r, :]
                ss = ss + jnp.sum(vk * vk, axis=-1, keepdims=True)
            rs = lax.rsqrt(ss * (1.0 / d) + EPS)
            for k in range(n_blk):
                vn = v_scr[k, r, :] * rs * gv_ref[:, k * tn:(k + 1) * tn]
                v_scr[k, r, :] = vn
                vn_ref[:, t * d + k * tn:t * d + (k + 1) * tn] = vn
        for t in reversed(range(n_slab)):
            for k in range(n_blk):
                c = slice(k * tn, (k + 1) * tn)
                acc = ws_ref[t * n_slab:t * n_slab + 1, c] * v_scr[k, slabs[0], :]
                for s in range(1, t + 1):
                    acc = acc + ws_ref[t * n_slab + s:t * n_slab + s + 1, c] * v_scr[k, slabs[s], :]
                v_scr[k, slabs[t], :] = acc + bias_ref[t:t + 1, c]

    @pl.when(j >= n_blk)
    def _():
        wu = wu_ref[...].astype(_BF16)
        wub_ref[...] = wu
        ya_ref[...] = (_dot(h_ref[...], wu) * v_scr[j - n_blk]).astype(_BF16)


def _s_gmlp_call(x, mod, gpre, w_in, gv, wvec, bvec, *, tn):
    slab = x.shape[0]
    d = gpre.shape[1]
    tm = x.shape[1] // d * slab
    n_blk = d // tn
    full = lambda a: pl.BlockSpec(a.shape, lambda j: (0,) * a.ndim)
    u_map = lambda j: (0, jnp.maximum(j - n_blk, 0))
    v_map = lambda j: (0, jnp.minimum(j, n_blk - 1))
    (ya, h, vn, w_v, w_u), _ = _run(
        functools.partial(_s_gmlp_kernel, tm=tm, tn=tn, n_blk=n_blk, slab=slab),
        grid=(2 * n_blk,),
        in_specs=[full(x),
                  pl.BlockSpec((slab, d), lambda j: (0, 0)), pl.BlockSpec((slab, d), lambda j: (0, 1)),
                  full(gpre),
                  pl.BlockSpec((d, tn), lambda j: (0, n_blk + jnp.minimum(j, n_blk - 1))),
                  pl.BlockSpec((d, tn), u_map),
                  full(gv), full(wvec), full(bvec)],
        args=(x, mod, mod, gpre, w_in, w_in, gv, wvec, bvec),
        out_specs=[pl.BlockSpec((tm, tn), u_map), pl.BlockSpec((tm, d), lambda j: (0, 0)), full(x),
                   pl.BlockSpec((d, tn), v_map), pl.BlockSpec((d, tn), u_map)],
        out_shape=[jax.ShapeDtypeStruct((tm, d), _BF16), jax.ShapeDtypeStruct((tm, d), _BF16),
                   jax.ShapeDtypeStruct(x.shape, _F32),
                   jax.ShapeDtypeStruct((d, d), _BF16), jax.ShapeDtypeStruct((d, d), _BF16)],
        scratch=[pltpu.VMEM((n_blk, tm, tn), _F32)],
        name="gmlp_sample")
    return ya, h, vn, w_v, w_u


def _s_shortconv_kernel(h_ref, wbg_ref, wcg_ref, wxb_ref, cw_ref, st0_ref, st1_ref,
                        yb_ref, t0_ref, t1_ref, *, slab, side):
    h = h_ref[...]
    bg = _dot(h, wbg_ref[...])
    p = _dot(h, wcg_ref[...]) * _dot(h, wxb_ref[...])
    cb, tail = _causal_conv_slabs(p, [st0_ref[...], st1_ref[...]], cw_ref, slab)
    yb_ref[...] = (bg * cb).astype(_BF16)
    t0_ref[...] = tail[0]
    t1_ref[...] = tail[1]


def _s_shortconv_call(h, w_bcx, cw, state, *, tn):
    tm, d = h.shape
    slab = state.shape[0]
    w = cw.shape[1]
    n_blk = w // tn
    wspec = lambda off: pl.BlockSpec((d, tn), lambda j: (0, off + j))
    sspec = lambda k: pl.BlockSpec((slab, tn), lambda j: (0, k * n_blk + j))
    tspec = pl.BlockSpec((slab, tn), lambda j: (0, j))
    (yb, t0, t1), _ = _run(
        functools.partial(_s_shortconv_kernel, slab=slab),
        grid=(n_blk,),
        in_specs=[pl.BlockSpec((tm, d), lambda j: (0, 0)),
                  wspec(0), wspec(n_blk), wspec(2 * n_blk),
                  pl.BlockSpec((CONV_K, tn), lambda j: (0, j)),
                  sspec(0), sspec(1)],
        args=(h, w_bcx, w_bcx, w_bcx, cw, state, state),
        out_specs=[pl.BlockSpec((tm, tn), lambda j: (0, j)), tspec, tspec],
        out_shape=[jax.ShapeDtypeStruct((tm, w), _BF16)] + [jax.ShapeDtypeStruct((slab, w), _F32)] * 2,
        name="shortconv_sample")
    return yb, t0, t1


def _merge_kernel(h_ref, ya_ref, yb_ref, wga_ref, wgb_ref, woa_ref, wob_ref, m_ref, *, tm, slab, side):
    _convert(side)
    for r in _slabs(tm, slab):
        h = h_ref[r, :]
        ga = jax.nn.sigmoid(_dot(h, wga_ref[...]))
        gb = jax.nn.sigmoid(_dot(h, wgb_ref[...]))
        m = ga * _dot(ya_ref[r, :], woa_ref[...]) + gb * _dot(yb_ref[r, :], wob_ref[...])
        m_ref[r, :] = m.astype(_BF16)


def _merge_call(h, ya, yb, w_gate, w_out_a, w_out_b, *, tm, tn, slab, side=(), name):
    m, d = h.shape
    n_blk = d // tn
    row = pl.BlockSpec((tm, d), lambda i, j: (i, 0))
    wspec = lambda off: pl.BlockSpec((d, tn), lambda i, j: (0, off + j))
    (mg,), copies = _run(
        functools.partial(_merge_kernel, tm=tm, slab=slab),
        grid=(m // tm, n_blk),
        in_specs=[row, row, row, wspec(0), wspec(n_blk), wspec(0), wspec(0)],
        args=(h, ya, yb, w_gate, w_gate, w_out_a, w_out_b),
        out_specs=[pl.BlockSpec((tm, tn), lambda i, j: (i, j))],
        out_shape=[jax.ShapeDtypeStruct((m, d), _BF16)],
        side=side, name=name)
    return mg, copies


def _s_proj_kernel(m_ref, wo_ref, x_ref, gt_ref, gpost_ref, sh_ref, sc_ref, gpre_ref,
                   x1_ref, h2_ref, y_scr, *, tm, tn, n_blk, slab, side):
    j = pl.program_id(0)
    d = n_blk * tn
    y_scr[j] = _dot(m_ref[...], wo_ref[...])

    @pl.when(j == n_blk - 1)
    def _():
        for t, r in enumerate(_slabs(tm, slab)):
            y = jnp.concatenate([y_scr[k, r, :] for k in range(n_blk)], axis=-1)
            x1, h2 = _post_mix(y, x_ref[:, t * d:(t + 1) * d], gt_ref[...], gpost_ref[...],
                               sh_ref[...], sc_ref[...], gpre_ref[...])
            x1_ref[r, :] = x1
            h2_ref[r, :] = h2


def _s_proj_call(mg, w_o, x, mod, gpost, gpre, *, tn):
    tm, d = mg.shape
    slab = x.shape[0]
    n_blk = d // tn
    row = pl.BlockSpec((tm, d), lambda j: (0, 0))
    vec = pl.BlockSpec((1, d), lambda j: (0, 0))
    mspec = lambda k: pl.BlockSpec((slab, d), lambda j: (0, k))
    (x1, h2), _ = _run(
        functools.partial(_s_proj_kernel, tm=tm, tn=tn, n_blk=n_blk, slab=slab),
        grid=(n_blk,),
        in_specs=[row, pl.BlockSpec((d, tn), lambda j: (0, j)),
                  pl.BlockSpec(x.shape, lambda j: (0, 0)), mspec(2), vec, mspec(3), mspec(4), vec],
        args=(mg, w_o, x, mod, gpost, mod, mod, gpre),
        out_specs=[row, row],
        out_shape=[jax.ShapeDtypeStruct((tm, d), _F32), jax.ShapeDtypeStruct((tm, d), _BF16)],
        scratch=[pltpu.VMEM((n_blk, tm, tn), _F32)],
        name="out_proj_sample")
    return x1, h2


def _s_ffn_kernel(h_ref, wa_ref, wb_ref, cw_ref, wd_ref, x1_ref, gt_ref, gpost_ref, st0_ref, st1_ref,
                  out_ref, t0_ref, t1_ref, *, tm, n_blk, slab, side):
    j = pl.program_id(0)
    d = x1_ref.shape[1]
    lanes = [slice(t * d, (t + 1) * d) for t in range(tm // slab)]

    @pl.when(j == 0)
    def _():
        out_ref[...] = jnp.zeros(out_ref.shape, _F32)

    prev = [st0_ref[...], st1_ref[...]]
    gs = []
    for r in _slabs(tm, 2 * slab):
        h = h_ref[r, :]
        ac, prev = _causal_conv_slabs(_dot(h, wa_ref[...]), prev, cw_ref, slab)
        gs.append((jax.nn.gelu(ac) * _dot(h, wb_ref[...])).astype(_BF16))
    for c, g in enumerate(gs):
        f = _dot(g, wd_ref[...])
        out_ref[:, lanes[2 * c]] += f[:slab, :]
        out_ref[:, lanes[2 * c + 1]] += f[slab:, :]
    t0_ref[...] = prev[0]
    t1_ref[...] = prev[1]

    @pl.when(j == n_blk - 1)
    def _():
        for t, r in enumerate(_slabs(tm, slab)):
            out_ref[:, lanes[t]] = (x1_ref[r, :]
                                    + gt_ref[...] * _rms(out_ref[:, lanes[t]], gpost_ref[...]))


def _s_ffn_call(h2, w_a, w_b, cw, w_down, x1, mod, gpost, state, *, tn):
    tm, d = x1.shape
    slab = state.shape[0]
    f = cw.shape[1]
    n_blk = f // tn
    out_shape = (slab, tm // slab * d)
    row = pl.BlockSpec((tm, d), lambda j: (0, 0))
    wspec = pl.BlockSpec((d, tn), lambda j: (0, j))
    sspec = lambda k: pl.BlockSpec((slab, tn), lambda j: (0, k * n_blk + j))
    tspec = pl.BlockSpec((slab, tn), lambda j: (0, j))
    (out, t0, t1), _ = _run(
        functools.partial(_s_ffn_kernel, tm=tm, n_blk=n_blk, slab=slab),
        grid=(n_blk,),
        in_specs=[row, wspec, wspec, pl.BlockSpec((CONV_K, tn), lambda j: (0, j)),
                  pl.BlockSpec((tn, d), lambda j: (j, 0)), row,
                  pl.BlockSpec((slab, d), lambda j: (0, 5)), pl.BlockSpec((1, d), lambda j: (0, 0)),
                  sspec(0), sspec(1)],
        args=(h2, w_a, w_b, cw, w_down, x1, mod, gpost, state, state),
        out_specs=[pl.BlockSpec(out_shape, lambda j: (0, 0)), tspec, tspec],
        out_shape=[jax.ShapeDtypeStruct(out_shape, _F32)]
        + [jax.ShapeDtypeStruct((slab, f), _F32)] * 2,
        name="convffn_sample")
    return out, t0, t1


def _p_gmlp_kernel(x_hbm, sh_ref, sc_ref, gpre_ref, wv_ref, wu_ref, gv_ref, ws_ref, bias_ref,
                   ya_ref, h_ref, x_buf, x_sem, v_scr, wt_scr, *, tm, tn, n_blk, slab, side):
    i = pl.program_id(0)
    j = pl.program_id(1)
    d = n_blk * tn
    slabs = _slabs(tm, slab)

    def x_copy(tile):
        return pltpu.make_async_copy(x_hbm.at[pl.ds(pl.multiple_of(tile * tm, tm), tm), :],
                                     x_buf, x_sem)

    @pl.when((i == 0) & (j == 0))
    def _():
        x_copy(0).start()
        tril = (lax.broadcasted_iota(jnp.int32, (CHUNK, CHUNK), 0)
                >= lax.broadcasted_iota(jnp.int32, (CHUNK, CHUNK), 1))
        for g in range(d // GROUP):
            wt_scr[g] = jnp.where(tril, ws_ref[g], 0.0).astype(_BF16)

    @pl.when(j == 0)
    def _():
        x_copy(i).wait()
        _convert(side)
        g = gpre_ref[...] * (1.0 + sc_ref[...])
        for r in slabs:
            h = (_rms(x_buf[r, :], g) + sh_ref[...]).astype(_BF16)
            h_ref[r, :] = h
            v_scr[0, r, :] = _dot(h, wv_ref[...])

    @pl.when((j == 0) & (i + 1 < pl.num_programs(0)))
    def _():
        x_copy(i + 1).start()

    @pl.when((j > 0) & (j < n_blk))
    def _():
        _convert(side)
        for r in slabs:
            v_scr[j, r, :] = _dot(h_ref[r, :], wv_ref[...])

    def _gate():
        gpb = tn // GROUP
        for c in range(tm // CHUNK):
            r = slice(c * CHUNK, (c + 1) * CHUNK)
            ss = 0.0
            for k in range(n_blk):
                vk = v_scr[k, r, :]
                ss = ss + jnp.sum(vk * vk, axis=-1, keepdims=True)
            rs = lax.rsqrt(ss * (1.0 / d) + EPS)
            for k in range(n_blk):
                vb = (v_scr[k, r, :] * rs * gv_ref[:, k * tn:(k + 1) * tn]).astype(_BF16)
                for gg in range(gpb):
                    g = k * gpb + gg
                    lanes = slice(gg * GROUP, (gg + 1) * GROUP)
                    v_scr[k, r, lanes] = (_dot(wt_scr[g], vb[:, lanes])
                                          + bias_ref[:, g * GROUP:(g + 1) * GROUP])

    @pl.when(j == n_blk)
    def _():
        _convert(side)
        _gate()
        for r in slabs:
            ya_ref[r, :] = (_dot(h_ref[r, :], wu_ref[...]) * v_scr[0, r, :]).astype(_BF16)

    @pl.when(j > n_blk)
    def _():
        _convert(side)
        for r in slabs:
            ya_ref[r, :] = (_dot(h_ref[r, :], wu_ref[...]) * v_scr[j - n_blk, r, :]).astype(_BF16)


def _p_gmlp_call(x, mod, gpre, w_v, w_u, gv, ws, bias, *, tm, tn, slab, tiles_per_seq, mod_row0,
                 side):
    m, d = x.shape
    n_blk = d // tn
    full = lambda a: pl.BlockSpec(a.shape, lambda i, j: (0,) * a.ndim)
    once = lambda a: pl.BlockSpec(a.shape, lambda i, j: (0,) * a.ndim, pipeline_mode=pl.Buffered(1))
    mspec = lambda k: pl.BlockSpec((None, 1, d), lambda i, j: (mod_row0 + i // tiles_per_seq, 0, k))
    u_map = lambda i, j: (i, jnp.maximum(j - n_blk, 0))
    (ya, h), copies = _run(
        functools.partial(_p_gmlp_kernel, tm=tm, tn=tn, n_blk=n_blk, slab=slab),
        grid=(m // tm, 2 * n_blk),
        in_specs=[pl.BlockSpec(memory_space=pl.ANY), mspec(0), mspec(1), full(gpre),
                  pl.BlockSpec((d, tn), lambda i, j: (0, jnp.minimum(j, n_blk - 1))),
                  pl.BlockSpec((d, tn), lambda i, j: (0, jnp.maximum(j - n_blk, 0))),
                  full(gv), once(ws), once(bias)],
        args=(x, mod, mod, gpre, w_v, w_u, gv, ws, bias),
        out_specs=[pl.BlockSpec((tm, tn), u_map), pl.BlockSpec((tm, d), lambda i, j: (i, 0))],
        out_shape=[jax.ShapeDtypeStruct((m, d), _BF16), jax.ShapeDtypeStruct((m, d), _BF16)],
        scratch=[pltpu.VMEM((tm, d), _F32), pltpu.SemaphoreType.DMA(()),
                 pltpu.VMEM((n_blk, tm, tn), _F32), pltpu.VMEM((d // GROUP, CHUNK, CHUNK), _BF16)],
        side=side, name="gmlp_prompt", vmem_limit_bytes=VMEM_LIMIT_BYTES_WIDE)
    return ya, h, copies


def _p_shortconv_kernel(h_ref, wbg_ref, wcg_ref, wxb_ref, cw_ref, yb_ref, tail_ref, carry_scr,
                        *, tm, tiles_per_seq, slab, side):
    i = pl.program_id(0)
    j = pl.program_id(1)

    @pl.when(i % tiles_per_seq == 0)
    def _():
        carry_scr[j] = jnp.zeros(carry_scr.shape[1:], _F32)

    _convert(side)
    prev = carry_scr[j]
    cw = cw_ref.at[j]
    for r in _slabs(tm, slab):
        h = h_ref[r, :]
        bg = _dot(h, wbg_ref[...])
        p = _dot(h, wcg_ref[...]) * _dot(h, wxb_ref[...])
        yb_ref[r, :] = (bg * _causal_conv_rows(p, prev, cw)).astype(_BF16)
        prev = p[slab - SUBLANES:, :]
    carry_scr[j] = prev
    tail_ref[j] = prev


def _p_shortconv_call(h, w_bcx, cw, *, tm, tn, tiles_per_seq, slab, side):
    m, d = h.shape
    w = cw.shape[1]
    n_blk = w // tn
    wspec = lambda off: pl.BlockSpec((d, tn), lambda i, j: (0, off + j))
    (yb, tail), copies = _run(
        functools.partial(_p_shortconv_kernel, tm=tm, tiles_per_seq=tiles_per_seq, slab=slab),
        grid=(m // tm, n_blk),
        in_specs=[pl.BlockSpec((tm, d), lambda i, j: (i, 0)),
                  wspec(0), wspec(n_blk), wspec(2 * n_blk),
                  pl.BlockSpec((n_blk, CONV_K, tn), lambda i, j: (0, 0, 0))],
        args=(h, w_bcx, w_bcx, w_bcx, _col_blocks(cw, tn)),
        out_specs=[pl.BlockSpec((tm, tn), lambda i, j: (i, j)),
                   pl.BlockSpec((None, n_blk, SUBLANES, tn), lambda i, j: (i, 0, 0, 0))],
        out_shape=[jax.ShapeDtypeStruct((m, w), _BF16),
                   jax.ShapeDtypeStruct((m // tm, n_blk, SUBLANES, tn), _F32)],
        scratch=[pltpu.VMEM((n_blk, SUBLANES, tn), _F32)],
        side=side, name="shortconv_prompt")
    return yb, tail, copies


def _p_proj_kernel(m_ref, wo_ref, x_ref, gt_ref, gpost_ref, sh_ref, sc_ref, gpre_ref,
                   x1_ref, h2_ref, *, tm, slab, side):
    g1 = gt_ref[...] * gpost_ref[...]
    g2 = gpre_ref[...] * (1.0 + sc_ref[...])
    for r in _slabs(tm, slab):
        y = _dot(m_ref[r, :], wo_ref[...])
        x1 = x_ref[r, :] + _rms(y, g1)
        x1_ref[r, :] = x1
        h2_ref[r, :] = (_rms(x1, g2) + sh_ref[...]).astype(_BF16)


def _p_proj_call(mg, w_o, x, mod, gpost, gpre, *, tm, slab, tiles_per_seq, mod_row0):
    m, d = x.shape
    row = pl.BlockSpec((tm, d), lambda i: (i, 0))
    vec = pl.BlockSpec((1, d), lambda i: (0, 0))
    mspec = lambda k: pl.BlockSpec((None, 1, d), lambda i: (mod_row0 + i // tiles_per_seq, 0, k))
    (x1, h2), _ = _run(
        functools.partial(_p_proj_kernel, tm=tm, slab=slab),
        grid=(m // tm,),
        in_specs=[row, pl.BlockSpec((d, d), lambda i: (0, 0)), row, mspec(2), vec, mspec(3),
                  mspec(4), vec],
        args=(mg, w_o, x, mod, gpost, mod, mod, gpre),
        out_specs=[row, row],
        out_shape=[jax.ShapeDtypeStruct((m, d), _F32), jax.ShapeDtypeStruct((m, d), _BF16)],
        name="out_proj_prompt")
    return x1, h2


def _p_ffn_kernel(h_ref, wa_ref, wb_ref, cw_ref, wd_ref, x1_hbm, gt_ref, gpost_ref,
                  out_ref, tail_ref, x1_buf, x1_sem, carry_scr, *, tm, n_blk, tiles_per_seq, slab,
                  side):
    i = pl.program_id(0)
    j = pl.program_id(1)
    x1_copy = pltpu.make_async_copy(x1_hbm.at[pl.ds(pl.multiple_of(i * tm, tm), tm), :],
                                    x1_buf, x1_sem)

    @pl.when(i % tiles_per_seq == 0)
    def _():
        carry_scr[j] = jnp.zeros(carry_scr.shape[1:], _F32)

    def step(first, last):
        prev = carry_scr[j]
        cw = cw_ref.at[j]
        slabs = _slabs(tm, slab)
        gs = []
        for r in slabs:
            h = h_ref[r, :]
            a = _dot(h, wa_ref[...])
            b = _dot(h, wb_ref[...])
            gs.append((jax.nn.gelu(_causal_conv_rows(a, prev, cw)) * b).astype(_BF16))
            prev = a[slab - SUBLANES:, :]
        for r, g in zip(slabs, gs):
            f = _dot(g, wd_ref[...])
            acc = f if first else out_ref[r, :] + f
            if last:
                acc = x1_buf[r, :] + _rms(acc, gt_ref[...] * gpost_ref[...])
            out_ref[r, :] = acc
        carry_scr[j] = prev
        tail_ref[j] = prev

    @pl.when(j == 0)
    def _():
        x1_copy.start()
        step(True, False)

    @pl.when((j > 0) & (j < n_blk - 1))
    def _():
        step(False, False)

    @pl.when(j == n_blk - 1)
    def _():
        x1_copy.wait()
        step(False, True)


def _p_ffn_call(h2, w_a, w_b, cw, w_down, x1, mod, gpost, *, tm, tn, tiles_per_seq, slab, mod_row0):
    m, d = x1.shape
    f = cw.shape[1]
    n_blk = f // tn
    row = pl.BlockSpec((tm, d), lambda i, j: (i, 0))
    wspec = pl.BlockSpec((d, tn), lambda i, j: (0, j))
    (out, tail), _ = _run(
        functools.partial(_p_ffn_kernel, tm=tm, n_blk=n_blk, tiles_per_seq=tiles_per_seq, slab=slab),
        grid=(m // tm, n_blk),
        in_specs=[row, wspec, wspec,
                  pl.BlockSpec((n_blk, CONV_K, tn), lambda i, j: (0, 0, 0)),
                  pl.BlockSpec((tn, d), lambda i, j: (j, 0)),
                  pl.BlockSpec(memory_space=pl.ANY),
                  pl.BlockSpec((None, 1, d), lambda i, j: (mod_row0 + i // tiles_per_seq, 0, 5)),
                  pl.BlockSpec((1, d), lambda i, j: (0, 0))],
        args=(h2, w_a, w_b, _col_blocks(cw, tn), w_down, x1, mod, gpost),
        out_specs=[row, pl.BlockSpec((None, n_blk, SUBLANES, tn), lambda i, j: (i, 0, 0, 0))],
        out_shape=[jax.ShapeDtypeStruct((m, d), _F32),
                   jax.ShapeDtypeStruct((m // tm, n_blk, SUBLANES, tn), _F32)],
        scratch=[pltpu.VMEM((tm, d), _F32), pltpu.SemaphoreType.DMA(()),
                 pltpu.VMEM((n_blk, SUBLANES, tn), _F32)],
        name="convffn_prompt")
    return out, tail


def _cols(src, first_col, n_cols, n_blocks):
    block = n_cols // n_blocks
    assert block * n_blocks == n_cols and block % 128 == 0 and first_col % block == 0
    return _Side(src, 1, first_col // block, block, n_blocks)


def _layer(xs, xp, mod, st_b, st_f, p, *, seq_len, mod_row0):
    d = p["g_v"].shape[1]
    wb = p["conv_b_w"].shape[1]
    f = p["conv_f_w"].shape[1]
    tn_s = SAMPLE_COL_TILE
    mod_rows = mod.reshape(mod.shape[0], 1, N_MOD * d)

    def tiles(name):
        tm, tn, slab = PROMPT_TILES[name]
        return dict(tm=tm, tn=tn, slab=slab, tiles_per_seq=seq_len // tm)

    def n_steps(name):
        tm, tn, _ = PROMPT_TILES[name]
        cols = {"gmlp": 2 * d, "shortconv": wb, "merge": d, "ffn": f}[name]
        return xp.shape[0] // tm * (cols // tn)

    def without(kw, *names):
        return {k: v for k, v in kw.items() if k not in names}

    def blocks(n_cols, budget):
        return max(n for n in range(1, budget + 1) if n_cols % (128 * n) == 0)

    w_in, w_up = p["w_in"], p["w_up"]

    ya_s, h1_s, vn, w_v, w_u = _s_gmlp_call(xs, mod, p["g_pre_mix"], w_in, p["g_v"], p["wvec"],
                                            p["bvec"], tn=tn_s)
    n = n_steps("gmlp")
    ya_p, h1_p, (w_bcx,) = _p_gmlp_call(
        xp, mod_rows, p["g_pre_mix"], w_v, w_u, p["g_v"], p["w_s"], p["bias"], mod_row0=mod_row0,
        side=(_cols(w_in, 2 * d, 3 * wb, blocks(3 * wb, n)),),
        **tiles("gmlp"))

    yb_s, tb0, tb1 = _s_shortconv_call(h1_s, w_bcx, p["conv_b_w"], st_b, tn=tn_s)
    n = n_steps("shortconv")
    yb_p, tail_b, (w_gate, w_oa, w_ob, w_o, w_a, w_b) = _p_shortconv_call(
        h1_p, w_bcx, p["conv_b_w"],
        side=(_cols(w_in, 2 * d + 3 * wb, 2 * d, blocks(2 * d, n)),
              _cols(p["w_out_a"], 0, d, blocks(d, n)), _cols(p["w_out_b"], 0, d, blocks(d, n)),
              _cols(p["w_o"], 0, d, blocks(d, n)),
              _cols(w_up, 0, f, blocks(f, n)), _cols(w_up, f, f, blocks(f, n))),
        **tiles("shortconv"))

    tm_s = h1_s.shape[0]
    mg_s, _ = _merge_call(h1_s, ya_s, yb_s, w_gate, w_oa, w_ob, tm=tm_s, tn=tn_s, slab=tm_s // 2,
                          name="gated_merge_sample")
    n_row_blocks = max(k for k in range(1, n_steps("merge") + 1)
                       if f % k == 0 and (f // k) % (2 * SUBLANES) == 0)
    mg_p, (w_d,) = _merge_call(h1_p, ya_p, yb_p, w_gate, w_oa, w_ob, name="gated_merge_prompt",
                               side=(_Side(p["w_down"], 0, 0, f // n_row_blocks, n_row_blocks),),
                               **without(tiles("merge"), "tiles_per_seq"))

    x1_s, h2_s = _s_proj_call(mg_s, w_o, xs, mod, p["g_post_mix"], p["g_pre_ffn"], tn=tn_s)
    x1_p, h2_p = _p_proj_call(mg_p, w_o, xp, mod_rows, p["g_post_mix"], p["g_pre_ffn"],
                              mod_row0=mod_row0, **without(tiles("proj"), "tn"))

    out_s, tf0, tf1 = _s_ffn_call(h2_s, w_a, w_b, p["conv_f_w"], w_d, x1_s, mod, p["g_post_ffn"],
                                  st_f, tn=tn_s)
    out_p, tail_f = _p_ffn_call(h2_p, w_a, w_b, p["conv_f_w"], w_d, x1_p, mod_rows, p["g_post_ffn"],
                                mod_row0=mod_row0, **tiles("ffn"))
    return out_s, out_p, (tb0, tb1), (tf0, tf1), vn, tail_b, tail_f


def kernel(x_prompt, x_sample, c_prompt, c_sample, state_conv_b, state_conv_ffn, w_ada, b_ada, g_pre_mix, g_post_mix, w_in, g_v, w_s, b_s, conv_b_w, w_out_a, w_out_b, w_o, g_pre_ffn, g_post_ffn, w_up, conv_f_w, w_down):
    depth = w_in.shape[0]
    bp, seq, d = x_prompt.shape
    bs, tdec, _ = x_sample.shape
    n_groups = w_s.shape[1]
    assert bs == CHUNK and tdec <= CHUNK
    assert all(seq % tm == 0 for tm, _, _ in PROMPT_TILES.values())

    xp = x_prompt.reshape(bp * seq, d)
    xs = x_sample.reshape(bs, tdec * d)
    pad = (-(bp + bs)) % SUBLANES
    c_all = jnp.concatenate([c_sample, c_prompt, jnp.zeros((pad, d), _F32)], axis=0)

    pb, sb, pf, sf, sv = [], [], [], [], []
    for l in range(depth):
        mod, _ = _mod_call(c_all, w_ada[l], b_ada[l][None, :], side=())
        vec = lambda a: a[l][None, :]
        bias_full = jnp.repeat(jnp.transpose(b_s[l]), GROUP, axis=1)
        wvec = jnp.repeat(
            jnp.transpose(w_s[l][:, :tdec, :tdec], (1, 2, 0)).reshape(tdec * tdec, n_groups),
            GROUP, axis=1)
        p = {
            "w_in": w_in[l], "w_out_a": w_out_a[l], "w_out_b": w_out_b[l],
            "w_o": w_o[l], "w_up": w_up[l], "w_down": w_down[l],
            "g_pre_mix": vec(g_pre_mix), "g_post_mix": vec(g_post_mix), "g_v": vec(g_v),
            "g_pre_ffn": vec(g_pre_ffn), "g_post_ffn": vec(g_post_ffn),
            "conv_b_w": conv_b_w[l], "conv_f_w": conv_f_w[l],
            "w_s": w_s[l], "bias": bias_full, "wvec": wvec, "bvec": bias_full[:tdec],
        }
        st_b = state_conv_b[l].reshape(bs, -1)
        st_f = state_conv_ffn[l].reshape(bs, -1)
        xs, xp, sbt, sft, vn, tb, tf = _layer(xs, xp, mod, st_b, st_f, p, seq_len=seq, mod_row0=bs)

        def prompt_tail(t):
            n_tiles, n_blk, _, tn = t.shape
            t = t.reshape(bp, n_tiles // bp, n_blk, SUBLANES, tn)[:, -1, :, SUBLANES - (CONV_K - 1):, :]
            return jnp.transpose(t, (0, 2, 1, 3)).reshape(bp, CONV_K - 1, n_blk * tn)

        pb.append(prompt_tail(tb))
        pf.append(prompt_tail(tf))
        sb.append(jnp.stack(sbt, axis=1))
        sf.append(jnp.stack(sft, axis=1))
        sv.append(vn.reshape(bs, tdec, d))

    y_prompt = xp.reshape(bp, seq, d)
    return (y_prompt, xs.reshape(bs, tdec, d), jnp.stack(pb), jnp.stack(sb), jnp.stack(pf), jnp.stack(sf),
            jnp.stack(sv))
```

```python
import functools
from typing import NamedTuple

import jax
import jax.numpy as jnp
from jax import lax
from jax.experimental import pallas as pl
from jax.experimental.pallas import tpu as pltpu

EPS = 1e-6
CHUNK = 128
GROUP = 128
CONV_K = 3
N_MOD = 6
SUBLANES = 8
VMEM_LIMIT_BYTES = 56 * 1024 * 1024
VMEM_LIMIT_BYTES_WIDE = 60 * 1024 * 1024
PROMPT_TILES = {
    "gmlp": (1024, 1024, 256),
    "shortconv": (1024, 512, 256),
    "merge": (512, 1024, 256),
    "proj": (512, None, 256),
    "ffn": (1024, 512, 256),
}
SAMPLE_COL_TILE = 512
SAMPLE_COL_TILE_NARROW = 256
ADALN_COL_TILE = 2048

_BF16 = jnp.bfloat16
_F32 = jnp.float32


def _dot(a, b):
    return jnp.dot(a, b, preferred_element_type=_F32)


def _rms(xf, g):
    ms = jnp.mean(xf * xf, axis=-1, keepdims=True)
    return xf * lax.rsqrt(ms + EPS) * g


def _causal_conv_rows(p, prev, cw_ref):
    r1 = pltpu.roll(p, 1, 0)
    r2 = pltpu.roll(p, 2, 0)
    row = lax.broadcasted_iota(jnp.int32, (SUBLANES, 1), 0)
    head1 = jnp.where(row == 0, prev[7:8, :], r1[:SUBLANES, :])
    head2 = jnp.where(row == 0, prev[6:7, :], jnp.where(row == 1, prev[7:8, :], r2[:SUBLANES, :]))
    m1 = jnp.concatenate([head1, r1[SUBLANES:, :]], axis=0)
    m2 = jnp.concatenate([head2, r2[SUBLANES:, :]], axis=0)
    return cw_ref[0:1, :] * m2 + cw_ref[1:2, :] * m1 + cw_ref[2:3, :] * p


def _causal_conv_slabs(x, prev, cw_ref, slab):
    w0, w1, w2 = cw_ref[0:1, :], cw_ref[1:2, :], cw_ref[2:3, :]
    seq = list(prev) + [x[t * slab:(t + 1) * slab, :] for t in range(x.shape[0] // slab)]
    y = [w0 * seq[t] + w1 * seq[t + 1] + w2 * seq[t + 2] for t in range(len(seq) - 2)]
    return jnp.concatenate(y, axis=0), seq[-2:]


def _slabs(tm, slab):
    return [slice(s * slab, (s + 1) * slab) for s in range(tm // slab)]


def _col_blocks(a, tn):
    r, c = a.shape
    return jnp.transpose(a.reshape(r, c // tn, tn), (1, 0, 2))


def _post_mix(y, x, gt, gpost, sh, sc, gpre):
    x1 = x + gt * _rms(y, gpost)
    h2 = _rms(x1, gpre) * (1.0 + sc) + sh
    return x1, h2.astype(_BF16)


class _Side(NamedTuple):
    src: jax.Array
    axis: int
    start: int
    block: int
    n_blocks: int


def _convert(side):
    for src_ref, dst_ref in side:
        dst_ref[...] = src_ref[...].astype(_BF16)


def _run(body, *, grid, in_specs, args, out_specs, out_shape, scratch=(), side=(), name,
         vmem_limit_bytes=VMEM_LIMIT_BYTES):
    n_in, n_out, n_side = len(args), len(out_shape), len(side)
    n_steps = functools.reduce(lambda a, b: a * b, grid)
    step_of = (lambda i: i) if len(grid) == 1 else (lambda i, j: i * grid[1] + j)
    side_in, side_out, side_shape = [], [], []
    for s in side:
        assert s.n_blocks <= n_steps, (name, s.n_blocks, n_steps)
        other = s.src.shape[1 - s.axis]
        pos = lambda *ids, s=s: jnp.minimum(step_of(*ids), s.n_blocks - 1)
        if s.axis == 1:
            blk, full = (other, s.block), (other, s.block * s.n_blocks)
            side_in.append(pl.BlockSpec(blk, lambda *ids, s=s, pos=pos: (0, s.start + pos(*ids))))
            side_out.append(pl.BlockSpec(blk, lambda *ids, pos=pos: (0, pos(*ids))))
        else:
            blk, full = (s.block, other), (s.block * s.n_blocks, other)
            side_in.append(pl.BlockSpec(blk, lambda *ids, s=s, pos=pos: (s.start + pos(*ids), 0)))
            side_out.append(pl.BlockSpec(blk, lambda *ids, pos=pos: (pos(*ids), 0)))
        side_shape.append(jax.ShapeDtypeStruct(full, _BF16))

    def kern(*refs):
        o0 = n_in + n_side
        s0 = o0 + n_out + n_side
        body(*refs[:n_in], *refs[o0:o0 + n_out], *refs[s0:],
             side=tuple(zip(refs[n_in:o0], refs[o0 + n_out:s0])))

    res = pl.pallas_call(
        kern, grid=grid,
        in_specs=list(in_specs) + side_in, out_specs=list(out_specs) + side_out,
        out_shape=list(out_shape) + side_shape, scratch_shapes=list(scratch),
        compiler_params=pltpu.CompilerParams(dimension_semantics=("arbitrary",) * len(grid),
                                             vmem_limit_bytes=vmem_limit_bytes),
        name=name,
    )(*args, *[s.src for s in side])
    return res[:n_out], res[n_out:]


def _mod_kernel(c_ref, w_ref, b_ref, o_ref, *, side):
    _convert(side)
    c = c_ref[...]
    a = (c * jax.nn.sigmoid(c)).astype(_BF16)
    o_ref[...] = _dot(a, w_ref[...].astype(_BF16)) + b_ref[...]


def _mod_call(c_all, w_ada, b_ada, *, side):
    rows, d = c_all.shape
    n = w_ada.shape[1]
    tn = ADALN_COL_TILE
    (mod,), copies = _run(
        _mod_kernel, grid=(n // tn,),
        in_specs=[pl.BlockSpec((rows, d), lambda j: (0, 0)),
                  pl.BlockSpec((d, tn), lambda j: (0, j)),
                  pl.BlockSpec((1, tn), lambda j: (0, j))],
        args=(c_all, w_ada, b_ada),
        out_specs=[pl.BlockSpec((rows, tn), lambda j: (0, j))],
        out_shape=[jax.ShapeDtypeStruct((rows, n), _F32)],
        side=side, name="adaln_mod")
    return mod, copies


def _s_gmlp_kernel(x_ref, sh_ref, sc_ref, gpre_ref, wv_ref, wu_ref, gv_ref, ws_ref, bias_ref,
                   ya_ref, h_ref, vn_ref, wvb_ref, wub_ref, v_scr, *, tm, tn, n_blk, slab, side):
    j = pl.program_id(0)
    d = n_blk * tn
    slabs = _slabs(tm, slab)
    n_slab = len(slabs)

    @pl.when(j == 0)
    def _():
        for t, r in enumerate(slabs):
            h_ref[r, :] = (_rms(x_ref[:, t * d:(t + 1) * d], gpre_ref[...]) * (1.0 + sc_ref[...])
                           + sh_ref[...]).astype(_BF16)

    @pl.when(j < n_blk)
    def _():
        wv = wv_ref[...].astype(_BF16)
        wvb_ref[...] = wv
        v_scr[j] = _dot(h_ref[...], wv)

    @pl.when(j == n_blk)
    def _():
        for t, r in enumerate(slabs):
            ss = 0.0
            for k in range(n_blk):
                vk = v_scr[k, r, :]
                ss = ss + jnp.sum(vk * vk, axis=-1, keepdims=True)
            rs = lax.rsqrt(ss * (1.0 / d) + EPS)
            for k in range(n_blk):
                vn = v_scr[k, r, :] * rs * gv_ref[:, k * tn:(k + 1) * tn]
                v_scr[k, r, :] = vn
                vn_ref[:, t * d + k * tn:t * d + (k + 1) * tn] = vn
        for t in reversed(range(n_slab)):
            for k in range(n_blk):
                c = slice(k * tn, (k + 1) * tn)
                acc = ws_ref[t * n_slab:t * n_slab + 1, c] * v_scr[k, slabs[0], :]
                for s in range(1, t + 1):
                    acc = acc + ws_ref[t * n_slab + s:t * n_slab + s + 1, c] * v_scr[k, slabs[s], :]
                v_scr[k, slabs[t], :] = acc + bias_ref[t:t + 1, c]

    @pl.when(j >= n_blk)
    def _():
        wu = wu_ref[...].astype(_BF16)
        wub_ref[...] = wu
        ya_ref[...] = (_dot(h_ref[...], wu) * v_scr[j - n_blk]).astype(_BF16)


def _s_gmlp_call(x, mod, gpre, w_in, gv, wvec, bvec, *, tn):
    slab = x.shape[0]
    d = gpre.shape[1]
    tm = x.shape[1] // d * slab
    n_blk = d // tn
    full = lambda a: pl.BlockSpec(a.shape, lambda j: (0,) * a.ndim)
    u_map = lambda j: (0, jnp.maximum(j - n_blk, 0))
    v_map = lambda j: (0, jnp.minimum(j, n_blk - 1))
    (ya, h, vn, w_v, w_u), _ = _run(
        functools.partial(_s_gmlp_kernel, tm=tm, tn=tn, n_blk=n_blk, slab=slab),
        grid=(2 * n_blk,),
        in_specs=[full(x),
                  pl.BlockSpec((slab, d), lambda j: (0, 0)), pl.BlockSpec((slab, d), lambda j: (0, 1)),
                  full(gpre),
                  pl.BlockSpec((d, tn), lambda j: (0, n_blk + jnp.minimum(j, n_blk - 1))),
                  pl.BlockSpec((d, tn), u_map),
                  full(gv), full(wvec), full(bvec)],
        args=(x, mod, mod, gpre, w_in, w_in, gv, wvec, bvec),
        out_specs=[pl.BlockSpec((tm, tn), u_map), pl.BlockSpec((tm, d), lambda j: (0, 0)), full(x),
                   pl.BlockSpec((d, tn), v_map), pl.BlockSpec((d, tn), u_map)],
        out_shape=[jax.ShapeDtypeStruct((tm, d), _BF16), jax.ShapeDtypeStruct((tm, d), _BF16),
                   jax.ShapeDtypeStruct(x.shape, _F32),
                   jax.ShapeDtypeStruct((d, d), _BF16), jax.ShapeDtypeStruct((d, d), _BF16)],
        scratch=[pltpu.VMEM((n_blk, tm, tn), _F32)],
        name="gmlp_sample")
    return ya, h, vn, w_v, w_u


def _s_shortconv_kernel(h_ref, wbg_ref, wcg_ref, wxb_ref, cw_ref, st0_ref, st1_ref,
                        yb_ref, t0_ref, t1_ref, *, slab, side):
    h = h_ref[...]
    bg = _dot(h, wbg_ref[...])
    p = _dot(h, wcg_ref[...]) * _dot(h, wxb_ref[...])
    cb, tail = _causal_conv_slabs(p, [st0_ref[...], st1_ref[...]], cw_ref, slab)
    yb_ref[...] = (bg * cb).astype(_BF16)
    t0_ref[...] = tail[0]
    t1_ref[...] = tail[1]


def _s_shortconv_call(h, w_bcx, cw, state, *, tn):
    tm, d = h.shape
    slab = state.shape[0]
    w = cw.shape[1]
    n_blk = w // tn
    wspec = lambda off: pl.BlockSpec((d, tn), lambda j: (0, off + j))
    sspec = lambda k: pl.BlockSpec((slab, tn), lambda j: (0, k * n_blk + j))
    tspec = pl.BlockSpec((slab, tn), lambda j: (0, j))
    (yb, t0, t1), _ = _run(
        functools.partial(_s_shortconv_kernel, slab=slab),
        grid=(n_blk,),
        in_specs=[pl.BlockSpec((tm, d), lambda j: (0, 0)),
                  wspec(0), wspec(n_blk), wspec(2 * n_blk),
                  pl.BlockSpec((CONV_K, tn), lambda j: (0, j)),
                  sspec(0), sspec(1)],
        args=(h, w_bcx, w_bcx, w_bcx, cw, state, state),
        out_specs=[pl.BlockSpec((tm, tn), lambda j: (0, j)), tspec, tspec],
        out_shape=[jax.ShapeDtypeStruct((tm, w), _BF16)] + [jax.ShapeDtypeStruct((slab, w), _F32)] * 2,
        name="shortconv_sample")
    return yb, t0, t1


def _merge_kernel(h_ref, ya_ref, yb_ref, wga_ref, wgb_ref, woa_ref, wob_ref, m_ref, *, tm, slab, side):
    _convert(side)
    for r in _slabs(tm, slab):
        h = h_ref[r, :]
        ga = jax.nn.sigmoid(_dot(h, wga_ref[...]))
        gb = jax.nn.sigmoid(_dot(h, wgb_ref[...]))
        m = ga * _dot(ya_ref[r, :], woa_ref[...]) + gb * _dot(yb_ref[r, :], wob_ref[...])
        m_ref[r, :] = m.astype(_BF16)


def _merge_call(h, ya, yb, w_gate, w_out_a, w_out_b, *, tm, tn, slab, side=(), name):
    m, d = h.shape
    n_blk = d // tn
    row = pl.BlockSpec((tm, d), lambda i, j: (i, 0))
    wspec = lambda off: pl.BlockSpec((d, tn), lambda i, j: (0, off + j))
    (mg,), copies = _run(
        functools.partial(_merge_kernel, tm=tm, slab=slab),
        grid=(m // tm, n_blk),
        in_specs=[row, row, row, wspec(0), wspec(n_blk), wspec(0), wspec(0)],
        args=(h, ya, yb, w_gate, w_gate, w_out_a, w_out_b),
        out_specs=[pl.BlockSpec((tm, tn), lambda i, j: (i, j))],
        out_shape=[jax.ShapeDtypeStruct((m, d), _BF16)],
        side=side, name=name)
    return mg, copies


def _s_proj_kernel(m_ref, wo_ref, x_ref, gt_ref, gpost_ref, sh_ref, sc_ref, gpre_ref,
                   x1_ref, h2_ref, y_scr, *, tm, tn, n_blk, slab, side):
    j = pl.program_id(0)
    d = n_blk * tn
    y_scr[j] = _dot(m_ref[...], wo_ref[...])

    @pl.when(j == n_blk - 1)
    def _():
        for t, r in enumerate(_slabs(tm, slab)):
            y = jnp.concatenate([y_scr[k, r, :] for k in range(n_blk)], axis=-1)
            x1, h2 = _post_mix(y, x_ref[:, t * d:(t + 1) * d], gt_ref[...], gpost_ref[...],
                               sh_ref[...], sc_ref[...], gpre_ref[...])
            x1_ref[r, :] = x1
            h2_ref[r, :] = h2


def _s_proj_call(mg, w_o, x, mod, gpost, gpre, *, tn):
    tm, d = mg.shape
    slab = x.shape[0]
    n_blk = d // tn
    row = pl.BlockSpec((tm, d), lambda j: (0, 0))
    vec = pl.BlockSpec((1, d), lambda j: (0, 0))
    mspec = lambda k: pl.BlockSpec((slab, d), lambda j: (0, k))
    (x1, h2), _ = _run(
        functools.partial(_s_proj_kernel, tm=tm, tn=tn, n_blk=n_blk, slab=slab),
        grid=(n_blk,),
        in_specs=[row, pl.BlockSpec((d, tn), lambda j: (0, j)),
                  pl.BlockSpec(x.shape, lambda j: (0, 0)), mspec(2), vec, mspec(3), mspec(4), vec],
        args=(mg, w_o, x, mod, gpost, mod, mod, gpre),
        out_specs=[row, row],
        out_shape=[jax.ShapeDtypeStruct((tm, d), _F32), jax.ShapeDtypeStruct((tm, d), _BF16)],
        scratch=[pltpu.VMEM((n_blk, tm, tn), _F32)],
        name="out_proj_sample")
    return x1, h2


def _s_ffn_kernel(h_ref, wa_ref, wb_ref, cw_ref, wd_ref, x1_ref, gt_ref, gpost_ref, st0_ref, st1_ref,
                  out_ref, t0_ref, t1_ref, *, tm, n_blk, slab, side):
    j = pl.program_id(0)
    d = x1_ref.shape[1]
    lanes = [slice(t * d, (t + 1) * d) for t in range(tm // slab)]

    @pl.when(j == 0)
    def _():
        out_ref[...] = jnp.zeros(out_ref.shape, _F32)

    prev = [st0_ref[...], st1_ref[...]]
    gs = []
    for r in _slabs(tm, 2 * slab):
        h = h_ref[r, :]
        ac, prev = _causal_conv_slabs(_dot(h, wa_ref[...]), prev, cw_ref, slab)
        gs.append((jax.nn.gelu(ac) * _dot(h, wb_ref[...])).astype(_BF16))
    for c, g in enumerate(gs):
        f = _dot(g, wd_ref[...])
        out_ref[:, lanes[2 * c]] += f[:slab, :]
        out_ref[:, lanes[2 * c + 1]] += f[slab:, :]
    t0_ref[...] = prev[0]
    t1_ref[...] = prev[1]

    @pl.when(j == n_blk - 1)
    def _():
        for t, r in enumerate(_slabs(tm, slab)):
            out_ref[:, lanes[t]] = (x1_ref[r, :]
                                    + gt_ref[...] * _rms(out_ref[:, lanes[t]], gpost_ref[...]))


def _s_ffn_call(h2, w_a, w_b, cw, w_down, x1, mod, gpost, state, *, tn):
    tm, d = x1.shape
    slab = state.shape[0]
    f = cw.shape[1]
    n_blk = f // tn
    out_shape = (slab, tm // slab * d)
    row = pl.BlockSpec((tm, d), lambda j: (0, 0))
    wspec = pl.BlockSpec((d, tn), lambda j: (0, j))
    sspec = lambda k: pl.BlockSpec((slab, tn), lambda j: (0, k * n_blk + j))
    tspec = pl.BlockSpec((slab, tn), lambda j: (0, j))
    (out, t0, t1), _ = _run(
        functools.partial(_s_ffn_kernel, tm=tm, n_blk=n_blk, slab=slab),
        grid=(n_blk,),
        in_specs=[row, wspec, wspec, pl.BlockSpec((CONV_K, tn), lambda j: (0, j)),
                  pl.BlockSpec((tn, d), lambda j: (j, 0)), row,
                  pl.BlockSpec((slab, d), lambda j: (0, 5)), pl.BlockSpec((1, d), lambda j: (0, 0)),
                  sspec(0), sspec(1)],
        args=(h2, w_a, w_b, cw, w_down, x1, mod, gpost, state, state),
        out_specs=[pl.BlockSpec(out_shape, lambda j: (0, 0)), tspec, tspec],
        out_shape=[jax.ShapeDtypeStruct(out_shape, _F32)]
        + [jax.ShapeDtypeStruct((slab, f), _F32)] * 2,
        name="convffn_sample")
    return out, t0, t1


def _p_gmlp_kernel(x_hbm, sh_ref, sc_ref, gpre_ref, wv_ref, wu_ref, gv_ref, ws_ref, bias_ref,
                   ya_ref, h_ref, x_buf, x_sem, v_scr, wt_scr, *, tm, tn, n_blk, slab, side):
    i = pl.program_id(0)
    j = pl.program_id(1)
    d = n_blk * tn
    slabs = _slabs(tm, slab)

    def x_copy(tile):
        return pltpu.make_async_copy(x_hbm.at[pl.ds(pl.multiple_of(tile * tm, tm), tm), :],
                                     x_buf, x_sem)

    @pl.when((i == 0) & (j == 0))
    def _():
        x_copy(0).start()
        tril = (lax.broadcasted_iota(jnp.int32, (CHUNK, CHUNK), 0)
                >= lax.broadcasted_iota(jnp.int32, (CHUNK, CHUNK), 1))
        for g in range(d // GROUP):
            wt_scr[g] = jnp.where(tril, ws_ref[g], 0.0).astype(_BF16)

    @pl.when(j == 0)
    def _():
        x_copy(i).wait()
        _convert(side)
        g = gpre_ref[...] * (1.0 + sc_ref[...])
        for r in slabs:
            h = (_rms(x_buf[r, :], g) + sh_ref[...]).astype(_BF16)
            h_ref[r, :] = h
            v_scr[0, r, :] = _dot(h, wv_ref[...])

    @pl.when((j == 0) & (i + 1 < pl.num_programs(0)))
    def _():
        x_copy(i + 1).start()

    @pl.when((j > 0) & (j < n_blk))
    def _():
        _convert(side)
        for r in slabs:
            v_scr[j, r, :] = _dot(h_ref[r, :], wv_ref[...])

    def _gate():
        gpb = tn // GROUP
        for c in range(tm // CHUNK):
            r = slice(c * CHUNK, (c + 1) * CHUNK)
            ss = 0.0
            for k in range(n_blk):
                vk = v_scr[k, r, :]
                ss = ss + jnp.sum(vk * vk, axis=-1, keepdims=True)
            rs = lax.rsqrt(ss * (1.0 / d) + EPS)
            for k in range(n_blk):
                vb = (v_scr[k, r, :] * rs * gv_ref[:, k * tn:(k + 1) * tn]).astype(_BF16)
                for gg in range(gpb):
                    g = k * gpb + gg
                    lanes = slice(gg * GROUP, (gg + 1) * GROUP)
                    v_scr[k, r, lanes] = (_dot(wt_scr[g], vb[:, lanes])
                                          + bias_ref[:, g * GROUP:(g + 1) * GROUP])

    @pl.when(j == n_blk)
    def _():
        _convert(side)
        _gate()
        for r in slabs:
            ya_ref[r, :] = (_dot(h_ref[r, :], wu_ref[...]) * v_scr[0, r, :]).astype(_BF16)

    @pl.when(j > n_blk)
    def _():
        _convert(side)
        for r in slabs:
            ya_ref[r, :] = (_dot(h_ref[r, :], wu_ref[...]) * v_scr[j - n_blk, r, :]).astype(_BF16)


def _p_gmlp_call(x, mod, gpre, w_v, w_u, gv, ws, bias, *, tm, tn, slab, tiles_per_seq, mod_row0,
                 side):
    m, d = x.shape
    n_blk = d // tn
    full = lambda a: pl.BlockSpec(a.shape, lambda i, j: (0,) * a.ndim)
    once = lambda a: pl.BlockSpec(a.shape, lambda i, j: (0,) * a.ndim, pipeline_mode=pl.Buffered(1))
    mspec = lambda k: pl.BlockSpec((None, 1, d), lambda i, j: (mod_row0 + i // tiles_per_seq, 0, k))
    u_map = lambda i, j: (i, jnp.maximum(j - n_blk, 0))
    (ya, h), copies = _run(
        functools.partial(_p_gmlp_kernel, tm=tm, tn=tn, n_blk=n_blk, slab=slab),
        grid=(m // tm, 2 * n_blk),
        in_specs=[pl.BlockSpec(memory_space=pl.ANY), mspec(0), mspec(1), full(gpre),
                  pl.BlockSpec((d, tn), lambda i, j: (0, jnp.minimum(j, n_blk - 1))),
                  pl.BlockSpec((d, tn), lambda i, j: (0, jnp.maximum(j - n_blk, 0))),
                  full(gv), once(ws), once(bias)],
        args=(x, mod, mod, gpre, w_v, w_u, gv, ws, bias),
        out_specs=[pl.BlockSpec((tm, tn), u_map), pl.BlockSpec((tm, d), lambda i, j: (i, 0))],
        out_shape=[jax.ShapeDtypeStruct((m, d), _BF16), jax.ShapeDtypeStruct((m, d), _BF16)],
        scratch=[pltpu.VMEM((tm, d), _F32), pltpu.SemaphoreType.DMA(()),
                 pltpu.VMEM((n_blk, tm, tn), _F32), pltpu.VMEM((d // GROUP, CHUNK, CHUNK), _BF16)],
        side=side, name="gmlp_prompt", vmem_limit_bytes=VMEM_LIMIT_BYTES_WIDE)
    return ya, h, copies


def _p_shortconv_kernel(h_ref, wbg_ref, wcg_ref, wxb_ref, cw_ref, yb_ref, tail_ref, carry_scr,
                        *, tm, tiles_per_seq, slab, side):
    i = pl.program_id(0)
    j = pl.program_id(1)

    @pl.when(i % tiles_per_seq == 0)
    def _():
        carry_scr[j] = jnp.zeros(carry_scr.shape[1:], _F32)

    _convert(side)
    prev = carry_scr[j]
    cw = cw_ref.at[j]
    for r in _slabs(tm, slab):
        h = h_ref[r, :]
        bg = _dot(h, wbg_ref[...])
        p = _dot(h, wcg_ref[...]) * _dot(h, wxb_ref[...])
        yb_ref[r, :] = (bg * _causal_conv_rows(p, prev, cw)).astype(_BF16)
        prev = p[slab - SUBLANES:, :]
    carry_scr[j] = prev
    tail_ref[j] = prev


def _p_shortconv_call(h, w_bcx, cw, *, tm, tn, tiles_per_seq, slab, side):
    m, d = h.shape
    w = cw.shape[1]
    n_blk = w // tn
    wspec = lambda off: pl.BlockSpec((d, tn), lambda i, j: (0, off + j))
    (yb, tail), copies = _run(
        functools.partial(_p_shortconv_kernel, tm=tm, tiles_per_seq=tiles_per_seq, slab=slab),
        grid=(m // tm, n_blk),
        in_specs=[pl.BlockSpec((tm, d), lambda i, j: (i, 0)),
                  wspec(0), wspec(n_blk), wspec(2 * n_blk),
                  pl.BlockSpec((n_blk, CONV_K, tn), lambda i, j: (0, 0, 0))],
        args=(h, w_bcx, w_bcx, w_bcx, _col_blocks(cw, tn)),
        out_specs=[pl.BlockSpec((tm, tn), lambda i, j: (i, j)),
                   pl.BlockSpec((None, n_blk, SUBLANES, tn), lambda i, j: (i, 0, 0, 0))],
        out_shape=[jax.ShapeDtypeStruct((m, w), _BF16),
                   jax.ShapeDtypeStruct((m // tm, n_blk, SUBLANES, tn), _F32)],
        scratch=[pltpu.VMEM((n_blk, SUBLANES, tn), _F32)],
        side=side, name="shortconv_prompt")
    return yb, tail, copies


def _p_proj_kernel(m_ref, wo_ref, x_ref, gt_ref, gpost_ref, sh_ref, sc_ref, gpre_ref,
                   x1_ref, h2_ref, *, tm, slab, side):
    g1 = gt_ref[...] * gpost_ref[...]
    g2 = gpre_ref[...] * (1.0 + sc_ref[...])
    for r in _slabs(tm, slab):
        y = _dot(m_ref[r, :], wo_ref[...])
        x1 = x_ref[r, :] + _rms(y, g1)
        x1_ref[r, :] = x1
        h2_ref[r, :] = (_rms(x1, g2) + sh_ref[...]).astype(_BF16)


def _p_proj_call(mg, w_o, x, mod, gpost, gpre, *, tm, slab, tiles_per_seq, mod_row0):
    m, d = x.shape
    row = pl.BlockSpec((tm, d), lambda i: (i, 0))
    vec = pl.BlockSpec((1, d), lambda i: (0, 0))
    mspec = lambda k: pl.BlockSpec((None, 1, d), lambda i: (mod_row0 + i // tiles_per_seq, 0, k))
    (x1, h2), _ = _run(
        functools.partial(_p_proj_kernel, tm=tm, slab=slab),
        grid=(m // tm,),
        in_specs=[row, pl.BlockSpec((d, d), lambda i: (0, 0)), row, mspec(2), vec, mspec(3),
                  mspec(4), vec],
        args=(mg, w_o, x, mod, gpost, mod, mod, gpre),
        out_specs=[row, row],
        out_shape=[jax.ShapeDtypeStruct((m, d), _F32), jax.ShapeDtypeStruct((m, d), _BF16)],
        name="out_proj_prompt")
    return x1, h2


def _p_ffn_kernel(h_ref, wa_ref, wb_ref, cw_ref, wd_ref, x1_hbm, gt_ref, gpost_ref,
                  out_ref, tail_ref, x1_buf, x1_sem, carry_scr, *, tm, n_blk, tiles_per_seq, slab,
                  side):
    i = pl.program_id(0)
    j = pl.program_id(1)
    x1_copy = pltpu.make_async_copy(x1_hbm.at[pl.ds(pl.multiple_of(i * tm, tm), tm), :],
                                    x1_buf, x1_sem)

    @pl.when(i % tiles_per_seq == 0)
    def _():
        carry_scr[j] = jnp.zeros(carry_scr.shape[1:], _F32)

    def step(first, last):
        prev = carry_scr[j]
        cw = cw_ref.at[j]
        slabs = _slabs(tm, slab)
        gs = []
        for r in slabs:
            h = h_ref[r, :]
            a = _dot(h, wa_ref[...])
            b = _dot(h, wb_ref[...])
            gs.append((jax.nn.gelu(_causal_conv_rows(a, prev, cw)) * b).astype(_BF16))
            prev = a[slab - SUBLANES:, :]
        for r, g in zip(slabs, gs):
            f = _dot(g, wd_ref[...])
            acc = f if first else out_ref[r, :] + f
            if last:
                acc = x1_buf[r, :] + _rms(acc, gt_ref[...] * gpost_ref[...])
            out_ref[r, :] = acc
        carry_scr[j] = prev
        tail_ref[j] = prev

    @pl.when(j == 0)
    def _():
        x1_copy.start()
        step(True, False)

    @pl.when((j > 0) & (j < n_blk - 1))
    def _():
        step(False, False)

    @pl.when(j == n_blk - 1)
    def _():
        x1_copy.wait()
        step(False, True)


def _p_ffn_call(h2, w_a, w_b, cw, w_down, x1, mod, gpost, *, tm, tn, tiles_per_seq, slab, mod_row0):
    m, d = x1.shape
    f = cw.shape[1]
    n_blk = f // tn
    row = pl.BlockSpec((tm, d), lambda i, j: (i, 0))
    wspec = pl.BlockSpec((d, tn), lambda i, j: (0, j))
    (out, tail), _ = _run(
        functools.partial(_p_ffn_kernel, tm=tm, n_blk=n_blk, tiles_per_seq=tiles_per_seq, slab=slab),
        grid=(m // tm, n_blk),
        in_specs=[row, wspec, wspec,
                  pl.BlockSpec((n_blk, CONV_K, tn), lambda i, j: (0, 0, 0)),
                  pl.BlockSpec((tn, d), lambda i, j: (j, 0)),
                  pl.BlockSpec(memory_space=pl.ANY),
                  pl.BlockSpec((None, 1, d), lambda i, j: (mod_row0 + i // tiles_per_seq, 0, 5)),
                  pl.BlockSpec((1, d), lambda i, j: (0, 0))],
        args=(h2, w_a, w_b, _col_blocks(cw, tn), w_down, x1, mod, gpost),
        out_specs=[row, pl.BlockSpec((None, n_blk, SUBLANES, tn), lambda i, j: (i, 0, 0, 0))],
        out_shape=[jax.ShapeDtypeStruct((m, d), _F32),
                   jax.ShapeDtypeStruct((m // tm, n_blk, SUBLANES, tn), _F32)],
        scratch=[pltpu.VMEM((tm, d), _F32), pltpu.SemaphoreType.DMA(()),
                 pltpu.VMEM((n_blk, SUBLANES, tn), _F32)],
        name="convffn_prompt")
    return out, tail


def _cols(src, first_col, n_cols, n_blocks):
    block = n_cols // n_blocks
    assert block * n_blocks == n_cols and block % 128 == 0 and first_col % block == 0
    return _Side(src, 1, first_col // block, block, n_blocks)


def _layer(xs, xp, mod, st_b, st_f, p, *, seq_len, mod_row0):
    d = p["g_v"].shape[1]
    wb = p["conv_b_w"].shape[1]
    f = p["conv_f_w"].shape[1]
    tn_s = SAMPLE_COL_TILE
    mod_rows = mod.reshape(mod.shape[0], 1, N_MOD * d)

    def tiles(name):
        tm, tn, slab = PROMPT_TILES[name]
        return dict(tm=tm, tn=tn, slab=slab, tiles_per_seq=seq_len // tm)

    def n_steps(name):
        tm, tn, _ = PROMPT_TILES[name]
        cols = {"gmlp": 2 * d, "shortconv": wb, "merge": d, "ffn": f}[name]
        return xp.shape[0] // tm * (cols // tn)

    def without(kw, *names):
        return {k: v for k, v in kw.items() if k not in names}

    def blocks(n_cols, budget):
        return max(n for n in range(1, budget + 1) if n_cols % (128 * n) == 0)

    w_in, w_up = p["w_in"], p["w_up"]

    ya_s, h1_s, vn, w_v, w_u = _s_gmlp_call(xs, mod, p["g_pre_mix"], w_in, p["g_v"], p["wvec"],
                                            p["bvec"], tn=tn_s)
    n = n_steps("gmlp")
    ya_p, h1_p, (w_bcx,) = _p_gmlp_call(
        xp, mod_rows, p["g_pre_mix"], w_v, w_u, p["g_v"], p["w_s"], p["bias"], mod_row0=mod_row0,
        side=(_cols(w_in, 2 * d, 3 * wb, blocks(3 * wb, n)),),
        **tiles("gmlp"))

    yb_s, tb0, tb1 = _s_shortconv_call(h1_s, w_bcx, p["conv_b_w"], st_b, tn=SAMPLE_COL_TILE_NARROW)
    n = n_steps("shortconv")
    yb_p, tail_b, (w_gate, w_oa, w_ob, w_o, w_a, w_b) = _p_shortconv_call(
        h1_p, w_bcx, p["conv_b_w"],
        side=(_cols(w_in, 2 * d + 3 * wb, 2 * d, blocks(2 * d, n)),
              _cols(p["w_out_a"], 0, d, blocks(d, n)), _cols(p["w_out_b"], 0, d, blocks(d, n)),
              _cols(p["w_o"], 0, d, blocks(d, n)),
              _cols(w_up, 0, f, blocks(f, n)), _cols(w_up, f, f, blocks(f, n))),
        **tiles("shortconv"))

    tm_s = h1_s.shape[0]
    mg_s, _ = _merge_call(h1_s, ya_s, yb_s, w_gate, w_oa, w_ob, tm=tm_s,
                          tn=SAMPLE_COL_TILE_NARROW, slab=tm_s // 2,
                          name="gated_merge_sample")
    n_row_blocks = max(k for k in range(1, n_steps("merge") + 1)
                       if f % k == 0 and (f // k) % (2 * SUBLANES) == 0)
    mg_p, (w_d,) = _merge_call(h1_p, ya_p, yb_p, w_gate, w_oa, w_ob, name="gated_merge_prompt",
                               side=(_Side(p["w_down"], 0, 0, f // n_row_blocks, n_row_blocks),),
                               **without(tiles("merge"), "tiles_per_seq"))

    x1_s, h2_s = _s_proj_call(mg_s, w_o, xs, mod, p["g_post_mix"], p["g_pre_ffn"], tn=tn_s)
    x1_p, h2_p = _p_proj_call(mg_p, w_o, xp, mod_rows, p["g_post_mix"], p["g_pre_ffn"],
                              mod_row0=mod_row0, **without(tiles("proj"), "tn"))

    out_s, tf0, tf1 = _s_ffn_call(h2_s, w_a, w_b, p["conv_f_w"], w_d, x1_s, mod, p["g_post_ffn"],
                                  st_f, tn=tn_s)
    out_p, tail_f = _p_ffn_call(h2_p, w_a, w_b, p["conv_f_w"], w_d, x1_p, mod_rows, p["g_post_ffn"],
                                mod_row0=mod_row0, **tiles("ffn"))
    return out_s, out_p, (tb0, tb1), (tf0, tf1), vn, tail_b, tail_f


def kernel(x_prompt, x_sample, c_prompt, c_sample, state_conv_b, state_conv_ffn, w_ada, b_ada, g_pre_mix, g_post_mix, w_in, g_v, w_s, b_s, conv_b_w, w_out_a, w_out_b, w_o, g_pre_ffn, g_post_ffn, w_up, conv_f_w, w_down):
    depth = w_in.shape[0]
    bp, seq, d = x_prompt.shape
    bs, tdec, _ = x_sample.shape
    n_groups = w_s.shape[1]
    assert bs == CHUNK and tdec <= CHUNK
    assert all(seq % tm == 0 for tm, _, _ in PROMPT_TILES.values())

    xp = x_prompt.reshape(bp * seq, d)
    xs = x_sample.reshape(bs, tdec * d)
    pad = (-(bp + bs)) % SUBLANES
    c_all = jnp.concatenate([c_sample, c_prompt, jnp.zeros((pad, d), _F32)], axis=0)

    pb, sb, pf, sf, sv = [], [], [], [], []
    for l in range(depth):
        mod, _ = _mod_call(c_all, w_ada[l], b_ada[l][None, :], side=())
        vec = lambda a: a[l][None, :]
        bias_full = jnp.repeat(jnp.transpose(b_s[l]), GROUP, axis=1)
        wvec = jnp.repeat(
            jnp.transpose(w_s[l][:, :tdec, :tdec], (1, 2, 0)).reshape(tdec * tdec, n_groups),
            GROUP, axis=1)
        p = {
            "w_in": w_in[l], "w_out_a": w_out_a[l], "w_out_b": w_out_b[l],
            "w_o": w_o[l], "w_up": w_up[l], "w_down": w_down[l],
            "g_pre_mix": vec(g_pre_mix), "g_post_mix": vec(g_post_mix), "g_v": vec(g_v),
            "g_pre_ffn": vec(g_pre_ffn), "g_post_ffn": vec(g_post_ffn),
            "conv_b_w": conv_b_w[l], "conv_f_w": conv_f_w[l],
            "w_s": w_s[l], "bias": bias_full, "wvec": wvec, "bvec": bias_full[:tdec],
        }
        st_b = state_conv_b[l].reshape(bs, -1)
        st_f = state_conv_ffn[l].reshape(bs, -1)
        xs, xp, sbt, sft, vn, tb, tf = _layer(xs, xp, mod, st_b, st_f, p, seq_len=seq, mod_row0=bs)

        def prompt_tail(t):
            n_tiles, n_blk, _, tn = t.shape
            t = t.reshape(bp, n_tiles // bp, n_blk, SUBLANES, tn)[:, -1, :, SUBLANES - (CONV_K - 1):, :]
            return jnp.transpose(t, (0, 2, 1, 3)).reshape(bp, CONV_K - 1, n_blk * tn)

        pb.append(prompt_tail(tb))
        pf.append(prompt_tail(tf))
        sb.append(jnp.stack(sbt, axis=1))
        sf.append(jnp.stack(sft, axis=1))
        sv.append(vn.reshape(bs, tdec, d))

    y_prompt = xp.reshape(bp, seq, d)
    return (y_prompt, xs.reshape(bs, tdec, d), jnp.stack(pb), jnp.stack(sb), jnp.stack(pf), jnp.stack(sf),
            jnp.stack(sv))
```

```python
import functools
from typing import NamedTuple

import jax
import jax.numpy as jnp
from jax import lax
from jax.experimental import pallas as pl
from jax.experimental.pallas import tpu as pltpu

EPS = 1e-6
CHUNK = 128
GROUP = 128
CONV_K = 3
N_MOD = 6
SUBLANES = 8
VMEM_LIMIT_BYTES = 56 * 1024 * 1024
VMEM_LIMIT_BYTES_WIDE = 60 * 1024 * 1024
PROMPT_TILES = {
    "gmlp": (1024, 1024, 256),
    "shortconv": (1024, 512, 256),
    "merge": (512, 1024, 256),
    "proj": (512, None, 256),
    "ffn": (1024, 512, 256),
}
SAMPLE_COL_TILE = 512
ADALN_COL_TILE = 1024

_BF16 = jnp.bfloat16
_F32 = jnp.float32


def _dot(a, b):
    return jnp.dot(a, b, preferred_element_type=_F32)


def _rms(xf, g):
    ms = jnp.mean(xf * xf, axis=-1, keepdims=True)
    return xf * lax.rsqrt(ms + EPS) * g


def _causal_conv_rows(p, prev, cw_ref):
    r1 = pltpu.roll(p, 1, 0)
    r2 = pltpu.roll(p, 2, 0)
    row = lax.broadcasted_iota(jnp.int32, (SUBLANES, 1), 0)
    head1 = jnp.where(row == 0, prev[7:8, :], r1[:SUBLANES, :])
    head2 = jnp.where(row == 0, prev[6:7, :], jnp.where(row == 1, prev[7:8, :], r2[:SUBLANES, :]))
    m1 = jnp.concatenate([head1, r1[SUBLANES:, :]], axis=0)
    m2 = jnp.concatenate([head2, r2[SUBLANES:, :]], axis=0)
    return cw_ref[0:1, :] * m2 + cw_ref[1:2, :] * m1 + cw_ref[2:3, :] * p


def _causal_conv_slabs(x, prev, cw_ref, slab):
    w0, w1, w2 = cw_ref[0:1, :], cw_ref[1:2, :], cw_ref[2:3, :]
    seq = list(prev) + [x[t * slab:(t + 1) * slab, :] for t in range(x.shape[0] // slab)]
    y = [w0 * seq[t] + w1 * seq[t + 1] + w2 * seq[t + 2] for t in range(len(seq) - 2)]
    return jnp.concatenate(y, axis=0), seq[-2:]


def _slabs(tm, slab):
    return [slice(s * slab, (s + 1) * slab) for s in range(tm // slab)]


def _col_blocks(a, tn):
    r, c = a.shape
    return jnp.transpose(a.reshape(r, c // tn, tn), (1, 0, 2))


def _post_mix(y, x, gt, gpost, sh, sc, gpre):
    x1 = x + gt * _rms(y, gpost)
    h2 = _rms(x1, gpre) * (1.0 + sc) + sh
    return x1, h2.astype(_BF16)


class _Side(NamedTuple):
    src: jax.Array
    axis: int
    start: int
    block: int
    n_blocks: int


def _convert(side):
    for src_ref, dst_ref in side:
        dst_ref[...] = src_ref[...].astype(_BF16)


def _run(body, *, grid, in_specs, args, out_specs, out_shape, scratch=(), side=(), name,
         vmem_limit_bytes=VMEM_LIMIT_BYTES):
    n_in, n_out, n_side = len(args), len(out_shape), len(side)
    n_steps = functools.reduce(lambda a, b: a * b, grid)
    step_of = (lambda i: i) if len(grid) == 1 else (lambda i, j: i * grid[1] + j)
    side_in, side_out, side_shape = [], [], []
    for s in side:
        assert s.n_blocks <= n_steps, (name, s.n_blocks, n_steps)
        other = s.src.shape[1 - s.axis]
        pos = lambda *ids, s=s: jnp.minimum(step_of(*ids), s.n_blocks - 1)
        if s.axis == 1:
            blk, full = (other, s.block), (other, s.block * s.n_blocks)
            side_in.append(pl.BlockSpec(blk, lambda *ids, s=s, pos=pos: (0, s.start + pos(*ids))))
            side_out.append(pl.BlockSpec(blk, lambda *ids, pos=pos: (0, pos(*ids))))
        else:
            blk, full = (s.block, other), (s.block * s.n_blocks, other)
            side_in.append(pl.BlockSpec(blk, lambda *ids, s=s, pos=pos: (s.start + pos(*ids), 0)))
            side_out.append(pl.BlockSpec(blk, lambda *ids, pos=pos: (pos(*ids), 0)))
        side_shape.append(jax.ShapeDtypeStruct(full, _BF16))

    def kern(*refs):
        o0 = n_in + n_side
        s0 = o0 + n_out + n_side
        body(*refs[:n_in], *refs[o0:o0 + n_out], *refs[s0:],
             side=tuple(zip(refs[n_in:o0], refs[o0 + n_out:s0])))

    res = pl.pallas_call(
        kern, grid=grid,
        in_specs=list(in_specs) + side_in, out_specs=list(out_specs) + side_out,
        out_shape=list(out_shape) + side_shape, scratch_shapes=list(scratch),
        compiler_params=pltpu.CompilerParams(dimension_semantics=("arbitrary",) * len(grid),
                                             vmem_limit_bytes=vmem_limit_bytes),
        name=name,
    )(*args, *[s.src for s in side])
    return res[:n_out], res[n_out:]


def _mod_kernel(c_ref, w_ref, b_ref, o_ref, *, side):
    _convert(side)
    c = c_ref[...]
    a = (c * jax.nn.sigmoid(c)).astype(_BF16)
    o_ref[...] = _dot(a, w_ref[...].astype(_BF16)) + b_ref[...]


def _mod_call(c_all, w_ada, b_ada, *, side):
    rows, d = c_all.shape
    n = w_ada.shape[1]
    tn = ADALN_COL_TILE
    (mod,), copies = _run(
        _mod_kernel, grid=(n // tn,),
        in_specs=[pl.BlockSpec((rows, d), lambda j: (0, 0)),
                  pl.BlockSpec((d, tn), lambda j: (0, j)),
                  pl.BlockSpec((1, tn), lambda j: (0, j))],
        args=(c_all, w_ada, b_ada),
        out_specs=[pl.BlockSpec((rows, tn), lambda j: (0, j))],
        out_shape=[jax.ShapeDtypeStruct((rows, n), _F32)],
        side=side, name="adaln_mod")
    return mod, copies


def _s_gmlp_kernel(x_hbm, sh_ref, sc_ref, gpre_ref, wv_ref, wu_ref, gv_ref, ws_ref, bias_ref,
                   ya_ref, h_ref, vn_hbm, wvb_ref, wub_ref, v_scr, x_buf, vn_buf, in_sem, out_sem,
                   *, tm, tn, n_blk, slab, side):
    j = pl.program_id(0)
    d = n_blk * tn
    slabs = _slabs(tm, slab)
    n_slab = len(slabs)
    x_copy = lambda t: pltpu.make_async_copy(x_hbm.at[:, t, :], x_buf.at[t], in_sem.at[t])
    vn_copy = lambda t: pltpu.make_async_copy(vn_buf.at[t], vn_hbm.at[:, t, :], out_sem.at[t])

    @pl.when(j == 0)
    def _():
        for t in range(n_slab):
            x_copy(t).start()
        for t, r in enumerate(slabs):
            x_copy(t).wait()
            h_ref[r, :] = (_rms(x_buf[t], gpre_ref[...]) * (1.0 + sc_ref[...])
                           + sh_ref[...]).astype(_BF16)

    @pl.when(j < n_blk)
    def _():
        wv = wv_ref[...].astype(_BF16)
        wvb_ref[...] = wv
        v_scr[j] = _dot(h_ref[...], wv)

    @pl.when(j == n_blk)
    def _():
        for t, r in enumerate(slabs):
            ss = 0.0
            for k in range(n_blk):
                vk = v_scr[k, r, :]
                ss = ss + jnp.sum(vk * vk, axis=-1, keepdims=True)
            rs = lax.rsqrt(ss * (1.0 / d) + EPS)
            for k in range(n_blk):
                vn = v_scr[k, r, :] * rs * gv_ref[:, k * tn:(k + 1) * tn]
                v_scr[k, r, :] = vn
                vn_buf[t, :, k * tn:(k + 1) * tn] = vn
            vn_copy(t).start()
        for t in reversed(range(n_slab)):
            for k in range(n_blk):
                c = slice(k * tn, (k + 1) * tn)
                acc = ws_ref[t * n_slab:t * n_slab + 1, c] * v_scr[k, slabs[0], :]
                for s in range(1, t + 1):
                    acc = acc + ws_ref[t * n_slab + s:t * n_slab + s + 1, c] * v_scr[k, slabs[s], :]
                v_scr[k, slabs[t], :] = acc + bias_ref[t:t + 1, c]

    @pl.when(j >= n_blk)
    def _():
        wu = wu_ref[...].astype(_BF16)
        wub_ref[...] = wu
        ya_ref[...] = (_dot(h_ref[...], wu) * v_scr[j - n_blk]).astype(_BF16)

    @pl.when(j == 2 * n_blk - 1)
    def _():
        for t in range(n_slab):
            vn_copy(t).wait()


def _s_gmlp_call(x, mod, gpre, w_in, gv, wvec, bvec, *, tn):
    slab, n_slab, d = x.shape
    tm = n_slab * slab
    n_blk = d // tn
    anywhere = pl.BlockSpec(memory_space=pl.ANY)
    full = lambda a: pl.BlockSpec(a.shape, lambda j: (0,) * a.ndim)
    u_map = lambda j: (0, jnp.maximum(j - n_blk, 0))
    v_map = lambda j: (0, jnp.minimum(j, n_blk - 1))
    (ya, h, vn, w_v, w_u), _ = _run(
        functools.partial(_s_gmlp_kernel, tm=tm, tn=tn, n_blk=n_blk, slab=slab),
        grid=(2 * n_blk,),
        in_specs=[anywhere,
                  pl.BlockSpec((slab, d), lambda j: (0, 0)), pl.BlockSpec((slab, d), lambda j: (0, 1)),
                  full(gpre),
                  pl.BlockSpec((d, tn), lambda j: (0, n_blk + jnp.minimum(j, n_blk - 1))),
                  pl.BlockSpec((d, tn), u_map),
                  full(gv), full(wvec), full(bvec)],
        args=(x, mod, mod, gpre, w_in, w_in, gv, wvec, bvec),
        out_specs=[pl.BlockSpec((tm, tn), u_map), pl.BlockSpec((tm, d), lambda j: (0, 0)), anywhere,
                   pl.BlockSpec((d, tn), v_map), pl.BlockSpec((d, tn), u_map)],
        out_shape=[jax.ShapeDtypeStruct((tm, d), _BF16), jax.ShapeDtypeStruct((tm, d), _BF16),
                   jax.ShapeDtypeStruct(x.shape, _F32),
                   jax.ShapeDtypeStruct((d, d), _BF16), jax.ShapeDtypeStruct((d, d), _BF16)],
        scratch=[pltpu.VMEM((n_blk, tm, tn), _F32),
                 pltpu.VMEM((n_slab, slab, d), _F32), pltpu.VMEM((n_slab, slab, d), _F32),
                 pltpu.SemaphoreType.DMA((n_slab,)), pltpu.SemaphoreType.DMA((n_slab,))],
        name="gmlp_sample")
    return ya, h, vn, w_v, w_u


def _s_shortconv_kernel(h_ref, wbg_ref, wcg_ref, wxb_ref, cw_ref, st0_ref, st1_ref,
                        yb_ref, t0_ref, t1_ref, *, slab, side):
    h = h_ref[...]
    bg = _dot(h, wbg_ref[...])
    p = _dot(h, wcg_ref[...]) * _dot(h, wxb_ref[...])
    cb, tail = _causal_conv_slabs(p, [st0_ref[...], st1_ref[...]], cw_ref, slab)
    yb_ref[...] = (bg * cb).astype(_BF16)
    t0_ref[...] = tail[0]
    t1_ref[...] = tail[1]


def _s_shortconv_call(h, w_bcx, cw, state, *, tn):
    tm, d = h.shape
    slab = state.shape[0]
    w = cw.shape[1]
    n_blk = w // tn
    wspec = lambda off: pl.BlockSpec((d, tn), lambda j: (0, off + j))
    sspec = lambda k: pl.BlockSpec((slab, tn), lambda j: (0, k * n_blk + j))
    tspec = pl.BlockSpec((slab, tn), lambda j: (0, j))
    (yb, t0, t1), _ = _run(
        functools.partial(_s_shortconv_kernel, slab=slab),
        grid=(n_blk,),
        in_specs=[pl.BlockSpec((tm, d), lambda j: (0, 0)),
                  wspec(0), wspec(n_blk), wspec(2 * n_blk),
                  pl.BlockSpec((CONV_K, tn), lambda j: (0, j)),
                  sspec(0), sspec(1)],
        args=(h, w_bcx, w_bcx, w_bcx, cw, state, state),
        out_specs=[pl.BlockSpec((tm, tn), lambda j: (0, j)), tspec, tspec],
        out_shape=[jax.ShapeDtypeStruct((tm, w), _BF16)] + [jax.ShapeDtypeStruct((slab, w), _F32)] * 2,
        name="shortconv_sample")
    return yb, t0, t1


def _merge_kernel(h_ref, ya_ref, yb_ref, wga_ref, wgb_ref, woa_ref, wob_ref, m_ref, *, tm, slab, side):
    _convert(side)
    for r in _slabs(tm, slab):
        h = h_ref[r, :]
        ga = jax.nn.sigmoid(_dot(h, wga_ref[...]))
        gb = jax.nn.sigmoid(_dot(h, wgb_ref[...]))
        m = ga * _dot(ya_ref[r, :], woa_ref[...]) + gb * _dot(yb_ref[r, :], wob_ref[...])
        m_ref[r, :] = m.astype(_BF16)


def _merge_call(h, ya, yb, w_gate, w_out_a, w_out_b, *, tm, tn, slab, side=(), name):
    m, d = h.shape
    n_blk = d // tn
    row = pl.BlockSpec((tm, d), lambda i, j: (i, 0))
    wspec = lambda off: pl.BlockSpec((d, tn), lambda i, j: (0, off + j))
    (mg,), copies = _run(
        functools.partial(_merge_kernel, tm=tm, slab=slab),
        grid=(m // tm, n_blk),
        in_specs=[row, row, row, wspec(0), wspec(n_blk), wspec(0), wspec(0)],
        args=(h, ya, yb, w_gate, w_gate, w_out_a, w_out_b),
        out_specs=[pl.BlockSpec((tm, tn), lambda i, j: (i, j))],
        out_shape=[jax.ShapeDtypeStruct((m, d), _BF16)],
        side=side, name=name)
    return mg, copies


def _s_proj_kernel(m_ref, wo_ref, x_hbm, gt_ref, gpost_ref, sh_ref, sc_ref, gpre_ref,
                   x1_ref, h2_ref, y_scr, x_buf, in_sem, *, tm, tn, n_blk, slab, side):
    j = pl.program_id(0)
    n_slab = tm // slab
    x_copy = lambda t: pltpu.make_async_copy(x_hbm.at[:, t, :], x_buf.at[t], in_sem.at[t])

    @pl.when(j == 0)
    def _():
        for t in range(n_slab):
            x_copy(t).start()

    y_scr[j] = _dot(m_ref[...], wo_ref[...])

    @pl.when(j == n_blk - 1)
    def _():
        for t, r in enumerate(_slabs(tm, slab)):
            x_copy(t).wait()
            y = jnp.concatenate([y_scr[k, r, :] for k in range(n_blk)], axis=-1)
            x1, h2 = _post_mix(y, x_buf[t], gt_ref[...], gpost_ref[...],
                               sh_ref[...], sc_ref[...], gpre_ref[...])
            x1_ref[r, :] = x1
            h2_ref[r, :] = h2


def _s_proj_call(mg, w_o, x, mod, gpost, gpre, *, tn):
    tm, d = mg.shape
    slab = x.shape[0]
    n_blk = d // tn
    row = pl.BlockSpec((tm, d), lambda j: (0, 0))
    vec = pl.BlockSpec((1, d), lambda j: (0, 0))
    mspec = lambda k: pl.BlockSpec((slab, d), lambda j: (0, k))
    (x1, h2), _ = _run(
        functools.partial(_s_proj_kernel, tm=tm, tn=tn, n_blk=n_blk, slab=slab),
        grid=(n_blk,),
        in_specs=[row, pl.BlockSpec((d, tn), lambda j: (0, j)),
                  pl.BlockSpec(memory_space=pl.ANY), mspec(2), vec, mspec(3), mspec(4), vec],
        args=(mg, w_o, x, mod, gpost, mod, mod, gpre),
        out_specs=[row, row],
        out_shape=[jax.ShapeDtypeStruct((tm, d), _F32), jax.ShapeDtypeStruct((tm, d), _BF16)],
        scratch=[pltpu.VMEM((n_blk, tm, tn), _F32), pltpu.VMEM((tm // slab, slab, d), _F32),
                 pltpu.SemaphoreType.DMA((tm // slab,))],
        name="out_proj_sample")
    return x1, h2


def _s_ffn_kernel(h_ref, wa_ref, wb_ref, cw_ref, wd_ref, x1_ref, gt_ref, gpost_ref, st0_ref, st1_ref,
                  out_hbm, t0_ref, t1_ref, acc_scr, out_sem, *, tm, n_blk, slab, side):
    j = pl.program_id(0)
    n_slab = tm // slab
    out_copy = lambda t: pltpu.make_async_copy(acc_scr.at[t], out_hbm.at[:, t, :], out_sem.at[t])

    @pl.when(j == 0)
    def _():
        acc_scr[...] = jnp.zeros(acc_scr.shape, _F32)

    prev = [st0_ref[...], st1_ref[...]]
    gs = []
    for r in _slabs(tm, 2 * slab):
        h = h_ref[r, :]
        ac, prev = _causal_conv_slabs(_dot(h, wa_ref[...]), prev, cw_ref, slab)
        gs.append((jax.nn.gelu(ac) * _dot(h, wb_ref[...])).astype(_BF16))
    for c, g in enumerate(gs):
        f = _dot(g, wd_ref[...])
        acc_scr[2 * c] += f[:slab, :]
        acc_scr[2 * c + 1] += f[slab:, :]
    t0_ref[...] = prev[0]
    t1_ref[...] = prev[1]

    @pl.when(j == n_blk - 1)
    def _():
        for t, r in enumerate(_slabs(tm, slab)):
            acc_scr[t] = x1_ref[r, :] + gt_ref[...] * _rms(acc_scr[t], gpost_ref[...])
            out_copy(t).start()
        for t in range(n_slab):
            out_copy(t).wait()


def _s_ffn_call(h2, w_a, w_b, cw, w_down, x1, mod, gpost, state, *, tn):
    tm, d = x1.shape
    slab = state.shape[0]
    f = cw.shape[1]
    n_blk = f // tn
    out_shape = (slab, tm // slab, d)
    row = pl.BlockSpec((tm, d), lambda j: (0, 0))
    wspec = pl.BlockSpec((d, tn), lambda j: (0, j))
    sspec = lambda k: pl.BlockSpec((slab, tn), lambda j: (0, k * n_blk + j))
    tspec = pl.BlockSpec((slab, tn), lambda j: (0, j))
    (out, t0, t1), _ = _run(
        functools.partial(_s_ffn_kernel, tm=tm, n_blk=n_blk, slab=slab),
        grid=(n_blk,),
        in_specs=[row, wspec, wspec, pl.BlockSpec((CONV_K, tn), lambda j: (0, j)),
                  pl.BlockSpec((tn, d), lambda j: (j, 0)), row,
                  pl.BlockSpec((slab, d), lambda j: (0, 5)), pl.BlockSpec((1, d), lambda j: (0, 0)),
                  sspec(0), sspec(1)],
        args=(h2, w_a, w_b, cw, w_down, x1, mod, gpost, state, state),
        out_specs=[pl.BlockSpec(memory_space=pl.ANY), tspec, tspec],
        out_shape=[jax.ShapeDtypeStruct(out_shape, _F32)]
        + [jax.ShapeDtypeStruct((slab, f), _F32)] * 2,
        scratch=[pltpu.VMEM((tm // slab, slab, d), _F32), pltpu.SemaphoreType.DMA((tm // slab,))],
        name="convffn_sample")
    return out, t0, t1


def _p_gmlp_kernel(x_hbm, sh_ref, sc_ref, gpre_ref, wv_ref, wu_ref, gv_ref, ws_ref, bias_ref,
                   ya_ref, h_ref, x_buf, x_sem, v_scr, wt_scr, *, tm, tn, n_blk, slab, side):
    i = pl.program_id(0)
    j = pl.program_id(1)
    d = n_blk * tn
    slabs = _slabs(tm, slab)

    def x_copy(tile):
        return pltpu.make_async_copy(x_hbm.at[pl.ds(pl.multiple_of(tile * tm, tm), tm), :],
                                     x_buf, x_sem)

    @pl.when((i == 0) & (j == 0))
    def _():
        x_copy(0).start()
        tril = (lax.broadcasted_iota(jnp.int32, (CHUNK, CHUNK), 0)
                >= lax.broadcasted_iota(jnp.int32, (CHUNK, CHUNK), 1))
        for g in range(d // GROUP):
            wt_scr[g] = jnp.where(tril, ws_ref[g], 0.0).astype(_BF16)

    @pl.when(j == 0)
    def _():
        x_copy(i).wait()
        _convert(side)
        g = gpre_ref[...] * (1.0 + sc_ref[...])
        for r in slabs:
            h = (_rms(x_buf[r, :], g) + sh_ref[...]).astype(_BF16)
            h_ref[r, :] = h
            v_scr[0, r, :] = _dot(h, wv_ref[...])

    @pl.when((j == 0) & (i + 1 < pl.num_programs(0)))
    def _():
        x_copy(i + 1).start()

    @pl.when((j > 0) & (j < n_blk))
    def _():
        _convert(side)
        for r in slabs:
            v_scr[j, r, :] = _dot(h_ref[r, :], wv_ref[...])

    def _gate():
        gpb = tn // GROUP
        for c in range(tm // CHUNK):
            r = slice(c * CHUNK, (c + 1) * CHUNK)
            ss = 0.0
            for k in range(n_blk):
                vk = v_scr[k, r, :]
                ss = ss + jnp.sum(vk * vk, axis=-1, keepdims=True)
            rs = lax.rsqrt(ss * (1.0 / d) + EPS)
            for k in range(n_blk):
                vb = (v_scr[k, r, :] * rs * gv_ref[:, k * tn:(k + 1) * tn]).astype(_BF16)
                for gg in range(gpb):
                    g = k * gpb + gg
                    lanes = slice(gg * GROUP, (gg + 1) * GROUP)
                    v_scr[k, r, lanes] = (_dot(wt_scr[g], vb[:, lanes])
                                          + bias_ref[:, g * GROUP:(g + 1) * GROUP])

    @pl.when(j == n_blk)
    def _():
        _convert(side)
        _gate()
        for r in slabs:
            ya_ref[r, :] = (_dot(h_ref[r, :], wu_ref[...]) * v_scr[0, r, :]).astype(_BF16)

    @pl.when(j > n_blk)
    def _():
        _convert(side)
        for r in slabs:
            ya_ref[r, :] = (_dot(h_ref[r, :], wu_ref[...]) * v_scr[j - n_blk, r, :]).astype(_BF16)


def _p_gmlp_call(x, mod, gpre, w_v, w_u, gv, ws, bias, *, tm, tn, slab, tiles_per_seq, mod_row0,
                 side):
    m, d = x.shape
    n_blk = d // tn
    full = lambda a: pl.BlockSpec(a.shape, lambda i, j: (0,) * a.ndim)
    once = lambda a: pl.BlockSpec(a.shape, lambda i, j: (0,) * a.ndim, pipeline_mode=pl.Buffered(1))
    mspec = lambda k: pl.BlockSpec((None, 1, d), lambda i, j: (mod_row0 + i // tiles_per_seq, 0, k))
    u_map = lambda i, j: (i, jnp.maximum(j - n_blk, 0))
    (ya, h), copies = _run(
        functools.partial(_p_gmlp_kernel, tm=tm, tn=tn, n_blk=n_blk, slab=slab),
        grid=(m // tm, 2 * n_blk),
        in_specs=[pl.BlockSpec(memory_space=pl.ANY), mspec(0), mspec(1), full(gpre),
                  pl.BlockSpec((d, tn), lambda i, j: (0, jnp.minimum(j, n_blk - 1))),
                  pl.BlockSpec((d, tn), lambda i, j: (0, jnp.maximum(j - n_blk, 0))),
                  full(gv), once(ws), once(bias)],
        args=(x, mod, mod, gpre, w_v, w_u, gv, ws, bias),
        out_specs=[pl.BlockSpec((tm, tn), u_map), pl.BlockSpec((tm, d), lambda i, j: (i, 0))],
        out_shape=[jax.ShapeDtypeStruct((m, d), _BF16), jax.ShapeDtypeStruct((m, d), _BF16)],
        scratch=[pltpu.VMEM((tm, d), _F32), pltpu.SemaphoreType.DMA(()),
                 pltpu.VMEM((n_blk, tm, tn), _F32), pltpu.VMEM((d // GROUP, CHUNK, CHUNK), _BF16)],
        side=side, name="gmlp_prompt", vmem_limit_bytes=VMEM_LIMIT_BYTES_WIDE)
    return ya, h, copies


def _p_shortconv_kernel(h_ref, wbg_ref, wcg_ref, wxb_ref, cw_ref, yb_ref, tail_ref, carry_scr,
                        *, tm, tiles_per_seq, slab, side):
    i = pl.program_id(0)
    j = pl.program_id(1)

    @pl.when(i % tiles_per_seq == 0)
    def _():
        carry_scr[j] = jnp.zeros(carry_scr.shape[1:], _F32)

    _convert(side)
    prev = carry_scr[j]
    cw = cw_ref.at[j]
    for r in _slabs(tm, slab):
        h = h_ref[r, :]
        bg = _dot(h, wbg_ref[...])
        p = _dot(h, wcg_ref[...]) * _dot(h, wxb_ref[...])
        yb_ref[r, :] = (bg * _causal_conv_rows(p, prev, cw)).astype(_BF16)
        prev = p[slab - SUBLANES:, :]
    carry_scr[j] = prev
    tail_ref[j] = prev


def _p_shortconv_call(h, w_bcx, cw, *, tm, tn, tiles_per_seq, slab, side):
    m, d = h.shape
    w = cw.shape[1]
    n_blk = w // tn
    wspec = lambda off: pl.BlockSpec((d, tn), lambda i, j: (0, off + j))
    (yb, tail), copies = _run(
        functools.partial(_p_shortconv_kernel, tm=tm, tiles_per_seq=tiles_per_seq, slab=slab),
        grid=(m // tm, n_blk),
        in_specs=[pl.BlockSpec((tm, d), lambda i, j: (i, 0)),
                  wspec(0), wspec(n_blk), wspec(2 * n_blk),
                  pl.BlockSpec((n_blk, CONV_K, tn), lambda i, j: (0, 0, 0))],
        args=(h, w_bcx, w_bcx, w_bcx, _col_blocks(cw, tn)),
        out_specs=[pl.BlockSpec((tm, tn), lambda i, j: (i, j)),
                   pl.BlockSpec((None, n_blk, SUBLANES, tn), lambda i, j: (i, 0, 0, 0))],
        out_shape=[jax.ShapeDtypeStruct((m, w), _BF16),
                   jax.ShapeDtypeStruct((m // tm, n_blk, SUBLANES, tn), _F32)],
        scratch=[pltpu.VMEM((n_blk, SUBLANES, tn), _F32)],
        side=side, name="shortconv_prompt")
    return yb, tail, copies


def _p_proj_kernel(m_ref, wo_ref, x_ref, gt_ref, gpost_ref, sh_ref, sc_ref, gpre_ref,
                   x1_ref, h2_ref, *, tm, slab, side):
    g1 = gt_ref[...] * gpost_ref[...]
    g2 = gpre_ref[...] * (1.0 + sc_ref[...])
    for r in _slabs(tm, slab):
        y = _dot(m_ref[r, :], wo_ref[...])
        x1 = x_ref[r, :] + _rms(y, g1)
        x1_ref[r, :] = x1
        h2_ref[r, :] = (_rms(x1, g2) + sh_ref[...]).astype(_BF16)


def _p_proj_call(mg, w_o, x, mod, gpost, gpre, *, tm, slab, tiles_per_seq, mod_row0):
    m, d = x.shape
    row = pl.BlockSpec((tm, d), lambda i: (i, 0))
    vec = pl.BlockSpec((1, d), lambda i: (0, 0))
    mspec = lambda k: pl.BlockSpec((None, 1, d), lambda i: (mod_row0 + i // tiles_per_seq, 0, k))
    (x1, h2), _ = _run(
        functools.partial(_p_proj_kernel, tm=tm, slab=slab),
        grid=(m // tm,),
        in_specs=[row, pl.BlockSpec((d, d), lambda i: (0, 0)), row, mspec(2), vec, mspec(3),
                  mspec(4), vec],
        args=(mg, w_o, x, mod, gpost, mod, mod, gpre),
        out_specs=[row, row],
        out_shape=[jax.ShapeDtypeStruct((m, d), _F32), jax.ShapeDtypeStruct((m, d), _BF16)],
        name="out_proj_prompt")
    return x1, h2


def _p_ffn_kernel(h_ref, wa_ref, wb_ref, cw_ref, wd_ref, x1_hbm, gt_ref, gpost_ref,
                  out_ref, tail_ref, x1_buf, x1_sem, carry_scr, *, tm, n_blk, tiles_per_seq, slab,
                  side):
    i = pl.program_id(0)
    j = pl.program_id(1)
    x1_copy = pltpu.make_async_copy(x1_hbm.at[pl.ds(pl.multiple_of(i * tm, tm), tm), :],
                                    x1_buf, x1_sem)

    @pl.when(i % tiles_per_seq == 0)
    def _():
        carry_scr[j] = jnp.zeros(carry_scr.shape[1:], _F32)

    def step(first, last):
        prev = carry_scr[j]
        cw = cw_ref.at[j]
        slabs = _slabs(tm, slab)
        gs = []
        for r in slabs:
            h = h_ref[r, :]
            a = _dot(h, wa_ref[...])
            b = _dot(h, wb_ref[...])
            gs.append((jax.nn.gelu(_causal_conv_rows(a, prev, cw)) * b).astype(_BF16))
            prev = a[slab - SUBLANES:, :]
        for r, g in zip(slabs, gs):
            f = _dot(g, wd_ref[...])
            acc = f if first else out_ref[r, :] + f
            if last:
                acc = x1_buf[r, :] + _rms(acc, gt_ref[...] * gpost_ref[...])
            out_ref[r, :] = acc
        carry_scr[j] = prev
        tail_ref[j] = prev

    @pl.when(j == 0)
    def _():
        x1_copy.start()
        step(True, False)

    @pl.when((j > 0) & (j < n_blk - 1))
    def _():
        step(False, False)

    @pl.when(j == n_blk - 1)
    def _():
        x1_copy.wait()
        step(False, True)


def _p_ffn_call(h2, w_a, w_b, cw, w_down, x1, mod, gpost, *, tm, tn, tiles_per_seq, slab, mod_row0):
    m, d = x1.shape
    f = cw.shape[1]
    n_blk = f // tn
    row = pl.BlockSpec((tm, d), lambda i, j: (i, 0))
    wspec = pl.BlockSpec((d, tn), lambda i, j: (0, j))
    (out, tail), _ = _run(
        functools.partial(_p_ffn_kernel, tm=tm, n_blk=n_blk, tiles_per_seq=tiles_per_seq, slab=slab),
        grid=(m // tm, n_blk),
        in_specs=[row, wspec, wspec,
                  pl.BlockSpec((n_blk, CONV_K, tn), lambda i, j: (0, 0, 0)),
                  pl.BlockSpec((tn, d), lambda i, j: (j, 0)),
                  pl.BlockSpec(memory_space=pl.ANY),
                  pl.BlockSpec((None, 1, d), lambda i, j: (mod_row0 + i // tiles_per_seq, 0, 5)),
                  pl.BlockSpec((1, d), lambda i, j: (0, 0))],
        args=(h2, w_a, w_b, _col_blocks(cw, tn), w_down, x1, mod, gpost),
        out_specs=[row, pl.BlockSpec((None, n_blk, SUBLANES, tn), lambda i, j: (i, 0, 0, 0))],
        out_shape=[jax.ShapeDtypeStruct((m, d), _F32),
                   jax.ShapeDtypeStruct((m // tm, n_blk, SUBLANES, tn), _F32)],
        scratch=[pltpu.VMEM((tm, d), _F32), pltpu.SemaphoreType.DMA(()),
                 pltpu.VMEM((n_blk, SUBLANES, tn), _F32)],
        name="convffn_prompt")
    return out, tail


def _cols(src, first_col, n_cols, n_blocks):
    block = n_cols // n_blocks
    assert block * n_blocks == n_cols and block % 128 == 0 and first_col % block == 0
    return _Side(src, 1, first_col // block, block, n_blocks)


def _layer(xs, xp, mod, st_b, st_f, p, *, seq_len, mod_row0):
    d = p["g_v"].shape[1]
    wb = p["conv_b_w"].shape[1]
    f = p["conv_f_w"].shape[1]
    tn_s = SAMPLE_COL_TILE
    mod_rows = mod.reshape(mod.shape[0], 1, N_MOD * d)

    def tiles(name):
        tm, tn, slab = PROMPT_TILES[name]
        return dict(tm=tm, tn=tn, slab=slab, tiles_per_seq=seq_len // tm)

    def n_steps(name):
        tm, tn, _ = PROMPT_TILES[name]
        cols = {"gmlp": 2 * d, "shortconv": wb, "merge": d, "ffn": f}[name]
        return xp.shape[0] // tm * (cols // tn)

    def without(kw, *names):
        return {k: v for k, v in kw.items() if k not in names}

    def blocks(n_cols, budget):
        return max(n for n in range(1, budget + 1) if n_cols % (128 * n) == 0)

    w_in, w_up = p["w_in"], p["w_up"]

    ya_s, h1_s, vn, w_v, w_u = _s_gmlp_call(xs, mod, p["g_pre_mix"], w_in, p["g_v"], p["wvec"],
                                            p["bvec"], tn=tn_s)
    n = n_steps("gmlp")
    ya_p, h1_p, (w_bcx,) = _p_gmlp_call(
        xp, mod_rows, p["g_pre_mix"], w_v, w_u, p["g_v"], p["w_s"], p["bias"], mod_row0=mod_row0,
        side=(_cols(w_in, 2 * d, 3 * wb, blocks(3 * wb, n)),),
        **tiles("gmlp"))

    yb_s, tb0, tb1 = _s_shortconv_call(h1_s, w_bcx, p["conv_b_w"], st_b, tn=tn_s)
    n = n_steps("shortconv")
    yb_p, tail_b, (w_gate, w_oa, w_ob, w_o, w_a, w_b) = _p_shortconv_call(
        h1_p, w_bcx, p["conv_b_w"],
        side=(_cols(w_in, 2 * d + 3 * wb, 2 * d, blocks(2 * d, n)),
              _cols(p["w_out_a"], 0, d, blocks(d, n)), _cols(p["w_out_b"], 0, d, blocks(d, n)),
              _cols(p["w_o"], 0, d, blocks(d, n)),
              _cols(w_up, 0, f, blocks(f, n)), _cols(w_up, f, f, blocks(f, n))),
        **tiles("shortconv"))

    tm_s = h1_s.shape[0]
    mg_s, _ = _merge_call(h1_s, ya_s, yb_s, w_gate, w_oa, w_ob, tm=tm_s, tn=tn_s, slab=tm_s // 2,
                          name="gated_merge_sample")
    n_row_blocks = max(k for k in range(1, n_steps("merge") + 1)
                       if f % k == 0 and (f // k) % (2 * SUBLANES) == 0)
    mg_p, (w_d,) = _merge_call(h1_p, ya_p, yb_p, w_gate, w_oa, w_ob, name="gated_merge_prompt",
                               side=(_Side(p["w_down"], 0, 0, f // n_row_blocks, n_row_blocks),),
                               **without(tiles("merge"), "tiles_per_seq"))

    x1_s, h2_s = _s_proj_call(mg_s, w_o, xs, mod, p["g_post_mix"], p["g_pre_ffn"], tn=tn_s)
    x1_p, h2_p = _p_proj_call(mg_p, w_o, xp, mod_rows, p["g_post_mix"], p["g_pre_ffn"],
                              mod_row0=mod_row0, **without(tiles("proj"), "tn"))

    out_s, tf0, tf1 = _s_ffn_call(h2_s, w_a, w_b, p["conv_f_w"], w_d, x1_s, mod, p["g_post_ffn"],
                                  st_f, tn=tn_s)
    out_p, tail_f = _p_ffn_call(h2_p, w_a, w_b, p["conv_f_w"], w_d, x1_p, mod_rows, p["g_post_ffn"],
                                mod_row0=mod_row0, **tiles("ffn"))
    return out_s, out_p, (tb0, tb1), (tf0, tf1), vn, tail_b, tail_f


def kernel(x_prompt, x_sample, c_prompt, c_sample, state_conv_b, state_conv_ffn, w_ada, b_ada, g_pre_mix, g_post_mix, w_in, g_v, w_s, b_s, conv_b_w, w_out_a, w_out_b, w_o, g_pre_ffn, g_post_ffn, w_up, conv_f_w, w_down):
    depth = w_in.shape[0]
    bp, seq, d = x_prompt.shape
    bs, tdec, _ = x_sample.shape
    n_groups = w_s.shape[1]
    assert bs == CHUNK and tdec <= CHUNK
    assert all(seq % tm == 0 for tm, _, _ in PROMPT_TILES.values())

    xp = x_prompt.reshape(bp * seq, d)
    xs = x_sample
    pad = (-(bp + bs)) % SUBLANES
    c_all = jnp.concatenate([c_sample, c_prompt, jnp.zeros((pad, d), _F32)], axis=0)

    pb, sb, pf, sf, sv = [], [], [], [], []
    for l in range(depth):
        mod, _ = _mod_call(c_all, w_ada[l], b_ada[l][None, :], side=())
        vec = lambda a: a[l][None, :]
        bias_full = jnp.repeat(jnp.transpose(b_s[l]), GROUP, axis=1)
        wvec = jnp.repeat(
            jnp.transpose(w_s[l][:, :tdec, :tdec], (1, 2, 0)).reshape(tdec * tdec, n_groups),
            GROUP, axis=1)
        p = {
            "w_in": w_in[l], "w_out_a": w_out_a[l], "w_out_b": w_out_b[l],
            "w_o": w_o[l], "w_up": w_up[l], "w_down": w_down[l],
            "g_pre_mix": vec(g_pre_mix), "g_post_mix": vec(g_post_mix), "g_v": vec(g_v),
            "g_pre_ffn": vec(g_pre_ffn), "g_post_ffn": vec(g_post_ffn),
            "conv_b_w": conv_b_w[l], "conv_f_w": conv_f_w[l],
            "w_s": w_s[l], "bias": bias_full, "wvec": wvec, "bvec": bias_full[:tdec],
        }
        st_b = state_conv_b[l].reshape(bs, -1)
        st_f = state_conv_ffn[l].reshape(bs, -1)
        xs, xp, sbt, sft, vn, tb, tf = _layer(xs, xp, mod, st_b, st_f, p, seq_len=seq, mod_row0=bs)

        def prompt_tail(t):
            n_tiles, n_blk, _, tn = t.shape
            t = t.reshape(bp, n_tiles // bp, n_blk, SUBLANES, tn)[:, -1, :, SUBLANES - (CONV_K - 1):, :]
            return jnp.transpose(t, (0, 2, 1, 3)).reshape(bp, CONV_K - 1, n_blk * tn)

        pb.append(prompt_tail(tb))
        pf.append(prompt_tail(tf))
        sb.append(jnp.stack(sbt, axis=1))
        sf.append(jnp.stack(sft, axis=1))
        sv.append(vn)

    y_prompt = xp.reshape(bp, seq, d)
    return (y_prompt, xs, jnp.stack(pb), jnp.stack(sb), jnp.stack(pf), jnp.stack(sf),
            jnp.stack(sv))
```

```python
import functools
from typing import NamedTuple

import jax
import jax.numpy as jnp
from jax import lax
from jax.experimental import pallas as pl
from jax.experimental.pallas import tpu as pltpu

EPS = 1e-6
CHUNK = 128
GROUP = 128
CONV_K = 3
N_MOD = 6
SUBLANES = 8
VMEM_LIMIT_BYTES = 56 * 1024 * 1024
VMEM_LIMIT_BYTES_WIDE = 60 * 1024 * 1024
PROMPT_TILES = {
    "gmlp": (1024, 1024, 256),
    "shortconv": (1024, 512, 256),
    "merge": (512, 1024, 256),
    "proj": (512, None, 256),
    "ffn": (1024, 512, 256),
}
SAMPLE_COL_TILE = 512
ADALN_COL_TILE = 1024

_BF16 = jnp.bfloat16
_F32 = jnp.float32


def _dot(a, b):
    return jnp.dot(a, b, preferred_element_type=_F32)


def _rms(xf, g):
    ms = jnp.mean(xf * xf, axis=-1, keepdims=True)
    return xf * lax.rsqrt(ms + EPS) * g


def _causal_conv_rows(p, prev, cw_ref):
    r1 = pltpu.roll(p, 1, 0)
    r2 = pltpu.roll(p, 2, 0)
    row = lax.broadcasted_iota(jnp.int32, (SUBLANES, 1), 0)
    head1 = jnp.where(row == 0, prev[7:8, :], r1[:SUBLANES, :])
    head2 = jnp.where(row == 0, prev[6:7, :], jnp.where(row == 1, prev[7:8, :], r2[:SUBLANES, :]))
    m1 = jnp.concatenate([head1, r1[SUBLANES:, :]], axis=0)
    m2 = jnp.concatenate([head2, r2[SUBLANES:, :]], axis=0)
    return cw_ref[0:1, :] * m2 + cw_ref[1:2, :] * m1 + cw_ref[2:3, :] * p


def _causal_conv_slabs(x, prev, cw_ref, slab):
    w0, w1, w2 = cw_ref[0:1, :], cw_ref[1:2, :], cw_ref[2:3, :]
    seq = list(prev) + [x[t * slab:(t + 1) * slab, :] for t in range(x.shape[0] // slab)]
    y = [w0 * seq[t] + w1 * seq[t + 1] + w2 * seq[t + 2] for t in range(len(seq) - 2)]
    return jnp.concatenate(y, axis=0), seq[-2:]


def _slabs(tm, slab):
    return [slice(s * slab, (s + 1) * slab) for s in range(tm // slab)]


def _col_blocks(a, tn):
    r, c = a.shape
    return jnp.transpose(a.reshape(r, c // tn, tn), (1, 0, 2))


def _post_mix(y, x, gt, gpost, sh, sc, gpre):
    x1 = x + gt * _rms(y, gpost)
    h2 = _rms(x1, gpre) * (1.0 + sc) + sh
    return x1, h2.astype(_BF16)


class _Side(NamedTuple):
    src: jax.Array
    axis: int
    start: int
    block: int
    n_blocks: int


def _convert(side):
    for src_ref, dst_ref in side:
        dst_ref[...] = src_ref[...].astype(_BF16)


def _run(body, *, grid, in_specs, args, out_specs, out_shape, scratch=(), side=(), name,
         vmem_limit_bytes=VMEM_LIMIT_BYTES):
    n_in, n_out, n_side = len(args), len(out_shape), len(side)
    n_steps = functools.reduce(lambda a, b: a * b, grid)
    step_of = (lambda i: i) if len(grid) == 1 else (lambda i, j: i * grid[1] + j)
    side_in, side_out, side_shape = [], [], []
    for s in side:
        assert s.n_blocks <= n_steps, (name, s.n_blocks, n_steps)
        other = s.src.shape[1 - s.axis]
        pos = lambda *ids, s=s: jnp.minimum(step_of(*ids), s.n_blocks - 1)
        if s.axis == 1:
            blk, full = (other, s.block), (other, s.block * s.n_blocks)
            side_in.append(pl.BlockSpec(blk, lambda *ids, s=s, pos=pos: (0, s.start + pos(*ids))))
            side_out.append(pl.BlockSpec(blk, lambda *ids, pos=pos: (0, pos(*ids))))
        else:
            blk, full = (s.block, other), (s.block * s.n_blocks, other)
            side_in.append(pl.BlockSpec(blk, lambda *ids, s=s, pos=pos: (s.start + pos(*ids), 0)))
            side_out.append(pl.BlockSpec(blk, lambda *ids, pos=pos: (pos(*ids), 0)))
        side_shape.append(jax.ShapeDtypeStruct(full, _BF16))

    def kern(*refs):
        o0 = n_in + n_side
        s0 = o0 + n_out + n_side
        body(*refs[:n_in], *refs[o0:o0 + n_out], *refs[s0:],
             side=tuple(zip(refs[n_in:o0], refs[o0 + n_out:s0])))

    res = pl.pallas_call(
        kern, grid=grid,
        in_specs=list(in_specs) + side_in, out_specs=list(out_specs) + side_out,
        out_shape=list(out_shape) + side_shape, scratch_shapes=list(scratch),
        compiler_params=pltpu.CompilerParams(dimension_semantics=("arbitrary",) * len(grid),
                                             vmem_limit_bytes=vmem_limit_bytes),
        name=name,
    )(*args, *[s.src for s in side])
    return res[:n_out], res[n_out:]


def _mod_kernel(c_ref, w_ref, b_ref, o_ref, *, side):
    _convert(side)
    c = c_ref[...]
    a = (c * jax.nn.sigmoid(c)).astype(_BF16)
    o_ref[...] = _dot(a, w_ref[...].astype(_BF16)) + b_ref[...]


def _mod_call(c_all, w_ada, b_ada, *, side):
    rows, d = c_all.shape
    n = w_ada.shape[1]
    tn = ADALN_COL_TILE
    (mod,), copies = _run(
        _mod_kernel, grid=(n // tn,),
        in_specs=[pl.BlockSpec((rows, d), lambda j: (0, 0)),
                  pl.BlockSpec((d, tn), lambda j: (0, j)),
                  pl.BlockSpec((1, tn), lambda j: (0, j))],
        args=(c_all, w_ada, b_ada),
        out_specs=[pl.BlockSpec((rows, tn), lambda j: (0, j))],
        out_shape=[jax.ShapeDtypeStruct((rows, n), _F32)],
        side=side, name="adaln_mod")
    return mod, copies


def _s_gmlp_kernel(x_hbm, sh_ref, sc_ref, gpre_ref, wv_ref, wu_ref, gv_ref, ws_ref, bias_ref,
                   ya_ref, h_ref, vn_hbm, wvb_ref, wub_ref, v_scr, x_buf, vn_buf, in_sem, out_sem,
                   *, tm, tn, n_blk, slab, side):
    j = pl.program_id(0)
    d = n_blk * tn
    slabs = _slabs(tm, slab)
    n_slab = len(slabs)
    x_copy = lambda t: pltpu.make_async_copy(x_hbm.at[:, t, :], x_buf.at[t], in_sem.at[t])
    vn_copy = lambda t: pltpu.make_async_copy(vn_buf.at[t], vn_hbm.at[:, t, :], out_sem.at[t])

    @pl.when(j == 0)
    def _():
        for t in range(n_slab):
            x_copy(t).start()
        for t, r in enumerate(slabs):
            x_copy(t).wait()
            h_ref[r, :] = (_rms(x_buf[t], gpre_ref[...]) * (1.0 + sc_ref[...])
                           + sh_ref[...]).astype(_BF16)

    @pl.when(j < n_blk)
    def _():
        wv = wv_ref[...].astype(_BF16)
        wvb_ref[...] = wv
        v_scr[j] = _dot(h_ref[...], wv)

    @pl.when(j == n_blk)
    def _():
        for t, r in enumerate(slabs):
            ss = 0.0
            for k in range(n_blk):
                vk = v_scr[k, r, :]
                ss = ss + jnp.sum(vk * vk, axis=-1, keepdims=True)
            rs = lax.rsqrt(ss * (1.0 / d) + EPS)
            for k in range(n_blk):
                vn = v_scr[k, r, :] * rs * gv_ref[:, k * tn:(k + 1) * tn]
                v_scr[k, r, :] = vn
                vn_buf[t, :, k * tn:(k + 1) * tn] = vn
            vn_copy(t).start()
        for t in reversed(range(n_slab)):
            for k in range(n_blk):
                c = slice(k * tn, (k + 1) * tn)
                acc = ws_ref[t * n_slab:t * n_slab + 1, c] * v_scr[k, slabs[0], :]
                for s in range(1, t + 1):
                    acc = acc + ws_ref[t * n_slab + s:t * n_slab + s + 1, c] * v_scr[k, slabs[s], :]
                v_scr[k, slabs[t], :] = acc + bias_ref[t:t + 1, c]

    @pl.when(j >= n_blk)
    def _():
        wu = wu_ref[...].astype(_BF16)
        wub_ref[...] = wu
        ya_ref[...] = (_dot(h_ref[...], wu) * v_scr[j - n_blk]).astype(_BF16)

    @pl.when(j == 2 * n_blk - 1)
    def _():
        for t in range(n_slab):
            vn_copy(t).wait()


def _s_gmlp_call(x, mod, gpre, w_in, gv, wvec, bvec, *, tn):
    slab, n_slab, d = x.shape
    tm = n_slab * slab
    n_blk = d // tn
    anywhere = pl.BlockSpec(memory_space=pl.ANY)
    full = lambda a: pl.BlockSpec(a.shape, lambda j: (0,) * a.ndim)
    u_map = lambda j: (0, jnp.maximum(j - n_blk, 0))
    v_map = lambda j: (0, jnp.minimum(j, n_blk - 1))
    (ya, h, vn, w_v, w_u), _ = _run(
        functools.partial(_s_gmlp_kernel, tm=tm, tn=tn, n_blk=n_blk, slab=slab),
        grid=(2 * n_blk,),
        in_specs=[anywhere,
                  pl.BlockSpec((slab, d), lambda j: (0, 0)), pl.BlockSpec((slab, d), lambda j: (0, 1)),
                  full(gpre),
                  pl.BlockSpec((d, tn), lambda j: (0, n_blk + jnp.minimum(j, n_blk - 1))),
                  pl.BlockSpec((d, tn), u_map),
                  full(gv), full(wvec), full(bvec)],
        args=(x, mod, mod, gpre, w_in, w_in, gv, wvec, bvec),
        out_specs=[pl.BlockSpec((tm, tn), u_map), pl.BlockSpec((tm, d), lambda j: (0, 0)), anywhere,
                   pl.BlockSpec((d, tn), v_map), pl.BlockSpec((d, tn), u_map)],
        out_shape=[jax.ShapeDtypeStruct((tm, d), _BF16), jax.ShapeDtypeStruct((tm, d), _BF16),
                   jax.ShapeDtypeStruct(x.shape, _F32),
                   jax.ShapeDtypeStruct((d, d), _BF16), jax.ShapeDtypeStruct((d, d), _BF16)],
        scratch=[pltpu.VMEM((n_blk, tm, tn), _F32),
                 pltpu.VMEM((n_slab, slab, d), _F32), pltpu.VMEM((n_slab, slab, d), _F32),
                 pltpu.SemaphoreType.DMA((n_slab,)), pltpu.SemaphoreType.DMA((n_slab,))],
        name="gmlp_sample")
    return ya, h, vn, w_v, w_u


class _StateIO:
    def __init__(self, st_hbm, new_hbm, st_buf, new_buf, in_sem, out_sem, n_blk, tn):
        self.n_rows, self.n_blk = st_buf.shape[0], n_blk
        col = lambda c: pl.ds(c * tn, tn) if isinstance(c, int) else pl.ds(pl.multiple_of(c * tn, tn), tn)
        self.load = lambda k, c: pltpu.make_async_copy(st_hbm.at[:, k, col(c)], st_buf.at[k, c],
                                                       in_sem.at[k, c])
        self.store = lambda k, c: pltpu.make_async_copy(new_buf.at[k, c], new_hbm.at[:, k, col(c)],
                                                        out_sem.at[k, c])
        self.st_buf, self.new_buf = st_buf, new_buf

    def read(self, j):
        @pl.when(j == 0)
        def _():
            for k in range(self.n_rows):
                for c in range(self.n_blk):
                    self.load(k, c).start()
        for k in range(self.n_rows):
            self.load(k, j).wait()
        return [self.st_buf[k, j] for k in range(self.n_rows)]

    def write(self, j, rows):
        for k, row in enumerate(rows):
            self.new_buf[k, j] = row
            self.store(k, j).start()

        @pl.when(j == self.n_blk - 1)
        def _():
            for k in range(self.n_rows):
                for c in range(self.n_blk):
                    self.store(k, c).wait()


def _state_scratch(state, n_blk, tn):
    slab, n_rows, _ = state.shape
    buf = pltpu.VMEM((n_rows, n_blk, slab, tn), _F32)
    sem = pltpu.SemaphoreType.DMA((n_rows, n_blk))
    return [buf, buf, sem, sem]


def _s_shortconv_kernel(h_ref, wbg_ref, wcg_ref, wxb_ref, cw_ref, st_hbm, yb_ref, new_hbm,
                        st_buf, new_buf, in_sem, out_sem, *, slab, n_blk, tn, side):
    j = pl.program_id(0)
    io = _StateIO(st_hbm, new_hbm, st_buf, new_buf, in_sem, out_sem, n_blk, tn)
    prev = io.read(j)
    h = h_ref[...]
    bg = _dot(h, wbg_ref[...])
    p = _dot(h, wcg_ref[...]) * _dot(h, wxb_ref[...])
    cb, tail = _causal_conv_slabs(p, prev, cw_ref, slab)
    yb_ref[...] = (bg * cb).astype(_BF16)
    io.write(j, tail)


def _s_shortconv_call(h, w_bcx, cw, state, *, tn):
    tm, d = h.shape
    slab = state.shape[0]
    w = cw.shape[1]
    n_blk = w // tn
    wspec = lambda off: pl.BlockSpec((d, tn), lambda j: (0, off + j))
    anywhere = pl.BlockSpec(memory_space=pl.ANY)
    (yb, new_state), _ = _run(
        functools.partial(_s_shortconv_kernel, slab=slab, n_blk=n_blk, tn=tn),
        grid=(n_blk,),
        in_specs=[pl.BlockSpec((tm, d), lambda j: (0, 0)),
                  wspec(0), wspec(n_blk), wspec(2 * n_blk),
                  pl.BlockSpec((CONV_K, tn), lambda j: (0, j)), anywhere],
        args=(h, w_bcx, w_bcx, w_bcx, cw, state),
        out_specs=[pl.BlockSpec((tm, tn), lambda j: (0, j)), anywhere],
        out_shape=[jax.ShapeDtypeStruct((tm, w), _BF16), jax.ShapeDtypeStruct(state.shape, _F32)],
        scratch=_state_scratch(state, n_blk, tn),
        name="shortconv_sample")
    return yb, new_state


def _merge_kernel(h_ref, ya_ref, yb_ref, wga_ref, wgb_ref, woa_ref, wob_ref, m_ref, *, tm, slab, side):
    _convert(side)
    for r in _slabs(tm, slab):
        h = h_ref[r, :]
        ga = jax.nn.sigmoid(_dot(h, wga_ref[...]))
        gb = jax.nn.sigmoid(_dot(h, wgb_ref[...]))
        m = ga * _dot(ya_ref[r, :], woa_ref[...]) + gb * _dot(yb_ref[r, :], wob_ref[...])
        m_ref[r, :] = m.astype(_BF16)


def _merge_call(h, ya, yb, w_gate, w_out_a, w_out_b, *, tm, tn, slab, side=(), name):
    m, d = h.shape
    n_blk = d // tn
    row = pl.BlockSpec((tm, d), lambda i, j: (i, 0))
    wspec = lambda off: pl.BlockSpec((d, tn), lambda i, j: (0, off + j))
    (mg,), copies = _run(
        functools.partial(_merge_kernel, tm=tm, slab=slab),
        grid=(m // tm, n_blk),
        in_specs=[row, row, row, wspec(0), wspec(n_blk), wspec(0), wspec(0)],
        args=(h, ya, yb, w_gate, w_gate, w_out_a, w_out_b),
        out_specs=[pl.BlockSpec((tm, tn), lambda i, j: (i, j))],
        out_shape=[jax.ShapeDtypeStruct((m, d), _BF16)],
        side=side, name=name)
    return mg, copies


def _s_proj_kernel(m_ref, wo_ref, x_hbm, gt_ref, gpost_ref, sh_ref, sc_ref, gpre_ref,
                   x1_ref, h2_ref, y_scr, x_buf, in_sem, *, tm, tn, n_blk, slab, side):
    j = pl.program_id(0)
    n_slab = tm // slab
    x_copy = lambda t: pltpu.make_async_copy(x_hbm.at[:, t, :], x_buf.at[t], in_sem.at[t])

    @pl.when(j == 0)
    def _():
        for t in range(n_slab):
            x_copy(t).start()

    y_scr[j] = _dot(m_ref[...], wo_ref[...])

    @pl.when(j == n_blk - 1)
    def _():
        for t, r in enumerate(_slabs(tm, slab)):
            x_copy(t).wait()
            y = jnp.concatenate([y_scr[k, r, :] for k in range(n_blk)], axis=-1)
            x1, h2 = _post_mix(y, x_buf[t], gt_ref[...], gpost_ref[...],
                               sh_ref[...], sc_ref[...], gpre_ref[...])
            x1_ref[r, :] = x1
            h2_ref[r, :] = h2


def _s_proj_call(mg, w_o, x, mod, gpost, gpre, *, tn):
    tm, d = mg.shape
    slab = x.shape[0]
    n_blk = d // tn
    row = pl.BlockSpec((tm, d), lambda j: (0, 0))
    vec = pl.BlockSpec((1, d), lambda j: (0, 0))
    mspec = lambda k: pl.BlockSpec((slab, d), lambda j: (0, k))
    (x1, h2), _ = _run(
        functools.partial(_s_proj_kernel, tm=tm, tn=tn, n_blk=n_blk, slab=slab),
        grid=(n_blk,),
        in_specs=[row, pl.BlockSpec((d, tn), lambda j: (0, j)),
                  pl.BlockSpec(memory_space=pl.ANY), mspec(2), vec, mspec(3), mspec(4), vec],
        args=(mg, w_o, x, mod, gpost, mod, mod, gpre),
        out_specs=[row, row],
        out_shape=[jax.ShapeDtypeStruct((tm, d), _F32), jax.ShapeDtypeStruct((tm, d), _BF16)],
        scratch=[pltpu.VMEM((n_blk, tm, tn), _F32), pltpu.VMEM((tm // slab, slab, d), _F32),
                 pltpu.SemaphoreType.DMA((tm // slab,))],
        name="out_proj_sample")
    return x1, h2


def _s_ffn_kernel(h_ref, wa_ref, wb_ref, cw_ref, wd_ref, x1_ref, gt_ref, gpost_ref, st_hbm,
                  out_hbm, new_hbm, acc_scr, out_sem, st_buf, new_buf, st_in_sem, st_out_sem,
                  *, tm, tn, n_blk, slab, side):
    j = pl.program_id(0)
    n_slab = tm // slab
    out_copy = lambda t: pltpu.make_async_copy(acc_scr.at[t], out_hbm.at[:, t, :], out_sem.at[t])
    io = _StateIO(st_hbm, new_hbm, st_buf, new_buf, st_in_sem, st_out_sem, n_blk, tn)

    @pl.when(j == 0)
    def _():
        acc_scr[...] = jnp.zeros(acc_scr.shape, _F32)

    prev = io.read(j)
    gs = []
    for r in _slabs(tm, 2 * slab):
        h = h_ref[r, :]
        ac, prev = _causal_conv_slabs(_dot(h, wa_ref[...]), prev, cw_ref, slab)
        gs.append((jax.nn.gelu(ac) * _dot(h, wb_ref[...])).astype(_BF16))
    for c, g in enumerate(gs):
        f = _dot(g, wd_ref[...])
        acc_scr[2 * c] += f[:slab, :]
        acc_scr[2 * c + 1] += f[slab:, :]
    io.write(j, prev)

    @pl.when(j == n_blk - 1)
    def _():
        for t, r in enumerate(_slabs(tm, slab)):
            acc_scr[t] = x1_ref[r, :] + gt_ref[...] * _rms(acc_scr[t], gpost_ref[...])
            out_copy(t).start()
        for t in range(n_slab):
            out_copy(t).wait()


def _s_ffn_call(h2, w_a, w_b, cw, w_down, x1, mod, gpost, state, *, tn):
    tm, d = x1.shape
    slab = state.shape[0]
    f = cw.shape[1]
    n_blk = f // tn
    out_shape = (slab, tm // slab, d)
    row = pl.BlockSpec((tm, d), lambda j: (0, 0))
    wspec = pl.BlockSpec((d, tn), lambda j: (0, j))
    anywhere = pl.BlockSpec(memory_space=pl.ANY)
    (out, new_state), _ = _run(
        functools.partial(_s_ffn_kernel, tm=tm, tn=tn, n_blk=n_blk, slab=slab),
        grid=(n_blk,),
        in_specs=[row, wspec, wspec, pl.BlockSpec((CONV_K, tn), lambda j: (0, j)),
                  pl.BlockSpec((tn, d), lambda j: (j, 0)), row,
                  pl.BlockSpec((slab, d), lambda j: (0, 5)), pl.BlockSpec((1, d), lambda j: (0, 0)),
                  anywhere],
        args=(h2, w_a, w_b, cw, w_down, x1, mod, gpost, state),
        out_specs=[anywhere, anywhere],
        out_shape=[jax.ShapeDtypeStruct(out_shape, _F32), jax.ShapeDtypeStruct(state.shape, _F32)],
        scratch=[pltpu.VMEM((tm // slab, slab, d), _F32), pltpu.SemaphoreType.DMA((tm // slab,))]
        + _state_scratch(state, n_blk, tn),
        name="convffn_sample")
    return out, new_state


def _p_gmlp_kernel(x_hbm, sh_ref, sc_ref, gpre_ref, wv_ref, wu_ref, gv_ref, ws_ref, bias_ref,
                   ya_ref, h_ref, x_buf, x_sem, v_scr, wt_scr, *, tm, tn, n_blk, slab, side):
    i = pl.program_id(0)
    j = pl.program_id(1)
    d = n_blk * tn
    slabs = _slabs(tm, slab)

    def x_copy(tile):
        return pltpu.make_async_copy(x_hbm.at[pl.ds(pl.multiple_of(tile * tm, tm), tm), :],
                                     x_buf, x_sem)

    @pl.when((i == 0) & (j == 0))
    def _():
        x_copy(0).start()
        tril = (lax.broadcasted_iota(jnp.int32, (CHUNK, CHUNK), 0)
                >= lax.broadcasted_iota(jnp.int32, (CHUNK, CHUNK), 1))
        for g in range(d // GROUP):
            wt_scr[g] = jnp.where(tril, ws_ref[g], 0.0).astype(_BF16)

    @pl.when(j == 0)
    def _():
        x_copy(i).wait()
        _convert(side)
        g = gpre_ref[...] * (1.0 + sc_ref[...])
        for r in slabs:
            h = (_rms(x_buf[r, :], g) + sh_ref[...]).astype(_BF16)
            h_ref[r, :] = h
            v_scr[0, r, :] = _dot(h, wv_ref[...])

    @pl.when((j == 0) & (i + 1 < pl.num_programs(0)))
    def _():
        x_copy(i + 1).start()

    @pl.when((j > 0) & (j < n_blk))
    def _():
        _convert(side)
        for r in slabs:
            v_scr[j, r, :] = _dot(h_ref[r, :], wv_ref[...])

    def _gate():
        gpb = tn // GROUP
        for c in range(tm // CHUNK):
            r = slice(c * CHUNK, (c + 1) * CHUNK)
            ss = 0.0
            for k in range(n_blk):
                vk = v_scr[k, r, :]
                ss = ss + jnp.sum(vk * vk, axis=-1, keepdims=True)
            rs = lax.rsqrt(ss * (1.0 / d) + EPS)
            for k in range(n_blk):
                vb = (v_scr[k, r, :] * rs * gv_ref[:, k * tn:(k + 1) * tn]).astype(_BF16)
                for gg in range(gpb):
                    g = k * gpb + gg
                    lanes = slice(gg * GROUP, (gg + 1) * GROUP)
                    v_scr[k, r, lanes] = (_dot(wt_scr[g], vb[:, lanes])
                                          + bias_ref[:, g * GROUP:(g + 1) * GROUP])

    @pl.when(j == n_blk)
    def _():
        _convert(side)
        _gate()
        for r in slabs:
            ya_ref[r, :] = (_dot(h_ref[r, :], wu_ref[...]) * v_scr[0, r, :]).astype(_BF16)

    @pl.when(j > n_blk)
    def _():
        _convert(side)
        for r in slabs:
            ya_ref[r, :] = (_dot(h_ref[r, :], wu_ref[...]) * v_scr[j - n_blk, r, :]).astype(_BF16)


def _p_gmlp_call(x, mod, gpre, w_v, w_u, gv, ws, bias, *, tm, tn, slab, tiles_per_seq, mod_row0,
                 side):
    m, d = x.shape
    n_blk = d // tn
    full = lambda a: pl.BlockSpec(a.shape, lambda i, j: (0,) * a.ndim)
    once = lambda a: pl.BlockSpec(a.shape, lambda i, j: (0,) * a.ndim, pipeline_mode=pl.Buffered(1))
    mspec = lambda k: pl.BlockSpec((None, 1, d), lambda i, j: (mod_row0 + i // tiles_per_seq, 0, k))
    u_map = lambda i, j: (i, jnp.maximum(j - n_blk, 0))
    (ya, h), copies = _run(
        functools.partial(_p_gmlp_kernel, tm=tm, tn=tn, n_blk=n_blk, slab=slab),
        grid=(m // tm, 2 * n_blk),
        in_specs=[pl.BlockSpec(memory_space=pl.ANY), mspec(0), mspec(1), full(gpre),
                  pl.BlockSpec((d, tn), lambda i, j: (0, jnp.minimum(j, n_blk - 1))),
                  pl.BlockSpec((d, tn), lambda i, j: (0, jnp.maximum(j - n_blk, 0))),
                  full(gv), once(ws), once(bias)],
        args=(x, mod, mod, gpre, w_v, w_u, gv, ws, bias),
        out_specs=[pl.BlockSpec((tm, tn), u_map), pl.BlockSpec((tm, d), lambda i, j: (i, 0))],
        out_shape=[jax.ShapeDtypeStruct((m, d), _BF16), jax.ShapeDtypeStruct((m, d), _BF16)],
        scratch=[pltpu.VMEM((tm, d), _F32), pltpu.SemaphoreType.DMA(()),
                 pltpu.VMEM((n_blk, tm, tn), _F32), pltpu.VMEM((d // GROUP, CHUNK, CHUNK), _BF16)],
        side=side, name="gmlp_prompt", vmem_limit_bytes=VMEM_LIMIT_BYTES_WIDE)
    return ya, h, copies


def _p_shortconv_kernel(h_ref, wbg_ref, wcg_ref, wxb_ref, cw_ref, yb_ref, tail_ref, carry_scr,
                        *, tm, tiles_per_seq, slab, side):
    i = pl.program_id(0)
    j = pl.program_id(1)

    @pl.when(i % tiles_per_seq == 0)
    def _():
        carry_scr[j] = jnp.zeros(carry_scr.shape[1:], _F32)

    _convert(side)
    prev = carry_scr[j]
    cw = cw_ref.at[j]
    for r in _slabs(tm, slab):
        h = h_ref[r, :]
        bg = _dot(h, wbg_ref[...])
        p = _dot(h, wcg_ref[...]) * _dot(h, wxb_ref[...])
        yb_ref[r, :] = (bg * _causal_conv_rows(p, prev, cw)).astype(_BF16)
        prev = p[slab - SUBLANES:, :]
    carry_scr[j] = prev
    tail_ref[j] = prev


def _p_shortconv_call(h, w_bcx, cw, *, tm, tn, tiles_per_seq, slab, side):
    m, d = h.shape
    w = cw.shape[1]
    n_blk = w // tn
    wspec = lambda off: pl.BlockSpec((d, tn), lambda i, j: (0, off + j))
    (yb, tail), copies = _run(
        functools.partial(_p_shortconv_kernel, tm=tm, tiles_per_seq=tiles_per_seq, slab=slab),
        grid=(m // tm, n_blk),
        in_specs=[pl.BlockSpec((tm, d), lambda i, j: (i, 0)),
                  wspec(0), wspec(n_blk), wspec(2 * n_blk),
                  pl.BlockSpec((n_blk, CONV_K, tn), lambda i, j: (0, 0, 0))],
        args=(h, w_bcx, w_bcx, w_bcx, _col_blocks(cw, tn)),
        out_specs=[pl.BlockSpec((tm, tn), lambda i, j: (i, j)),
                   pl.BlockSpec((None, n_blk, SUBLANES, tn), lambda i, j: (i, 0, 0, 0))],
        out_shape=[jax.ShapeDtypeStruct((m, w), _BF16),
                   jax.ShapeDtypeStruct((m // tm, n_blk, SUBLANES, tn), _F32)],
        scratch=[pltpu.VMEM((n_blk, SUBLANES, tn), _F32)],
        side=side, name="shortconv_prompt")
    return yb, tail, copies


def _p_proj_kernel(m_ref, wo_ref, x_ref, gt_ref, gpost_ref, sh_ref, sc_ref, gpre_ref,
                   x1_ref, h2_ref, *, tm, slab, side):
    g1 = gt_ref[...] * gpost_ref[...]
    g2 = gpre_ref[...] * (1.0 + sc_ref[...])
    for r in _slabs(tm, slab):
        y = _dot(m_ref[r, :], wo_ref[...])
        x1 = x_ref[r, :] + _rms(y, g1)
        x1_ref[r, :] = x1
        h2_ref[r, :] = (_rms(x1, g2) + sh_ref[...]).astype(_BF16)


def _p_proj_call(mg, w_o, x, mod, gpost, gpre, *, tm, slab, tiles_per_seq, mod_row0):
    m, d = x.shape
    row = pl.BlockSpec((tm, d), lambda i: (i, 0))
    vec = pl.BlockSpec((1, d), lambda i: (0, 0))
    mspec = lambda k: pl.BlockSpec((None, 1, d), lambda i: (mod_row0 + i // tiles_per_seq, 0, k))
    (x1, h2), _ = _run(
        functools.partial(_p_proj_kernel, tm=tm, slab=slab),
        grid=(m // tm,),
        in_specs=[row, pl.BlockSpec((d, d), lambda i: (0, 0)), row, mspec(2), vec, mspec(3),
                  mspec(4), vec],
        args=(mg, w_o, x, mod, gpost, mod, mod, gpre),
        out_specs=[row, row],
        out_shape=[jax.ShapeDtypeStruct((m, d), _F32), jax.ShapeDtypeStruct((m, d), _BF16)],
        name="out_proj_prompt")
    return x1, h2


def _p_ffn_kernel(h_ref, wa_ref, wb_ref, cw_ref, wd_ref, x1_hbm, gt_ref, gpost_ref,
                  out_ref, tail_ref, x1_buf, x1_sem, carry_scr, *, tm, n_blk, tiles_per_seq, slab,
                  side):
    i = pl.program_id(0)
    j = pl.program_id(1)
    x1_copy = pltpu.make_async_copy(x1_hbm.at[pl.ds(pl.multiple_of(i * tm, tm), tm), :],
                                    x1_buf, x1_sem)

    @pl.when(i % tiles_per_seq == 0)
    def _():
        carry_scr[j] = jnp.zeros(carry_scr.shape[1:], _F32)

    def step(first, last):
        prev = carry_scr[j]
        cw = cw_ref.at[j]
        slabs = _slabs(tm, slab)
        gs = []
        for r in slabs:
            h = h_ref[r, :]
            a = _dot(h, wa_ref[...])
            b = _dot(h, wb_ref[...])
            gs.append((jax.nn.gelu(_causal_conv_rows(a, prev, cw)) * b).astype(_BF16))
            prev = a[slab - SUBLANES:, :]
        for r, g in zip(slabs, gs):
            f = _dot(g, wd_ref[...])
            acc = f if first else out_ref[r, :] + f
            if last:
                acc = x1_buf[r, :] + _rms(acc, gt_ref[...] * gpost_ref[...])
            out_ref[r, :] = acc
        carry_scr[j] = prev
        tail_ref[j] = prev

    @pl.when(j == 0)
    def _():
        x1_copy.start()
        step(True, False)

    @pl.when((j > 0) & (j < n_blk - 1))
    def _():
        step(False, False)

    @pl.when(j == n_blk - 1)
    def _():
        x1_copy.wait()
        step(False, True)


def _p_ffn_call(h2, w_a, w_b, cw, w_down, x1, mod, gpost, *, tm, tn, tiles_per_seq, slab, mod_row0):
    m, d = x1.shape
    f = cw.shape[1]
    n_blk = f // tn
    row = pl.BlockSpec((tm, d), lambda i, j: (i, 0))
    wspec = pl.BlockSpec((d, tn), lambda i, j: (0, j))
    (out, tail), _ = _run(
        functools.partial(_p_ffn_kernel, tm=tm, n_blk=n_blk, tiles_per_seq=tiles_per_seq, slab=slab),
        grid=(m // tm, n_blk),
        in_specs=[row, wspec, wspec,
                  pl.BlockSpec((n_blk, CONV_K, tn), lambda i, j: (0, 0, 0)),
                  pl.BlockSpec((tn, d), lambda i, j: (j, 0)),
                  pl.BlockSpec(memory_space=pl.ANY),
                  pl.BlockSpec((None, 1, d), lambda i, j: (mod_row0 + i // tiles_per_seq, 0, 5)),
                  pl.BlockSpec((1, d), lambda i, j: (0, 0))],
        args=(h2, w_a, w_b, _col_blocks(cw, tn), w_down, x1, mod, gpost),
        out_specs=[row, pl.BlockSpec((None, n_blk, SUBLANES, tn), lambda i, j: (i, 0, 0, 0))],
        out_shape=[jax.ShapeDtypeStruct((m, d), _F32),
                   jax.ShapeDtypeStruct((m // tm, n_blk, SUBLANES, tn), _F32)],
        scratch=[pltpu.VMEM((tm, d), _F32), pltpu.SemaphoreType.DMA(()),
                 pltpu.VMEM((n_blk, SUBLANES, tn), _F32)],
        name="convffn_prompt")
    return out, tail


def _cols(src, first_col, n_cols, n_blocks):
    block = n_cols // n_blocks
    assert block * n_blocks == n_cols and block % 128 == 0 and first_col % block == 0
    return _Side(src, 1, first_col // block, block, n_blocks)


def _layer(xs, xp, mod, st_b, st_f, p, *, seq_len, mod_row0):
    d = p["g_v"].shape[1]
    wb = p["conv_b_w"].shape[1]
    f = p["conv_f_w"].shape[1]
    tn_s = SAMPLE_COL_TILE
    mod_rows = mod.reshape(mod.shape[0], 1, N_MOD * d)

    def tiles(name):
        tm, tn, slab = PROMPT_TILES[name]
        return dict(tm=tm, tn=tn, slab=slab, tiles_per_seq=seq_len // tm)

    def n_steps(name):
        tm, tn, _ = PROMPT_TILES[name]
        cols = {"gmlp": 2 * d, "shortconv": wb, "merge": d, "ffn": f}[name]
        return xp.shape[0] // tm * (cols // tn)

    def without(kw, *names):
        return {k: v for k, v in kw.items() if k not in names}

    def blocks(n_cols, budget):
        return max(n for n in range(1, budget + 1) if n_cols % (128 * n) == 0)

    w_in, w_up = p["w_in"], p["w_up"]

    ya_s, h1_s, vn, w_v, w_u = _s_gmlp_call(xs, mod, p["g_pre_mix"], w_in, p["g_v"], p["wvec"],
                                            p["bvec"], tn=tn_s)
    n = n_steps("gmlp")
    ya_p, h1_p, (w_bcx,) = _p_gmlp_call(
        xp, mod_rows, p["g_pre_mix"], w_v, w_u, p["g_v"], p["w_s"], p["bias"], mod_row0=mod_row0,
        side=(_cols(w_in, 2 * d, 3 * wb, blocks(3 * wb, n)),),
        **tiles("gmlp"))

    yb_s, new_b = _s_shortconv_call(h1_s, w_bcx, p["conv_b_w"], st_b, tn=tn_s)
    n = n_steps("shortconv")
    yb_p, tail_b, (w_gate, w_oa, w_ob, w_o, w_a, w_b) = _p_shortconv_call(
        h1_p, w_bcx, p["conv_b_w"],
        side=(_cols(w_in, 2 * d + 3 * wb, 2 * d, blocks(2 * d, n)),
              _cols(p["w_out_a"], 0, d, blocks(d, n)), _cols(p["w_out_b"], 0, d, blocks(d, n)),
              _cols(p["w_o"], 0, d, blocks(d, n)),
              _cols(w_up, 0, f, blocks(f, n)), _cols(w_up, f, f, blocks(f, n))),
        **tiles("shortconv"))

    tm_s = h1_s.shape[0]
    mg_s, _ = _merge_call(h1_s, ya_s, yb_s, w_gate, w_oa, w_ob, tm=tm_s, tn=tn_s, slab=tm_s // 2,
                          name="gated_merge_sample")
    n_row_blocks = max(k for k in range(1, n_steps("merge") + 1)
                       if f % k == 0 and (f // k) % (2 * SUBLANES) == 0)
    mg_p, (w_d,) = _merge_call(h1_p, ya_p, yb_p, w_gate, w_oa, w_ob, name="gated_merge_prompt",
                               side=(_Side(p["w_down"], 0, 0, f // n_row_blocks, n_row_blocks),),
                               **without(tiles("merge"), "tiles_per_seq"))

    x1_s, h2_s = _s_proj_call(mg_s, w_o, xs, mod, p["g_post_mix"], p["g_pre_ffn"], tn=tn_s)
    x1_p, h2_p = _p_proj_call(mg_p, w_o, xp, mod_rows, p["g_post_mix"], p["g_pre_ffn"],
                              mod_row0=mod_row0, **without(tiles("proj"), "tn"))

    out_s, new_f = _s_ffn_call(h2_s, w_a, w_b, p["conv_f_w"], w_d, x1_s, mod, p["g_post_ffn"],
                               st_f, tn=tn_s)
    out_p, tail_f = _p_ffn_call(h2_p, w_a, w_b, p["conv_f_w"], w_d, x1_p, mod_rows, p["g_post_ffn"],
                                mod_row0=mod_row0, **tiles("ffn"))
    return out_s, out_p, new_b, new_f, vn, tail_b, tail_f


def kernel(x_prompt, x_sample, c_prompt, c_sample, state_conv_b, state_conv_ffn, w_ada, b_ada, g_pre_mix, g_post_mix, w_in, g_v, w_s, b_s, conv_b_w, w_out_a, w_out_b, w_o, g_pre_ffn, g_post_ffn, w_up, conv_f_w, w_down):
    depth = w_in.shape[0]
    bp, seq, d = x_prompt.shape
    bs, tdec, _ = x_sample.shape
    n_groups = w_s.shape[1]
    assert bs == CHUNK and tdec <= CHUNK
    assert all(seq % tm == 0 for tm, _, _ in PROMPT_TILES.values())

    xp = x_prompt.reshape(bp * seq, d)
    xs = x_sample
    pad = (-(bp + bs)) % SUBLANES
    c_all = jnp.concatenate([c_sample, c_prompt, jnp.zeros((pad, d), _F32)], axis=0)

    pb, sb, pf, sf, sv = [], [], [], [], []
    for l in range(depth):
        mod, _ = _mod_call(c_all, w_ada[l], b_ada[l][None, :], side=())
        vec = lambda a: a[l][None, :]
        bias_full = jnp.repeat(jnp.transpose(b_s[l]), GROUP, axis=1)
        wvec = jnp.repeat(
            jnp.transpose(w_s[l][:, :tdec, :tdec], (1, 2, 0)).reshape(tdec * tdec, n_groups),
            GROUP, axis=1)
        p = {
            "w_in": w_in[l], "w_out_a": w_out_a[l], "w_out_b": w_out_b[l],
            "w_o": w_o[l], "w_up": w_up[l], "w_down": w_down[l],
            "g_pre_mix": vec(g_pre_mix), "g_post_mix": vec(g_post_mix), "g_v": vec(g_v),
            "g_pre_ffn": vec(g_pre_ffn), "g_post_ffn": vec(g_post_ffn),
            "conv_b_w": conv_b_w[l], "conv_f_w": conv_f_w[l],
            "w_s": w_s[l], "bias": bias_full, "wvec": wvec, "bvec": bias_full[:tdec],
        }
        xs, xp, sbt, sft, vn, tb, tf = _layer(xs, xp, mod, state_conv_b[l], state_conv_ffn[l], p,
                                              seq_len=seq, mod_row0=bs)

        def prompt_tail(t):
            n_tiles, n_blk, _, tn = t.shape
            t = t.reshape(bp, n_tiles // bp, n_blk, SUBLANES, tn)[:, -1, :, SUBLANES - (CONV_K - 1):, :]
            return jnp.transpose(t, (0, 2, 1, 3)).reshape(bp, CONV_K - 1, n_blk * tn)

        pb.append(prompt_tail(tb))
        pf.append(prompt_tail(tf))
        sb.append(sbt)
        sf.append(sft)
        sv.append(vn)

    y_prompt = xp.reshape(bp, seq, d)
    return (y_prompt, xs, jnp.stack(pb), jnp.stack(sb), jnp.stack(pf), jnp.stack(sf),
            jnp.stack(sv))
```

```python
import functools
from typing import NamedTuple

import jax
import jax.numpy as jnp
from jax import lax
from jax.experimental import pallas as pl
from jax.experimental.pallas import tpu as pltpu

EPS = 1e-6
CHUNK = 128
GROUP = 128
CONV_K = 3
SUBLANES = 8
VMEM_LIMIT_BYTES = 56 * 1024 * 1024
VMEM_LIMIT_BYTES_WIDE = 60 * 1024 * 1024
PROMPT_TILES = {
    "gmlp": (1024, 1024, 256),
    "shortconv": (1024, 512, 256),
    "merge": (512, 1024, 256),
    "proj": (512, None, 256),
    "ffn": (1024, 512, 256),
}
SAMPLE_COL_TILE = 512
ADALN_COL_TILE = 1024

_BF16 = jnp.bfloat16
_F32 = jnp.float32


def _dot(a, b):
    return jnp.dot(a, b, preferred_element_type=_F32)


def _rms(xf, g):
    ms = jnp.mean(xf * xf, axis=-1, keepdims=True)
    return xf * lax.rsqrt(ms + EPS) * g


def _causal_conv_rows(p, prev, cw_ref):
    r1 = pltpu.roll(p, 1, 0)
    r2 = pltpu.roll(p, 2, 0)
    row = lax.broadcasted_iota(jnp.int32, (SUBLANES, 1), 0)
    head1 = jnp.where(row == 0, prev[7:8, :], r1[:SUBLANES, :])
    head2 = jnp.where(row == 0, prev[6:7, :], jnp.where(row == 1, prev[7:8, :], r2[:SUBLANES, :]))
    m1 = jnp.concatenate([head1, r1[SUBLANES:, :]], axis=0)
    m2 = jnp.concatenate([head2, r2[SUBLANES:, :]], axis=0)
    return cw_ref[0:1, :] * m2 + cw_ref[1:2, :] * m1 + cw_ref[2:3, :] * p


def _causal_conv_slabs(x, prev, cw_ref, slab):
    w0, w1, w2 = cw_ref[0:1, :], cw_ref[1:2, :], cw_ref[2:3, :]
    seq = list(prev) + [x[t * slab:(t + 1) * slab, :] for t in range(x.shape[0] // slab)]
    y = [w0 * seq[t] + w1 * seq[t + 1] + w2 * seq[t + 2] for t in range(len(seq) - 2)]
    return jnp.concatenate(y, axis=0), seq[-2:]


def _slabs(tm, slab):
    return [slice(s * slab, (s + 1) * slab) for s in range(tm // slab)]


def _mod_row_spec(d, k, tiles_per_seq, mod_row0, n_grid_axes):
    idx = lambda i: ((mod_row0 + i // tiles_per_seq) // SUBLANES, k)
    return pl.BlockSpec((SUBLANES, d), (lambda i, j: idx(i)) if n_grid_axes == 2 else idx)


def _mod_row(ref, tiles_per_seq, mod_row0):
    b = (mod_row0 + pl.program_id(0) // tiles_per_seq) % SUBLANES
    return ref[pl.ds(b, 1), :]


def _col_blocks(a, tn):
    r, c = a.shape
    return jnp.transpose(a.reshape(r, c // tn, tn), (1, 0, 2))


def _post_mix(y, x, gt, gpost, sh, sc, gpre):
    x1 = x + gt * _rms(y, gpost)
    h2 = _rms(x1, gpre) * (1.0 + sc) + sh
    return x1, h2.astype(_BF16)


class _Side(NamedTuple):
    src: jax.Array
    axis: int
    start: int
    block: int
    n_blocks: int


def _convert(side):
    for src_ref, dst_ref in side:
        dst_ref[...] = src_ref[...].astype(_BF16)


def _run(body, *, grid, in_specs, args, out_specs, out_shape, scratch=(), side=(), name,
         vmem_limit_bytes=VMEM_LIMIT_BYTES):
    n_in, n_out, n_side = len(args), len(out_shape), len(side)
    n_steps = functools.reduce(lambda a, b: a * b, grid)
    step_of = (lambda i: i) if len(grid) == 1 else (lambda i, j: i * grid[1] + j)
    side_in, side_out, side_shape = [], [], []
    for s in side:
        assert s.n_blocks <= n_steps, (name, s.n_blocks, n_steps)
        other = s.src.shape[1 - s.axis]
        pos = lambda *ids, s=s: jnp.minimum(step_of(*ids), s.n_blocks - 1)
        if s.axis == 1:
            blk, full = (other, s.block), (other, s.block * s.n_blocks)
            side_in.append(pl.BlockSpec(blk, lambda *ids, s=s, pos=pos: (0, s.start + pos(*ids))))
            side_out.append(pl.BlockSpec(blk, lambda *ids, pos=pos: (0, pos(*ids))))
        else:
            blk, full = (s.block, other), (s.block * s.n_blocks, other)
            side_in.append(pl.BlockSpec(blk, lambda *ids, s=s, pos=pos: (s.start + pos(*ids), 0)))
            side_out.append(pl.BlockSpec(blk, lambda *ids, pos=pos: (pos(*ids), 0)))
        side_shape.append(jax.ShapeDtypeStruct(full, _BF16))

    def kern(*refs):
        o0 = n_in + n_side
        s0 = o0 + n_out + n_side
        body(*refs[:n_in], *refs[o0:o0 + n_out], *refs[s0:],
             side=tuple(zip(refs[n_in:o0], refs[o0 + n_out:s0])))

    res = pl.pallas_call(
        kern, grid=grid,
        in_specs=list(in_specs) + side_in, out_specs=list(out_specs) + side_out,
        out_shape=list(out_shape) + side_shape, scratch_shapes=list(scratch),
        compiler_params=pltpu.CompilerParams(dimension_semantics=("arbitrary",) * len(grid),
                                             vmem_limit_bytes=vmem_limit_bytes),
        name=name,
    )(*args, *[s.src for s in side])
    return res[:n_out], res[n_out:]


def _mod_kernel(c_ref, w_ref, b_ref, o_ref, *, side):
    _convert(side)
    c = c_ref[...]
    a = (c * jax.nn.sigmoid(c)).astype(_BF16)
    o_ref[...] = _dot(a, w_ref[...].astype(_BF16)) + b_ref[...]


def _mod_call(c_all, w_ada, b_ada, *, side):
    rows, d = c_all.shape
    n = w_ada.shape[1]
    tn = ADALN_COL_TILE
    (mod,), copies = _run(
        _mod_kernel, grid=(n // tn,),
        in_specs=[pl.BlockSpec((rows, d), lambda j: (0, 0)),
                  pl.BlockSpec((d, tn), lambda j: (0, j)),
                  pl.BlockSpec((1, tn), lambda j: (0, j))],
        args=(c_all, w_ada, b_ada),
        out_specs=[pl.BlockSpec((rows, tn), lambda j: (0, j))],
        out_shape=[jax.ShapeDtypeStruct((rows, n), _F32)],
        side=side, name="adaln_mod")
    return mod, copies


def _s_gmlp_kernel(x_hbm, sh_ref, sc_ref, gpre_ref, wv_ref, wu_ref, gv_ref, ws_ref, bias_ref,
                   ya_ref, h_ref, vn_hbm, wvb_ref, wub_ref, v_scr, x_buf, vn_buf, in_sem, out_sem,
                   *, tm, tn, n_blk, slab, side):
    j = pl.program_id(0)
    d = n_blk * tn
    slabs = _slabs(tm, slab)
    n_slab = len(slabs)
    x_copy = lambda t: pltpu.make_async_copy(x_hbm.at[:, t, :], x_buf.at[t], in_sem.at[t])
    vn_copy = lambda t: pltpu.make_async_copy(vn_buf.at[t], vn_hbm.at[:, t, :], out_sem.at[t])

    @pl.when(j == 0)
    def _():
        for t in range(n_slab):
            x_copy(t).start()
        for t, r in enumerate(slabs):
            x_copy(t).wait()
            h_ref[r, :] = (_rms(x_buf[t], gpre_ref[...]) * (1.0 + sc_ref[...])
                           + sh_ref[...]).astype(_BF16)

    @pl.when(j < n_blk)
    def _():
        wv = wv_ref[...].astype(_BF16)
        wvb_ref[...] = wv
        v_scr[j] = _dot(h_ref[...], wv)

    @pl.when(j == n_blk)
    def _():
        for t, r in enumerate(slabs):
            ss = 0.0
            for k in range(n_blk):
                vk = v_scr[k, r, :]
                ss = ss + jnp.sum(vk * vk, axis=-1, keepdims=True)
            rs = lax.rsqrt(ss * (1.0 / d) + EPS)
            for k in range(n_blk):
                vn = v_scr[k, r, :] * rs * gv_ref[:, k * tn:(k + 1) * tn]
                v_scr[k, r, :] = vn
                vn_buf[t, :, k * tn:(k + 1) * tn] = vn
            vn_copy(t).start()
        for t in reversed(range(n_slab)):
            for k in range(n_blk):
                c = slice(k * tn, (k + 1) * tn)
                acc = ws_ref[t * n_slab:t * n_slab + 1, c] * v_scr[k, slabs[0], :]
                for s in range(1, t + 1):
                    acc = acc + ws_ref[t * n_slab + s:t * n_slab + s + 1, c] * v_scr[k, slabs[s], :]
                v_scr[k, slabs[t], :] = acc + bias_ref[t:t + 1, c]

    @pl.when(j >= n_blk)
    def _():
        wu = wu_ref[...].astype(_BF16)
        wub_ref[...] = wu
        ya_ref[...] = (_dot(h_ref[...], wu) * v_scr[j - n_blk]).astype(_BF16)

    @pl.when(j == 2 * n_blk - 1)
    def _():
        for t in range(n_slab):
            vn_copy(t).wait()


def _s_gmlp_call(x, mod, gpre, w_in, gv, wvec, bvec, *, tn):
    slab, n_slab, d = x.shape
    tm = n_slab * slab
    n_blk = d // tn
    anywhere = pl.BlockSpec(memory_space=pl.ANY)
    full = lambda a: pl.BlockSpec(a.shape, lambda j: (0,) * a.ndim)
    u_map = lambda j: (0, jnp.maximum(j - n_blk, 0))
    v_map = lambda j: (0, jnp.minimum(j, n_blk - 1))
    (ya, h, vn, w_v, w_u), _ = _run(
        functools.partial(_s_gmlp_kernel, tm=tm, tn=tn, n_blk=n_blk, slab=slab),
        grid=(2 * n_blk,),
        in_specs=[anywhere,
                  pl.BlockSpec((slab, d), lambda j: (0, 0)), pl.BlockSpec((slab, d), lambda j: (0, 1)),
                  full(gpre),
                  pl.BlockSpec((d, tn), lambda j: (0, n_blk + jnp.minimum(j, n_blk - 1))),
                  pl.BlockSpec((d, tn), u_map),
                  full(gv), full(wvec), full(bvec)],
        args=(x, mod, mod, gpre, w_in, w_in, gv, wvec, bvec),
        out_specs=[pl.BlockSpec((tm, tn), u_map), pl.BlockSpec((tm, d), lambda j: (0, 0)), anywhere,
                   pl.BlockSpec((d, tn), v_map), pl.BlockSpec((d, tn), u_map)],
        out_shape=[jax.ShapeDtypeStruct((tm, d), _BF16), jax.ShapeDtypeStruct((tm, d), _BF16),
                   jax.ShapeDtypeStruct(x.shape, _F32),
                   jax.ShapeDtypeStruct((d, d), _BF16), jax.ShapeDtypeStruct((d, d), _BF16)],
        scratch=[pltpu.VMEM((n_blk, tm, tn), _F32),
                 pltpu.VMEM((n_slab, slab, d), _F32), pltpu.VMEM((n_slab, slab, d), _F32),
                 pltpu.SemaphoreType.DMA((n_slab,)), pltpu.SemaphoreType.DMA((n_slab,))],
        name="gmlp_sample")
    return ya, h, vn, w_v, w_u


class _StateIO:
    def __init__(self, st_hbm, new_hbm, st_buf, new_buf, in_sem, out_sem, n_blk, tn):
        self.n_rows, self.n_blk = st_buf.shape[0], n_blk
        col = lambda c: pl.ds(c * tn, tn) if isinstance(c, int) else pl.ds(pl.multiple_of(c * tn, tn), tn)
        self.load = lambda k, c: pltpu.make_async_copy(st_hbm.at[:, k, col(c)], st_buf.at[k, c],
                                                       in_sem.at[k, c])
        self.store = lambda k, c: pltpu.make_async_copy(new_buf.at[k, c], new_hbm.at[:, k, col(c)],
                                                        out_sem.at[k, c])
        self.st_buf, self.new_buf = st_buf, new_buf

    def read(self, j):
        @pl.when(j == 0)
        def _():
            for k in range(self.n_rows):
                for c in range(self.n_blk):
                    self.load(k, c).start()
        for k in range(self.n_rows):
            self.load(k, j).wait()
        return [self.st_buf[k, j] for k in range(self.n_rows)]

    def write(self, j, rows):
        for k, row in enumerate(rows):
            self.new_buf[k, j] = row
            self.store(k, j).start()

        @pl.when(j == self.n_blk - 1)
        def _():
            for k in range(self.n_rows):
                for c in range(self.n_blk):
                    self.store(k, c).wait()


def _state_scratch(state, n_blk, tn):
    slab, n_rows, _ = state.shape
    buf = pltpu.VMEM((n_rows, n_blk, slab, tn), _F32)
    sem = pltpu.SemaphoreType.DMA((n_rows, n_blk))
    return [buf, buf, sem, sem]


def _s_shortconv_kernel(h_ref, wbg_ref, wcg_ref, wxb_ref, cw_ref, st_hbm, yb_ref, new_hbm,
                        st_buf, new_buf, in_sem, out_sem, *, slab, n_blk, tn, side):
    j = pl.program_id(0)
    io = _StateIO(st_hbm, new_hbm, st_buf, new_buf, in_sem, out_sem, n_blk, tn)
    prev = io.read(j)
    h = h_ref[...]
    bg = _dot(h, wbg_ref[...])
    p = _dot(h, wcg_ref[...]) * _dot(h, wxb_ref[...])
    cb, tail = _causal_conv_slabs(p, prev, cw_ref, slab)
    yb_ref[...] = (bg * cb).astype(_BF16)
    io.write(j, tail)


def _s_shortconv_call(h, w_bcx, cw, state, *, tn):
    tm, d = h.shape
    slab = state.shape[0]
    w = cw.shape[1]
    n_blk = w // tn
    wspec = lambda off: pl.BlockSpec((d, tn), lambda j: (0, off + j))
    anywhere = pl.BlockSpec(memory_space=pl.ANY)
    (yb, new_state), _ = _run(
        functools.partial(_s_shortconv_kernel, slab=slab, n_blk=n_blk, tn=tn),
        grid=(n_blk,),
        in_specs=[pl.BlockSpec((tm, d), lambda j: (0, 0)),
                  wspec(0), wspec(n_blk), wspec(2 * n_blk),
                  pl.BlockSpec((CONV_K, tn), lambda j: (0, j)), anywhere],
        args=(h, w_bcx, w_bcx, w_bcx, cw, state),
        out_specs=[pl.BlockSpec((tm, tn), lambda j: (0, j)), anywhere],
        out_shape=[jax.ShapeDtypeStruct((tm, w), _BF16), jax.ShapeDtypeStruct(state.shape, _F32)],
        scratch=_state_scratch(state, n_blk, tn),
        name="shortconv_sample")
    return yb, new_state


def _merge_kernel(h_ref, ya_ref, yb_ref, wga_ref, wgb_ref, woa_ref, wob_ref, m_ref, *, tm, slab, side):
    _convert(side)
    for r in _slabs(tm, slab):
        h = h_ref[r, :]
        ga = jax.nn.sigmoid(_dot(h, wga_ref[...]))
        gb = jax.nn.sigmoid(_dot(h, wgb_ref[...]))
        m = ga * _dot(ya_ref[r, :], woa_ref[...]) + gb * _dot(yb_ref[r, :], wob_ref[...])
        m_ref[r, :] = m.astype(_BF16)


def _merge_call(h, ya, yb, w_gate, w_out_a, w_out_b, *, tm, tn, slab, side=(), name):
    m, d = h.shape
    n_blk = d // tn
    row = pl.BlockSpec((tm, d), lambda i, j: (i, 0))
    wspec = lambda off: pl.BlockSpec((d, tn), lambda i, j: (0, off + j))
    (mg,), copies = _run(
        functools.partial(_merge_kernel, tm=tm, slab=slab),
        grid=(m // tm, n_blk),
        in_specs=[row, row, row, wspec(0), wspec(n_blk), wspec(0), wspec(0)],
        args=(h, ya, yb, w_gate, w_gate, w_out_a, w_out_b),
        out_specs=[pl.BlockSpec((tm, tn), lambda i, j: (i, j))],
        out_shape=[jax.ShapeDtypeStruct((m, d), _BF16)],
        side=side, name=name)
    return mg, copies


def _s_proj_kernel(m_ref, wo_ref, x_hbm, gt_ref, gpost_ref, sh_ref, sc_ref, gpre_ref,
                   x1_ref, h2_ref, y_scr, x_buf, in_sem, *, tm, tn, n_blk, slab, side):
    j = pl.program_id(0)
    n_slab = tm // slab
    x_copy = lambda t: pltpu.make_async_copy(x_hbm.at[:, t, :], x_buf.at[t], in_sem.at[t])

    @pl.when(j == 0)
    def _():
        for t in range(n_slab):
            x_copy(t).start()

    y_scr[j] = _dot(m_ref[...], wo_ref[...])

    @pl.when(j == n_blk - 1)
    def _():
        for t, r in enumerate(_slabs(tm, slab)):
            x_copy(t).wait()
            y = jnp.concatenate([y_scr[k, r, :] for k in range(n_blk)], axis=-1)
            x1, h2 = _post_mix(y, x_buf[t], gt_ref[...], gpost_ref[...],
                               sh_ref[...], sc_ref[...], gpre_ref[...])
            x1_ref[r, :] = x1
            h2_ref[r, :] = h2


def _s_proj_call(mg, w_o, x, mod, gpost, gpre, *, tn):
    tm, d = mg.shape
    slab = x.shape[0]
    n_blk = d // tn
    row = pl.BlockSpec((tm, d), lambda j: (0, 0))
    vec = pl.BlockSpec((1, d), lambda j: (0, 0))
    mspec = lambda k: pl.BlockSpec((slab, d), lambda j: (0, k))
    (x1, h2), _ = _run(
        functools.partial(_s_proj_kernel, tm=tm, tn=tn, n_blk=n_blk, slab=slab),
        grid=(n_blk,),
        in_specs=[row, pl.BlockSpec((d, tn), lambda j: (0, j)),
                  pl.BlockSpec(memory_space=pl.ANY), mspec(2), vec, mspec(3), mspec(4), vec],
        args=(mg, w_o, x, mod, gpost, mod, mod, gpre),
        out_specs=[row, row],
        out_shape=[jax.ShapeDtypeStruct((tm, d), _F32), jax.ShapeDtypeStruct((tm, d), _BF16)],
        scratch=[pltpu.VMEM((n_blk, tm, tn), _F32), pltpu.VMEM((tm // slab, slab, d), _F32),
                 pltpu.SemaphoreType.DMA((tm // slab,))],
        name="out_proj_sample")
    return x1, h2


def _s_ffn_kernel(h_ref, wa_ref, wb_ref, cw_ref, wd_ref, x1_ref, gt_ref, gpost_ref, st_hbm,
                  out_hbm, new_hbm, acc_scr, out_sem, st_buf, new_buf, st_in_sem, st_out_sem,
                  *, tm, tn, n_blk, slab, side):
    j = pl.program_id(0)
    n_slab = tm // slab
    out_copy = lambda t: pltpu.make_async_copy(acc_scr.at[t], out_hbm.at[:, t, :], out_sem.at[t])
    io = _StateIO(st_hbm, new_hbm, st_buf, new_buf, st_in_sem, st_out_sem, n_blk, tn)

    @pl.when(j == 0)
    def _():
        acc_scr[...] = jnp.zeros(acc_scr.shape, _F32)

    prev = io.read(j)
    gs = []
    for r in _slabs(tm, 2 * slab):
        h = h_ref[r, :]
        ac, prev = _causal_conv_slabs(_dot(h, wa_ref[...]), prev, cw_ref, slab)
        gs.append((jax.nn.gelu(ac) * _dot(h, wb_ref[...])).astype(_BF16))
    for c, g in enumerate(gs):
        f = _dot(g, wd_ref[...])
        acc_scr[2 * c] += f[:slab, :]
        acc_scr[2 * c + 1] += f[slab:, :]
    io.write(j, prev)

    @pl.when(j == n_blk - 1)
    def _():
        for t, r in enumerate(_slabs(tm, slab)):
            acc_scr[t] = x1_ref[r, :] + gt_ref[...] * _rms(acc_scr[t], gpost_ref[...])
            out_copy(t).start()
        for t in range(n_slab):
            out_copy(t).wait()


def _s_ffn_call(h2, w_a, w_b, cw, w_down, x1, mod, gpost, state, *, tn):
    tm, d = x1.shape
    slab = state.shape[0]
    f = cw.shape[1]
    n_blk = f // tn
    out_shape = (slab, tm // slab, d)
    row = pl.BlockSpec((tm, d), lambda j: (0, 0))
    wspec = pl.BlockSpec((d, tn), lambda j: (0, j))
    anywhere = pl.BlockSpec(memory_space=pl.ANY)
    (out, new_state), _ = _run(
        functools.partial(_s_ffn_kernel, tm=tm, tn=tn, n_blk=n_blk, slab=slab),
        grid=(n_blk,),
        in_specs=[row, wspec, wspec, pl.BlockSpec((CONV_K, tn), lambda j: (0, j)),
                  pl.BlockSpec((tn, d), lambda j: (j, 0)), row,
                  pl.BlockSpec((slab, d), lambda j: (0, 5)), pl.BlockSpec((1, d), lambda j: (0, 0)),
                  anywhere],
        args=(h2, w_a, w_b, cw, w_down, x1, mod, gpost, state),
        out_specs=[anywhere, anywhere],
        out_shape=[jax.ShapeDtypeStruct(out_shape, _F32), jax.ShapeDtypeStruct(state.shape, _F32)],
        scratch=[pltpu.VMEM((tm // slab, slab, d), _F32), pltpu.SemaphoreType.DMA((tm // slab,))]
        + _state_scratch(state, n_blk, tn),
        name="convffn_sample")
    return out, new_state


def _p_gmlp_kernel(x_hbm, sh_ref, sc_ref, gpre_ref, wv_ref, wu_ref, gv_ref, ws_ref, bias_ref,
                   ya_ref, h_ref, x_buf, x_sem, v_scr, wt_scr, *, tm, tn, n_blk, slab, tiles_per_seq,
                   mod_row0, side):
    i = pl.program_id(0)
    j = pl.program_id(1)
    d = n_blk * tn
    slabs = _slabs(tm, slab)

    def x_copy(tile):
        return pltpu.make_async_copy(x_hbm.at[pl.ds(pl.multiple_of(tile * tm, tm), tm), :],
                                     x_buf, x_sem)

    @pl.when((i == 0) & (j == 0))
    def _():
        x_copy(0).start()
        tril = (lax.broadcasted_iota(jnp.int32, (CHUNK, CHUNK), 0)
                >= lax.broadcasted_iota(jnp.int32, (CHUNK, CHUNK), 1))
        for g in range(d // GROUP):
            wt_scr[g] = jnp.where(tril, ws_ref[g], 0.0).astype(_BF16)

    @pl.when(j == 0)
    def _():
        x_copy(i).wait()
        _convert(side)
        g = gpre_ref[...] * (1.0 + _mod_row(sc_ref, tiles_per_seq, mod_row0))
        sh = _mod_row(sh_ref, tiles_per_seq, mod_row0)
        for r in slabs:
            h = (_rms(x_buf[r, :], g) + sh).astype(_BF16)
            h_ref[r, :] = h
            v_scr[0, r, :] = _dot(h, wv_ref[...])

    @pl.when((j == 0) & (i + 1 < pl.num_programs(0)))
    def _():
        x_copy(i + 1).start()

    @pl.when((j > 0) & (j < n_blk))
    def _():
        _convert(side)
        for r in slabs:
            v_scr[j, r, :] = _dot(h_ref[r, :], wv_ref[...])

    def _gate():
        gpb = tn // GROUP
        for c in range(tm // CHUNK):
            r = slice(c * CHUNK, (c + 1) * CHUNK)
            ss = 0.0
            for k in range(n_blk):
                vk = v_scr[k, r, :]
                ss = ss + jnp.sum(vk * vk, axis=-1, keepdims=True)
            rs = lax.rsqrt(ss * (1.0 / d) + EPS)
            for k in range(n_blk):
                vb = (v_scr[k, r, :] * rs * gv_ref[:, k * tn:(k + 1) * tn]).astype(_BF16)
                for gg in range(gpb):
                    g = k * gpb + gg
                    lanes = slice(gg * GROUP, (gg + 1) * GROUP)
                    v_scr[k, r, lanes] = (_dot(wt_scr[g], vb[:, lanes])
                                          + bias_ref[:, g * GROUP:(g + 1) * GROUP])

    @pl.when(j == n_blk)
    def _():
        _convert(side)
        _gate()
        for r in slabs:
            ya_ref[r, :] = (_dot(h_ref[r, :], wu_ref[...]) * v_scr[0, r, :]).astype(_BF16)

    @pl.when(j > n_blk)
    def _():
        _convert(side)
        for r in slabs:
            ya_ref[r, :] = (_dot(h_ref[r, :], wu_ref[...]) * v_scr[j - n_blk, r, :]).astype(_BF16)


def _p_gmlp_call(x, mod, gpre, w_v, w_u, gv, ws, bias, *, tm, tn, slab, tiles_per_seq, mod_row0,
                 side):
    m, d = x.shape
    n_blk = d // tn
    full = lambda a: pl.BlockSpec(a.shape, lambda i, j: (0,) * a.ndim)
    once = lambda a: pl.BlockSpec(a.shape, lambda i, j: (0,) * a.ndim, pipeline_mode=pl.Buffered(1))
    mspec = lambda k: _mod_row_spec(d, k, tiles_per_seq, mod_row0, 2)
    u_map = lambda i, j: (i, jnp.maximum(j - n_blk, 0))
    (ya, h), copies = _run(
        functools.partial(_p_gmlp_kernel, tm=tm, tn=tn, n_blk=n_blk, slab=slab,
                          tiles_per_seq=tiles_per_seq, mod_row0=mod_row0),
        grid=(m // tm, 2 * n_blk),
        in_specs=[pl.BlockSpec(memory_space=pl.ANY), mspec(0), mspec(1), full(gpre),
                  pl.BlockSpec((d, tn), lambda i, j: (0, jnp.minimum(j, n_blk - 1))),
                  pl.BlockSpec((d, tn), lambda i, j: (0, jnp.maximum(j - n_blk, 0))),
                  full(gv), once(ws), once(bias)],
        args=(x, mod, mod, gpre, w_v, w_u, gv, ws, bias),
        out_specs=[pl.BlockSpec((tm, tn), u_map), pl.BlockSpec((tm, d), lambda i, j: (i, 0))],
        out_shape=[jax.ShapeDtypeStruct((m, d), _BF16), jax.ShapeDtypeStruct((m, d), _BF16)],
        scratch=[pltpu.VMEM((tm, d), _F32), pltpu.SemaphoreType.DMA(()),
                 pltpu.VMEM((n_blk, tm, tn), _F32), pltpu.VMEM((d // GROUP, CHUNK, CHUNK), _BF16)],
        side=side, name="gmlp_prompt", vmem_limit_bytes=VMEM_LIMIT_BYTES_WIDE)
    return ya, h, copies


def _p_shortconv_kernel(h_ref, wbg_ref, wcg_ref, wxb_ref, cw_ref, yb_ref, tail_ref, carry_scr,
                        *, tm, tiles_per_seq, slab, side):
    i = pl.program_id(0)
    j = pl.program_id(1)

    @pl.when(i % tiles_per_seq == 0)
    def _():
        carry_scr[j] = jnp.zeros(carry_scr.shape[1:], _F32)

    _convert(side)
    prev = carry_scr[j]
    cw = cw_ref.at[j]
    for r in _slabs(tm, slab):
        h = h_ref[r, :]
        bg = _dot(h, wbg_ref[...])
        p = _dot(h, wcg_ref[...]) * _dot(h, wxb_ref[...])
        yb_ref[r, :] = (bg * _causal_conv_rows(p, prev, cw)).astype(_BF16)
        prev = p[slab - SUBLANES:, :]
    carry_scr[j] = prev
    tail_ref[j] = prev


def _p_shortconv_call(h, w_bcx, cw, *, tm, tn, tiles_per_seq, slab, side):
    m, d = h.shape
    w = cw.shape[1]
    n_blk = w // tn
    wspec = lambda off: pl.BlockSpec((d, tn), lambda i, j: (0, off + j))
    (yb, tail), copies = _run(
        functools.partial(_p_shortconv_kernel, tm=tm, tiles_per_seq=tiles_per_seq, slab=slab),
        grid=(m // tm, n_blk),
        in_specs=[pl.BlockSpec((tm, d), lambda i, j: (i, 0)),
                  wspec(0), wspec(n_blk), wspec(2 * n_blk),
                  pl.BlockSpec((n_blk, CONV_K, tn), lambda i, j: (0, 0, 0))],
        args=(h, w_bcx, w_bcx, w_bcx, _col_blocks(cw, tn)),
        out_specs=[pl.BlockSpec((tm, tn), lambda i, j: (i, j)),
                   pl.BlockSpec((None, n_blk, SUBLANES, tn), lambda i, j: (i, 0, 0, 0))],
        out_shape=[jax.ShapeDtypeStruct((m, w), _BF16),
                   jax.ShapeDtypeStruct((m // tm, n_blk, SUBLANES, tn), _F32)],
        scratch=[pltpu.VMEM((n_blk, SUBLANES, tn), _F32)],
        side=side, name="shortconv_prompt")
    return yb, tail, copies


def _p_proj_kernel(m_ref, wo_ref, x_ref, gt_ref, gpost_ref, sh_ref, sc_ref, gpre_ref,
                   x1_ref, h2_ref, *, tm, slab, tiles_per_seq, mod_row0, side):
    g1 = _mod_row(gt_ref, tiles_per_seq, mod_row0) * gpost_ref[...]
    g2 = gpre_ref[...] * (1.0 + _mod_row(sc_ref, tiles_per_seq, mod_row0))
    sh = _mod_row(sh_ref, tiles_per_seq, mod_row0)
    for r in _slabs(tm, slab):
        y = _dot(m_ref[r, :], wo_ref[...])
        x1 = x_ref[r, :] + _rms(y, g1)
        x1_ref[r, :] = x1
        h2_ref[r, :] = (_rms(x1, g2) + sh).astype(_BF16)


def _p_proj_call(mg, w_o, x, mod, gpost, gpre, *, tm, slab, tiles_per_seq, mod_row0):
    m, d = x.shape
    row = pl.BlockSpec((tm, d), lambda i: (i, 0))
    vec = pl.BlockSpec((1, d), lambda i: (0, 0))
    mspec = lambda k: _mod_row_spec(d, k, tiles_per_seq, mod_row0, 1)
    (x1, h2), _ = _run(
        functools.partial(_p_proj_kernel, tm=tm, slab=slab, tiles_per_seq=tiles_per_seq,
                          mod_row0=mod_row0),
        grid=(m // tm,),
        in_specs=[row, pl.BlockSpec((d, d), lambda i: (0, 0)), row, mspec(2), vec, mspec(3),
                  mspec(4), vec],
        args=(mg, w_o, x, mod, gpost, mod, mod, gpre),
        out_specs=[row, row],
        out_shape=[jax.ShapeDtypeStruct((m, d), _F32), jax.ShapeDtypeStruct((m, d), _BF16)],
        name="out_proj_prompt")
    return x1, h2


def _p_ffn_kernel(h_ref, wa_ref, wb_ref, cw_ref, wd_ref, x1_hbm, gt_ref, gpost_ref,
                  out_ref, tail_ref, x1_buf, x1_sem, carry_scr, *, tm, n_blk, tiles_per_seq, slab,
                  mod_row0, side):
    i = pl.program_id(0)
    j = pl.program_id(1)
    x1_copy = pltpu.make_async_copy(x1_hbm.at[pl.ds(pl.multiple_of(i * tm, tm), tm), :],
                                    x1_buf, x1_sem)

    @pl.when(i % tiles_per_seq == 0)
    def _():
        carry_scr[j] = jnp.zeros(carry_scr.shape[1:], _F32)

    def step(first, last):
        prev = carry_scr[j]
        cw = cw_ref.at[j]
        slabs = _slabs(tm, slab)
        if last:
            g_last = _mod_row(gt_ref, tiles_per_seq, mod_row0) * gpost_ref[...]
        gs = []
        for r in slabs:
            h = h_ref[r, :]
            a = _dot(h, wa_ref[...])
            b = _dot(h, wb_ref[...])
            gs.append((jax.nn.gelu(_causal_conv_rows(a, prev, cw)) * b).astype(_BF16))
            prev = a[slab - SUBLANES:, :]
        for r, g in zip(slabs, gs):
            f = _dot(g, wd_ref[...])
            acc = f if first else out_ref[r, :] + f
            if last:
                acc = x1_buf[r, :] + _rms(acc, g_last)
            out_ref[r, :] = acc
        carry_scr[j] = prev
        tail_ref[j] = prev

    @pl.when(j == 0)
    def _():
        x1_copy.start()
        step(True, False)

    @pl.when((j > 0) & (j < n_blk - 1))
    def _():
        step(False, False)

    @pl.when(j == n_blk - 1)
    def _():
        x1_copy.wait()
        step(False, True)


def _p_ffn_call(h2, w_a, w_b, cw, w_down, x1, mod, gpost, *, tm, tn, tiles_per_seq, slab, mod_row0):
    m, d = x1.shape
    f = cw.shape[1]
    n_blk = f // tn
    row = pl.BlockSpec((tm, d), lambda i, j: (i, 0))
    wspec = pl.BlockSpec((d, tn), lambda i, j: (0, j))
    (out, tail), _ = _run(
        functools.partial(_p_ffn_kernel, tm=tm, n_blk=n_blk, tiles_per_seq=tiles_per_seq, slab=slab,
                          mod_row0=mod_row0),
        grid=(m // tm, n_blk),
        in_specs=[row, wspec, wspec,
                  pl.BlockSpec((n_blk, CONV_K, tn), lambda i, j: (0, 0, 0)),
                  pl.BlockSpec((tn, d), lambda i, j: (j, 0)),
                  pl.BlockSpec(memory_space=pl.ANY),
                  _mod_row_spec(d, 5, tiles_per_seq, mod_row0, 2),
                  pl.BlockSpec((1, d), lambda i, j: (0, 0))],
        args=(h2, w_a, w_b, _col_blocks(cw, tn), w_down, x1, mod, gpost),
        out_specs=[row, pl.BlockSpec((None, n_blk, SUBLANES, tn), lambda i, j: (i, 0, 0, 0))],
        out_shape=[jax.ShapeDtypeStruct((m, d), _F32),
                   jax.ShapeDtypeStruct((m // tm, n_blk, SUBLANES, tn), _F32)],
        scratch=[pltpu.VMEM((tm, d), _F32), pltpu.SemaphoreType.DMA(()),
                 pltpu.VMEM((n_blk, SUBLANES, tn), _F32)],
        name="convffn_prompt")
    return out, tail


def _cols(src, first_col, n_cols, n_blocks):
    block = n_cols // n_blocks
    assert block * n_blocks == n_cols and block % 128 == 0 and first_col % block == 0
    return _Side(src, 1, first_col // block, block, n_blocks)


def _layer(xs, xp, mod, st_b, st_f, p, *, seq_len, mod_row0):
    d = p["g_v"].shape[1]
    wb = p["conv_b_w"].shape[1]
    f = p["conv_f_w"].shape[1]
    tn_s = SAMPLE_COL_TILE

    def tiles(name):
        tm, tn, slab = PROMPT_TILES[name]
        return dict(tm=tm, tn=tn, slab=slab, tiles_per_seq=seq_len // tm)

    def n_steps(name):
        tm, tn, _ = PROMPT_TILES[name]
        cols = {"gmlp": 2 * d, "shortconv": wb, "merge": d, "ffn": f}[name]
        return xp.shape[0] // tm * (cols // tn)

    def without(kw, *names):
        return {k: v for k, v in kw.items() if k not in names}

    def blocks(n_cols, budget):
        return max(n for n in range(1, budget + 1) if n_cols % (128 * n) == 0)

    w_in, w_up = p["w_in"], p["w_up"]

    ya_s, h1_s, vn, w_v, w_u = _s_gmlp_call(xs, mod, p["g_pre_mix"], w_in, p["g_v"], p["wvec"],
                                            p["bvec"], tn=tn_s)
    n = n_steps("gmlp")
    ya_p, h1_p, (w_bcx,) = _p_gmlp_call(
        xp, mod, p["g_pre_mix"], w_v, w_u, p["g_v"], p["w_s"], p["bias"], mod_row0=mod_row0,
        side=(_cols(w_in, 2 * d, 3 * wb, blocks(3 * wb, n)),),
        **tiles("gmlp"))

    yb_s, new_b = _s_shortconv_call(h1_s, w_bcx, p["conv_b_w"], st_b, tn=tn_s)
    n = n_steps("shortconv")
    yb_p, tail_b, (w_gate, w_oa, w_ob, w_o, w_a, w_b) = _p_shortconv_call(
        h1_p, w_bcx, p["conv_b_w"],
        side=(_cols(w_in, 2 * d + 3 * wb, 2 * d, blocks(2 * d, n)),
              _cols(p["w_out_a"], 0, d, blocks(d, n)), _cols(p["w_out_b"], 0, d, blocks(d, n)),
              _cols(p["w_o"], 0, d, blocks(d, n)),
              _cols(w_up, 0, f, blocks(f, n)), _cols(w_up, f, f, blocks(f, n))),
        **tiles("shortconv"))

    tm_s = h1_s.shape[0]
    mg_s, _ = _merge_call(h1_s, ya_s, yb_s, w_gate, w_oa, w_ob, tm=tm_s, tn=tn_s, slab=tm_s // 2,
                          name="gated_merge_sample")
    n_row_blocks = max(k for k in range(1, n_steps("merge") + 1)
                       if f % k == 0 and (f // k) % (2 * SUBLANES) == 0)
    mg_p, (w_d,) = _merge_call(h1_p, ya_p, yb_p, w_gate, w_oa, w_ob, name="gated_merge_prompt",
                               side=(_Side(p["w_down"], 0, 0, f // n_row_blocks, n_row_blocks),),
                               **without(tiles("merge"), "tiles_per_seq"))

    x1_s, h2_s = _s_proj_call(mg_s, w_o, xs, mod, p["g_post_mix"], p["g_pre_ffn"], tn=tn_s)
    x1_p, h2_p = _p_proj_call(mg_p, w_o, xp, mod, p["g_post_mix"], p["g_pre_ffn"],
                              mod_row0=mod_row0, **without(tiles("proj"), "tn"))

    out_s, new_f = _s_ffn_call(h2_s, w_a, w_b, p["conv_f_w"], w_d, x1_s, mod, p["g_post_ffn"],
                               st_f, tn=tn_s)
    out_p, tail_f = _p_ffn_call(h2_p, w_a, w_b, p["conv_f_w"], w_d, x1_p, mod, p["g_post_ffn"],
                                mod_row0=mod_row0, **tiles("ffn"))
    return out_s, out_p, new_b, new_f, vn, tail_b, tail_f


def kernel(x_prompt, x_sample, c_prompt, c_sample, state_conv_b, state_conv_ffn, w_ada, b_ada, g_pre_mix, g_post_mix, w_in, g_v, w_s, b_s, conv_b_w, w_out_a, w_out_b, w_o, g_pre_ffn, g_post_ffn, w_up, conv_f_w, w_down):
    depth = w_in.shape[0]
    bp, seq, d = x_prompt.shape
    bs, tdec, _ = x_sample.shape
    n_groups = w_s.shape[1]
    assert bs == CHUNK and tdec <= CHUNK
    assert all(seq % tm == 0 for tm, _, _ in PROMPT_TILES.values())

    xp = x_prompt.reshape(bp * seq, d)
    xs = x_sample
    pad = (-(bp + bs)) % SUBLANES
    c_all = jnp.concatenate([c_sample, c_prompt, jnp.zeros((pad, d), _F32)], axis=0)

    pb, sb, pf, sf, sv = [], [], [], [], []
    for l in range(depth):
        mod, _ = _mod_call(c_all, w_ada[l], b_ada[l][None, :], side=())
        vec = lambda a: a[l][None, :]
        bias_full = jnp.repeat(jnp.transpose(b_s[l]), GROUP, axis=1)
        wvec = jnp.repeat(
            jnp.transpose(w_s[l][:, :tdec, :tdec], (1, 2, 0)).reshape(tdec * tdec, n_groups),
            GROUP, axis=1)
        p = {
            "w_in": w_in[l], "w_out_a": w_out_a[l], "w_out_b": w_out_b[l],
            "w_o": w_o[l], "w_up": w_up[l], "w_down": w_down[l],
            "g_pre_mix": vec(g_pre_mix), "g_post_mix": vec(g_post_mix), "g_v": vec(g_v),
            "g_pre_ffn": vec(g_pre_ffn), "g_post_ffn": vec(g_post_ffn),
            "conv_b_w": conv_b_w[l], "conv_f_w": conv_f_w[l],
            "w_s": w_s[l], "bias": bias_full, "wvec": wvec, "bvec": bias_full[:tdec],
        }
        xs, xp, sbt, sft, vn, tb, tf = _layer(xs, xp, mod, state_conv_b[l], state_conv_ffn[l], p,
                                              seq_len=seq, mod_row0=bs)

        def prompt_tail(t):
            n_tiles, n_blk, _, tn = t.shape
            t = t.reshape(bp, n_tiles // bp, n_blk, SUBLANES, tn)[:, -1, :, SUBLANES - (CONV_K - 1):, :]
            return jnp.transpose(t, (0, 2, 1, 3)).reshape(bp, CONV_K - 1, n_blk * tn)

        pb.append(prompt_tail(tb))
        pf.append(prompt_tail(tf))
        sb.append(sbt)
        sf.append(sft)
        sv.append(vn)

    y_prompt = xp.reshape(bp, seq, d)
    return (y_prompt, xs, jnp.stack(pb), jnp.stack(sb), jnp.stack(pf), jnp.stack(sf),
            jnp.stack(sv))
```

```python
import functools
from typing import NamedTuple

import jax
import jax.numpy as jnp
from jax import lax
from jax.experimental import pallas as pl
from jax.experimental.pallas import tpu as pltpu

EPS = 1e-6
CHUNK = 128
GROUP = 128
CONV_K = 3
SUBLANES = 8
VMEM_LIMIT_BYTES = 56 * 1024 * 1024
VMEM_LIMIT_BYTES_WIDE = 60 * 1024 * 1024
PROMPT_TILES = {
    "gmlp": (1024, 1024, 256),
    "shortconv": (1024, 512, 256),
    "merge": (512, 1024, 256),
    "proj": (512, None, 256),
    "ffn": (1024, 512, 256),
}
SAMPLE_COL_TILE = 512
ADALN_COL_TILE = 1024

_BF16 = jnp.bfloat16
_F32 = jnp.float32


def _dot(a, b):
    return jnp.dot(a, b, preferred_element_type=_F32)


def _rms(xf, g):
    ms = jnp.mean(xf * xf, axis=-1, keepdims=True)
    return xf * lax.rsqrt(ms + EPS) * g


def _causal_conv_rows(p, prev, cw_ref):
    r1 = pltpu.roll(p, 1, 0)
    r2 = pltpu.roll(p, 2, 0)
    row = lax.broadcasted_iota(jnp.int32, (SUBLANES, 1), 0)
    head1 = jnp.where(row == 0, prev[7:8, :], r1[:SUBLANES, :])
    head2 = jnp.where(row == 0, prev[6:7, :], jnp.where(row == 1, prev[7:8, :], r2[:SUBLANES, :]))
    m1 = jnp.concatenate([head1, r1[SUBLANES:, :]], axis=0)
    m2 = jnp.concatenate([head2, r2[SUBLANES:, :]], axis=0)
    return cw_ref[0:1, :] * m2 + cw_ref[1:2, :] * m1 + cw_ref[2:3, :] * p


def _causal_conv_slabs(x, prev, cw_ref, slab):
    w0, w1, w2 = cw_ref[0:1, :], cw_ref[1:2, :], cw_ref[2:3, :]
    seq = list(prev) + [x[t * slab:(t + 1) * slab, :] for t in range(x.shape[0] // slab)]
    y = [w0 * seq[t] + w1 * seq[t + 1] + w2 * seq[t + 2] for t in range(len(seq) - 2)]
    return jnp.concatenate(y, axis=0), seq[-2:]


def _slabs(tm, slab):
    return [slice(s * slab, (s + 1) * slab) for s in range(tm // slab)]


def _mod_row_spec(d, k, tiles_per_seq, mod_row0, n_grid_axes):
    idx = lambda i: ((mod_row0 + i // tiles_per_seq) // SUBLANES, k)
    return pl.BlockSpec((SUBLANES, d), (lambda i, j: idx(i)) if n_grid_axes == 2 else idx)


def _mod_row(ref, tiles_per_seq, mod_row0):
    b = (mod_row0 + pl.program_id(0) // tiles_per_seq) % SUBLANES
    return ref[pl.ds(b, 1), :]


def _col_blocks(a, tn):
    r, c = a.shape
    return jnp.transpose(a.reshape(r, c // tn, tn), (1, 0, 2))


def _post_mix(y, x, gt, gpost, sh, sc, gpre):
    x1 = x + gt * _rms(y, gpost)
    h2 = _rms(x1, gpre) * (1.0 + sc) + sh
    return x1, h2.astype(_BF16)


class _Side(NamedTuple):
    src: jax.Array
    axis: int
    start: int
    block: int
    n_blocks: int


def _convert(side):
    for src_ref, dst_ref in side:
        dst_ref[...] = src_ref[...].astype(_BF16)


def _run(body, *, grid, in_specs, args, out_specs, out_shape, scratch=(), side=(), name,
         vmem_limit_bytes=VMEM_LIMIT_BYTES):
    n_in, n_out, n_side = len(args), len(out_shape), len(side)
    n_steps = functools.reduce(lambda a, b: a * b, grid)
    step_of = (lambda i: i) if len(grid) == 1 else (lambda i, j: i * grid[1] + j)
    side_in, side_out, side_shape = [], [], []
    for s in side:
        assert s.n_blocks <= n_steps, (name, s.n_blocks, n_steps)
        other = s.src.shape[1 - s.axis]
        pos = lambda *ids, s=s: jnp.minimum(step_of(*ids), s.n_blocks - 1)
        if s.axis == 1:
            blk, full = (other, s.block), (other, s.block * s.n_blocks)
            side_in.append(pl.BlockSpec(blk, lambda *ids, s=s, pos=pos: (0, s.start + pos(*ids))))
            side_out.append(pl.BlockSpec(blk, lambda *ids, pos=pos: (0, pos(*ids))))
        else:
            blk, full = (s.block, other), (s.block * s.n_blocks, other)
            side_in.append(pl.BlockSpec(blk, lambda *ids, s=s, pos=pos: (s.start + pos(*ids), 0)))
            side_out.append(pl.BlockSpec(blk, lambda *ids, pos=pos: (pos(*ids), 0)))
        side_shape.append(jax.ShapeDtypeStruct(full, _BF16))

    def kern(*refs):
        o0 = n_in + n_side
        s0 = o0 + n_out + n_side
        body(*refs[:n_in], *refs[o0:o0 + n_out], *refs[s0:],
             side=tuple(zip(refs[n_in:o0], refs[o0 + n_out:s0])))

    res = pl.pallas_call(
        kern, grid=grid,
        in_specs=list(in_specs) + side_in, out_specs=list(out_specs) + side_out,
        out_shape=list(out_shape) + side_shape, scratch_shapes=list(scratch),
        compiler_params=pltpu.CompilerParams(dimension_semantics=("arbitrary",) * len(grid),
                                             vmem_limit_bytes=vmem_limit_bytes),
        name=name,
    )(*args, *[s.src for s in side])
    return res[:n_out], res[n_out:]


def _mod_kernel(c_ref, w_ref, b_ref, o_ref, *, side):
    _convert(side)
    c = c_ref[...]
    a = (c * jax.nn.sigmoid(c)).astype(_BF16)
    o_ref[...] = _dot(a, w_ref[...].astype(_BF16)) + b_ref[...]


def _mod_call(c_all, w_ada, b_ada, *, side):
    rows, d = c_all.shape
    n = w_ada.shape[1]
    tn = ADALN_COL_TILE
    (mod,), copies = _run(
        _mod_kernel, grid=(n // tn,),
        in_specs=[pl.BlockSpec((rows, d), lambda j: (0, 0)),
                  pl.BlockSpec((d, tn), lambda j: (0, j)),
                  pl.BlockSpec((1, tn), lambda j: (0, j))],
        args=(c_all, w_ada, b_ada),
        out_specs=[pl.BlockSpec((rows, tn), lambda j: (0, j))],
        out_shape=[jax.ShapeDtypeStruct((rows, n), _F32)],
        side=side, name="adaln_mod")
    return mod, copies


def _s_gmlp_kernel(x_hbm, sh_ref, sc_ref, gpre_ref, wv_ref, wu_ref, gv_ref, ws_ref, bias_ref,
                   ya_ref, h_ref, vn_hbm, wvb_ref, wub_ref, v_scr, x_buf, vn_buf, in_sem, out_sem,
                   *, tm, tn, n_blk, slab, side):
    j = pl.program_id(0)
    d = n_blk * tn
    slabs = _slabs(tm, slab)
    n_slab = len(slabs)
    x_copy = lambda t: pltpu.make_async_copy(x_hbm.at[:, t, :], x_buf.at[t], in_sem.at[t])
    vn_copy = lambda t: pltpu.make_async_copy(vn_buf.at[t], vn_hbm.at[:, t, :], out_sem.at[t])

    @pl.when(j == 0)
    def _():
        for t in range(n_slab):
            x_copy(t).start()
        for t, r in enumerate(slabs):
            x_copy(t).wait()
            h_ref[r, :] = (_rms(x_buf[t], gpre_ref[...]) * (1.0 + sc_ref[...])
                           + sh_ref[...]).astype(_BF16)

    @pl.when(j < n_blk)
    def _():
        wv = wv_ref[...].astype(_BF16)
        wvb_ref[...] = wv
        v_scr[j] = _dot(h_ref[...], wv)

    @pl.when(j == n_blk)
    def _():
        for t, r in enumerate(slabs):
            ss = 0.0
            for k in range(n_blk):
                vk = v_scr[k, r, :]
                ss = ss + jnp.sum(vk * vk, axis=-1, keepdims=True)
            rs = lax.rsqrt(ss * (1.0 / d) + EPS)
            for k in range(n_blk):
                vn = v_scr[k, r, :] * rs * gv_ref[:, k * tn:(k + 1) * tn]
                v_scr[k, r, :] = vn
                vn_buf[t, :, k * tn:(k + 1) * tn] = vn
            vn_copy(t).start()
        for t in reversed(range(n_slab)):
            for k in range(n_blk):
                c = slice(k * tn, (k + 1) * tn)
                acc = ws_ref[t * n_slab:t * n_slab + 1, c] * v_scr[k, slabs[0], :]
                for s in range(1, t + 1):
                    acc = acc + ws_ref[t * n_slab + s:t * n_slab + s + 1, c] * v_scr[k, slabs[s], :]
                v_scr[k, slabs[t], :] = acc + bias_ref[t:t + 1, c]

    @pl.when(j >= n_blk)
    def _():
        wu = wu_ref[...].astype(_BF16)
        wub_ref[...] = wu
        ya_ref[...] = (_dot(h_ref[...], wu) * v_scr[j - n_blk]).astype(_BF16)

    @pl.when(j == 2 * n_blk - 1)
    def _():
        for t in range(n_slab):
            vn_copy(t).wait()


def _s_gmlp_call(x, mod, gpre, w_in, gv, wvec, bvec, *, tn):
    slab, n_slab, d = x.shape
    tm = n_slab * slab
    n_blk = d // tn
    anywhere = pl.BlockSpec(memory_space=pl.ANY)
    full = lambda a: pl.BlockSpec(a.shape, lambda j: (0,) * a.ndim)
    u_map = lambda j: (0, jnp.maximum(j - n_blk, 0))
    v_map = lambda j: (0, jnp.minimum(j, n_blk - 1))
    (ya, h, vn, w_v, w_u), _ = _run(
        functools.partial(_s_gmlp_kernel, tm=tm, tn=tn, n_blk=n_blk, slab=slab),
        grid=(2 * n_blk,),
        in_specs=[anywhere,
                  pl.BlockSpec((slab, d), lambda j: (0, 0)), pl.BlockSpec((slab, d), lambda j: (0, 1)),
                  full(gpre),
                  pl.BlockSpec((d, tn), lambda j: (0, n_blk + jnp.minimum(j, n_blk - 1))),
                  pl.BlockSpec((d, tn), u_map),
                  full(gv), full(wvec), full(bvec)],
        args=(x, mod, mod, gpre, w_in, w_in, gv, wvec, bvec),
        out_specs=[pl.BlockSpec((tm, tn), u_map), pl.BlockSpec((tm, d), lambda j: (0, 0)), anywhere,
                   pl.BlockSpec((d, tn), v_map), pl.BlockSpec((d, tn), u_map)],
        out_shape=[jax.ShapeDtypeStruct((tm, d), _BF16), jax.ShapeDtypeStruct((tm, d), _BF16),
                   jax.ShapeDtypeStruct(x.shape, _F32),
                   jax.ShapeDtypeStruct((d, d), _BF16), jax.ShapeDtypeStruct((d, d), _BF16)],
        scratch=[pltpu.VMEM((n_blk, tm, tn), _F32),
                 pltpu.VMEM((n_slab, slab, d), _F32), pltpu.VMEM((n_slab, slab, d), _F32),
                 pltpu.SemaphoreType.DMA((n_slab,)), pltpu.SemaphoreType.DMA((n_slab,))],
        name="gmlp_sample")
    return ya, h, vn, w_v, w_u


class _StateIO:
    def __init__(self, st_hbm, new_hbm, st_buf, new_buf, in_sem, out_sem, n_blk, tn):
        self.n_rows, self.n_blk = st_buf.shape[0], n_blk
        col = lambda c: pl.ds(c * tn, tn) if isinstance(c, int) else pl.ds(pl.multiple_of(c * tn, tn), tn)
        self.load = lambda k, c: pltpu.make_async_copy(st_hbm.at[:, k, col(c)], st_buf.at[k, c],
                                                       in_sem.at[k, c])
        self.store = lambda k, c: pltpu.make_async_copy(new_buf.at[k, c], new_hbm.at[:, k, col(c)],
                                                        out_sem.at[k, c])
        self.st_buf, self.new_buf = st_buf, new_buf

    def read(self, j):
        @pl.when(j == 0)
        def _():
            for k in range(self.n_rows):
                for c in range(self.n_blk):
                    self.load(k, c).start()
        for k in range(self.n_rows):
            self.load(k, j).wait()
        return [self.st_buf[k, j] for k in range(self.n_rows)]

    def write(self, j, rows):
        for k, row in enumerate(rows):
            self.new_buf[k, j] = row
            self.store(k, j).start()

        @pl.when(j == self.n_blk - 1)
        def _():
            for k in range(self.n_rows):
                for c in range(self.n_blk):
                    self.store(k, c).wait()


def _state_scratch(state, n_blk, tn):
    slab, n_rows, _ = state.shape
    buf = pltpu.VMEM((n_rows, n_blk, slab, tn), _F32)
    sem = pltpu.SemaphoreType.DMA((n_rows, n_blk))
    return [buf, buf, sem, sem]


def _s_shortconv_kernel(h_ref, wbg_ref, wcg_ref, wxb_ref, cw_ref, st_hbm, yb_ref, new_hbm,
                        st_buf, new_buf, in_sem, out_sem, *, slab, n_blk, tn, side):
    j = pl.program_id(0)
    io = _StateIO(st_hbm, new_hbm, st_buf, new_buf, in_sem, out_sem, n_blk, tn)
    prev = io.read(j)
    h = h_ref[...]
    bg = _dot(h, wbg_ref[...])
    p = _dot(h, wcg_ref[...]) * _dot(h, wxb_ref[...])
    cb, tail = _causal_conv_slabs(p, prev, cw_ref, slab)
    yb_ref[...] = (bg * cb).astype(_BF16)
    io.write(j, tail)


def _s_shortconv_call(h, w_bcx, cw, state, *, tn):
    tm, d = h.shape
    slab = state.shape[0]
    w = cw.shape[1]
    n_blk = w // tn
    wspec = lambda off: pl.BlockSpec((d, tn), lambda j: (0, off + j))
    anywhere = pl.BlockSpec(memory_space=pl.ANY)
    (yb, new_state), _ = _run(
        functools.partial(_s_shortconv_kernel, slab=slab, n_blk=n_blk, tn=tn),
        grid=(n_blk,),
        in_specs=[pl.BlockSpec((tm, d), lambda j: (0, 0)),
                  wspec(0), wspec(n_blk), wspec(2 * n_blk),
                  pl.BlockSpec((CONV_K, tn), lambda j: (0, j)), anywhere],
        args=(h, w_bcx, w_bcx, w_bcx, cw, state),
        out_specs=[pl.BlockSpec((tm, tn), lambda j: (0, j)), anywhere],
        out_shape=[jax.ShapeDtypeStruct((tm, w), _BF16), jax.ShapeDtypeStruct(state.shape, _F32)],
        scratch=_state_scratch(state, n_blk, tn),
        name="shortconv_sample")
    return yb, new_state


def _merge_kernel(h_ref, ya_ref, yb_ref, wga_ref, wgb_ref, woa_ref, wob_ref, m_ref, *, tm, slab, side):
    _convert(side)
    for r in _slabs(tm, slab):
        h = h_ref[r, :]
        ga = jax.nn.sigmoid(_dot(h, wga_ref[...]))
        gb = jax.nn.sigmoid(_dot(h, wgb_ref[...]))
        m = ga * _dot(ya_ref[r, :], woa_ref[...]) + gb * _dot(yb_ref[r, :], wob_ref[...])
        m_ref[r, :] = m.astype(_BF16)


def _merge_call(h, ya, yb, w_gate, w_out_a, w_out_b, *, tm, tn, slab, side=(), name):
    m, d = h.shape
    n_blk = d // tn
    row = pl.BlockSpec((tm, d), lambda i, j: (i, 0))
    wspec = lambda off: pl.BlockSpec((d, tn), lambda i, j: (0, off + j))
    (mg,), copies = _run(
        functools.partial(_merge_kernel, tm=tm, slab=slab),
        grid=(m // tm, n_blk),
        in_specs=[row, row, row, wspec(0), wspec(n_blk), wspec(0), wspec(0)],
        args=(h, ya, yb, w_gate, w_gate, w_out_a, w_out_b),
        out_specs=[pl.BlockSpec((tm, tn), lambda i, j: (i, j))],
        out_shape=[jax.ShapeDtypeStruct((m, d), _BF16)],
        side=side, name=name)
    return mg, copies


def _s_merge_proj_kernel(h_ref, ya_ref, yb_ref, wga_ref, wgb_ref, woa_ref, wob_ref, wo_ref, x_hbm,
                         gt_ref, gpost_ref, sh_ref, sc_ref, gpre_ref, x1_ref, h2_ref,
                         y_scr, x_buf, in_sem, *, tm, n_blk, slab, side):
    j = pl.program_id(0)
    n_slab = tm // slab
    x_copy = lambda t: pltpu.make_async_copy(x_hbm.at[:, t, :], x_buf.at[t], in_sem.at[t])

    @pl.when(j == 0)
    def _():
        for t in range(n_slab):
            x_copy(t).start()
        y_scr[...] = jnp.zeros(y_scr.shape, _F32)

    chains = _slabs(tm, 2 * slab)
    ms = []
    for r in chains:
        h = h_ref[r, :]
        ga = jax.nn.sigmoid(_dot(h, wga_ref[...]))
        gb = jax.nn.sigmoid(_dot(h, wgb_ref[...]))
        m = ga * _dot(ya_ref[r, :], woa_ref[...]) + gb * _dot(yb_ref[r, :], wob_ref[...])
        ms.append(m.astype(_BF16))
    for r, m in zip(chains, ms):
        y_scr[r, :] += _dot(m, wo_ref[...])

    @pl.when(j == n_blk - 1)
    def _():
        for t, r in enumerate(_slabs(tm, slab)):
            x_copy(t).wait()
            x1, h2 = _post_mix(y_scr[r, :], x_buf[t], gt_ref[...], gpost_ref[...],
                               sh_ref[...], sc_ref[...], gpre_ref[...])
            x1_ref[r, :] = x1
            h2_ref[r, :] = h2


def _s_merge_proj_call(h, ya, yb, w_gate, w_out_a, w_out_b, w_o, x, mod, gpost, gpre, *, tn):
    tm, d = h.shape
    slab = x.shape[0]
    n_blk = d // tn
    row_in = pl.BlockSpec((tm, d), lambda j: (0, 0), pipeline_mode=pl.Buffered(1))
    row_out = pl.BlockSpec((tm, d), lambda j: (0, 0))
    wspec = lambda off: pl.BlockSpec((d, tn), lambda j: (0, off + j))
    vec = pl.BlockSpec((1, d), lambda j: (0, 0))
    mspec = lambda k: pl.BlockSpec((slab, d), lambda j: (0, k))
    (x1, h2), _ = _run(
        functools.partial(_s_merge_proj_kernel, tm=tm, n_blk=n_blk, slab=slab),
        grid=(n_blk,),
        in_specs=[row_in, row_in, row_in, wspec(0), wspec(n_blk), wspec(0), wspec(0),
                  pl.BlockSpec((tn, d), lambda j: (j, 0)), pl.BlockSpec(memory_space=pl.ANY),
                  mspec(2), vec, mspec(3), mspec(4), vec],
        args=(h, ya, yb, w_gate, w_gate, w_out_a, w_out_b, w_o, x, mod, gpost, mod, mod, gpre),
        out_specs=[row_out, row_out],
        out_shape=[jax.ShapeDtypeStruct((tm, d), _F32), jax.ShapeDtypeStruct((tm, d), _BF16)],
        scratch=[pltpu.VMEM((tm, d), _F32), pltpu.VMEM((tm // slab, slab, d), _F32),
                 pltpu.SemaphoreType.DMA((tm // slab,))],
        name="merge_proj_sample")
    return x1, h2


def _s_ffn_kernel(h_ref, wa_ref, wb_ref, cw_ref, wd_ref, x1_ref, gt_ref, gpost_ref, st_hbm,
                  out_hbm, new_hbm, acc_scr, out_sem, st_buf, new_buf, st_in_sem, st_out_sem,
                  *, tm, tn, n_blk, slab, side):
    j = pl.program_id(0)
    n_slab = tm // slab
    out_copy = lambda t: pltpu.make_async_copy(acc_scr.at[t], out_hbm.at[:, t, :], out_sem.at[t])
    io = _StateIO(st_hbm, new_hbm, st_buf, new_buf, st_in_sem, st_out_sem, n_blk, tn)

    @pl.when(j == 0)
    def _():
        acc_scr[...] = jnp.zeros(acc_scr.shape, _F32)

    prev = io.read(j)
    gs = []
    for r in _slabs(tm, 2 * slab):
        h = h_ref[r, :]
        ac, prev = _causal_conv_slabs(_dot(h, wa_ref[...]), prev, cw_ref, slab)
        gs.append((jax.nn.gelu(ac) * _dot(h, wb_ref[...])).astype(_BF16))
    for c, g in enumerate(gs):
        f = _dot(g, wd_ref[...])
        acc_scr[2 * c] += f[:slab, :]
        acc_scr[2 * c + 1] += f[slab:, :]
    io.write(j, prev)

    @pl.when(j == n_blk - 1)
    def _():
        for t, r in enumerate(_slabs(tm, slab)):
            acc_scr[t] = x1_ref[r, :] + gt_ref[...] * _rms(acc_scr[t], gpost_ref[...])
            out_copy(t).start()
        for t in range(n_slab):
            out_copy(t).wait()


def _s_ffn_call(h2, w_a, w_b, cw, w_down, x1, mod, gpost, state, *, tn):
    tm, d = x1.shape
    slab = state.shape[0]
    f = cw.shape[1]
    n_blk = f // tn
    out_shape = (slab, tm // slab, d)
    row = pl.BlockSpec((tm, d), lambda j: (0, 0))
    wspec = pl.BlockSpec((d, tn), lambda j: (0, j))
    anywhere = pl.BlockSpec(memory_space=pl.ANY)
    (out, new_state), _ = _run(
        functools.partial(_s_ffn_kernel, tm=tm, tn=tn, n_blk=n_blk, slab=slab),
        grid=(n_blk,),
        in_specs=[row, wspec, wspec, pl.BlockSpec((CONV_K, tn), lambda j: (0, j)),
                  pl.BlockSpec((tn, d), lambda j: (j, 0)), row,
                  pl.BlockSpec((slab, d), lambda j: (0, 5)), pl.BlockSpec((1, d), lambda j: (0, 0)),
                  anywhere],
        args=(h2, w_a, w_b, cw, w_down, x1, mod, gpost, state),
        out_specs=[anywhere, anywhere],
        out_shape=[jax.ShapeDtypeStruct(out_shape, _F32), jax.ShapeDtypeStruct(state.shape, _F32)],
        scratch=[pltpu.VMEM((tm // slab, slab, d), _F32), pltpu.SemaphoreType.DMA((tm // slab,))]
        + _state_scratch(state, n_blk, tn),
        name="convffn_sample")
    return out, new_state


def _p_gmlp_kernel(x_hbm, sh_ref, sc_ref, gpre_ref, wv_ref, wu_ref, gv_ref, ws_ref, bias_ref,
                   ya_ref, h_ref, x_buf, x_sem, v_scr, wt_scr, *, tm, tn, n_blk, slab, tiles_per_seq,
                   mod_row0, side):
    i = pl.program_id(0)
    j = pl.program_id(1)
    d = n_blk * tn
    slabs = _slabs(tm, slab)

    def x_copy(tile):
        return pltpu.make_async_copy(x_hbm.at[pl.ds(pl.multiple_of(tile * tm, tm), tm), :],
                                     x_buf, x_sem)

    @pl.when((i == 0) & (j == 0))
    def _():
        x_copy(0).start()
        tril = (lax.broadcasted_iota(jnp.int32, (CHUNK, CHUNK), 0)
                >= lax.broadcasted_iota(jnp.int32, (CHUNK, CHUNK), 1))
        for g in range(d // GROUP):
            wt_scr[g] = jnp.where(tril, ws_ref[g], 0.0).astype(_BF16)

    @pl.when(j == 0)
    def _():
        x_copy(i).wait()
        _convert(side)
        g = gpre_ref[...] * (1.0 + _mod_row(sc_ref, tiles_per_seq, mod_row0))
        sh = _mod_row(sh_ref, tiles_per_seq, mod_row0)
        for r in slabs:
            h = (_rms(x_buf[r, :], g) + sh).astype(_BF16)
            h_ref[r, :] = h
            v_scr[0, r, :] = _dot(h, wv_ref[...])

    @pl.when((j == 0) & (i + 1 < pl.num_programs(0)))
    def _():
        x_copy(i + 1).start()

    @pl.when((j > 0) & (j < n_blk))
    def _():
        _convert(side)
        for r in slabs:
            v_scr[j, r, :] = _dot(h_ref[r, :], wv_ref[...])

    def _gate():
        gpb = tn // GROUP
        for c in range(tm // CHUNK):
            r = slice(c * CHUNK, (c + 1) * CHUNK)
            ss = 0.0
            for k in range(n_blk):
                vk = v_scr[k, r, :]
                ss = ss + jnp.sum(vk * vk, axis=-1, keepdims=True)
            rs = lax.rsqrt(ss * (1.0 / d) + EPS)
            for k in range(n_blk):
                vb = (v_scr[k, r, :] * rs * gv_ref[:, k * tn:(k + 1) * tn]).astype(_BF16)
                for gg in range(gpb):
                    g = k * gpb + gg
                    lanes = slice(gg * GROUP, (gg + 1) * GROUP)
                    v_scr[k, r, lanes] = (_dot(wt_scr[g], vb[:, lanes])
                                          + bias_ref[:, g * GROUP:(g + 1) * GROUP])

    @pl.when(j == n_blk)
    def _():
        _convert(side)
        _gate()
        for r in slabs:
            ya_ref[r, :] = (_dot(h_ref[r, :], wu_ref[...]) * v_scr[0, r, :]).astype(_BF16)

    @pl.when(j > n_blk)
    def _():
        _convert(side)
        for r in slabs:
            ya_ref[r, :] = (_dot(h_ref[r, :], wu_ref[...]) * v_scr[j - n_blk, r, :]).astype(_BF16)


def _p_gmlp_call(x, mod, gpre, w_v, w_u, gv, ws, bias, *, tm, tn, slab, tiles_per_seq, mod_row0,
                 side):
    m, d = x.shape
    n_blk = d // tn
    full = lambda a: pl.BlockSpec(a.shape, lambda i, j: (0,) * a.ndim)
    once = lambda a: pl.BlockSpec(a.shape, lambda i, j: (0,) * a.ndim, pipeline_mode=pl.Buffered(1))
    mspec = lambda k: _mod_row_spec(d, k, tiles_per_seq, mod_row0, 2)
    u_map = lambda i, j: (i, jnp.maximum(j - n_blk, 0))
    (ya, h), copies = _run(
        functools.partial(_p_gmlp_kernel, tm=tm, tn=tn, n_blk=n_blk, slab=slab,
                          tiles_per_seq=tiles_per_seq, mod_row0=mod_row0),
        grid=(m // tm, 2 * n_blk),
        in_specs=[pl.BlockSpec(memory_space=pl.ANY), mspec(0), mspec(1), full(gpre),
                  pl.BlockSpec((d, tn), lambda i, j: (0, jnp.minimum(j, n_blk - 1))),
                  pl.BlockSpec((d, tn), lambda i, j: (0, jnp.maximum(j - n_blk, 0))),
                  full(gv), once(ws), once(bias)],
        args=(x, mod, mod, gpre, w_v, w_u, gv, ws, bias),
        out_specs=[pl.BlockSpec((tm, tn), u_map), pl.BlockSpec((tm, d), lambda i, j: (i, 0))],
        out_shape=[jax.ShapeDtypeStruct((m, d), _BF16), jax.ShapeDtypeStruct((m, d), _BF16)],
        scratch=[pltpu.VMEM((tm, d), _F32), pltpu.SemaphoreType.DMA(()),
                 pltpu.VMEM((n_blk, tm, tn), _F32), pltpu.VMEM((d // GROUP, CHUNK, CHUNK), _BF16)],
        side=side, name="gmlp_prompt", vmem_limit_bytes=VMEM_LIMIT_BYTES_WIDE)
    return ya, h, copies


def _p_shortconv_kernel(h_ref, wbg_ref, wcg_ref, wxb_ref, cw_ref, yb_ref, tail_ref, carry_scr,
                        *, tm, tiles_per_seq, slab, side):
    i = pl.program_id(0)
    j = pl.program_id(1)

    @pl.when(i % tiles_per_seq == 0)
    def _():
        carry_scr[j] = jnp.zeros(carry_scr.shape[1:], _F32)

    _convert(side)
    prev = carry_scr[j]
    cw = cw_ref.at[j]
    for r in _slabs(tm, slab):
        h = h_ref[r, :]
        bg = _dot(h, wbg_ref[...])
        p = _dot(h, wcg_ref[...]) * _dot(h, wxb_ref[...])
        yb_ref[r, :] = (bg * _causal_conv_rows(p, prev, cw)).astype(_BF16)
        prev = p[slab - SUBLANES:, :]
    carry_scr[j] = prev
    tail_ref[j] = prev


def _p_shortconv_call(h, w_bcx, cw, *, tm, tn, tiles_per_seq, slab, side):
    m, d = h.shape
    w = cw.shape[1]
    n_blk = w // tn
    wspec = lambda off: pl.BlockSpec((d, tn), lambda i, j: (0, off + j))
    (yb, tail), copies = _run(
        functools.partial(_p_shortconv_kernel, tm=tm, tiles_per_seq=tiles_per_seq, slab=slab),
        grid=(m // tm, n_blk),
        in_specs=[pl.BlockSpec((tm, d), lambda i, j: (i, 0)),
                  wspec(0), wspec(n_blk), wspec(2 * n_blk),
                  pl.BlockSpec((n_blk, CONV_K, tn), lambda i, j: (0, 0, 0))],
        args=(h, w_bcx, w_bcx, w_bcx, _col_blocks(cw, tn)),
        out_specs=[pl.BlockSpec((tm, tn), lambda i, j: (i, j)),
                   pl.BlockSpec((None, n_blk, SUBLANES, tn), lambda i, j: (i, 0, 0, 0))],
        out_shape=[jax.ShapeDtypeStruct((m, w), _BF16),
                   jax.ShapeDtypeStruct((m // tm, n_blk, SUBLANES, tn), _F32)],
        scratch=[pltpu.VMEM((n_blk, SUBLANES, tn), _F32)],
        side=side, name="shortconv_prompt")
    return yb, tail, copies


def _p_proj_kernel(m_ref, wo_ref, x_ref, gt_ref, gpost_ref, sh_ref, sc_ref, gpre_ref,
                   x1_ref, h2_ref, *, tm, slab, tiles_per_seq, mod_row0, side):
    g1 = _mod_row(gt_ref, tiles_per_seq, mod_row0) * gpost_ref[...]
    g2 = gpre_ref[...] * (1.0 + _mod_row(sc_ref, tiles_per_seq, mod_row0))
    sh = _mod_row(sh_ref, tiles_per_seq, mod_row0)
    for r in _slabs(tm, slab):
        y = _dot(m_ref[r, :], wo_ref[...])
        x1 = x_ref[r, :] + _rms(y, g1)
        x1_ref[r, :] = x1
        h2_ref[r, :] = (_rms(x1, g2) + sh).astype(_BF16)


def _p_proj_call(mg, w_o, x, mod, gpost, gpre, *, tm, slab, tiles_per_seq, mod_row0):
    m, d = x.shape
    row = pl.BlockSpec((tm, d), lambda i: (i, 0))
    vec = pl.BlockSpec((1, d), lambda i: (0, 0))
    mspec = lambda k: _mod_row_spec(d, k, tiles_per_seq, mod_row0, 1)
    (x1, h2), _ = _run(
        functools.partial(_p_proj_kernel, tm=tm, slab=slab, tiles_per_seq=tiles_per_seq,
                          mod_row0=mod_row0),
        grid=(m // tm,),
        in_specs=[row, pl.BlockSpec((d, d), lambda i: (0, 0)), row, mspec(2), vec, mspec(3),
                  mspec(4), vec],
        args=(mg, w_o, x, mod, gpost, mod, mod, gpre),
        out_specs=[row, row],
        out_shape=[jax.ShapeDtypeStruct((m, d), _F32), jax.ShapeDtypeStruct((m, d), _BF16)],
        name="out_proj_prompt")
    return x1, h2


def _p_ffn_kernel(h_ref, wa_ref, wb_ref, cw_ref, wd_ref, x1_hbm, gt_ref, gpost_ref,
                  out_ref, tail_ref, x1_buf, x1_sem, carry_scr, *, tm, n_blk, tiles_per_seq, slab,
                  mod_row0, side):
    i = pl.program_id(0)
    j = pl.program_id(1)
    x1_copy = pltpu.make_async_copy(x1_hbm.at[pl.ds(pl.multiple_of(i * tm, tm), tm), :],
                                    x1_buf, x1_sem)

    @pl.when(i % tiles_per_seq == 0)
    def _():
        carry_scr[j] = jnp.zeros(carry_scr.shape[1:], _F32)

    def step(first, last):
        prev = carry_scr[j]
        cw = cw_ref.at[j]
        slabs = _slabs(tm, slab)
        if last:
            g_last = _mod_row(gt_ref, tiles_per_seq, mod_row0) * gpost_ref[...]
        gs = []
        for r in slabs:
            h = h_ref[r, :]
            a = _dot(h, wa_ref[...])
            b = _dot(h, wb_ref[...])
            gs.append((jax.nn.gelu(_causal_conv_rows(a, prev, cw)) * b).astype(_BF16))
            prev = a[slab - SUBLANES:, :]
        for r, g in zip(slabs, gs):
            f = _dot(g, wd_ref[...])
            acc = f if first else out_ref[r, :] + f
            if last:
                acc = x1_buf[r, :] + _rms(acc, g_last)
            out_ref[r, :] = acc
        carry_scr[j] = prev
        tail_ref[j] = prev

    @pl.when(j == 0)
    def _():
        x1_copy.start()
        step(True, False)

    @pl.when((j > 0) & (j < n_blk - 1))
    def _():
        step(False, False)

    @pl.when(j == n_blk - 1)
    def _():
        x1_copy.wait()
        step(False, True)


def _p_ffn_call(h2, w_a, w_b, cw, w_down, x1, mod, gpost, *, tm, tn, tiles_per_seq, slab, mod_row0):
    m, d = x1.shape
    f = cw.shape[1]
    n_blk = f // tn
    row = pl.BlockSpec((tm, d), lambda i, j: (i, 0))
    wspec = pl.BlockSpec((d, tn), lambda i, j: (0, j))
    (out, tail), _ = _run(
        functools.partial(_p_ffn_kernel, tm=tm, n_blk=n_blk, tiles_per_seq=tiles_per_seq, slab=slab,
                          mod_row0=mod_row0),
        grid=(m // tm, n_blk),
        in_specs=[row, wspec, wspec,
                  pl.BlockSpec((n_blk, CONV_K, tn), lambda i, j: (0, 0, 0)),
                  pl.BlockSpec((tn, d), lambda i, j: (j, 0)),
                  pl.BlockSpec(memory_space=pl.ANY),
                  _mod_row_spec(d, 5, tiles_per_seq, mod_row0, 2),
                  pl.BlockSpec((1, d), lambda i, j: (0, 0))],
        args=(h2, w_a, w_b, _col_blocks(cw, tn), w_down, x1, mod, gpost),
        out_specs=[row, pl.BlockSpec((None, n_blk, SUBLANES, tn), lambda i, j: (i, 0, 0, 0))],
        out_shape=[jax.ShapeDtypeStruct((m, d), _F32),
                   jax.ShapeDtypeStruct((m // tm, n_blk, SUBLANES, tn), _F32)],
        scratch=[pltpu.VMEM((tm, d), _F32), pltpu.SemaphoreType.DMA(()),
                 pltpu.VMEM((n_blk, SUBLANES, tn), _F32)],
        name="convffn_prompt")
    return out, tail


def _cols(src, first_col, n_cols, n_blocks):
    block = n_cols // n_blocks
    assert block * n_blocks == n_cols and block % 128 == 0 and first_col % block == 0
    return _Side(src, 1, first_col // block, block, n_blocks)


def _layer(xs, xp, mod, st_b, st_f, p, *, seq_len, mod_row0):
    d = p["g_v"].shape[1]
    wb = p["conv_b_w"].shape[1]
    f = p["conv_f_w"].shape[1]
    tn_s = SAMPLE_COL_TILE

    def tiles(name):
        tm, tn, slab = PROMPT_TILES[name]
        return dict(tm=tm, tn=tn, slab=slab, tiles_per_seq=seq_len // tm)

    def n_steps(name):
        tm, tn, _ = PROMPT_TILES[name]
        cols = {"gmlp": 2 * d, "shortconv": wb, "merge": d, "ffn": f}[name]
        return xp.shape[0] // tm * (cols // tn)

    def without(kw, *names):
        return {k: v for k, v in kw.items() if k not in names}

    def blocks(n_cols, budget):
        return max(n for n in range(1, budget + 1) if n_cols % (128 * n) == 0)

    w_in, w_up = p["w_in"], p["w_up"]

    ya_s, h1_s, vn, w_v, w_u = _s_gmlp_call(xs, mod, p["g_pre_mix"], w_in, p["g_v"], p["wvec"],
                                            p["bvec"], tn=tn_s)
    n = n_steps("gmlp")
    ya_p, h1_p, (w_bcx,) = _p_gmlp_call(
        xp, mod, p["g_pre_mix"], w_v, w_u, p["g_v"], p["w_s"], p["bias"], mod_row0=mod_row0,
        side=(_cols(w_in, 2 * d, 3 * wb, blocks(3 * wb, n)),),
        **tiles("gmlp"))

    yb_s, new_b = _s_shortconv_call(h1_s, w_bcx, p["conv_b_w"], st_b, tn=tn_s)
    n = n_steps("shortconv")
    yb_p, tail_b, (w_gate, w_oa, w_ob, w_o, w_a, w_b) = _p_shortconv_call(
        h1_p, w_bcx, p["conv_b_w"],
        side=(_cols(w_in, 2 * d + 3 * wb, 2 * d, blocks(2 * d, n)),
              _cols(p["w_out_a"], 0, d, blocks(d, n)), _cols(p["w_out_b"], 0, d, blocks(d, n)),
              _cols(p["w_o"], 0, d, blocks(d, n)),
              _cols(w_up, 0, f, blocks(f, n)), _cols(w_up, f, f, blocks(f, n))),
        **tiles("shortconv"))

    x1_s, h2_s = _s_merge_proj_call(h1_s, ya_s, yb_s, w_gate, w_oa, w_ob, w_o, xs, mod,
                                    p["g_post_mix"], p["g_pre_ffn"], tn=tn_s)
    n_row_blocks = max(k for k in range(1, n_steps("merge") + 1)
                       if f % k == 0 and (f // k) % (2 * SUBLANES) == 0)
    mg_p, (w_d,) = _merge_call(h1_p, ya_p, yb_p, w_gate, w_oa, w_ob, name="gated_merge_prompt",
                               side=(_Side(p["w_down"], 0, 0, f // n_row_blocks, n_row_blocks),),
                               **without(tiles("merge"), "tiles_per_seq"))

    x1_p, h2_p = _p_proj_call(mg_p, w_o, xp, mod, p["g_post_mix"], p["g_pre_ffn"],
                              mod_row0=mod_row0, **without(tiles("proj"), "tn"))

    out_s, new_f = _s_ffn_call(h2_s, w_a, w_b, p["conv_f_w"], w_d, x1_s, mod, p["g_post_ffn"],
                               st_f, tn=tn_s)
    out_p, tail_f = _p_ffn_call(h2_p, w_a, w_b, p["conv_f_w"], w_d, x1_p, mod, p["g_post_ffn"],
                                mod_row0=mod_row0, **tiles("ffn"))
    return out_s, out_p, new_b, new_f, vn, tail_b, tail_f


def kernel(x_prompt, x_sample, c_prompt, c_sample, state_conv_b, state_conv_ffn, w_ada, b_ada, g_pre_mix, g_post_mix, w_in, g_v, w_s, b_s, conv_b_w, w_out_a, w_out_b, w_o, g_pre_ffn, g_post_ffn, w_up, conv_f_w, w_down):
    depth = w_in.shape[0]
    bp, seq, d = x_prompt.shape
    bs, tdec, _ = x_sample.shape
    n_groups = w_s.shape[1]
    assert bs == CHUNK and tdec <= CHUNK
    assert all(seq % tm == 0 for tm, _, _ in PROMPT_TILES.values())

    xp = x_prompt.reshape(bp * seq, d)
    xs = x_sample
    pad = (-(bp + bs)) % SUBLANES
    c_all = jnp.concatenate([c_sample, c_prompt, jnp.zeros((pad, d), _F32)], axis=0)

    pb, sb, pf, sf, sv = [], [], [], [], []
    for l in range(depth):
        mod, _ = _mod_call(c_all, w_ada[l], b_ada[l][None, :], side=())
        vec = lambda a: a[l][None, :]
        bias_full = jnp.repeat(jnp.transpose(b_s[l]), GROUP, axis=1)
        wvec = jnp.repeat(
            jnp.transpose(w_s[l][:, :tdec, :tdec], (1, 2, 0)).reshape(tdec * tdec, n_groups),
            GROUP, axis=1)
        p = {
            "w_in": w_in[l], "w_out_a": w_out_a[l], "w_out_b": w_out_b[l],
            "w_o": w_o[l], "w_up": w_up[l], "w_down": w_down[l],
            "g_pre_mix": vec(g_pre_mix), "g_post_mix": vec(g_post_mix), "g_v": vec(g_v),
            "g_pre_ffn": vec(g_pre_ffn), "g_post_ffn": vec(g_post_ffn),
            "conv_b_w": conv_b_w[l], "conv_f_w": conv_f_w[l],
            "w_s": w_s[l], "bias": bias_full, "wvec": wvec, "bvec": bias_full[:tdec],
        }
        xs, xp, sbt, sft, vn, tb, tf = _layer(xs, xp, mod, state_conv_b[l], state_conv_ffn[l], p,
                                              seq_len=seq, mod_row0=bs)

        def prompt_tail(t):
            n_tiles, n_blk, _, tn = t.shape
            t = t.reshape(bp, n_tiles // bp, n_blk, SUBLANES, tn)[:, -1, :, SUBLANES - (CONV_K - 1):, :]
            return jnp.transpose(t, (0, 2, 1, 3)).reshape(bp, CONV_K - 1, n_blk * tn)

        pb.append(prompt_tail(tb))
        pf.append(prompt_tail(tf))
        sb.append(sbt)
        sf.append(sft)
        sv.append(vn)

    y_prompt = xp.reshape(bp, seq, d)
    return (y_prompt, xs, jnp.stack(pb), jnp.stack(sb), jnp.stack(pf), jnp.stack(sf),
            jnp.stack(sv))
```

```python
import functools
from typing import NamedTuple

import jax
import jax.numpy as jnp
from jax import lax
from jax.experimental import pallas as pl
from jax.experimental.pallas import tpu as pltpu

EPS = 1e-6
CHUNK = 128
GROUP = 128
CONV_K = 3
SUBLANES = 8
VMEM_LIMIT_BYTES = 56 * 1024 * 1024
VMEM_LIMIT_BYTES_WIDE = 60 * 1024 * 1024
PROMPT_TILES = {
    "gmlp": (1024, 1024, 256),
    "shortconv": (1024, 512, 256),
    "merge": (512, 1024, 256),
    "proj": (512, None, 256),
    "ffn": (1024, 512, 256),
}
SAMPLE_COL_TILE = 512
ADALN_COL_TILE = 1024

_BF16 = jnp.bfloat16
_F32 = jnp.float32


def _dot(a, b):
    return jnp.dot(a, b, preferred_element_type=_F32)


def _rms(xf, g):
    ms = jnp.mean(xf * xf, axis=-1, keepdims=True)
    return xf * lax.rsqrt(ms + EPS) * g


def _causal_conv_rows(p, prev, cw_ref):
    r1 = pltpu.roll(p, 1, 0)
    r2 = pltpu.roll(p, 2, 0)
    row = lax.broadcasted_iota(jnp.int32, (SUBLANES, 1), 0)
    head1 = jnp.where(row == 0, prev[7:8, :], r1[:SUBLANES, :])
    head2 = jnp.where(row == 0, prev[6:7, :], jnp.where(row == 1, prev[7:8, :], r2[:SUBLANES, :]))
    m1 = jnp.concatenate([head1, r1[SUBLANES:, :]], axis=0)
    m2 = jnp.concatenate([head2, r2[SUBLANES:, :]], axis=0)
    return cw_ref[0:1, :] * m2 + cw_ref[1:2, :] * m1 + cw_ref[2:3, :] * p


def _causal_conv_slabs(x, prev, cw_ref, slab):
    w0, w1, w2 = cw_ref[0:1, :], cw_ref[1:2, :], cw_ref[2:3, :]
    seq = list(prev) + [x[t * slab:(t + 1) * slab, :] for t in range(x.shape[0] // slab)]
    y = [w0 * seq[t] + w1 * seq[t + 1] + w2 * seq[t + 2] for t in range(len(seq) - 2)]
    return jnp.concatenate(y, axis=0), seq[-2:]


def _slabs(tm, slab):
    return [slice(s * slab, (s + 1) * slab) for s in range(tm // slab)]


def _mod_row_spec(d, k, tiles_per_seq, mod_row0, n_grid_axes):
    idx = lambda i: ((mod_row0 + i // tiles_per_seq) // SUBLANES, k)
    return pl.BlockSpec((SUBLANES, d), (lambda i, j: idx(i)) if n_grid_axes == 2 else idx)


def _mod_row(ref, tiles_per_seq, mod_row0):
    b = (mod_row0 + pl.program_id(0) // tiles_per_seq) % SUBLANES
    return ref[pl.ds(b, 1), :]


def _col_blocks(a, tn):
    r, c = a.shape
    return jnp.transpose(a.reshape(r, c // tn, tn), (1, 0, 2))


def _post_mix(y, x, gt, gpost, sh, sc, gpre):
    x1 = x + gt * _rms(y, gpost)
    h2 = _rms(x1, gpre) * (1.0 + sc) + sh
    return x1, h2.astype(_BF16)


class _Side(NamedTuple):
    src: jax.Array
    axis: int
    start: int
    block: int
    n_blocks: int


def _convert(side):
    for src_ref, dst_ref in side:
        dst_ref[...] = src_ref[...].astype(_BF16)


def _run(body, *, grid, in_specs, args, out_specs, out_shape, scratch=(), side=(), name,
         vmem_limit_bytes=VMEM_LIMIT_BYTES):
    n_in, n_out, n_side = len(args), len(out_shape), len(side)
    n_steps = functools.reduce(lambda a, b: a * b, grid)
    step_of = (lambda i: i) if len(grid) == 1 else (lambda i, j: i * grid[1] + j)
    side_in, side_out, side_shape = [], [], []
    for s in side:
        assert s.n_blocks <= n_steps, (name, s.n_blocks, n_steps)
        other = s.src.shape[1 - s.axis]
        pos = lambda *ids, s=s: jnp.minimum(step_of(*ids), s.n_blocks - 1)
        if s.axis == 1:
            blk, full = (other, s.block), (other, s.block * s.n_blocks)
            side_in.append(pl.BlockSpec(blk, lambda *ids, s=s, pos=pos: (0, s.start + pos(*ids))))
            side_out.append(pl.BlockSpec(blk, lambda *ids, pos=pos: (0, pos(*ids))))
        else:
            blk, full = (s.block, other), (s.block * s.n_blocks, other)
            side_in.append(pl.BlockSpec(blk, lambda *ids, s=s, pos=pos: (s.start + pos(*ids), 0)))
            side_out.append(pl.BlockSpec(blk, lambda *ids, pos=pos: (pos(*ids), 0)))
        side_shape.append(jax.ShapeDtypeStruct(full, _BF16))

    def kern(*refs):
        o0 = n_in + n_side
        s0 = o0 + n_out + n_side
        body(*refs[:n_in], *refs[o0:o0 + n_out], *refs[s0:],
             side=tuple(zip(refs[n_in:o0], refs[o0 + n_out:s0])))

    res = pl.pallas_call(
        kern, grid=grid,
        in_specs=list(in_specs) + side_in, out_specs=list(out_specs) + side_out,
        out_shape=list(out_shape) + side_shape, scratch_shapes=list(scratch),
        compiler_params=pltpu.CompilerParams(dimension_semantics=("arbitrary",) * len(grid),
                                             vmem_limit_bytes=vmem_limit_bytes),
        name=name,
    )(*args, *[s.src for s in side])
    return res[:n_out], res[n_out:]


def _mod_kernel(c_ref, w_ref, b_ref, o_ref, *, side):
    _convert(side)
    c = c_ref[...]
    a = (c * jax.nn.sigmoid(c)).astype(_BF16)
    o_ref[...] = _dot(a, w_ref[...].astype(_BF16)) + b_ref[...]


def _mod_call(c_all, w_ada, b_ada, *, side):
    rows, d = c_all.shape
    n = w_ada.shape[1]
    tn = ADALN_COL_TILE
    (mod,), copies = _run(
        _mod_kernel, grid=(n // tn,),
        in_specs=[pl.BlockSpec((rows, d), lambda j: (0, 0)),
                  pl.BlockSpec((d, tn), lambda j: (0, j)),
                  pl.BlockSpec((1, tn), lambda j: (0, j))],
        args=(c_all, w_ada, b_ada),
        out_specs=[pl.BlockSpec((rows, tn), lambda j: (0, j))],
        out_shape=[jax.ShapeDtypeStruct((rows, n), _F32)],
        side=side, name="adaln_mod")
    return mod, copies


def _s_gmlp_kernel(x_hbm, sh_ref, sc_ref, gpre_ref, wv_ref, wu_ref, gv_ref, ws_ref, bias_ref,
                   ya_ref, h_ref, vn_hbm, wvb_ref, wub_ref, v_scr, x_buf, vn_buf, in_sem, out_sem,
                   *, tm, tn, n_blk, slab, side):
    j = pl.program_id(0)
    d = n_blk * tn
    slabs = _slabs(tm, slab)
    n_slab = len(slabs)
    x_copy = lambda t: pltpu.make_async_copy(x_hbm.at[:, t, :], x_buf.at[t], in_sem.at[t])
    vn_copy = lambda t: pltpu.make_async_copy(vn_buf.at[t], vn_hbm.at[:, t, :], out_sem.at[t])

    @pl.when(j == 0)
    def _():
        for t in range(n_slab):
            x_copy(t).start(priority=t % 2)
        for t, r in enumerate(slabs):
            x_copy(t).wait()
            h_ref[r, :] = (_rms(x_buf[t], gpre_ref[...]) * (1.0 + sc_ref[...])
                           + sh_ref[...]).astype(_BF16)

    @pl.when(j < n_blk)
    def _():
        wv = wv_ref[...].astype(_BF16)
        wvb_ref[...] = wv
        v_scr[j] = _dot(h_ref[...], wv)

    @pl.when(j == n_blk)
    def _():
        for t, r in enumerate(slabs):
            ss = 0.0
            for k in range(n_blk):
                vk = v_scr[k, r, :]
                ss = ss + jnp.sum(vk * vk, axis=-1, keepdims=True)
            rs = lax.rsqrt(ss * (1.0 / d) + EPS)
            for k in range(n_blk):
                vn = v_scr[k, r, :] * rs * gv_ref[:, k * tn:(k + 1) * tn]
                v_scr[k, r, :] = vn
                vn_buf[t, :, k * tn:(k + 1) * tn] = vn
            vn_copy(t).start()
        for t in reversed(range(n_slab)):
            for k in range(n_blk):
                c = slice(k * tn, (k + 1) * tn)
                acc = ws_ref[t * n_slab:t * n_slab + 1, c] * v_scr[k, slabs[0], :]
                for s in range(1, t + 1):
                    acc = acc + ws_ref[t * n_slab + s:t * n_slab + s + 1, c] * v_scr[k, slabs[s], :]
                v_scr[k, slabs[t], :] = acc + bias_ref[t:t + 1, c]

    @pl.when(j >= n_blk)
    def _():
        wu = wu_ref[...].astype(_BF16)
        wub_ref[...] = wu
        ya_ref[...] = (_dot(h_ref[...], wu) * v_scr[j - n_blk]).astype(_BF16)

    @pl.when(j == 2 * n_blk - 1)
    def _():
        for t in range(n_slab):
            vn_copy(t).wait()


def _s_gmlp_call(x, mod, gpre, w_in, gv, wvec, bvec, *, tn):
    slab, n_slab, d = x.shape
    tm = n_slab * slab
    n_blk = d // tn
    anywhere = pl.BlockSpec(memory_space=pl.ANY)
    full = lambda a: pl.BlockSpec(a.shape, lambda j: (0,) * a.ndim)
    u_map = lambda j: (0, jnp.maximum(j - n_blk, 0))
    v_map = lambda j: (0, jnp.minimum(j, n_blk - 1))
    (ya, h, vn, w_v, w_u), _ = _run(
        functools.partial(_s_gmlp_kernel, tm=tm, tn=tn, n_blk=n_blk, slab=slab),
        grid=(2 * n_blk,),
        in_specs=[anywhere,
                  pl.BlockSpec((slab, d), lambda j: (0, 0)), pl.BlockSpec((slab, d), lambda j: (0, 1)),
                  full(gpre),
                  pl.BlockSpec((d, tn), lambda j: (0, n_blk + jnp.minimum(j, n_blk - 1))),
                  pl.BlockSpec((d, tn), u_map),
                  full(gv), full(wvec), full(bvec)],
        args=(x, mod, mod, gpre, w_in, w_in, gv, wvec, bvec),
        out_specs=[pl.BlockSpec((tm, tn), u_map), pl.BlockSpec((tm, d), lambda j: (0, 0)), anywhere,
                   pl.BlockSpec((d, tn), v_map), pl.BlockSpec((d, tn), u_map)],
        out_shape=[jax.ShapeDtypeStruct((tm, d), _BF16), jax.ShapeDtypeStruct((tm, d), _BF16),
                   jax.ShapeDtypeStruct(x.shape, _F32),
                   jax.ShapeDtypeStruct((d, d), _BF16), jax.ShapeDtypeStruct((d, d), _BF16)],
        scratch=[pltpu.VMEM((n_blk, tm, tn), _F32),
                 pltpu.VMEM((n_slab, slab, d), _F32), pltpu.VMEM((n_slab, slab, d), _F32),
                 pltpu.SemaphoreType.DMA((n_slab,)), pltpu.SemaphoreType.DMA((n_slab,))],
        name="gmlp_sample")
    return ya, h, vn, w_v, w_u


class _StateIO:
    def __init__(self, st_hbm, new_hbm, st_buf, new_buf, in_sem, out_sem, n_blk, tn):
        self.n_rows, self.n_blk = st_buf.shape[0], n_blk
        col = lambda c: pl.ds(c * tn, tn) if isinstance(c, int) else pl.ds(pl.multiple_of(c * tn, tn), tn)
        self.load = lambda k, c: pltpu.make_async_copy(st_hbm.at[:, k, col(c)], st_buf.at[k, c],
                                                       in_sem.at[k, c])
        self.store = lambda k, c: pltpu.make_async_copy(new_buf.at[k, c], new_hbm.at[:, k, col(c)],
                                                        out_sem.at[k, c])
        self.st_buf, self.new_buf = st_buf, new_buf

    def read(self, j):
        @pl.when(j == 0)
        def _():
            for k in range(self.n_rows):
                for c in range(self.n_blk):
                    self.load(k, c).start(priority=c % 2)
        for k in range(self.n_rows):
            self.load(k, j).wait()
        return [self.st_buf[k, j] for k in range(self.n_rows)]

    def write(self, j, rows):
        for k, row in enumerate(rows):
            self.new_buf[k, j] = row
            self.store(k, j).start(priority=k % 2)

        @pl.when(j == self.n_blk - 1)
        def _():
            for k in range(self.n_rows):
                for c in range(self.n_blk):
                    self.store(k, c).wait()


def _state_scratch(state, n_blk, tn):
    slab, n_rows, _ = state.shape
    buf = pltpu.VMEM((n_rows, n_blk, slab, tn), _F32)
    sem = pltpu.SemaphoreType.DMA((n_rows, n_blk))
    return [buf, buf, sem, sem]


def _s_shortconv_kernel(h_ref, wbg_ref, wcg_ref, wxb_ref, cw_ref, st_hbm, yb_ref, new_hbm,
                        st_buf, new_buf, in_sem, out_sem, *, slab, n_blk, tn, side):
    j = pl.program_id(0)
    io = _StateIO(st_hbm, new_hbm, st_buf, new_buf, in_sem, out_sem, n_blk, tn)
    prev = io.read(j)
    h = h_ref[...]
    bg = _dot(h, wbg_ref[...])
    p = _dot(h, wcg_ref[...]) * _dot(h, wxb_ref[...])
    cb, tail = _causal_conv_slabs(p, prev, cw_ref, slab)
    yb_ref[...] = (bg * cb).astype(_BF16)
    io.write(j, tail)


def _s_shortconv_call(h, w_bcx, cw, state, *, tn):
    tm, d = h.shape
    slab = state.shape[0]
    w = cw.shape[1]
    n_blk = w // tn
    wspec = lambda off: pl.BlockSpec((d, tn), lambda j: (0, off + j))
    anywhere = pl.BlockSpec(memory_space=pl.ANY)
    (yb, new_state), _ = _run(
        functools.partial(_s_shortconv_kernel, slab=slab, n_blk=n_blk, tn=tn),
        grid=(n_blk,),
        in_specs=[pl.BlockSpec((tm, d), lambda j: (0, 0)),
                  wspec(0), wspec(n_blk), wspec(2 * n_blk),
                  pl.BlockSpec((CONV_K, tn), lambda j: (0, j)), anywhere],
        args=(h, w_bcx, w_bcx, w_bcx, cw, state),
        out_specs=[pl.BlockSpec((tm, tn), lambda j: (0, j)), anywhere],
        out_shape=[jax.ShapeDtypeStruct((tm, w), _BF16), jax.ShapeDtypeStruct(state.shape, _F32)],
        scratch=_state_scratch(state, n_blk, tn),
        name="shortconv_sample")
    return yb, new_state


def _merge_kernel(h_ref, ya_ref, yb_ref, wga_ref, wgb_ref, woa_ref, wob_ref, m_ref, *, tm, slab, side):
    _convert(side)
    for r in _slabs(tm, slab):
        h = h_ref[r, :]
        ga = jax.nn.sigmoid(_dot(h, wga_ref[...]))
        gb = jax.nn.sigmoid(_dot(h, wgb_ref[...]))
        m = ga * _dot(ya_ref[r, :], woa_ref[...]) + gb * _dot(yb_ref[r, :], wob_ref[...])
        m_ref[r, :] = m.astype(_BF16)


def _merge_call(h, ya, yb, w_gate, w_out_a, w_out_b, *, tm, tn, slab, side=(), name):
    m, d = h.shape
    n_blk = d // tn
    row = pl.BlockSpec((tm, d), lambda i, j: (i, 0))
    wspec = lambda off: pl.BlockSpec((d, tn), lambda i, j: (0, off + j))
    (mg,), copies = _run(
        functools.partial(_merge_kernel, tm=tm, slab=slab),
        grid=(m // tm, n_blk),
        in_specs=[row, row, row, wspec(0), wspec(n_blk), wspec(0), wspec(0)],
        args=(h, ya, yb, w_gate, w_gate, w_out_a, w_out_b),
        out_specs=[pl.BlockSpec((tm, tn), lambda i, j: (i, j))],
        out_shape=[jax.ShapeDtypeStruct((m, d), _BF16)],
        side=side, name=name)
    return mg, copies


def _s_merge_proj_kernel(h_ref, ya_ref, yb_ref, wga_ref, wgb_ref, woa_ref, wob_ref, wo_ref, x_hbm,
                         gt_ref, gpost_ref, sh_ref, sc_ref, gpre_ref, x1_ref, h2_ref,
                         y_scr, x_buf, in_sem, *, tm, n_blk, slab, side):
    j = pl.program_id(0)
    n_slab = tm // slab
    x_copy = lambda t: pltpu.make_async_copy(x_hbm.at[:, t, :], x_buf.at[t], in_sem.at[t])

    @pl.when(j == 0)
    def _():
        for t in range(n_slab):
            x_copy(t).start(priority=t % 2)
        y_scr[...] = jnp.zeros(y_scr.shape, _F32)

    chains = _slabs(tm, 2 * slab)
    ms = []
    for r in chains:
        h = h_ref[r, :]
        ga = jax.nn.sigmoid(_dot(h, wga_ref[...]))
        gb = jax.nn.sigmoid(_dot(h, wgb_ref[...]))
        m = ga * _dot(ya_ref[r, :], woa_ref[...]) + gb * _dot(yb_ref[r, :], wob_ref[...])
        ms.append(m.astype(_BF16))
    for r, m in zip(chains, ms):
        y_scr[r, :] += _dot(m, wo_ref[...])

    @pl.when(j == n_blk - 1)
    def _():
        for t, r in enumerate(_slabs(tm, slab)):
            x_copy(t).wait()
            x1, h2 = _post_mix(y_scr[r, :], x_buf[t], gt_ref[...], gpost_ref[...],
                               sh_ref[...], sc_ref[...], gpre_ref[...])
            x1_ref[r, :] = x1
            h2_ref[r, :] = h2


def _s_merge_proj_call(h, ya, yb, w_gate, w_out_a, w_out_b, w_o, x, mod, gpost, gpre, *, tn):
    tm, d = h.shape
    slab = x.shape[0]
    n_blk = d // tn
    row_in = pl.BlockSpec((tm, d), lambda j: (0, 0), pipeline_mode=pl.Buffered(1))
    row_out = pl.BlockSpec((tm, d), lambda j: (0, 0))
    wspec = lambda off: pl.BlockSpec((d, tn), lambda j: (0, off + j))
    vec = pl.BlockSpec((1, d), lambda j: (0, 0))
    mspec = lambda k: pl.BlockSpec((slab, d), lambda j: (0, k))
    (x1, h2), _ = _run(
        functools.partial(_s_merge_proj_kernel, tm=tm, n_blk=n_blk, slab=slab),
        grid=(n_blk,),
        in_specs=[row_in, row_in, row_in, wspec(0), wspec(n_blk), wspec(0), wspec(0),
                  pl.BlockSpec((tn, d), lambda j: (j, 0)), pl.BlockSpec(memory_space=pl.ANY),
                  mspec(2), vec, mspec(3), mspec(4), vec],
        args=(h, ya, yb, w_gate, w_gate, w_out_a, w_out_b, w_o, x, mod, gpost, mod, mod, gpre),
        out_specs=[row_out, row_out],
        out_shape=[jax.ShapeDtypeStruct((tm, d), _F32), jax.ShapeDtypeStruct((tm, d), _BF16)],
        scratch=[pltpu.VMEM((tm, d), _F32), pltpu.VMEM((tm // slab, slab, d), _F32),
                 pltpu.SemaphoreType.DMA((tm // slab,))],
        name="merge_proj_sample")
    return x1, h2


def _s_ffn_kernel(h_ref, wa_ref, wb_ref, cw_ref, wd_ref, x1_ref, gt_ref, gpost_ref, st_hbm,
                  out_hbm, new_hbm, acc_scr, out_sem, st_buf, new_buf, st_in_sem, st_out_sem,
                  *, tm, tn, n_blk, slab, side):
    j = pl.program_id(0)
    n_slab = tm // slab
    out_copy = lambda t: pltpu.make_async_copy(acc_scr.at[t], out_hbm.at[:, t, :], out_sem.at[t])
    io = _StateIO(st_hbm, new_hbm, st_buf, new_buf, st_in_sem, st_out_sem, n_blk, tn)

    @pl.when(j == 0)
    def _():
        acc_scr[...] = jnp.zeros(acc_scr.shape, _F32)

    prev = io.read(j)
    gs = []
    for r in _slabs(tm, 2 * slab):
        h = h_ref[r, :]
        ac, prev = _causal_conv_slabs(_dot(h, wa_ref[...]), prev, cw_ref, slab)
        gs.append((jax.nn.gelu(ac) * _dot(h, wb_ref[...])).astype(_BF16))
    for c, g in enumerate(gs):
        f = _dot(g, wd_ref[...])
        acc_scr[2 * c] += f[:slab, :]
        acc_scr[2 * c + 1] += f[slab:, :]
    io.write(j, prev)

    @pl.when(j == n_blk - 1)
    def _():
        for t, r in enumerate(_slabs(tm, slab)):
            acc_scr[t] = x1_ref[r, :] + gt_ref[...] * _rms(acc_scr[t], gpost_ref[...])
            out_copy(t).start()
        for t in range(n_slab):
            out_copy(t).wait()


def _s_ffn_call(h2, w_a, w_b, cw, w_down, x1, mod, gpost, state, *, tn):
    tm, d = x1.shape
    slab = state.shape[0]
    f = cw.shape[1]
    n_blk = f // tn
    out_shape = (slab, tm // slab, d)
    row = pl.BlockSpec((tm, d), lambda j: (0, 0))
    wspec = pl.BlockSpec((d, tn), lambda j: (0, j))
    anywhere = pl.BlockSpec(memory_space=pl.ANY)
    (out, new_state), _ = _run(
        functools.partial(_s_ffn_kernel, tm=tm, tn=tn, n_blk=n_blk, slab=slab),
        grid=(n_blk,),
        in_specs=[row, wspec, wspec, pl.BlockSpec((CONV_K, tn), lambda j: (0, j)),
                  pl.BlockSpec((tn, d), lambda j: (j, 0)), row,
                  pl.BlockSpec((slab, d), lambda j: (0, 5)), pl.BlockSpec((1, d), lambda j: (0, 0)),
                  anywhere],
        args=(h2, w_a, w_b, cw, w_down, x1, mod, gpost, state),
        out_specs=[anywhere, anywhere],
        out_shape=[jax.ShapeDtypeStruct(out_shape, _F32), jax.ShapeDtypeStruct(state.shape, _F32)],
        scratch=[pltpu.VMEM((tm // slab, slab, d), _F32), pltpu.SemaphoreType.DMA((tm // slab,))]
        + _state_scratch(state, n_blk, tn),
        name="convffn_sample")
    return out, new_state


def _p_gmlp_kernel(x_hbm, sh_ref, sc_ref, gpre_ref, wv_ref, wu_ref, gv_ref, ws_ref, bias_ref,
                   ya_ref, h_ref, x_buf, x_sem, v_scr, wt_scr, *, tm, tn, n_blk, slab, tiles_per_seq,
                   mod_row0, side):
    i = pl.program_id(0)
    j = pl.program_id(1)
    d = n_blk * tn
    slabs = _slabs(tm, slab)

    def x_copy(tile):
        return pltpu.make_async_copy(x_hbm.at[pl.ds(pl.multiple_of(tile * tm, tm), tm), :],
                                     x_buf, x_sem)

    @pl.when((i == 0) & (j == 0))
    def _():
        x_copy(0).start()
        tril = (lax.broadcasted_iota(jnp.int32, (CHUNK, CHUNK), 0)
                >= lax.broadcasted_iota(jnp.int32, (CHUNK, CHUNK), 1))
        for g in range(d // GROUP):
            wt_scr[g] = jnp.where(tril, ws_ref[g], 0.0).astype(_BF16)

    @pl.when(j == 0)
    def _():
        x_copy(i).wait()
        _convert(side)
        g = gpre_ref[...] * (1.0 + _mod_row(sc_ref, tiles_per_seq, mod_row0))
        sh = _mod_row(sh_ref, tiles_per_seq, mod_row0)
        for r in slabs:
            h = (_rms(x_buf[r, :], g) + sh).astype(_BF16)
            h_ref[r, :] = h
            v_scr[0, r, :] = _dot(h, wv_ref[...])

    @pl.when((j == 0) & (i + 1 < pl.num_programs(0)))
    def _():
        x_copy(i + 1).start()

    @pl.when((j > 0) & (j < n_blk))
    def _():
        _convert(side)
        for r in slabs:
            v_scr[j, r, :] = _dot(h_ref[r, :], wv_ref[...])

    def _gate():
        gpb = tn // GROUP
        for c in range(tm // CHUNK):
            r = slice(c * CHUNK, (c + 1) * CHUNK)
            ss = 0.0
            for k in range(n_blk):
                vk = v_scr[k, r, :]
                ss = ss + jnp.sum(vk * vk, axis=-1, keepdims=True)
            rs = lax.rsqrt(ss * (1.0 / d) + EPS)
            for k in range(n_blk):
                vb = (v_scr[k, r, :] * rs * gv_ref[:, k * tn:(k + 1) * tn]).astype(_BF16)
                for gg in range(gpb):
                    g = k * gpb + gg
                    lanes = slice(gg * GROUP, (gg + 1) * GROUP)
                    v_scr[k, r, lanes] = (_dot(wt_scr[g], vb[:, lanes])
                                          + bias_ref[:, g * GROUP:(g + 1) * GROUP])

    @pl.when(j == n_blk)
    def _():
        _convert(side)
        _gate()
        for r in slabs:
            ya_ref[r, :] = (_dot(h_ref[r, :], wu_ref[...]) * v_scr[0, r, :]).astype(_BF16)

    @pl.when(j > n_blk)
    def _():
        _convert(side)
        for r in slabs:
            ya_ref[r, :] = (_dot(h_ref[r, :], wu_ref[...]) * v_scr[j - n_blk, r, :]).astype(_BF16)


def _p_gmlp_call(x, mod, gpre, w_v, w_u, gv, ws, bias, *, tm, tn, slab, tiles_per_seq, mod_row0,
                 side):
    m, d = x.shape
    n_blk = d // tn
    full = lambda a: pl.BlockSpec(a.shape, lambda i, j: (0,) * a.ndim)
    once = lambda a: pl.BlockSpec(a.shape, lambda i, j: (0,) * a.ndim, pipeline_mode=pl.Buffered(1))
    mspec = lambda k: _mod_row_spec(d, k, tiles_per_seq, mod_row0, 2)
    u_map = lambda i, j: (i, jnp.maximum(j - n_blk, 0))
    (ya, h), copies = _run(
        functools.partial(_p_gmlp_kernel, tm=tm, tn=tn, n_blk=n_blk, slab=slab,
                          tiles_per_seq=tiles_per_seq, mod_row0=mod_row0),
        grid=(m // tm, 2 * n_blk),
        in_specs=[pl.BlockSpec(memory_space=pl.ANY), mspec(0), mspec(1), full(gpre),
                  pl.BlockSpec((d, tn), lambda i, j: (0, jnp.minimum(j, n_blk - 1))),
                  pl.BlockSpec((d, tn), lambda i, j: (0, jnp.maximum(j - n_blk, 0))),
                  full(gv), once(ws), once(bias)],
        args=(x, mod, mod, gpre, w_v, w_u, gv, ws, bias),
        out_specs=[pl.BlockSpec((tm, tn), u_map), pl.BlockSpec((tm, d), lambda i, j: (i, 0))],
        out_shape=[jax.ShapeDtypeStruct((m, d), _BF16), jax.ShapeDtypeStruct((m, d), _BF16)],
        scratch=[pltpu.VMEM((tm, d), _F32), pltpu.SemaphoreType.DMA(()),
                 pltpu.VMEM((n_blk, tm, tn), _F32), pltpu.VMEM((d // GROUP, CHUNK, CHUNK), _BF16)],
        side=side, name="gmlp_prompt", vmem_limit_bytes=VMEM_LIMIT_BYTES_WIDE)
    return ya, h, copies


def _p_shortconv_kernel(h_ref, wbg_ref, wcg_ref, wxb_ref, cw_ref, yb_ref, tail_ref, carry_scr,
                        *, tm, tiles_per_seq, slab, side):
    i = pl.program_id(0)
    j = pl.program_id(1)

    @pl.when(i % tiles_per_seq == 0)
    def _():
        carry_scr[j] = jnp.zeros(carry_scr.shape[1:], _F32)

    _convert(side)
    prev = carry_scr[j]
    cw = cw_ref.at[j]
    for r in _slabs(tm, slab):
        h = h_ref[r, :]
        bg = _dot(h, wbg_ref[...])
        p = _dot(h, wcg_ref[...]) * _dot(h, wxb_ref[...])
        yb_ref[r, :] = (bg * _causal_conv_rows(p, prev, cw)).astype(_BF16)
        prev = p[slab - SUBLANES:, :]
    carry_scr[j] = prev
    tail_ref[j] = prev


def _p_shortconv_call(h, w_bcx, cw, *, tm, tn, tiles_per_seq, slab, side):
    m, d = h.shape
    w = cw.shape[1]
    n_blk = w // tn
    wspec = lambda off: pl.BlockSpec((d, tn), lambda i, j: (0, off + j))
    (yb, tail), copies = _run(
        functools.partial(_p_shortconv_kernel, tm=tm, tiles_per_seq=tiles_per_seq, slab=slab),
        grid=(m // tm, n_blk),
        in_specs=[pl.BlockSpec((tm, d), lambda i, j: (i, 0)),
                  wspec(0), wspec(n_blk), wspec(2 * n_blk),
                  pl.BlockSpec((n_blk, CONV_K, tn), lambda i, j: (0, 0, 0))],
        args=(h, w_bcx, w_bcx, w_bcx, _col_blocks(cw, tn)),
        out_specs=[pl.BlockSpec((tm, tn), lambda i, j: (i, j)),
                   pl.BlockSpec((None, n_blk, SUBLANES, tn), lambda i, j: (i, 0, 0, 0))],
        out_shape=[jax.ShapeDtypeStruct((m, w), _BF16),
                   jax.ShapeDtypeStruct((m // tm, n_blk, SUBLANES, tn), _F32)],
        scratch=[pltpu.VMEM((n_blk, SUBLANES, tn), _F32)],
        side=side, name="shortconv_prompt")
    return yb, tail, copies


def _p_proj_kernel(m_ref, wo_ref, x_ref, gt_ref, gpost_ref, sh_ref, sc_ref, gpre_ref,
                   x1_ref, h2_ref, *, tm, slab, tiles_per_seq, mod_row0, side):
    g1 = _mod_row(gt_ref, tiles_per_seq, mod_row0) * gpost_ref[...]
    g2 = gpre_ref[...] * (1.0 + _mod_row(sc_ref, tiles_per_seq, mod_row0))
    sh = _mod_row(sh_ref, tiles_per_seq, mod_row0)
    for r in _slabs(tm, slab):
        y = _dot(m_ref[r, :], wo_ref[...])
        x1 = x_ref[r, :] + _rms(y, g1)
        x1_ref[r, :] = x1
        h2_ref[r, :] = (_rms(x1, g2) + sh).astype(_BF16)


def _p_proj_call(mg, w_o, x, mod, gpost, gpre, *, tm, slab, tiles_per_seq, mod_row0):
    m, d = x.shape
    row = pl.BlockSpec((tm, d), lambda i: (i, 0))
    vec = pl.BlockSpec((1, d), lambda i: (0, 0))
    mspec = lambda k: _mod_row_spec(d, k, tiles_per_seq, mod_row0, 1)
    (x1, h2), _ = _run(
        functools.partial(_p_proj_kernel, tm=tm, slab=slab, tiles_per_seq=tiles_per_seq,
                          mod_row0=mod_row0),
        grid=(m // tm,),
        in_specs=[row, pl.BlockSpec((d, d), lambda i: (0, 0)), row, mspec(2), vec, mspec(3),
                  mspec(4), vec],
        args=(mg, w_o, x, mod, gpost, mod, mod, gpre),
        out_specs=[row, row],
        out_shape=[jax.ShapeDtypeStruct((m, d), _F32), jax.ShapeDtypeStruct((m, d), _BF16)],
        name="out_proj_prompt")
    return x1, h2


def _p_ffn_kernel(h_ref, wa_ref, wb_ref, cw_ref, wd_ref, x1_hbm, gt_ref, gpost_ref,
                  out_ref, tail_ref, x1_buf, x1_sem, carry_scr, *, tm, n_blk, tiles_per_seq, slab,
                  mod_row0, side):
    i = pl.program_id(0)
    j = pl.program_id(1)
    x1_copy = pltpu.make_async_copy(x1_hbm.at[pl.ds(pl.multiple_of(i * tm, tm), tm), :],
                                    x1_buf, x1_sem)

    @pl.when(i % tiles_per_seq == 0)
    def _():
        carry_scr[j] = jnp.zeros(carry_scr.shape[1:], _F32)

    def step(first, last):
        prev = carry_scr[j]
        cw = cw_ref.at[j]
        slabs = _slabs(tm, slab)
        if last:
            g_last = _mod_row(gt_ref, tiles_per_seq, mod_row0) * gpost_ref[...]
        gs = []
        for r in slabs:
            h = h_ref[r, :]
            a = _dot(h, wa_ref[...])
            b = _dot(h, wb_ref[...])
            gs.append((jax.nn.gelu(_causal_conv_rows(a, prev, cw)) * b).astype(_BF16))
            prev = a[slab - SUBLANES:, :]
        for r, g in zip(slabs, gs):
            f = _dot(g, wd_ref[...])
            acc = f if first else out_ref[r, :] + f
            if last:
                acc = x1_buf[r, :] + _rms(acc, g_last)
            out_ref[r, :] = acc
        carry_scr[j] = prev
        tail_ref[j] = prev

    @pl.when(j == 0)
    def _():
        x1_copy.start()
        step(True, False)

    @pl.when((j > 0) & (j < n_blk - 1))
    def _():
        step(False, False)

    @pl.when(j == n_blk - 1)
    def _():
        x1_copy.wait()
        step(False, True)


def _p_ffn_call(h2, w_a, w_b, cw, w_down, x1, mod, gpost, *, tm, tn, tiles_per_seq, slab, mod_row0):
    m, d = x1.shape
    f = cw.shape[1]
    n_blk = f // tn
    row = pl.BlockSpec((tm, d), lambda i, j: (i, 0))
    wspec = pl.BlockSpec((d, tn), lambda i, j: (0, j))
    (out, tail), _ = _run(
        functools.partial(_p_ffn_kernel, tm=tm, n_blk=n_blk, tiles_per_seq=tiles_per_seq, slab=slab,
                          mod_row0=mod_row0),
        grid=(m // tm, n_blk),
        in_specs=[row, wspec, wspec,
                  pl.BlockSpec((n_blk, CONV_K, tn), lambda i, j: (0, 0, 0)),
                  pl.BlockSpec((tn, d), lambda i, j: (j, 0)),
                  pl.BlockSpec(memory_space=pl.ANY),
                  _mod_row_spec(d, 5, tiles_per_seq, mod_row0, 2),
                  pl.BlockSpec((1, d), lambda i, j: (0, 0))],
        args=(h2, w_a, w_b, _col_blocks(cw, tn), w_down, x1, mod, gpost),
        out_specs=[row, pl.BlockSpec((None, n_blk, SUBLANES, tn), lambda i, j: (i, 0, 0, 0))],
        out_shape=[jax.ShapeDtypeStruct((m, d), _F32),
                   jax.ShapeDtypeStruct((m // tm, n_blk, SUBLANES, tn), _F32)],
        scratch=[pltpu.VMEM((tm, d), _F32), pltpu.SemaphoreType.DMA(()),
                 pltpu.VMEM((n_blk, SUBLANES, tn), _F32)],
        name="convffn_prompt")
    return out, tail


def _cols(src, first_col, n_cols, n_blocks):
    block = n_cols // n_blocks
    assert block * n_blocks == n_cols and block % 128 == 0 and first_col % block == 0
    return _Side(src, 1, first_col // block, block, n_blocks)


def _layer(xs, xp, mod, st_b, st_f, p, *, seq_len, mod_row0):
    d = p["g_v"].shape[1]
    wb = p["conv_b_w"].shape[1]
    f = p["conv_f_w"].shape[1]
    tn_s = SAMPLE_COL_TILE

    def tiles(name):
        tm, tn, slab = PROMPT_TILES[name]
        return dict(tm=tm, tn=tn, slab=slab, tiles_per_seq=seq_len // tm)

    def n_steps(name):
        tm, tn, _ = PROMPT_TILES[name]
        cols = {"gmlp": 2 * d, "shortconv": wb, "merge": d, "ffn": f}[name]
        return xp.shape[0] // tm * (cols // tn)

    def without(kw, *names):
        return {k: v for k, v in kw.items() if k not in names}

    def blocks(n_cols, budget):
        return max(n for n in range(1, budget + 1) if n_cols % (128 * n) == 0)

    w_in, w_up = p["w_in"], p["w_up"]

    ya_s, h1_s, vn, w_v, w_u = _s_gmlp_call(xs, mod, p["g_pre_mix"], w_in, p["g_v"], p["wvec"],
                                            p["bvec"], tn=tn_s)
    n = n_steps("gmlp")
    ya_p, h1_p, (w_bcx,) = _p_gmlp_call(
        xp, mod, p["g_pre_mix"], w_v, w_u, p["g_v"], p["w_s"], p["bias"], mod_row0=mod_row0,
        side=(_cols(w_in, 2 * d, 3 * wb, blocks(3 * wb, n)),),
        **tiles("gmlp"))

    yb_s, new_b = _s_shortconv_call(h1_s, w_bcx, p["conv_b_w"], st_b, tn=tn_s)
    n = n_steps("shortconv")
    yb_p, tail_b, (w_gate, w_oa, w_ob, w_o, w_a, w_b) = _p_shortconv_call(
        h1_p, w_bcx, p["conv_b_w"],
        side=(_cols(w_in, 2 * d + 3 * wb, 2 * d, blocks(2 * d, n)),
              _cols(p["w_out_a"], 0, d, blocks(d, n)), _cols(p["w_out_b"], 0, d, blocks(d, n)),
              _cols(p["w_o"], 0, d, blocks(d, n)),
              _cols(w_up, 0, f, blocks(f, n)), _cols(w_up, f, f, blocks(f, n))),
        **tiles("shortconv"))

    x1_s, h2_s = _s_merge_proj_call(h1_s, ya_s, yb_s, w_gate, w_oa, w_ob, w_o, xs, mod,
                                    p["g_post_mix"], p["g_pre_ffn"], tn=tn_s)
    n_row_blocks = max(k for k in range(1, n_steps("merge") + 1)
                       if f % k == 0 and (f // k) % (2 * SUBLANES) == 0)
    mg_p, (w_d,) = _merge_call(h1_p, ya_p, yb_p, w_gate, w_oa, w_ob, name="gated_merge_prompt",
                               side=(_Side(p["w_down"], 0, 0, f // n_row_blocks, n_row_blocks),),
                               **without(tiles("merge"), "tiles_per_seq"))

    x1_p, h2_p = _p_proj_call(mg_p, w_o, xp, mod, p["g_post_mix"], p["g_pre_ffn"],
                              mod_row0=mod_row0, **without(tiles("proj"), "tn"))

    out_s, new_f = _s_ffn_call(h2_s, w_a, w_b, p["conv_f_w"], w_d, x1_s, mod, p["g_post_ffn"],
                               st_f, tn=tn_s)
    out_p, tail_f = _p_ffn_call(h2_p, w_a, w_b, p["conv_f_w"], w_d, x1_p, mod, p["g_post_ffn"],
                                mod_row0=mod_row0, **tiles("ffn"))
    return out_s, out_p, new_b, new_f, vn, tail_b, tail_f


def kernel(x_prompt, x_sample, c_prompt, c_sample, state_conv_b, state_conv_ffn, w_ada, b_ada, g_pre_mix, g_post_mix, w_in, g_v, w_s, b_s, conv_b_w, w_out_a, w_out_b, w_o, g_pre_ffn, g_post_ffn, w_up, conv_f_w, w_down):
    depth = w_in.shape[0]
    bp, seq, d = x_prompt.shape
    bs, tdec, _ = x_sample.shape
    n_groups = w_s.shape[1]
    assert bs == CHUNK and tdec <= CHUNK
    assert all(seq % tm == 0 for tm, _, _ in PROMPT_TILES.values())

    xp = x_prompt.reshape(bp * seq, d)
    xs = x_sample
    pad = (-(bp + bs)) % SUBLANES
    c_all = jnp.concatenate([c_sample, c_prompt, jnp.zeros((pad, d), _F32)], axis=0)

    pb, sb, pf, sf, sv = [], [], [], [], []
    for l in range(depth):
        mod, _ = _mod_call(c_all, w_ada[l], b_ada[l][None, :], side=())
        vec = lambda a: a[l][None, :]
        bias_full = jnp.repeat(jnp.transpose(b_s[l]), GROUP, axis=1)
        wvec = jnp.repeat(
            jnp.transpose(w_s[l][:, :tdec, :tdec], (1, 2, 0)).reshape(tdec * tdec, n_groups),
            GROUP, axis=1)
        p = {
            "w_in": w_in[l], "w_out_a": w_out_a[l], "w_out_b": w_out_b[l],
            "w_o": w_o[l], "w_up": w_up[l], "w_down": w_down[l],
            "g_pre_mix": vec(g_pre_mix), "g_post_mix": vec(g_post_mix), "g_v": vec(g_v),
            "g_pre_ffn": vec(g_pre_ffn), "g_post_ffn": vec(g_post_ffn),
            "conv_b_w": conv_b_w[l], "conv_f_w": conv_f_w[l],
            "w_s": w_s[l], "bias": bias_full, "wvec": wvec, "bvec": bias_full[:tdec],
        }
        xs, xp, sbt, sft, vn, tb, tf = _layer(xs, xp, mod, state_conv_b[l], state_conv_ffn[l], p,
                                              seq_len=seq, mod_row0=bs)

        def prompt_tail(t):
            n_tiles, n_blk, _, tn = t.shape
            t = t.reshape(bp, n_tiles // bp, n_blk, SUBLANES, tn)[:, -1, :, SUBLANES - (CONV_K - 1):, :]
            return jnp.transpose(t, (0, 2, 1, 3)).reshape(bp, CONV_K - 1, n_blk * tn)

        pb.append(prompt_tail(tb))
        pf.append(prompt_tail(tf))
        sb.append(sbt)
        sf.append(sft)
        sv.append(vn)

    y_prompt = xp.reshape(bp, seq, d)
    return (y_prompt, xs, jnp.stack(pb), jnp.stack(sb), jnp.stack(pf), jnp.stack(sf),
            jnp.stack(sv))
```
